```python
import math
import jax
import jax.numpy as jnp
from jax import lax
import numpy as np

D_MODEL = 1024
BATCH = 8
SEQ = 8192
DEPTH = 2

CTX_LEN = 256
GRID_W = 64
RMS_EPS = 1e-6

HEAD_DIM = 64
N_Q_HEADS = 8
N_KV_HEADS = 2
Q_PER_KV = N_Q_HEADS // N_KV_HEADS
ATTN_W = N_Q_HEADS * HEAD_DIM
KV_W = N_KV_HEADS * HEAD_DIM
Q_BLOCK = 128
ROPE_THETA = 10000.0

HYENA_W = 256
HYENA_ORDER = 2
HYENA_BANDS = 16
HYENA_POS_DIM = 1 + 2 * HYENA_BANDS
HYENA_HIDDEN = 64
HYENA_INNER = 2
HYENA_FILTER_OUT = HYENA_ORDER * 2 * HYENA_W
HYENA_DECAY_TARGET = 1e-2
HYENA_FAST_PCT = 0.3
HYENA_SLOW_PCT = 1.5
HYENA_MIN_DECAY = -math.log(HYENA_DECAY_TARGET) / HYENA_SLOW_PCT
HYENA_MAX_DECAY = -math.log(HYENA_DECAY_TARGET) / HYENA_FAST_PCT

POOL_W = 256
POOL_WINDOWS = (2, 4, 8, 16)
POOL_GROUP_W = POOL_W // len(POOL_WINDOWS)

GMLP_W = 256
GMLP_CHUNK = 128
GMLP_GROUPS = 4
GMLP_GROUP_W = GMLP_W // GMLP_GROUPS

N_BRANCH = 4
FFN_HIDDEN = ((8 * D_MODEL // 3 + 255) // 256) * 256

Q_OFF = 0
K_OFF = Q_OFF + ATTN_W
V_OFF = K_OFF + KV_W
HY_OFF = V_OFF + KV_W
POOL_OFF = HY_OFF + 3 * HYENA_W
GM_OFF = POOL_OFF + POOL_W
GATE_OFF = GM_OFF + 2 * GMLP_W
IN_W = GATE_OFF + N_BRANCH * D_MODEL

kernel_name = 'hybrid_prefix_diffusion_trunk'


def rms_norm(x, w):
    xf = x.astype(jnp.float32)
    y = xf * lax.rsqrt(jnp.mean(xf * xf, axis=-1, keepdims=True) + RMS_EPS)
    return (y * w.astype(jnp.float32)).astype(x.dtype)


def layer_norm(x, w):
    xf = x.astype(jnp.float32)
    mu = jnp.mean(xf, axis=-1, keepdims=True)
    var = jnp.mean(jnp.square(xf - mu), axis=-1, keepdims=True)
    return ((xf - mu) * lax.rsqrt(var + RMS_EPS) * w.astype(jnp.float32)).astype(x.dtype)


def modulate(x, norm_w, shift, scale):
    return rms_norm(x, norm_w) * (1 + scale) + shift


def axial_rope_tables(n_tokens, dtype):
    rows = n_tokens // GRID_W
    r, col = jnp.meshgrid(jnp.arange(rows, dtype=jnp.float32), jnp.arange(GRID_W, dtype=jnp.float32), indexing='ij')
    axis_dim = HEAD_DIM // 2
    inv_freq = ROPE_THETA ** (-jnp.arange(0, axis_dim, 2, dtype=jnp.float32) / axis_dim)
    ang = jnp.concatenate([r.reshape(-1, 1) * inv_freq, col.reshape(-1, 1) * inv_freq], axis=-1)
    return jnp.cos(ang).astype(dtype), jnp.sin(ang).astype(dtype)


def apply_rope(t, cos, sin):
    c = cos[None, :, None, :]
    s = sin[None, :, None, :]
    t1 = t[..., 0::2]
    t2 = t[..., 1::2]
    return jnp.stack([t1 * c - t2 * s, t1 * s + t2 * c], axis=-1).reshape(t.shape)


def attn_q(p_q, q_norm_w):
    b, n = p_q.shape[:2]
    return rms_norm(p_q.reshape(b, n, N_Q_HEADS, HEAD_DIM), q_norm_w)


def attn_kv(p_kv, k_norm_w):
    b, n = p_kv.shape[:2]
    k = rms_norm(p_kv[..., :KV_W].reshape(b, n, N_KV_HEADS, HEAD_DIM), k_norm_w)
    v = p_kv[..., KV_W:].reshape(b, n, N_KV_HEADS, HEAD_DIM)
    return k, v


def gqa_attend(q, k, v):
    b, lq = q.shape[:2]
    qg = q.reshape(b, lq, N_KV_HEADS, Q_PER_KV, HEAD_DIM)
    s = jnp.einsum('bqkgd,bskd->bkgqs', qg, k, preferred_element_type=jnp.float32) * (HEAD_DIM ** -0.5)
    p = jax.nn.softmax(s, axis=-1).astype(v.dtype)
    o = jnp.einsum('bkgqs,bskd->bqkgd', p, v)
    return o.reshape(b, lq, ATTN_W)


def blocked_attention(q, k, v):
    b, lq = q.shape[:2]
    nb = lq // Q_BLOCK
    qb = q.reshape(b, nb, Q_BLOCK, N_Q_HEADS, HEAD_DIM).transpose(1, 0, 2, 3, 4)
    ob = lax.map(lambda qi: gqa_attend(qi, k, v), qb)
    return ob.transpose(1, 0, 2, 3).reshape(b, lq, ATTN_W)


def hyena_filters(n, lp):
    pos = jnp.arange(n, dtype=jnp.float32)
    t = pos / (n - 1)
    bands = jnp.linspace(1e-4, HYENA_BANDS - 1, HYENA_BANDS, dtype=jnp.float32)
    ang = (2.0 * math.pi / n) * pos[:, None] * bands
    z = jnp.concatenate([t[:, None], jnp.cos(ang), jnp.sin(ang)], axis=-1)
    freq = lp['hy_freq'].astype(jnp.float32)
    hdn = jnp.sin(freq * (z @ lp['hy_w1'].astype(jnp.float32) + lp['hy_b1'].astype(jnp.float32)))
    for i in range(HYENA_INNER):
        hdn = jnp.sin(freq * (hdn @ lp['hy_w2'][i].astype(jnp.float32) + lp['hy_b2'][i].astype(jnp.float32)))
    window = jnp.exp(-t[:, None] * jnp.abs(lp['hy_decay'].astype(jnp.float32)))
    filt = ((hdn @ lp['hy_w3'].astype(jnp.float32)) * window).reshape(n, HYENA_ORDER, 2, HYENA_W)
    fwd = filt[:, :, 0]
    bwd = filt[:, :, 1]
    k = jnp.concatenate([fwd, jnp.zeros((1, HYENA_ORDER, HYENA_W), jnp.float32), bwd[:0:-1]], axis=0)
    return k / jnp.sum(jnp.abs(k), axis=0, keepdims=True)


def long_conv(u, kf, skip):
    n = u.shape[1]
    uf = jnp.fft.rfft(u.astype(jnp.float32), n=2 * n, axis=1)
    y = jnp.fft.irfft(uf * kf[None], n=2 * n, axis=1)[:, :n]
    return (y + u.astype(jnp.float32) * skip.astype(jnp.float32)).astype(u.dtype)


def hyena_mixer(p, lp):
    n = p.shape[1]
    w = lp['hy_conv_w']
    zp = jnp.pad(p, ((0, 0), (1, 1), (0, 0)))
    z = zp[:, :-2] * w[0] + zp[:, 1:-1] * w[1] + zp[:, 2:] * w[2] + lp['hy_conv_b']
    v, x1, x2 = jnp.split(z, 3, axis=-1)
    kf = jnp.fft.rfft(hyena_filters(n, lp), n=2 * n, axis=0)
    y = v
    for o, gate in enumerate((x1, x2)):
        y = gate * long_conv(y, kf[:, o], lp['hy_skip'][o])
    return y


def pool_mixer(z, pool_w, pool_scale):
    b, n, _ = z.shape
    zf = z.astype(jnp.float32)
    csum = jnp.concatenate([jnp.zeros((b, 1, POOL_W), jnp.float32), jnp.cumsum(zf, axis=1)], axis=1)
    pos = jnp.arange(n)
    means = []
    for g, win in enumerate(POOL_WINDOWS):
        lo = jnp.clip(pos - win // 2, 0, n)
        hi = jnp.clip(pos + win - win // 2, 0, n)
        cg = csum[..., g * POOL_GROUP_W:(g + 1) * POOL_GROUP_W]
        means.append((cg[:, hi] - cg[:, lo]) / (hi - lo).astype(jnp.float32)[None, :, None])
    d = (jnp.concatenate(means, axis=-1) - zf).astype(z.dtype).reshape(b, n, len(POOL_WINDOWS), POOL_GROUP_W)
    y = jnp.einsum('bngc,gcd->bngd', d, pool_w).reshape(b, n, POOL_W)
    return y * pool_scale


def gmlp_mixer(p, norm_w, ws, bs):
    b, n, _ = p.shape
    u, v = jnp.split(p, 2, axis=-1)
    v = layer_norm(v, norm_w).reshape(b, n // GMLP_CHUNK, GMLP_CHUNK, GMLP_GROUPS, GMLP_GROUP_W)
    mixed = jnp.einsum('gts,bcsgd->bctgd', ws, v) + bs.T[None, None, :, :, None]
    return u * mixed.reshape(b, n, GMLP_W)


def parallel_mixers(p, attn_out, lp):
    hy = hyena_mixer(p[..., HY_OFF:POOL_OFF], lp)
    pl = pool_mixer(p[..., POOL_OFF:GM_OFF], lp['pool_w'], lp['pool_scale'])
    gm = gmlp_mixer(p[..., GM_OFF:GATE_OFF], lp['gm_norm_w'], lp['gm_ws'], lp['gm_bs'])
    g = jax.nn.sigmoid(p[..., GATE_OFF:].reshape(p.shape[:-1] + (N_BRANCH, D_MODEL)))
    merged = (g[..., 0, :] * (attn_out @ lp['w_br_attn'])
              + g[..., 1, :] * (hy @ lp['w_br_hyena'])
              + g[..., 2, :] * (pl @ lp['w_br_pool'])
              + g[..., 3, :] * (gm @ lp['w_br_gmlp']))
    return merged @ lp['w_out']


def swiglu(h, wg, wu, wd):
    return (jax.nn.silu(h @ wg) * (h @ wu)) @ wd


def setup_inputs(seed: int = 0) -> dict:
    key = jax.random.key(seed)
    ks = iter(jax.random.split(key, 48))

    def nrm(shape, scale):
        return jax.random.normal(next(ks), shape, jnp.float32) * scale

    L = DEPTH
    D = D_MODEL
    decay_base = jnp.linspace(HYENA_MIN_DECAY, HYENA_MAX_DECAY, HYENA_FILTER_OUT, dtype=jnp.float32)[None]
    return {
        'x': nrm((BATCH, SEQ, D), 1.0),
        'c': nrm((BATCH, D), 1.0),
        'ctx': nrm((BATCH, CTX_LEN, D), 1.0),
        'c_ctx': nrm((D,), 1.0),
        'w_mod': nrm((L, D, 6 * D), 0.5 * D ** -0.5),
        'b_mod': nrm((L, 6 * D), 0.02),
        'norm1_w': 1.0 + nrm((L, D), 0.05),
        'norm2_w': 1.0 + nrm((L, D), 0.05),
        'w_in': nrm((L, D, IN_W), D ** -0.5),
        'q_norm_w': 1.0 + nrm((L, HEAD_DIM), 0.05),
        'k_norm_w': 1.0 + nrm((L, HEAD_DIM), 0.05),
        'hy_conv_w': nrm((L, 3, 3 * HYENA_W), 3 ** -0.5),
        'hy_conv_b': nrm((L, 3 * HYENA_W), 0.02),
        'hy_w1': nrm((L, HYENA_POS_DIM, HYENA_HIDDEN), HYENA_POS_DIM ** -0.5),
        'hy_b1': nrm((L, HYENA_HIDDEN), 0.1),
        'hy_w2': nrm((L, HYENA_INNER, HYENA_HIDDEN, HYENA_HIDDEN), HYENA_HIDDEN ** -0.5),
        'hy_b2': nrm((L, HYENA_INNER, HYENA_HIDDEN), 0.1),
        'hy_w3': nrm((L, HYENA_HIDDEN, HYENA_FILTER_OUT), HYENA_HIDDEN ** -0.5),
        'hy_freq': 1.0 + nrm((L, HYENA_HIDDEN), 0.05),
        'hy_decay': decay_base * (1.0 + nrm((L, HYENA_FILTER_OUT), 0.05)),
        'hy_skip': nrm((L, HYENA_ORDER, HYENA_W), 0.5),
        'pool_w': nrm((L, len(POOL_WINDOWS), POOL_GROUP_W, POOL_GROUP_W), POOL_GROUP_W ** -0.5),
        'pool_scale': 1.0 + nrm((L, POOL_W), 0.1),
        'gm_norm_w': 1.0 + nrm((L, GMLP_W), 0.05),
        'gm_ws': nrm((L, GMLP_GROUPS, GMLP_CHUNK, GMLP_CHUNK), GMLP_CHUNK ** -0.5),
        'gm_bs': nrm((L, GMLP_GROUPS, GMLP_CHUNK), 0.1),
        'w_br_attn': nrm((L, ATTN_W, D), ATTN_W ** -0.5),
        'w_br_hyena': nrm((L, HYENA_W, D), HYENA_W ** -0.5),
        'w_br_pool': nrm((L, POOL_W, D), POOL_W ** -0.5),
        'w_br_gmlp': nrm((L, GMLP_W, D), GMLP_W ** -0.5),
        'w_out': nrm((L, D, D), D ** -0.5),
        'ffn_w_gate': nrm((L, D, FFN_HIDDEN), D ** -0.5),
        'ffn_w_up': nrm((L, D, FFN_HIDDEN), D ** -0.5),
        'ffn_w_down': nrm((L, FFN_HIDDEN, D), FFN_HIDDEN ** -0.5),
        'final_norm_w': 1.0 + nrm((D,), 0.05),
    }


def reference(x, c, ctx, c_ctx, w_mod, b_mod, norm1_w, norm2_w, w_in, q_norm_w, k_norm_w,
              hy_conv_w, hy_conv_b, hy_w1, hy_b1, hy_w2, hy_b2, hy_w3, hy_freq, hy_decay, hy_skip,
              pool_w, pool_scale, gm_norm_w, gm_ws, gm_bs,
              w_br_attn, w_br_hyena, w_br_pool, w_br_gmlp, w_out,
              ffn_w_gate, ffn_w_up, ffn_w_down, final_norm_w):
    cos, sin = axial_rope_tables(x.shape[1], x.dtype)
    xc = ctx
    act_lat = jax.nn.silu(c)
    act_ctx = jax.nn.silu(c_ctx)
    for l in range(DEPTH):
        lp = {
            'hy_conv_w': hy_conv_w[l], 'hy_conv_b': hy_conv_b[l],
            'hy_w1': hy_w1[l], 'hy_b1': hy_b1[l], 'hy_w2': hy_w2[l], 'hy_b2': hy_b2[l],
            'hy_w3': hy_w3[l], 'hy_freq': hy_freq[l], 'hy_decay': hy_decay[l], 'hy_skip': hy_skip[l],
            'pool_w': pool_w[l], 'pool_scale': pool_scale[l],
            'gm_norm_w': gm_norm_w[l], 'gm_ws': gm_ws[l], 'gm_bs': gm_bs[l],
            'w_br_attn': w_br_attn[l], 'w_br_hyena': w_br_hyena[l],
            'w_br_pool': w_br_pool[l], 'w_br_gmlp': w_br_gmlp[l], 'w_out': w_out[l],
        }
        mod = (act_lat @ w_mod[l] + b_mod[l])[:, None, :]
        sh1, s1, g1, sh2, s2, g2 = jnp.split(mod, 6, axis=-1)
        csh1, cs1, cg1, csh2, cs2, cg2 = jnp.split(act_ctx @ w_mod[l] + b_mod[l], 6)

        hc = modulate(xc, norm1_w[l], csh1, cs1)
        if l < DEPTH - 1:
            pc = hc @ w_in[l]
            k_c, v_c = attn_kv(pc[..., K_OFF:HY_OFF], k_norm_w[l])
            q_c = attn_q(pc[..., Q_OFF:K_OFF], q_norm_w[l])
            xc = xc + cg1 * parallel_mixers(pc, gqa_attend(q_c, k_c, v_c), lp)
            xc = xc + cg2 * swiglu(modulate(xc, norm2_w[l], csh2, cs2), ffn_w_gate[l], ffn_w_up[l], ffn_w_down[l])
        else:
            k_c, v_c = attn_kv(hc @ w_in[l][:, K_OFF:HY_OFF], k_norm_w[l])

        p = modulate(x, norm1_w[l], sh1, s1) @ w_in[l]
        q = apply_rope(attn_q(p[..., Q_OFF:K_OFF], q_norm_w[l]), cos, sin)
        k, v = attn_kv(p[..., K_OFF:HY_OFF], k_norm_w[l])
        k = apply_rope(k, cos, sin)
        attn = blocked_attention(q, jnp.concatenate([k_c, k], axis=1), jnp.concatenate([v_c, v], axis=1))
        x = x + g1 * parallel_mixers(p, attn, lp)
        x = x + g2 * swiglu(modulate(x, norm2_w[l], sh2, s2), ffn_w_gate[l], ffn_w_up[l], ffn_w_down[l])
    return rms_norm(x, final_norm_w)
```

```python
import functools
import math

import numpy as np
import jax
import jax.numpy as jnp
from jax import lax
from jax.experimental import pallas as pl
from jax.experimental.pallas import tpu as pltpu

F32 = jnp.float32
BF16 = jnp.bfloat16

D_MODEL = 1024
GRID_W = 64
RMS_EPS = 1e-6
HEAD_DIM = 64
N_Q_HEADS = 8
N_KV_HEADS = 2
ATTN_W = N_Q_HEADS * HEAD_DIM
KV_W = N_KV_HEADS * HEAD_DIM
ROPE_THETA = 10000.0
HYENA_W = 256
HYENA_ORDER = 2
HYENA_BANDS = 16
HYENA_HIDDEN = 64
HYENA_INNER = 2
POOL_W = 256
POOL_WINDOWS = (2, 4, 8, 16)
POOL_GROUP_W = POOL_W // len(POOL_WINDOWS)
POOL_HALO = 8
GMLP_W = 256
GMLP_CHUNK = 128
GMLP_GROUPS = 4
GMLP_GROUP_W = GMLP_W // GMLP_GROUPS
N_BRANCH = 4
FFN_HIDDEN = ((8 * D_MODEL // 3 + 255) // 256) * 256

Q_OFF = 0
K_OFF = Q_OFF + ATTN_W
V_OFF = K_OFF + KV_W
HY_OFF = V_OFF + KV_W
POOL_OFF = HY_OFF + 3 * HYENA_W
GM_OFF = POOL_OFF + POOL_W
GATE_OFF = GM_OFF + 2 * GMLP_W
IN_W = GATE_OFF + N_BRANCH * D_MODEL

P_GATE = 0
P_Q = P_GATE + N_BRANCH * D_MODEL
P_HY = P_Q + ATTN_W
P_KV = P_HY + 3 * HYENA_W
P_POOL = P_KV + 2 * KV_W
P_GMU = P_POOL + POOL_W
P_GMV = P_GMU + GMLP_W

LANES = 128
VMEM_LIMIT = 52 * 1024 * 1024


def _cparams(sem):
    return pltpu.CompilerParams(dimension_semantics=sem, vmem_limit_bytes=VMEM_LIMIT)


def _tile(n, cap, mult=8):
    best = None
    for t in range(mult, min(n, cap) + 1, mult):
        if n % t == 0:
            best = t
    assert best is not None, (n, cap, mult)
    return best


def _split(x):
    hi = x.astype(BF16)
    lo = (x - hi.astype(F32)).astype(BF16)
    return hi, lo


def _dot(a, b):
    return jnp.dot(a, b, preferred_element_type=F32)


def _dot3(a_hi, a_lo, x):
    x_hi, x_lo = _split(x)
    return _dot(a_hi, x_hi) + (_dot(a_lo, x_hi) + _dot(a_hi, x_lo))


def _dot3x(x, b_hi, b_lo):
    x_hi, x_lo = _split(x)
    return _dot(x_hi, b_hi) + (_dot(x_lo, b_hi) + _dot(x_hi, b_lo))


def _sigmoid(x):
    return 1.0 / (1.0 + jnp.exp(-x))


def _silu(x):
    return x * _sigmoid(x)


def _modulated_norm(x, nw, scale, shift):
    ms = jnp.mean(x * x, axis=-1, keepdims=True)
    y = x * lax.rsqrt(ms + RMS_EPS) * nw
    return y * (1.0 + scale) + shift


def _mod_kernel(a_ref, w_ref, b_ref, o_ref):
    a = _silu(a_ref[...]).astype(BF16)
    o_ref[...] = _dot(a, w_ref[...].astype(BF16)) + b_ref[...]


def _mod_call(act, w_mod, b_mod):
    L, D, W6 = w_mod.shape
    R = act.shape[0]
    tn = 1024
    return pl.pallas_call(
        _mod_kernel,
        grid=(L, W6 // tn),
        in_specs=[
            pl.BlockSpec((R, D), lambda l, j: (0, 0)),
            pl.BlockSpec((None, D, tn), lambda l, j: (l, 0, j)),
            pl.BlockSpec((None, 1, tn), lambda l, j: (l, 0, j)),
        ],
        out_specs=pl.BlockSpec((None, R, tn), lambda l, j: (l, 0, j)),
        out_shape=jax.ShapeDtypeStruct((L, R, W6), F32),
        compiler_params=_cparams(("arbitrary", "arbitrary")),
        name="mod",
    )(act, w_mod, b_mod.reshape(L, 1, W6))


def _inproj_kernel(x_ref, sh_ref, sc_ref, nw_ref, w_ref, o_ref, h_ref):
    @pl.when(pl.program_id(2) == 0)
    def _():
        h_ref[...] = _modulated_norm(x_ref[...], nw_ref[...], sc_ref[...], sh_ref[...]).astype(BF16)

    o_ref[...] = _dot(h_ref[...], w_ref[...])


def _inproj_call(x, shift, scale, norm_w, w, tn):
    B, n, D = x.shape
    Nw = w.shape[1]
    tm = _tile(n, 1024)
    return pl.pallas_call(
        _inproj_kernel,
        grid=(B, n // tm, Nw // tn),
        in_specs=[
            pl.BlockSpec((None, tm, D), lambda b, i, j: (b, i, 0)),
            pl.BlockSpec((None, 1, D), lambda b, i, j: (b, 0, 0)),
            pl.BlockSpec((None, 1, D), lambda b, i, j: (b, 0, 0)),
            pl.BlockSpec((1, D), lambda b, i, j: (0, 0)),
            pl.BlockSpec((D, tn), lambda b, i, j: (0, j)),
        ],
        out_specs=pl.BlockSpec((None, tm, tn), lambda b, i, j: (b, i, j)),
        out_shape=jax.ShapeDtypeStruct((B, n, Nw), F32),
        scratch_shapes=[pltpu.VMEM((tm, D), BF16)],
        compiler_params=_cparams(("parallel", "parallel", "arbitrary")),
        name="inproj",
    )(x, shift, scale, norm_w.reshape(1, D), w)


def _head_norm_rope(t, w, cs, sn, bd_ref):
    sq_hi, sq_lo = _split(t * t)
    ms = _dot(sq_hi, bd_ref[...]) + _dot(sq_lo, bd_ref[...])
    y = t * lax.rsqrt(ms + RMS_EPS) * w
    lane = lax.broadcasted_iota(jnp.int32, y.shape, 1)
    swapped = jnp.where(lane % 2 == 0, pltpu.roll(y, LANES - 1, 1), pltpu.roll(y, 1, 1))
    return y * cs + swapped * sn


def _qkprep_kernel(q_ref, kv_ref, cs_ref, sn_ref, qw_ref, kw_ref, bd_ref, qo_ref, ko_ref, vo_ref, *, q_scale):
    cs = cs_ref[...]
    sn = sn_ref[...]
    for c in range(ATTN_W // LANES):
        sl = slice(c * LANES, (c + 1) * LANES)
        r = _head_norm_rope(q_ref[:, sl], qw_ref[:, sl], cs, sn, bd_ref)
        qo_ref[:, sl] = (r * q_scale).astype(BF16)
    ko_ref[...] = _head_norm_rope(kv_ref[:, :KV_W], kw_ref[...], cs, sn, bd_ref).astype(BF16)
    vo_ref[...] = kv_ref[:, KV_W:].astype(BF16)


def _qkprep_call(p, cs, sn, qw, kw, q_scale):
    B, n, _ = p.shape
    tm = _tile(n, 512)
    bd = np.kron(np.eye(LANES // HEAD_DIM), np.full((HEAD_DIM, HEAD_DIM), 1.0 / HEAD_DIM))
    bd = jnp.asarray(bd, BF16)
    return pl.pallas_call(
        functools.partial(_qkprep_kernel, q_scale=q_scale),
        grid=(n // tm, B),
        in_specs=[
            pl.BlockSpec((None, tm, ATTN_W), lambda i, b: (b, i, P_Q // ATTN_W)),
            pl.BlockSpec((None, tm, 2 * KV_W), lambda i, b: (b, i, P_KV // (2 * KV_W))),
            pl.BlockSpec((tm, LANES), lambda i, b: (i, 0)),
            pl.BlockSpec((tm, LANES), lambda i, b: (i, 0)),
            pl.BlockSpec((1, ATTN_W), lambda i, b: (0, 0)),
            pl.BlockSpec((1, KV_W), lambda i, b: (0, 0)),
            pl.BlockSpec((LANES, LANES), lambda i, b: (0, 0)),
        ],
        out_specs=[
            pl.BlockSpec((None, tm, ATTN_W), lambda i, b: (b, i, 0)),
            pl.BlockSpec((None, tm, KV_W), lambda i, b: (b, i, 0)),
            pl.BlockSpec((None, tm, KV_W), lambda i, b: (b, i, 0)),
        ],
        out_shape=[
            jax.ShapeDtypeStruct((B, n, ATTN_W), BF16),
            jax.ShapeDtypeStruct((B, n, KV_W), BF16),
            jax.ShapeDtypeStruct((B, n, KV_W), BF16),
        ],
        compiler_params=_cparams(("parallel", "parallel")),
        name="qkprep",
    )(p, p, cs, sn, qw, kw, bd)


def _attn_kernel(q_ref, k_ref, v_ref, o_ref, m_ref, l_ref, acc_ref):
    j = pl.program_id(2)

    @pl.when(j == 0)
    def _():
        m_ref[...] = jnp.full(m_ref.shape, -jnp.inf, F32)
        l_ref[...] = jnp.zeros(l_ref.shape, F32)
        acc_ref[...] = jnp.zeros(acc_ref.shape, F32)

    for h in range(N_Q_HEADS):
        g = h // (N_Q_HEADS // N_KV_HEADS)
        hs = slice(h * HEAD_DIM, (h + 1) * HEAD_DIM)
        gs = slice(g * HEAD_DIM, (g + 1) * HEAD_DIM)
        s = lax.dot_general(q_ref[:, hs], k_ref[:, gs], (((1,), (1,)), ((), ())),
                            preferred_element_type=F32)
        m_prev = m_ref[h]
        m_new = jnp.maximum(m_prev, jnp.max(s, axis=-1, keepdims=True))
        alpha = jnp.exp2(m_prev - m_new)
        p = jnp.exp2(s - m_new)
        l_ref[h] = alpha * l_ref[h] + jnp.sum(p, axis=-1, keepdims=True)
        acc_ref[h] = alpha * acc_ref[h] + _dot(p.astype(BF16), v_ref[:, gs])
        m_ref[h] = m_new

    @pl.when(j == pl.num_programs(2) - 1)
    def _():
        for h in range(N_Q_HEADS):
            o_ref[:, h * HEAD_DIM:(h + 1) * HEAD_DIM] = (acc_ref[h] / l_ref[h]).astype(BF16)


def _attn_call(q, k, v):
    B, n, _ = q.shape
    Lk = k.shape[1]
    tq = _tile(n, 512)
    tk = _tile(Lk, 1024, LANES)
    return pl.pallas_call(
        _attn_kernel,
        grid=(B, n // tq, Lk // tk),
        in_specs=[
            pl.BlockSpec((None, tq, ATTN_W), lambda b, i, j: (b, i, 0)),
            pl.BlockSpec((None, tk, KV_W), lambda b, i, j: (b, j, 0)),
            pl.BlockSpec((None, tk, KV_W), lambda b, i, j: (b, j, 0)),
        ],
        out_specs=pl.BlockSpec((None, tq, ATTN_W), lambda b, i, j: (b, i, 0)),
        out_shape=jax.ShapeDtypeStruct((B, n, ATTN_W), BF16),
        scratch_shapes=[
            pltpu.VMEM((N_Q_HEADS, tq, 1), F32),
            pltpu.VMEM((N_Q_HEADS, tq, 1), F32),
            pltpu.VMEM((N_Q_HEADS, tq, HEAD_DIM), F32),
        ],
        compiler_params=_cparams(("parallel", "parallel", "arbitrary")),
        name="attn",
    )(q, k, v)


def _shift_rows(cur, prev_row, next_row, rows):
    tm = cur.shape[0]
    up = jnp.where(rows == 0, prev_row, pltpu.roll(cur, 1, 0))
    dn = jnp.where(rows == tm - 1, next_row, pltpu.roll(cur, tm - 1, 0))
    return up, dn


def _local_kernel(hy_ref, hyp_ref, hyn_ref, po_ref, pop_ref, pon_ref, gu_ref, gv_ref,
                  cw_ref, cb_ref, pw_ref, ps_ref, gnw_ref, gws_ref, gb_ref,
                  v_ref, x1_ref, x2_ref, pl_ref, gm_ref, *, n_seq):
    i = pl.program_id(1)
    last = pl.num_programs(1) - 1
    tm = hy_ref.shape[0]
    has_prev = (i > 0).astype(F32)
    has_next = (i < last).astype(F32)

    cur = hy_ref[...]
    rows = lax.broadcasted_iota(jnp.int32, cur.shape, 0)
    prev_row = hyp_ref[POOL_HALO - 1:POOL_HALO, :] * has_prev
    next_row = hyn_ref[0:1, :] * has_next
    up, dn = _shift_rows(cur, prev_row, next_row, rows)
    z = up * cw_ref[0:1, :] + cur * cw_ref[1:2, :] + dn * cw_ref[2:3, :] + cb_ref[...]
    v_ref[...] = z[:, :HYENA_W]
    x1_ref[...] = z[:, HYENA_W:2 * HYENA_W]
    x2_ref[...] = z[:, 2 * HYENA_W:]

    zc = po_ref[...]
    ext = jnp.concatenate([pop_ref[...] * has_prev, zc, pon_ref[...] * has_next], axis=0)
    te = tm + 2 * POOL_HALO
    lane = lax.broadcasted_iota(jnp.int32, zc.shape, 1)
    prow = lax.broadcasted_iota(jnp.int32, zc.shape, 0) + i * tm
    half = jnp.left_shift(1, lane // POOL_GROUP_W)
    acc = jnp.zeros(zc.shape, F32)
    for d in range(-POOL_HALO, POOL_HALO):
        sh = pltpu.roll(ext, (te - (POOL_HALO + d)) % te, 0)[:tm]
        inside = (d >= -half) & (d < half)
        acc = acc + jnp.where(inside, sh, 0.0)
    cnt = jnp.minimum(prow + half, n_seq) - jnp.maximum(prow - half, 0)
    dlt = acc / cnt.astype(F32) - zc
    pl_ref[...] = (_dot(dlt.astype(BF16), pw_ref[...]) * ps_ref[...]).astype(BF16)

    gv = gv_ref[...]
    mu = jnp.mean(gv, axis=-1, keepdims=True)
    ctr = gv - mu
    var = jnp.mean(ctr * ctr, axis=-1, keepdims=True)
    vn = (ctr * lax.rsqrt(var + RMS_EPS) * gnw_ref[...]).astype(BF16)
    group = lax.broadcasted_iota(jnp.int32, (GMLP_CHUNK, GMLP_W), 1) // GMLP_GROUP_W
    for c in range(tm // GMLP_CHUNK):
        rs = slice(c * GMLP_CHUNK, (c + 1) * GMLP_CHUNK)
        vc = vn[rs]
        mixed = gb_ref[...]
        for g in range(GMLP_GROUPS):
            mg = _dot(gws_ref[g], vc)
            mixed = mixed + jnp.where(group == g, mg, 0.0)
        gm_ref[rs, :] = (gu_ref[rs, :] * mixed).astype(BF16)


def _local_call(p, lw):
    B, n, _ = p.shape
    tm = _tile(n, 512, GMLP_CHUNK)
    nb8 = n // POOL_HALO
    r8 = tm // POOL_HALO
    HW = 3 * HYENA_W

    def cur(w, off):
        return pl.BlockSpec((None, tm, w), lambda b, i: (b, i, off // w))

    def prev(w, off):
        return pl.BlockSpec((None, POOL_HALO, w), lambda b, i: (b, jnp.maximum(i * r8 - 1, 0), off // w))

    def nxt(w, off):
        return pl.BlockSpec((None, POOL_HALO, w), lambda b, i: (b, jnp.minimum((i + 1) * r8, nb8 - 1), off // w))

    def full(shape):
        return pl.BlockSpec(shape, lambda b, i: (0,) * len(shape))

    def out(w):
        return pl.BlockSpec((None, tm, w), lambda b, i: (b, i, 0))

    return pl.pallas_call(
        functools.partial(_local_kernel, n_seq=n),
        grid=(B, n // tm),
        in_specs=[
            cur(HW, P_HY), prev(HW, P_HY), nxt(HW, P_HY),
            cur(POOL_W, P_POOL), prev(POOL_W, P_POOL), nxt(POOL_W, P_POOL),
            cur(GMLP_W, P_GMU), cur(GMLP_W, P_GMV),
            full((3, HW)), full((1, HW)), full((POOL_W, POOL_W)), full((1, POOL_W)),
            full((1, GMLP_W)), full((GMLP_GROUPS, GMLP_CHUNK, GMLP_CHUNK)), full((GMLP_CHUNK, GMLP_W)),
        ],
        out_specs=[out(HYENA_W), out(HYENA_W), out(HYENA_W), out(POOL_W), out(GMLP_W)],
        out_shape=[
            jax.ShapeDtypeStruct((B, n, HYENA_W), F32),
            jax.ShapeDtypeStruct((B, n, HYENA_W), F32),
            jax.ShapeDtypeStruct((B, n, HYENA_W), F32),
            jax.ShapeDtypeStruct((B, n, POOL_W), BF16),
            jax.ShapeDtypeStruct((B, n, GMLP_W), BF16),
        ],
        compiler_params=_cparams(("parallel", "parallel")),
        name="local",
    )(p, p, p, p, p, p, p, p, lw["conv_w"], lw["conv_b"], lw["pool_bd"], lw["pool_scale"],
      lw["gm_norm_w"], lw["gm_ws"], lw["gm_bias"])


def _filt_kernel(z_ref, t_ref, w1h_ref, w1l_ref, b1_ref, w2h_ref, w2l_ref, b2_ref, fr_ref,
                 w3h_ref, w3l_ref, dec_ref, k_ref, s_ref, *, n_seq):
    i = pl.program_id(0)
    tb = z_ref.shape[0]
    fr = fr_ref[...]
    hdn = jnp.sin(fr * (_dot3x(z_ref[...], w1h_ref[...], w1l_ref[...]) + b1_ref[...]))
    for u in range(HYENA_INNER):
        hdn = jnp.sin(fr * (_dot3x(hdn, w2h_ref[u], w2l_ref[u]) + b2_ref[u:u + 1, :]))
    t = t_ref[...]
    tt = jnp.concatenate([t] * (HYENA_ORDER * HYENA_W // LANES), axis=1)
    window = jnp.exp(-tt * jnp.abs(dec_ref[...]))
    k = _dot3x(hdn, w3h_ref[...], w3l_ref[...]) * window
    rows = lax.broadcasted_iota(jnp.int32, k.shape, 0) + i * tb
    k = jnp.where(rows == n_seq, 0.0, k)
    k_ref[...] = k

    @pl.when(i == 0)
    def _():
        s_ref[...] = jnp.zeros(s_ref.shape, F32)

    s_ref[...] += jnp.sum(jnp.abs(k), axis=0, keepdims=True)


def _filt_call(n, lw):
    N = 2 * n
    tb = _tile(n, 512)
    nb_half = n // tb
    pos = np.arange(N)
    pos = np.where(pos < n, pos, N - pos).astype(np.float64)
    t = pos / (n - 1)
    bands = np.linspace(1e-4, HYENA_BANDS - 1, HYENA_BANDS)
    ang = (2.0 * math.pi / n) * pos[:, None] * bands
    z = np.concatenate([t[:, None], np.cos(ang), np.sin(ang)], axis=-1)
    zp = np.zeros((N, LANES), np.float32)
    zp[:, :z.shape[1]] = z
    tl = np.broadcast_to(t[:, None], (N, LANES)).astype(np.float32)
    CW = HYENA_ORDER * HYENA_W

    def full(shape):
        return pl.BlockSpec(shape, lambda i: (0,) * len(shape))

    def side(rows_):
        return pl.BlockSpec((None, rows_, CW), lambda i: (i // nb_half, 0, 0))

    return pl.pallas_call(
        functools.partial(_filt_kernel, n_seq=n),
        grid=(N // tb,),
        in_specs=[
            pl.BlockSpec((tb, LANES), lambda i: (i, 0)),
            pl.BlockSpec((tb, LANES), lambda i: (i, 0)),
            full((LANES, HYENA_HIDDEN)), full((LANES, HYENA_HIDDEN)), full((1, HYENA_HIDDEN)),
            full((HYENA_INNER, HYENA_HIDDEN, HYENA_HIDDEN)), full((HYENA_INNER, HYENA_HIDDEN, HYENA_HIDDEN)),
            full((HYENA_INNER, HYENA_HIDDEN)), full((1, HYENA_HIDDEN)),
            side(HYENA_HIDDEN), side(HYENA_HIDDEN), side(1),
        ],
        out_specs=[pl.BlockSpec((tb, CW), lambda i: (i, 0)), pl.BlockSpec((1, CW), lambda i: (0, 0))],
        out_shape=[jax.ShapeDtypeStruct((N, CW), F32), jax.ShapeDtypeStruct((1, CW), F32)],
        compiler_params=_cparams(("arbitrary",)),
        name="filt",
    )(jnp.asarray(zp), jnp.asarray(tl), lw["hy_w1h"], lw["hy_w1l"], lw["hy_b1"], lw["hy_w2h"], lw["hy_w2l"],
      lw["hy_b2"], lw["hy_freq"], lw["hy_w3h"], lw["hy_w3l"], lw["hy_decay"])


def _fft_dims(N):
    e = int(round(math.log2(N)))
    assert 2 ** e == N
    N1 = 2 ** ((e + 1) // 2)
    return N1, N // N1


def _hilo(a):
    a = np.asarray(a, np.float64)
    hi = jnp.asarray(a, F32).astype(BF16)
    lo = (jnp.asarray(a, F32) - hi.astype(F32)).astype(BF16)
    return hi, lo


@functools.lru_cache(maxsize=None)
def _fft_consts_np(N):
    N1, N2 = _fft_dims(N)
    h = N1 // 2
    k1 = np.arange(N1)
    F1 = np.exp(-2j * np.pi * np.outer(k1, k1) / N1)
    k2 = np.arange(N2)
    F2 = np.exp(-2j * np.pi * np.outer(k2, k2) / N2)
    tw = np.exp(-2j * np.pi * np.outer(k1, k2) / N)
    g1 = np.zeros((2 * N1, N1))
    g1[0::2, :h] = F1.real[:, :h]
    g1[0::2, h:] = -F1.imag[:, :h]
    g1[1::2, :h] = F1.imag[:, :h]
    g1[1::2, h:] = F1.real[:, :h]
    g1f = np.zeros((2 * N1, N1))
    g1f[0::2] = F1.real
    g1f[1::2] = F1.imag
    g2 = np.block([[F2.real, -F2.imag], [F2.imag, F2.real]])
    g2c = np.block([[F2.real, F2.imag], [-F2.imag, F2.real]])
    g3 = np.zeros((N1, 2 * N1))
    g3[:h, 0::2] = F1.real[:h]
    g3[:h, 1::2] = F1.imag[:h]
    g3[h:, 0::2] = -F1.imag[:h]
    g3[h:, 1::2] = F1.real[:h]
    g3 /= N
    twt = np.stack([tw.real, tw.imag], axis=1)
    return dict(g1=g1, g1f=g1f, g2=g2, g2c=g2c, g3=g3, tw=twt)


def _fft_consts(N):
    c = _fft_consts_np(N)
    out = {k: _hilo(c[k]) for k in ("g1", "g1f", "g2", "g2c", "g3")}
    N1, N2 = _fft_dims(N)
    tw = jnp.asarray(c["tw"], F32)
    out["tw"] = jnp.broadcast_to(tw[..., None], (N1, 2, N2, LANES))
    return out


def _lmul_kernel(gh_ref, gl_ref, x_ref, o_ref):
    o_ref[...] = _dot3(gh_ref[...], gl_ref[...], x_ref[...])


def _lmul_call(g, x):
    gh, gl = g
    M, K = gh.shape
    G, K2, L = x.shape
    assert K == K2
    tc = _tile(L, 2048, LANES)
    return pl.pallas_call(
        _lmul_kernel,
        grid=(G, L // tc),
        in_specs=[
            pl.BlockSpec((M, K), lambda b, j: (0, 0)),
            pl.BlockSpec((M, K), lambda b, j: (0, 0)),
            pl.BlockSpec((None, K, tc), lambda b, j: (b, 0, j)),
        ],
        out_specs=pl.BlockSpec((None, M, tc), lambda b, j: (b, 0, j)),
        out_shape=jax.ShapeDtypeStruct((G, M, L), F32),
        compiler_params=_cparams(("parallel", "parallel")),
        name="dft_outer",
    )(gh, gl, x)


def _cmul(ar, ai, br, bi):
    return ar * br - ai * bi, ar * bi + ai * br


def _lane_tile(t, c):
    return jnp.concatenate([t] * (c // LANES), axis=-1)


def _spec_kernel(a_ref, tw_ref, g2h_ref, g2l_ref, s_ref, o_ref):
    kb, _, N2, C = a_ref.shape
    inv = 1.0 / s_ref[...]
    for k in range(kb):
        twr = _lane_tile(tw_ref[k, 0], C)
        twi = _lane_tile(tw_ref[k, 1], C)
        br, bi = _cmul(a_ref[k, 0], a_ref[k, 1], twr, twi)
        zz = _dot3(g2h_ref[...], g2l_ref[...], jnp.concatenate([br, bi], axis=0))
        o_ref[k, 0] = zz[:N2] * inv
        o_ref[k, 1] = zz[N2:] * inv


def _conv_kernel(a_ref, tw_ref, kf_ref, g2h_ref, g2l_ref, g2ch_ref, g2cl_ref, o_ref):
    kb, _, N2, C = a_ref.shape
    for k in range(kb):
        twr = _lane_tile(tw_ref[k, 0], C)
        twi = _lane_tile(tw_ref[k, 1], C)
        br, bi = _cmul(a_ref[k, 0], a_ref[k, 1], twr, twi)
        zz = _dot3(g2h_ref[...], g2l_ref[...], jnp.concatenate([br, bi], axis=0))
        yr, yi = _cmul(zz[:N2], zz[N2:], kf_ref[k, 0], kf_ref[k, 1])
        cc = _dot3(g2ch_ref[...], g2cl_ref[...], jnp.concatenate([yr, yi], axis=0))
        dr, di = _cmul(cc[:N2], cc[N2:], twr, -twi)
        o_ref[k, 0] = dr
        o_ref[k, 1] = di


def _spec_call(a5, fc, s):
    G, N1, _, N2, C = a5.shape
    kb = _tile(N1, 8, 1)
    blk = pl.BlockSpec((None, kb, 2, N2, C), lambda i, b: (b, i, 0, 0, 0))
    mat = pl.BlockSpec((2 * N2, 2 * N2), lambda i, b: (0, 0))
    return pl.pallas_call(
        _spec_kernel,
        grid=(N1 // kb, G),
        in_specs=[blk, pl.BlockSpec((kb, 2, N2, LANES), lambda i, b: (i, 0, 0, 0)), mat, mat,
                  pl.BlockSpec((None, 1, C), lambda i, b: (b, 0, 0))],
        out_specs=blk,
        out_shape=jax.ShapeDtypeStruct(a5.shape, F32),
        compiler_params=_cparams(("parallel", "parallel")),
        name="filt_spec",
    )(a5, fc["tw"], fc["g2"][0], fc["g2"][1], s)


def _conv_call(a5, kf, fc):
    G, N1, _, N2, C = a5.shape
    kb = _tile(N1, 8, 1)
    blk = pl.BlockSpec((None, kb, 2, N2, C), lambda i, b: (b, i, 0, 0, 0))
    mat = pl.BlockSpec((2 * N2, 2 * N2), lambda i, b: (0, 0))
    return pl.pallas_call(
        _conv_kernel,
        grid=(N1 // kb, G),
        in_specs=[blk, pl.BlockSpec((kb, 2, N2, LANES), lambda i, b: (i, 0, 0, 0)),
                  pl.BlockSpec((kb, 2, N2, C), lambda i, b: (i, 0, 0, 0)), mat, mat, mat, mat],
        out_specs=blk,
        out_shape=jax.ShapeDtypeStruct(a5.shape, F32),
        compiler_params=_cparams(("parallel", "parallel")),
        name="spec_conv",
    )(a5, fc["tw"], kf, fc["g2"][0], fc["g2"][1], fc["g2c"][0], fc["g2c"][1])


def _filter_spectra(n, lw, fc):
    N = 2 * n
    N1, N2 = _fft_dims(N)
    k, s = _filt_call(n, lw)
    k = k.reshape(N1, N2, HYENA_ORDER, HYENA_W).transpose(2, 0, 1, 3).reshape(HYENA_ORDER, N1, N2 * HYENA_W)
    a = _lmul_call(fc["g1f"], k)
    a5 = a.reshape(HYENA_ORDER, N1, 2, N2, HYENA_W)
    return _spec_call(a5, fc, s.reshape(HYENA_ORDER, 1, HYENA_W))


def _long_conv(u, kf, fc):
    B, n, W = u.shape
    assert B % 2 == 0
    N1, N2 = _fft_dims(2 * n)
    x = u.reshape(B // 2, N1, N2 * W)
    a = _lmul_call(fc["g1"], x)
    d = _conv_call(a.reshape(B // 2, N1, 2, N2, W), kf, fc)
    y = _lmul_call(fc["g3"], d.reshape(B // 2, 2 * N1, N2 * W))
    return y.reshape(B, n, W)


def _gate_kernel(c_ref, u_ref, g_ref, sk_ref, o_ref):
    o_ref[...] = g_ref[...] * (c_ref[...] + u_ref[...] * sk_ref[...])


def _gate_call(conv, u, gate, skip):
    B, n, W = u.shape
    tm = _tile(n, 1024)
    blk = pl.BlockSpec((None, tm, W), lambda b, i: (b, i, 0))
    return pl.pallas_call(
        _gate_kernel,
        grid=(B, n // tm),
        in_specs=[blk, blk, blk, pl.BlockSpec((1, W), lambda b, i: (0, 0))],
        out_specs=blk,
        out_shape=jax.ShapeDtypeStruct((B, n, W), F32),
        compiler_params=_cparams(("parallel", "parallel")),
        name="hy_gate",
    )(conv, u, gate, skip)


def _hyena(v, x1, x2, kf, lw, fc):
    y = v
    for o, gate in enumerate((x1, x2)):
        y = _gate_call(_long_conv(y, kf[o], fc), y, gate, lw["hy_skip"][o:o + 1])
    return y


def _merge_kernel(at_ref, hy_ref, po_ref, gm_ref, g0_ref, g1_ref, g2_ref, g3_ref, x_ref, gr_ref,
                  wa_ref, wh_ref, wp_ref, wg_ref, wo_ref, o_ref):
    merged = _sigmoid(g0_ref[...]) * _dot(at_ref[...], wa_ref[...])
    merged = merged + _sigmoid(g1_ref[...]) * _dot(hy_ref[...].astype(BF16), wh_ref[...])
    merged = merged + _sigmoid(g2_ref[...]) * _dot(po_ref[...], wp_ref[...])
    merged = merged + _sigmoid(g3_ref[...]) * _dot(gm_ref[...], wg_ref[...])
    o_ref[...] = x_ref[...] + gr_ref[...] * _dot(merged.astype(BF16), wo_ref[...])


def _merge_call(attn, hy, po, gm, p, x, gate_res, lw):
    B, n, D = x.shape
    tm = _tile(n, 512)

    def rows(w, col=0):
        return pl.BlockSpec((None, tm, w), lambda b, i: (b, i, col))

    def full(shape):
        return pl.BlockSpec(shape, lambda b, i: (0,) * len(shape))

    return pl.pallas_call(
        _merge_kernel,
        grid=(B, n // tm),
        in_specs=[
            rows(ATTN_W), rows(HYENA_W), rows(POOL_W), rows(GMLP_W),
            rows(D, 0), rows(D, 1), rows(D, 2), rows(D, 3), rows(D),
            pl.BlockSpec((None, 1, D), lambda b, i: (b, 0, 0)),
            full((ATTN_W, D)), full((HYENA_W, D)), full((POOL_W, D)), full((GMLP_W, D)), full((D, D)),
        ],
        out_specs=rows(D),
        out_shape=jax.ShapeDtypeStruct((B, n, D), F32),
        compiler_params=_cparams(("parallel", "parallel")),
        name="merge",
    )(attn, hy, po, gm, p, p, p, p, x, gate_res, lw["w_br_attn"], lw["w_br_hyena"], lw["w_br_pool"],
      lw["w_br_gmlp"], lw["w_out"])


def _ffn_kernel(x_ref, sh_ref, sc_ref, nw_ref, gr_ref, wg_ref, wu_ref, wd_ref, o_ref, h_ref, acc_ref):
    j = pl.program_id(2)

    @pl.when(j == 0)
    def _():
        h_ref[...] = _modulated_norm(x_ref[...], nw_ref[...], sc_ref[...], sh_ref[...]).astype(BF16)
        acc_ref[...] = jnp.zeros(acc_ref.shape, F32)

    h = h_ref[...]
    a = _silu(_dot(h, wg_ref[...])) * _dot(h, wu_ref[...])
    acc_ref[...] += _dot(a.astype(BF16), wd_ref[...])

    @pl.when(j == pl.num_programs(2) - 1)
    def _():
        o_ref[...] = x_ref[...] + gr_ref[...] * acc_ref[...]


def _ffn_call(x, shift, scale, norm_w, gate_res, wg, wu, wd):
    B, n, D = x.shape
    H = wg.shape[1]
    tm = _tile(n, 1024)
    th = 256
    vec = pl.BlockSpec((None, 1, D), lambda b, i, j: (b, 0, 0))
    return pl.pallas_call(
        _ffn_kernel,
        grid=(B, n // tm, H // th),
        in_specs=[
            pl.BlockSpec((None, tm, D), lambda b, i, j: (b, i, 0)),
            vec, vec, pl.BlockSpec((1, D), lambda b, i, j: (0, 0)), vec,
            pl.BlockSpec((D, th), lambda b, i, j: (0, j)),
            pl.BlockSpec((D, th), lambda b, i, j: (0, j)),
            pl.BlockSpec((th, D), lambda b, i, j: (j, 0)),
        ],
        out_specs=pl.BlockSpec((None, tm, D), lambda b, i, j: (b, i, 0)),
        out_shape=jax.ShapeDtypeStruct((B, n, D), F32),
        scratch_shapes=[pltpu.VMEM((tm, D), BF16), pltpu.VMEM((tm, D), F32)],
        compiler_params=_cparams(("parallel", "parallel", "arbitrary")),
        name="ffn",
    )(x, shift, scale, norm_w.reshape(1, D), gate_res, wg, wu, wd)


def _rms_kernel(x_ref, w_ref, o_ref):
    x = x_ref[...]
    ms = jnp.mean(x * x, axis=-1, keepdims=True)
    o_ref[...] = x * lax.rsqrt(ms + RMS_EPS) * w_ref[...]


def _rms_call(x, w):
    B, n, D = x.shape
    tm = _tile(n, 1024)
    blk = pl.BlockSpec((None, tm, D), lambda b, i: (b, i, 0))
    return pl.pallas_call(
        _rms_kernel,
        grid=(B, n // tm),
        in_specs=[blk, pl.BlockSpec((1, D), lambda b, i: (0, 0))],
        out_specs=blk,
        out_shape=jax.ShapeDtypeStruct((B, n, D), F32),
        compiler_params=_cparams(("parallel", "parallel")),
        name="final_norm",
    )(x, w.reshape(1, D))


def _rope_tables(n):
    tok = np.arange(n)
    axis_dim = HEAD_DIM // 2
    inv_freq = ROPE_THETA ** (-np.arange(0, axis_dim, 2, dtype=np.float64) / axis_dim)
    inv_freq = inv_freq.astype(np.float32).astype(np.float64)
    ang = np.concatenate([(tok // GRID_W)[:, None] * inv_freq, (tok % GRID_W)[:, None] * inv_freq], axis=-1)
    ang = ang.astype(np.float32)
    cos = np.repeat(np.cos(ang), 2, axis=-1)
    sin = np.repeat(np.sin(ang), 2, axis=-1)
    sin[:, 0::2] *= -1.0
    reps = LANES // HEAD_DIM
    return jnp.asarray(np.tile(cos, (1, reps)), F32), jnp.asarray(np.tile(sin, (1, reps)), F32)


def _identity_tables(n):
    return jnp.ones((n, LANES), F32), jnp.zeros((n, LANES), F32)


def _layer_weights(l, a):
    w_in = a["w_in"][l]
    w_perm = jnp.concatenate([
        w_in[:, GATE_OFF:], w_in[:, Q_OFF:K_OFF], w_in[:, HY_OFF:POOL_OFF], w_in[:, K_OFF:HY_OFF],
        w_in[:, POOL_OFF:GM_OFF], w_in[:, GM_OFF:GATE_OFF]], axis=1).astype(BF16)
    w3 = a["hy_w3"][l].reshape(HYENA_HIDDEN, HYENA_ORDER, 2, HYENA_W).transpose(2, 0, 1, 3)
    w3 = w3.reshape(2, HYENA_HIDDEN, HYENA_ORDER * HYENA_W)
    dec = a["hy_decay"][l].reshape(HYENA_ORDER, 2, HYENA_W).transpose(1, 0, 2).reshape(2, 1, HYENA_ORDER * HYENA_W)
    w1 = jnp.zeros((LANES, HYENA_HIDDEN), F32).at[:a["hy_w1"].shape[1]].set(a["hy_w1"][l])
    w1h, w1l = _split(w1)
    w2h, w2l = _split(a["hy_w2"][l])
    w3h, w3l = _split(w3)
    eye = jnp.eye(len(POOL_WINDOWS), dtype=F32)
    pool_bd = jnp.einsum("gh,gcd->gchd", eye, a["pool_w"][l]).reshape(POOL_W, POOL_W).astype(BF16)
    gm_bias = jnp.repeat(a["gm_bs"][l].T, GMLP_GROUP_W, axis=1)
    return dict(
        w_in=w_perm,
        q_norm_w=jnp.tile(a["q_norm_w"][l], N_Q_HEADS).reshape(1, ATTN_W),
        k_norm_w=jnp.tile(a["k_norm_w"][l], N_KV_HEADS).reshape(1, KV_W),
        conv_w=a["hy_conv_w"][l], conv_b=a["hy_conv_b"][l].reshape(1, -1),
        hy_w1h=w1h, hy_w1l=w1l, hy_b1=a["hy_b1"][l].reshape(1, -1), hy_w2h=w2h, hy_w2l=w2l, hy_b2=a["hy_b2"][l],
        hy_w3h=w3h, hy_w3l=w3l, hy_freq=a["hy_freq"][l].reshape(1, -1), hy_decay=dec, hy_skip=a["hy_skip"][l],
        pool_bd=pool_bd, pool_scale=a["pool_scale"][l].reshape(1, -1),
        gm_norm_w=a["gm_norm_w"][l].reshape(1, -1), gm_ws=a["gm_ws"][l].astype(BF16), gm_bias=gm_bias,
        w_br_attn=a["w_br_attn"][l].astype(BF16), w_br_hyena=a["w_br_hyena"][l].astype(BF16),
        w_br_pool=a["w_br_pool"][l].astype(BF16), w_br_gmlp=a["w_br_gmlp"][l].astype(BF16),
        w_out=a["w_out"][l].astype(BF16),
        ffn_w_gate=a["ffn_w_gate"][l].astype(BF16), ffn_w_up=a["ffn_w_up"][l].astype(BF16),
        ffn_w_down=a["ffn_w_down"][l].astype(BF16),
        norm1_w=a["norm1_w"][l], norm2_w=a["norm2_w"][l],
    )


Q_SCALE = (HEAD_DIM ** -0.5) * math.log2(math.e)


def _mixer_block(x, mods, lw, rope, k_ctx, v_ctx, kf, fc):
    sh1, s1, g1, sh2, s2, g2 = mods
    p = _inproj_call(x, sh1, s1, lw["norm1_w"], lw["w_in"], 1280)
    q, k, v = _qkprep_call(p, rope[0], rope[1], lw["q_norm_w"], lw["k_norm_w"], Q_SCALE)
    if k_ctx is not None:
        k_all = jnp.concatenate([k_ctx, k], axis=1)
        v_all = jnp.concatenate([v_ctx, v], axis=1)
    else:
        k_all, v_all = k, v
    attn = _attn_call(q, k_all, v_all)
    hv, hx1, hx2, po, gm = _local_call(p, lw)
    hy = _hyena(hv, hx1, hx2, kf, lw, fc)
    x = _merge_call(attn, hy, po, gm, p, x, g1, lw)
    x = _ffn_call(x, sh2, s2, lw["norm2_w"], g2, lw["ffn_w_gate"], lw["ffn_w_up"], lw["ffn_w_down"])
    return x, k, v


def kernel(x, c, ctx, c_ctx, w_mod, b_mod, norm1_w, norm2_w, w_in, q_norm_w, k_norm_w, hy_conv_w, hy_conv_b,
           hy_w1, hy_b1, hy_w2, hy_b2, hy_w3, hy_freq, hy_decay, hy_skip, pool_w, pool_scale, gm_norm_w, gm_ws,
           gm_bs, w_br_attn, w_br_hyena, w_br_pool, w_br_gmlp, w_out, ffn_w_gate, ffn_w_up, ffn_w_down,
           final_norm_w):
    a = dict(w_in=w_in, q_norm_w=q_norm_w, k_norm_w=k_norm_w, hy_conv_w=hy_conv_w, hy_conv_b=hy_conv_b,
             hy_w1=hy_w1, hy_b1=hy_b1, hy_w2=hy_w2, hy_b2=hy_b2, hy_w3=hy_w3, hy_freq=hy_freq, hy_decay=hy_decay,
             hy_skip=hy_skip, pool_w=pool_w, pool_scale=pool_scale, gm_norm_w=gm_norm_w, gm_ws=gm_ws, gm_bs=gm_bs,
             w_br_attn=w_br_attn, w_br_hyena=w_br_hyena, w_br_pool=w_br_pool, w_br_gmlp=w_br_gmlp, w_out=w_out,
             ffn_w_gate=ffn_w_gate, ffn_w_up=ffn_w_up, ffn_w_down=ffn_w_down, norm1_w=norm1_w, norm2_w=norm2_w)
    B, n, D = x.shape
    nc = ctx.shape[1]
    depth = w_mod.shape[0]

    R = ((B + 1 + 15) // 16) * 16
    act = jnp.zeros((R, D), F32).at[:B].set(c).at[B].set(c_ctx)
    mod = _mod_call(act, w_mod, b_mod)

    rope = _rope_tables(n)
    rope_ctx = _identity_tables(nc)
    fc = _fft_consts(2 * n)
    fc_ctx = _fft_consts(2 * nc)

    xc = ctx
    for l in range(depth):
        lw = _layer_weights(l, a)
        mods = [m.reshape(B, 1, D) for m in jnp.split(mod[l, :B], 6, axis=-1)]
        cmods = [jnp.broadcast_to(m.reshape(1, 1, D), (B, 1, D)) for m in jnp.split(mod[l, B], 6, axis=-1)]
        if l < depth - 1:
            kf_ctx = _filter_spectra(nc, lw, fc_ctx)
            xc, k_c, v_c = _mixer_block(xc, cmods, lw, rope_ctx, None, None, kf_ctx, fc_ctx)
        else:
            pc = _inproj_call(xc, cmods[0], cmods[1], lw["norm1_w"], lw["w_in"][:, P_KV:P_KV + 2 * KV_W], 2 * KV_W)
            k_c, v_c = _kvprep_ctx(pc, lw)
        kf = _filter_spectra(n, lw, fc)
        x, _, _ = _mixer_block(x, mods, lw, rope, k_c, v_c, kf, fc)
    return _rms_call(x, final_norm_w)


def _kvprep_only_kernel(kv_ref, kw_ref, bd_ref, ko_ref, vo_ref):
    t = kv_ref[:, :KV_W]
    sq_hi, sq_lo = _split(t * t)
    ms = _dot(sq_hi, bd_ref[...]) + _dot(sq_lo, bd_ref[...])
    ko_ref[...] = (t * lax.rsqrt(ms + RMS_EPS) * kw_ref[...]).astype(BF16)
    vo_ref[...] = kv_ref[:, KV_W:].astype(BF16)


def _kvprep_ctx(pc, lw):
    B, nc, _ = pc.shape
    bd = np.kron(np.eye(LANES // HEAD_DIM), np.full((HEAD_DIM, HEAD_DIM), 1.0 / HEAD_DIM))
    blk = pl.BlockSpec((None, nc, KV_W), lambda b: (b, 0, 0))
    return pl.pallas_call(
        _kvprep_only_kernel,
        grid=(B,),
        in_specs=[pl.BlockSpec((None, nc, 2 * KV_W), lambda b: (b, 0, 0)),
                  pl.BlockSpec((1, KV_W), lambda b: (0, 0)), pl.BlockSpec((LANES, LANES), lambda b: (0, 0))],
        out_specs=[blk, blk],
        out_shape=[jax.ShapeDtypeStruct((B, nc, KV_W), BF16)] * 2,
        compiler_params=_cparams(("parallel",)),
        name="kvprep_ctx",
    )(pc, lw["k_norm_w"], jnp.asarray(bd, BF16))
```

```python
import functools
import math

import numpy as np
import jax
import jax.numpy as jnp
from jax import lax
from jax.experimental import pallas as pl
from jax.experimental.pallas import tpu as pltpu

F32 = jnp.float32
BF16 = jnp.bfloat16

D_MODEL = 1024
GRID_W = 64
RMS_EPS = 1e-6
HEAD_DIM = 64
N_Q_HEADS = 8
N_KV_HEADS = 2
ATTN_W = N_Q_HEADS * HEAD_DIM
KV_W = N_KV_HEADS * HEAD_DIM
ROPE_THETA = 10000.0
HYENA_W = 256
HYENA_ORDER = 2
HYENA_BANDS = 16
HYENA_HIDDEN = 64
HYENA_INNER = 2
POOL_W = 256
POOL_WINDOWS = (2, 4, 8, 16)
POOL_GROUP_W = POOL_W // len(POOL_WINDOWS)
POOL_HALO = 8
GMLP_W = 256
GMLP_CHUNK = 128
GMLP_GROUPS = 4
GMLP_GROUP_W = GMLP_W // GMLP_GROUPS
N_BRANCH = 4
FFN_HIDDEN = ((8 * D_MODEL // 3 + 255) // 256) * 256

Q_OFF = 0
K_OFF = Q_OFF + ATTN_W
V_OFF = K_OFF + KV_W
HY_OFF = V_OFF + KV_W
POOL_OFF = HY_OFF + 3 * HYENA_W
GM_OFF = POOL_OFF + POOL_W
GATE_OFF = GM_OFF + 2 * GMLP_W
IN_W = GATE_OFF + N_BRANCH * D_MODEL

P_GATE = 0
P_Q = P_GATE + N_BRANCH * D_MODEL
P_HY = P_Q + ATTN_W
P_KV = P_HY + 3 * HYENA_W
P_POOL = P_KV + 2 * KV_W
P_GMU = P_POOL + POOL_W
P_GMV = P_GMU + GMLP_W

LANES = 128
VMEM_LIMIT = 52 * 1024 * 1024


def _cparams(sem):
    return pltpu.CompilerParams(dimension_semantics=sem, vmem_limit_bytes=VMEM_LIMIT)


def _tile(n, cap, mult=8):
    best = None
    for t in range(mult, min(n, cap) + 1, mult):
        if n % t == 0:
            best = t
    assert best is not None, (n, cap, mult)
    return best


def _split(x):
    hi = x.astype(BF16)
    lo = (x - hi.astype(F32)).astype(BF16)
    return hi, lo


def _dot(a, b):
    return jnp.dot(a, b, preferred_element_type=F32)


def _dot3(a_hi, a_lo, x):
    x_hi, x_lo = _split(x)
    return _dot(a_hi, x_hi) + (_dot(a_lo, x_hi) + _dot(a_hi, x_lo))


def _dot3x(x, b_hi, b_lo):
    x_hi, x_lo = _split(x)
    return _dot(x_hi, b_hi) + (_dot(x_lo, b_hi) + _dot(x_hi, b_lo))


def _sigmoid(x):
    return 1.0 / (1.0 + jnp.exp(-x))


def _silu(x):
    return x * _sigmoid(x)


def _modulated_norm(x, nw, scale, shift):
    ms = jnp.mean(x * x, axis=-1, keepdims=True)
    y = x * lax.rsqrt(ms + RMS_EPS) * nw
    return y * (1.0 + scale) + shift


def _mod_kernel(a_ref, w_ref, b_ref, o_ref):
    a = _silu(a_ref[...]).astype(BF16)
    o_ref[...] = _dot(a, w_ref[...].astype(BF16)) + b_ref[...]


def _mod_call(act, w_mod, b_mod):
    L, D, W6 = w_mod.shape
    R = act.shape[0]
    tn = 1024
    return pl.pallas_call(
        _mod_kernel,
        grid=(L, W6 // tn),
        in_specs=[
            pl.BlockSpec((R, D), lambda l, j: (0, 0)),
            pl.BlockSpec((None, D, tn), lambda l, j: (l, 0, j)),
            pl.BlockSpec((None, 1, tn), lambda l, j: (l, 0, j)),
        ],
        out_specs=pl.BlockSpec((None, R, tn), lambda l, j: (l, 0, j)),
        out_shape=jax.ShapeDtypeStruct((L, R, W6), F32),
        compiler_params=_cparams(("arbitrary", "arbitrary")),
        name="mod",
    )(act, w_mod, b_mod.reshape(L, 1, W6))


def _inproj_kernel(x_ref, sh_ref, sc_ref, nw_ref, w_ref, o_ref, h_ref):
    @pl.when(pl.program_id(2) == 0)
    def _():
        h_ref[...] = _modulated_norm(x_ref[...], nw_ref[...], sc_ref[...], sh_ref[...]).astype(BF16)

    o_ref[...] = _dot(h_ref[...], w_ref[...])


def _inproj_call(x, shift, scale, norm_w, w, tn):
    B, n, D = x.shape
    Nw = w.shape[1]
    tm = _tile(n, 1024)
    return pl.pallas_call(
        _inproj_kernel,
        grid=(B, n // tm, Nw // tn),
        in_specs=[
            pl.BlockSpec((None, tm, D), lambda b, i, j: (b, i, 0)),
            pl.BlockSpec((None, 1, D), lambda b, i, j: (b, 0, 0)),
            pl.BlockSpec((None, 1, D), lambda b, i, j: (b, 0, 0)),
            pl.BlockSpec((1, D), lambda b, i, j: (0, 0)),
            pl.BlockSpec((D, tn), lambda b, i, j: (0, j)),
        ],
        out_specs=pl.BlockSpec((None, tm, tn), lambda b, i, j: (b, i, j)),
        out_shape=jax.ShapeDtypeStruct((B, n, Nw), F32),
        scratch_shapes=[pltpu.VMEM((tm, D), BF16)],
        compiler_params=_cparams(("parallel", "parallel", "arbitrary")),
        name="inproj",
    )(x, shift, scale, norm_w.reshape(1, D), w)


def _head_norm_rope(t, w, cs, sn, bd_ref):
    sq_hi, sq_lo = _split(t * t)
    ms = _dot(sq_hi, bd_ref[...]) + _dot(sq_lo, bd_ref[...])
    y = t * lax.rsqrt(ms + RMS_EPS) * w
    lane = lax.broadcasted_iota(jnp.int32, y.shape, 1)
    swapped = jnp.where(lane % 2 == 0, pltpu.roll(y, LANES - 1, 1), pltpu.roll(y, 1, 1))
    return y * cs + swapped * sn


V_ROWS = 80
Q_PER_KV = N_Q_HEADS // N_KV_HEADS


def _value_rows(vt_g):
    tail = lax.broadcasted_iota(jnp.int32, (V_ROWS - HEAD_DIM, vt_g.shape[1]), 0)
    return jnp.concatenate([vt_g, jnp.where(tail == 0, 1.0, 0.0)], axis=0).astype(BF16)


def _qkprep_kernel(q_ref, kv_ref, cs_ref, sn_ref, qw_ref, kw_ref, bd_ref, qo_ref, ko_ref, vo_ref, *, q_scale):
    cs = cs_ref[...]
    sn = sn_ref[...]
    tm = q_ref.shape[0]
    zeros = jnp.zeros((HEAD_DIM, tm), F32)
    for c in range(ATTN_W // LANES):
        sl = slice(c * LANES, (c + 1) * LANES)
        r = _head_norm_rope(q_ref[:, sl], qw_ref[:, sl], cs, sn, bd_ref) * q_scale
        rt = r.T
        for half in range(LANES // HEAD_DIM):
            h = c * (LANES // HEAD_DIM) + half
            blk = rt[half * HEAD_DIM:(half + 1) * HEAD_DIM]
            parts = [blk if g == h // Q_PER_KV else zeros for g in range(N_KV_HEADS)]
            qo_ref[h] = jnp.concatenate(parts, axis=0).astype(BF16)
    ko_ref[...] = _head_norm_rope(kv_ref[:, :KV_W], kw_ref[...], cs, sn, bd_ref).astype(BF16)
    vt = kv_ref[:, KV_W:].T
    for g in range(N_KV_HEADS):
        vo_ref[g] = _value_rows(vt[g * HEAD_DIM:(g + 1) * HEAD_DIM])


def _qkprep_call(p, cs, sn, qw, kw, q_scale):
    B, n, _ = p.shape
    tm = _tile(n, 512, LANES)
    bd = np.kron(np.eye(LANES // HEAD_DIM), np.full((HEAD_DIM, HEAD_DIM), 1.0 / HEAD_DIM))
    bd = jnp.asarray(bd, BF16)
    return pl.pallas_call(
        functools.partial(_qkprep_kernel, q_scale=q_scale),
        grid=(n // tm, B),
        in_specs=[
            pl.BlockSpec((None, tm, ATTN_W), lambda i, b: (b, i, P_Q // ATTN_W)),
            pl.BlockSpec((None, tm, 2 * KV_W), lambda i, b: (b, i, P_KV // (2 * KV_W))),
            pl.BlockSpec((tm, LANES), lambda i, b: (i, 0)),
            pl.BlockSpec((tm, LANES), lambda i, b: (i, 0)),
            pl.BlockSpec((1, ATTN_W), lambda i, b: (0, 0)),
            pl.BlockSpec((1, KV_W), lambda i, b: (0, 0)),
            pl.BlockSpec((LANES, LANES), lambda i, b: (0, 0)),
        ],
        out_specs=[
            pl.BlockSpec((None, N_Q_HEADS, KV_W, tm), lambda i, b: (b, 0, 0, i)),
            pl.BlockSpec((None, tm, KV_W), lambda i, b: (b, i, 0)),
            pl.BlockSpec((None, N_KV_HEADS, V_ROWS, tm), lambda i, b: (b, 0, 0, i)),
        ],
        out_shape=[
            jax.ShapeDtypeStruct((B, N_Q_HEADS, KV_W, n), BF16),
            jax.ShapeDtypeStruct((B, n, KV_W), BF16),
            jax.ShapeDtypeStruct((B, N_KV_HEADS, V_ROWS, n), BF16),
        ],
        compiler_params=_cparams(("parallel", "parallel")),
        name="qkprep",
    )(p, p, cs, sn, qw, kw, bd)


def _attn_kernel(qt_ref, k_ref, vt_ref, o_ref, m_ref, acc_ref, *, tk):
    nk = vt_ref.shape[0]
    m_ref[...] = jnp.full(m_ref.shape, -jnp.inf, F32)
    acc_ref[...] = jnp.zeros(acc_ref.shape, F32)

    def chunk(j, carry):
        kb = k_ref[pl.ds(pl.multiple_of(j * tk, tk), tk), :]
        s_next = _dot(kb, qt_ref[0])
        for h in range(N_Q_HEADS):
            s = s_next
            if h + 1 < N_Q_HEADS:
                s_next = _dot(kb, qt_ref[h + 1])
            m_old = m_ref[h]
            m_new = jnp.maximum(m_old, jnp.max(s, axis=0, keepdims=True))
            p = jnp.exp2(s - m_new).astype(BF16)
            acc_ref[h] = jnp.exp2(m_old - m_new) * acc_ref[h] + _dot(vt_ref[j, h // Q_PER_KV], p)
            m_ref[h] = m_new
        return carry

    lax.fori_loop(0, nk, chunk, 0)

    for c in range(ATTN_W // LANES):
        heads = []
        for h in range(c * (LANES // HEAD_DIM), (c + 1) * (LANES // HEAD_DIM)):
            a = acc_ref[h]
            heads.append(a[:HEAD_DIM] / a[HEAD_DIM:HEAD_DIM + 1])
        o_ref[:, c * LANES:(c + 1) * LANES] = jnp.concatenate(heads, axis=0).T.astype(BF16)


def _attn_call(qt, k, vt):
    B, _, _, n = qt.shape
    Lk = k.shape[1]
    tq = _tile(n, 512, LANES)
    tk = _tile(Lk, 768, LANES)
    nk = Lk // tk
    vt = vt.reshape(B, N_KV_HEADS, V_ROWS, nk, tk).transpose(0, 3, 1, 2, 4)
    return pl.pallas_call(
        functools.partial(_attn_kernel, tk=tk),
        grid=(B, n // tq),
        in_specs=[
            pl.BlockSpec((None, N_Q_HEADS, KV_W, tq), lambda b, i: (b, 0, 0, i)),
            pl.BlockSpec((None, Lk, KV_W), lambda b, i: (b, 0, 0)),
            pl.BlockSpec((None, nk, N_KV_HEADS, V_ROWS, tk), lambda b, i: (b, 0, 0, 0, 0)),
        ],
        out_specs=pl.BlockSpec((None, tq, ATTN_W), lambda b, i: (b, i, 0)),
        out_shape=jax.ShapeDtypeStruct((B, n, ATTN_W), BF16),
        scratch_shapes=[
            pltpu.VMEM((N_Q_HEADS, 1, tq), F32),
            pltpu.VMEM((N_Q_HEADS, V_ROWS, tq), F32),
        ],
        compiler_params=_cparams(("parallel", "parallel")),
        name="attn",
    )(qt, k, vt)


def _shift_rows(cur, prev_row, next_row, rows):
    tm = cur.shape[0]
    up = jnp.where(rows == 0, prev_row, pltpu.roll(cur, 1, 0))
    dn = jnp.where(rows == tm - 1, next_row, pltpu.roll(cur, tm - 1, 0))
    return up, dn


def _local_kernel(hy_ref, hyp_ref, hyn_ref, po_ref, pop_ref, pon_ref, gu_ref, gv_ref,
                  cw_ref, cb_ref, pw_ref, ps_ref, gnw_ref, gws_ref, gb_ref,
                  v_ref, x1_ref, x2_ref, pl_ref, gm_ref, *, n_seq):
    i = pl.program_id(1)
    last = pl.num_programs(1) - 1
    tm = hy_ref.shape[0]
    has_prev = (i > 0).astype(F32)
    has_next = (i < last).astype(F32)

    cur = hy_ref[...]
    rows = lax.broadcasted_iota(jnp.int32, cur.shape, 0)
    prev_row = hyp_ref[POOL_HALO - 1:POOL_HALO, :] * has_prev
    next_row = hyn_ref[0:1, :] * has_next
    up, dn = _shift_rows(cur, prev_row, next_row, rows)
    z = up * cw_ref[0:1, :] + cur * cw_ref[1:2, :] + dn * cw_ref[2:3, :] + cb_ref[...]
    v_ref[...] = z[:, :HYENA_W]
    x1_ref[...] = z[:, HYENA_W:2 * HYENA_W]
    x2_ref[...] = z[:, 2 * HYENA_W:]

    zc = po_ref[...]
    ext = jnp.concatenate([pop_ref[...] * has_prev, zc, pon_ref[...] * has_next], axis=0)
    te = tm + 2 * POOL_HALO
    lane = lax.broadcasted_iota(jnp.int32, zc.shape, 1)
    prow = lax.broadcasted_iota(jnp.int32, zc.shape, 0) + i * tm
    half = jnp.left_shift(1, lane // POOL_GROUP_W)
    acc = jnp.zeros(zc.shape, F32)
    for d in range(-POOL_HALO, POOL_HALO):
        sh = pltpu.roll(ext, (te - (POOL_HALO + d)) % te, 0)[:tm]
        inside = (d >= -half) & (d < half)
        acc = acc + jnp.where(inside, sh, 0.0)
    cnt = jnp.minimum(prow + half, n_seq) - jnp.maximum(prow - half, 0)
    dlt = acc / cnt.astype(F32) - zc
    pl_ref[...] = (_dot(dlt.astype(BF16), pw_ref[...]) * ps_ref[...]).astype(BF16)

    gv = gv_ref[...]
    mu = jnp.mean(gv, axis=-1, keepdims=True)
    ctr = gv - mu
    var = jnp.mean(ctr * ctr, axis=-1, keepdims=True)
    vn = (ctr * lax.rsqrt(var + RMS_EPS) * gnw_ref[...]).astype(BF16)
    group = lax.broadcasted_iota(jnp.int32, (GMLP_CHUNK, GMLP_W), 1) // GMLP_GROUP_W
    for c in range(tm // GMLP_CHUNK):
        rs = slice(c * GMLP_CHUNK, (c + 1) * GMLP_CHUNK)
        vc = vn[rs]
        mixed = gb_ref[...]
        for g in range(GMLP_GROUPS):
            mg = _dot(gws_ref[g], vc)
            mixed = mixed + jnp.where(group == g, mg, 0.0)
        gm_ref[rs, :] = (gu_ref[rs, :] * mixed).astype(BF16)


def _local_call(p, lw):
    B, n, _ = p.shape
    tm = _tile(n, 512, GMLP_CHUNK)
    nb8 = n // POOL_HALO
    r8 = tm // POOL_HALO
    HW = 3 * HYENA_W

    def cur(w, off):
        return pl.BlockSpec((None, tm, w), lambda b, i: (b, i, off // w))

    def prev(w, off):
        return pl.BlockSpec((None, POOL_HALO, w), lambda b, i: (b, jnp.maximum(i * r8 - 1, 0), off // w))

    def nxt(w, off):
        return pl.BlockSpec((None, POOL_HALO, w), lambda b, i: (b, jnp.minimum((i + 1) * r8, nb8 - 1), off // w))

    def full(shape):
        return pl.BlockSpec(shape, lambda b, i: (0,) * len(shape))

    def out(w):
        return pl.BlockSpec((None, tm, w), lambda b, i: (b, i, 0))

    return pl.pallas_call(
        functools.partial(_local_kernel, n_seq=n),
        grid=(B, n // tm),
        in_specs=[
            cur(HW, P_HY), prev(HW, P_HY), nxt(HW, P_HY),
            cur(POOL_W, P_POOL), prev(POOL_W, P_POOL), nxt(POOL_W, P_POOL),
            cur(GMLP_W, P_GMU), cur(GMLP_W, P_GMV),
            full((3, HW)), full((1, HW)), full((POOL_W, POOL_W)), full((1, POOL_W)),
            full((1, GMLP_W)), full((GMLP_GROUPS, GMLP_CHUNK, GMLP_CHUNK)), full((GMLP_CHUNK, GMLP_W)),
        ],
        out_specs=[out(HYENA_W), out(HYENA_W), out(HYENA_W), out(POOL_W), out(GMLP_W)],
        out_shape=[
            jax.ShapeDtypeStruct((B, n, HYENA_W), F32),
            jax.ShapeDtypeStruct((B, n, HYENA_W), F32),
            jax.ShapeDtypeStruct((B, n, HYENA_W), F32),
            jax.ShapeDtypeStruct((B, n, POOL_W), BF16),
            jax.ShapeDtypeStruct((B, n, GMLP_W), BF16),
        ],
        compiler_params=_cparams(("parallel", "parallel")),
        name="local",
    )(p, p, p, p, p, p, p, p, lw["conv_w"], lw["conv_b"], lw["pool_bd"], lw["pool_scale"],
      lw["gm_norm_w"], lw["gm_ws"], lw["gm_bias"])


def _filt_kernel(z_ref, t_ref, w1h_ref, w1l_ref, b1_ref, w2h_ref, w2l_ref, b2_ref, fr_ref,
                 w3h_ref, w3l_ref, dec_ref, k_ref, s_ref, *, n_seq):
    i = pl.program_id(0)
    tb = z_ref.shape[0]
    fr = fr_ref[...]
    hdn = jnp.sin(fr * (_dot3x(z_ref[...], w1h_ref[...], w1l_ref[...]) + b1_ref[...]))
    for u in range(HYENA_INNER):
        hdn = jnp.sin(fr * (_dot3x(hdn, w2h_ref[u], w2l_ref[u]) + b2_ref[u:u + 1, :]))
    t = t_ref[...]
    tt = jnp.concatenate([t] * (HYENA_ORDER * HYENA_W // LANES), axis=1)
    window = jnp.exp(-tt * jnp.abs(dec_ref[...]))
    k = _dot3x(hdn, w3h_ref[...], w3l_ref[...]) * window
    rows = lax.broadcasted_iota(jnp.int32, k.shape, 0) + i * tb
    k = jnp.where(rows == n_seq, 0.0, k)
    k_ref[...] = k

    @pl.when(i == 0)
    def _():
        s_ref[...] = jnp.zeros(s_ref.shape, F32)

    s_ref[...] += jnp.sum(jnp.abs(k), axis=0, keepdims=True)


def _filt_call(n, lw):
    N = 2 * n
    tb = _tile(n, 512)
    nb_half = n // tb
    pos = np.arange(N)
    pos = np.where(pos < n, pos, N - pos).astype(np.float64)
    t = pos / (n - 1)
    bands = np.linspace(1e-4, HYENA_BANDS - 1, HYENA_BANDS)
    ang = (2.0 * math.pi / n) * pos[:, None] * bands
    z = np.concatenate([t[:, None], np.cos(ang), np.sin(ang)], axis=-1)
    zp = np.zeros((N, LANES), np.float32)
    zp[:, :z.shape[1]] = z
    tl = np.broadcast_to(t[:, None], (N, LANES)).astype(np.float32)
    CW = HYENA_ORDER * HYENA_W

    def full(shape):
        return pl.BlockSpec(shape, lambda i: (0,) * len(shape))

    def side(rows_):
        return pl.BlockSpec((None, rows_, CW), lambda i: (i // nb_half, 0, 0))

    return pl.pallas_call(
        functools.partial(_filt_kernel, n_seq=n),
        grid=(N // tb,),
        in_specs=[
            pl.BlockSpec((tb, LANES), lambda i: (i, 0)),
            pl.BlockSpec((tb, LANES), lambda i: (i, 0)),
            full((LANES, HYENA_HIDDEN)), full((LANES, HYENA_HIDDEN)), full((1, HYENA_HIDDEN)),
            full((HYENA_INNER, HYENA_HIDDEN, HYENA_HIDDEN)), full((HYENA_INNER, HYENA_HIDDEN, HYENA_HIDDEN)),
            full((HYENA_INNER, HYENA_HIDDEN)), full((1, HYENA_HIDDEN)),
            side(HYENA_HIDDEN), side(HYENA_HIDDEN), side(1),
        ],
        out_specs=[pl.BlockSpec((tb, CW), lambda i: (i, 0)), pl.BlockSpec((1, CW), lambda i: (0, 0))],
        out_shape=[jax.ShapeDtypeStruct((N, CW), F32), jax.ShapeDtypeStruct((1, CW), F32)],
        compiler_params=_cparams(("arbitrary",)),
        name="filt",
    )(jnp.asarray(zp), jnp.asarray(tl), lw["hy_w1h"], lw["hy_w1l"], lw["hy_b1"], lw["hy_w2h"], lw["hy_w2l"],
      lw["hy_b2"], lw["hy_freq"], lw["hy_w3h"], lw["hy_w3l"], lw["hy_decay"])


def _fft_dims(N):
    e = int(round(math.log2(N)))
    assert 2 ** e == N
    N1 = 2 ** ((e + 1) // 2)
    return N1, N // N1


def _hilo(a):
    a = np.asarray(a, np.float64)
    hi = jnp.asarray(a, F32).astype(BF16)
    lo = (jnp.asarray(a, F32) - hi.astype(F32)).astype(BF16)
    return hi, lo


@functools.lru_cache(maxsize=None)
def _fft_consts_np(N):
    N1, N2 = _fft_dims(N)
    h = N1 // 2
    k1 = np.arange(N1)
    F1 = np.exp(-2j * np.pi * np.outer(k1, k1) / N1)
    k2 = np.arange(N2)
    F2 = np.exp(-2j * np.pi * np.outer(k2, k2) / N2)
    tw = np.exp(-2j * np.pi * np.outer(k1, k2) / N)
    g1 = np.zeros((2 * N1, N1))
    g1[0::2, :h] = F1.real[:, :h]
    g1[0::2, h:] = -F1.imag[:, :h]
    g1[1::2, :h] = F1.imag[:, :h]
    g1[1::2, h:] = F1.real[:, :h]
    g1f = np.zeros((2 * N1, N1))
    g1f[0::2] = F1.real
    g1f[1::2] = F1.imag
    g2 = np.block([[F2.real, -F2.imag], [F2.imag, F2.real]])
    g2c = np.block([[F2.real, F2.imag], [-F2.imag, F2.real]])
    g3 = np.zeros((N1, 2 * N1))
    g3[:h, 0::2] = F1.real[:h]
    g3[:h, 1::2] = F1.imag[:h]
    g3[h:, 0::2] = -F1.imag[:h]
    g3[h:, 1::2] = F1.real[:h]
    g3 /= N
    twt = np.stack([tw.real, tw.imag], axis=1)
    return dict(g1=g1, g1f=g1f, g2=g2, g2c=g2c, g3=g3, tw=twt)


def _fft_consts(N):
    c = _fft_consts_np(N)
    out = {k: _hilo(c[k]) for k in ("g1", "g1f", "g2", "g2c", "g3")}
    N1, N2 = _fft_dims(N)
    tw = jnp.asarray(c["tw"], F32)
    out["tw"] = jnp.broadcast_to(tw[..., None], (N1, 2, N2, LANES))
    return out


def _lmul_kernel(gh_ref, gl_ref, x_ref, o_ref):
    o_ref[...] = _dot3(gh_ref[...], gl_ref[...], x_ref[...])


def _lmul_call(g, x):
    gh, gl = g
    M, K = gh.shape
    G, K2, L = x.shape
    assert K == K2
    tc = _tile(L, 2048, LANES)
    return pl.pallas_call(
        _lmul_kernel,
        grid=(G, L // tc),
        in_specs=[
            pl.BlockSpec((M, K), lambda b, j: (0, 0)),
            pl.BlockSpec((M, K), lambda b, j: (0, 0)),
            pl.BlockSpec((None, K, tc), lambda b, j: (b, 0, j)),
        ],
        out_specs=pl.BlockSpec((None, M, tc), lambda b, j: (b, 0, j)),
        out_shape=jax.ShapeDtypeStruct((G, M, L), F32),
        compiler_params=_cparams(("parallel", "parallel")),
        name="dft_outer",
    )(gh, gl, x)


def _cmul(ar, ai, br, bi):
    return ar * br - ai * bi, ar * bi + ai * br


def _lane_tile(t, c):
    return jnp.concatenate([t] * (c // LANES), axis=-1)


def _spec_kernel(a_ref, tw_ref, g2h_ref, g2l_ref, s_ref, o_ref):
    kb, _, N2, C = a_ref.shape
    inv = 1.0 / s_ref[...]
    for k in range(kb):
        twr = _lane_tile(tw_ref[k, 0], C)
        twi = _lane_tile(tw_ref[k, 1], C)
        br, bi = _cmul(a_ref[k, 0], a_ref[k, 1], twr, twi)
        zz = _dot3(g2h_ref[...], g2l_ref[...], jnp.concatenate([br, bi], axis=0))
        o_ref[k, 0] = zz[:N2] * inv
        o_ref[k, 1] = zz[N2:] * inv


def _conv_kernel(a_ref, tw_ref, kf_ref, g2h_ref, g2l_ref, g2ch_ref, g2cl_ref, o_ref):
    kb, _, N2, C = a_ref.shape
    for k in range(kb):
        twr = _lane_tile(tw_ref[k, 0], C)
        twi = _lane_tile(tw_ref[k, 1], C)
        br, bi = _cmul(a_ref[k, 0], a_ref[k, 1], twr, twi)
        zz = _dot3(g2h_ref[...], g2l_ref[...], jnp.concatenate([br, bi], axis=0))
        yr, yi = _cmul(zz[:N2], zz[N2:], kf_ref[k, 0], kf_ref[k, 1])
        cc = _dot3(g2ch_ref[...], g2cl_ref[...], jnp.concatenate([yr, yi], axis=0))
        dr, di = _cmul(cc[:N2], cc[N2:], twr, -twi)
        o_ref[k, 0] = dr
        o_ref[k, 1] = di


def _spec_call(a5, fc, s):
    G, N1, _, N2, C = a5.shape
    kb = _tile(N1, 8, 1)
    blk = pl.BlockSpec((None, kb, 2, N2, C), lambda i, b: (b, i, 0, 0, 0))
    mat = pl.BlockSpec((2 * N2, 2 * N2), lambda i, b: (0, 0))
    return pl.pallas_call(
        _spec_kernel,
        grid=(N1 // kb, G),
        in_specs=[blk, pl.BlockSpec((kb, 2, N2, LANES), lambda i, b: (i, 0, 0, 0)), mat, mat,
                  pl.BlockSpec((None, 1, C), lambda i, b: (b, 0, 0))],
        out_specs=blk,
        out_shape=jax.ShapeDtypeStruct(a5.shape, F32),
        compiler_params=_cparams(("parallel", "parallel")),
        name="filt_spec",
    )(a5, fc["tw"], fc["g2"][0], fc["g2"][1], s)


def _conv_call(a5, kf, fc):
    G, N1, _, N2, C = a5.shape
    kb = _tile(N1, 8, 1)
    blk = pl.BlockSpec((None, kb, 2, N2, C), lambda i, b: (b, i, 0, 0, 0))
    mat = pl.BlockSpec((2 * N2, 2 * N2), lambda i, b: (0, 0))
    return pl.pallas_call(
        _conv_kernel,
        grid=(N1 // kb, G),
        in_specs=[blk, pl.BlockSpec((kb, 2, N2, LANES), lambda i, b: (i, 0, 0, 0)),
                  pl.BlockSpec((kb, 2, N2, C), lambda i, b: (i, 0, 0, 0)), mat, mat, mat, mat],
        out_specs=blk,
        out_shape=jax.ShapeDtypeStruct(a5.shape, F32),
        compiler_params=_cparams(("parallel", "parallel")),
        name="spec_conv",
    )(a5, fc["tw"], kf, fc["g2"][0], fc["g2"][1], fc["g2c"][0], fc["g2c"][1])


def _filter_spectra(n, lw, fc):
    N = 2 * n
    N1, N2 = _fft_dims(N)
    k, s = _filt_call(n, lw)
    k = k.reshape(N1, N2, HYENA_ORDER, HYENA_W).transpose(2, 0, 1, 3).reshape(HYENA_ORDER, N1, N2 * HYENA_W)
    a = _lmul_call(fc["g1f"], k)
    a5 = a.reshape(HYENA_ORDER, N1, 2, N2, HYENA_W)
    return _spec_call(a5, fc, s.reshape(HYENA_ORDER, 1, HYENA_W))


def _long_conv(u, kf, fc):
    B, n, W = u.shape
    assert B % 2 == 0
    N1, N2 = _fft_dims(2 * n)
    x = u.reshape(B // 2, N1, N2 * W)
    a = _lmul_call(fc["g1"], x)
    d = _conv_call(a.reshape(B // 2, N1, 2, N2, W), kf, fc)
    y = _lmul_call(fc["g3"], d.reshape(B // 2, 2 * N1, N2 * W))
    return y.reshape(B, n, W)


def _gate_kernel(c_ref, u_ref, g_ref, sk_ref, o_ref):
    o_ref[...] = g_ref[...] * (c_ref[...] + u_ref[...] * sk_ref[...])


def _gate_call(conv, u, gate, skip):
    B, n, W = u.shape
    tm = _tile(n, 1024)
    blk = pl.BlockSpec((None, tm, W), lambda b, i: (b, i, 0))
    return pl.pallas_call(
        _gate_kernel,
        grid=(B, n // tm),
        in_specs=[blk, blk, blk, pl.BlockSpec((1, W), lambda b, i: (0, 0))],
        out_specs=blk,
        out_shape=jax.ShapeDtypeStruct((B, n, W), F32),
        compiler_params=_cparams(("parallel", "parallel")),
        name="hy_gate",
    )(conv, u, gate, skip)


def _hyena(v, x1, x2, kf, lw, fc):
    y = v
    for o, gate in enumerate((x1, x2)):
        y = _gate_call(_long_conv(y, kf[o], fc), y, gate, lw["hy_skip"][o:o + 1])
    return y


def _merge_kernel(at_ref, hy_ref, po_ref, gm_ref, g0_ref, g1_ref, g2_ref, g3_ref, x_ref, gr_ref,
                  wa_ref, wh_ref, wp_ref, wg_ref, wo_ref, o_ref):
    merged = _sigmoid(g0_ref[...]) * _dot(at_ref[...], wa_ref[...])
    merged = merged + _sigmoid(g1_ref[...]) * _dot(hy_ref[...].astype(BF16), wh_ref[...])
    merged = merged + _sigmoid(g2_ref[...]) * _dot(po_ref[...], wp_ref[...])
    merged = merged + _sigmoid(g3_ref[...]) * _dot(gm_ref[...], wg_ref[...])
    o_ref[...] = x_ref[...] + gr_ref[...] * _dot(merged.astype(BF16), wo_ref[...])


def _merge_call(attn, hy, po, gm, p, x, gate_res, lw):
    B, n, D = x.shape
    tm = _tile(n, 512)

    def rows(w, col=0):
        return pl.BlockSpec((None, tm, w), lambda b, i: (b, i, col))

    def full(shape):
        return pl.BlockSpec(shape, lambda b, i: (0,) * len(shape))

    return pl.pallas_call(
        _merge_kernel,
        grid=(B, n // tm),
        in_specs=[
            rows(ATTN_W), rows(HYENA_W), rows(POOL_W), rows(GMLP_W),
            rows(D, 0), rows(D, 1), rows(D, 2), rows(D, 3), rows(D),
            pl.BlockSpec((None, 1, D), lambda b, i: (b, 0, 0)),
            full((ATTN_W, D)), full((HYENA_W, D)), full((POOL_W, D)), full((GMLP_W, D)), full((D, D)),
        ],
        out_specs=rows(D),
        out_shape=jax.ShapeDtypeStruct((B, n, D), F32),
        compiler_params=_cparams(("parallel", "parallel")),
        name="merge",
    )(attn, hy, po, gm, p, p, p, p, x, gate_res, lw["w_br_attn"], lw["w_br_hyena"], lw["w_br_pool"],
      lw["w_br_gmlp"], lw["w_out"])


def _ffn_kernel(x_ref, sh_ref, sc_ref, nw_ref, gr_ref, wg_ref, wu_ref, wd_ref, o_ref, h_ref, acc_ref):
    j = pl.program_id(2)

    @pl.when(j == 0)
    def _():
        h_ref[...] = _modulated_norm(x_ref[...], nw_ref[...], sc_ref[...], sh_ref[...]).astype(BF16)
        acc_ref[...] = jnp.zeros(acc_ref.shape, F32)

    h = h_ref[...]
    a = _silu(_dot(h, wg_ref[...])) * _dot(h, wu_ref[...])
    acc_ref[...] += _dot(a.astype(BF16), wd_ref[...])

    @pl.when(j == pl.num_programs(2) - 1)
    def _():
        o_ref[...] = x_ref[...] + gr_ref[...] * acc_ref[...]


def _ffn_call(x, shift, scale, norm_w, gate_res, wg, wu, wd):
    B, n, D = x.shape
    H = wg.shape[1]
    tm = _tile(n, 1024)
    th = 256
    vec = pl.BlockSpec((None, 1, D), lambda b, i, j: (b, 0, 0))
    return pl.pallas_call(
        _ffn_kernel,
        grid=(B, n // tm, H // th),
        in_specs=[
            pl.BlockSpec((None, tm, D), lambda b, i, j: (b, i, 0)),
            vec, vec, pl.BlockSpec((1, D), lambda b, i, j: (0, 0)), vec,
            pl.BlockSpec((D, th), lambda b, i, j: (0, j)),
            pl.BlockSpec((D, th), lambda b, i, j: (0, j)),
            pl.BlockSpec((th, D), lambda b, i, j: (j, 0)),
        ],
        out_specs=pl.BlockSpec((None, tm, D), lambda b, i, j: (b, i, 0)),
        out_shape=jax.ShapeDtypeStruct((B, n, D), F32),
        scratch_shapes=[pltpu.VMEM((tm, D), BF16), pltpu.VMEM((tm, D), F32)],
        compiler_params=_cparams(("parallel", "parallel", "arbitrary")),
        name="ffn",
    )(x, shift, scale, norm_w.reshape(1, D), gate_res, wg, wu, wd)


def _rms_kernel(x_ref, w_ref, o_ref):
    x = x_ref[...]
    ms = jnp.mean(x * x, axis=-1, keepdims=True)
    o_ref[...] = x * lax.rsqrt(ms + RMS_EPS) * w_ref[...]


def _rms_call(x, w):
    B, n, D = x.shape
    tm = _tile(n, 1024)
    blk = pl.BlockSpec((None, tm, D), lambda b, i: (b, i, 0))
    return pl.pallas_call(
        _rms_kernel,
        grid=(B, n // tm),
        in_specs=[blk, pl.BlockSpec((1, D), lambda b, i: (0, 0))],
        out_specs=blk,
        out_shape=jax.ShapeDtypeStruct((B, n, D), F32),
        compiler_params=_cparams(("parallel", "parallel")),
        name="final_norm",
    )(x, w.reshape(1, D))


def _rope_tables(n):
    tok = np.arange(n)
    axis_dim = HEAD_DIM // 2
    inv_freq = ROPE_THETA ** (-np.arange(0, axis_dim, 2, dtype=np.float64) / axis_dim)
    inv_freq = inv_freq.astype(np.float32).astype(np.float64)
    ang = np.concatenate([(tok // GRID_W)[:, None] * inv_freq, (tok % GRID_W)[:, None] * inv_freq], axis=-1)
    ang = ang.astype(np.float32)
    cos = np.repeat(np.cos(ang), 2, axis=-1)
    sin = np.repeat(np.sin(ang), 2, axis=-1)
    sin[:, 0::2] *= -1.0
    reps = LANES // HEAD_DIM
    return jnp.asarray(np.tile(cos, (1, reps)), F32), jnp.asarray(np.tile(sin, (1, reps)), F32)


def _identity_tables(n):
    return jnp.ones((n, LANES), F32), jnp.zeros((n, LANES), F32)


def _layer_weights(l, a):
    w_in = a["w_in"][l]
    w_perm = jnp.concatenate([
        w_in[:, GATE_OFF:], w_in[:, Q_OFF:K_OFF], w_in[:, HY_OFF:POOL_OFF], w_in[:, K_OFF:HY_OFF],
        w_in[:, POOL_OFF:GM_OFF], w_in[:, GM_OFF:GATE_OFF]], axis=1).astype(BF16)
    w3 = a["hy_w3"][l].reshape(HYENA_HIDDEN, HYENA_ORDER, 2, HYENA_W).transpose(2, 0, 1, 3)
    w3 = w3.reshape(2, HYENA_HIDDEN, HYENA_ORDER * HYENA_W)
    dec = a["hy_decay"][l].reshape(HYENA_ORDER, 2, HYENA_W).transpose(1, 0, 2).reshape(2, 1, HYENA_ORDER * HYENA_W)
    w1 = jnp.zeros((LANES, HYENA_HIDDEN), F32).at[:a["hy_w1"].shape[1]].set(a["hy_w1"][l])
    w1h, w1l = _split(w1)
    w2h, w2l = _split(a["hy_w2"][l])
    w3h, w3l = _split(w3)
    eye = jnp.eye(len(POOL_WINDOWS), dtype=F32)
    pool_bd = jnp.einsum("gh,gcd->gchd", eye, a["pool_w"][l]).reshape(POOL_W, POOL_W).astype(BF16)
    gm_bias = jnp.repeat(a["gm_bs"][l].T, GMLP_GROUP_W, axis=1)
    return dict(
        w_in=w_perm,
        q_norm_w=jnp.tile(a["q_norm_w"][l], N_Q_HEADS).reshape(1, ATTN_W),
        k_norm_w=jnp.tile(a["k_norm_w"][l], N_KV_HEADS).reshape(1, KV_W),
        conv_w=a["hy_conv_w"][l], conv_b=a["hy_conv_b"][l].reshape(1, -1),
        hy_w1h=w1h, hy_w1l=w1l, hy_b1=a["hy_b1"][l].reshape(1, -1), hy_w2h=w2h, hy_w2l=w2l, hy_b2=a["hy_b2"][l],
        hy_w3h=w3h, hy_w3l=w3l, hy_freq=a["hy_freq"][l].reshape(1, -1), hy_decay=dec, hy_skip=a["hy_skip"][l],
        pool_bd=pool_bd, pool_scale=a["pool_scale"][l].reshape(1, -1),
        gm_norm_w=a["gm_norm_w"][l].reshape(1, -1), gm_ws=a["gm_ws"][l].astype(BF16), gm_bias=gm_bias,
        w_br_attn=a["w_br_attn"][l].astype(BF16), w_br_hyena=a["w_br_hyena"][l].astype(BF16),
        w_br_pool=a["w_br_pool"][l].astype(BF16), w_br_gmlp=a["w_br_gmlp"][l].astype(BF16),
        w_out=a["w_out"][l].astype(BF16),
        ffn_w_gate=a["ffn_w_gate"][l].astype(BF16), ffn_w_up=a["ffn_w_up"][l].astype(BF16),
        ffn_w_down=a["ffn_w_down"][l].astype(BF16),
        norm1_w=a["norm1_w"][l], norm2_w=a["norm2_w"][l],
    )


Q_SCALE = (HEAD_DIM ** -0.5) * math.log2(math.e)


def _mixer_block(x, mods, lw, rope, k_ctx, v_ctx, kf, fc):
    sh1, s1, g1, sh2, s2, g2 = mods
    p = _inproj_call(x, sh1, s1, lw["norm1_w"], lw["w_in"], 1280)
    q, k, v = _qkprep_call(p, rope[0], rope[1], lw["q_norm_w"], lw["k_norm_w"], Q_SCALE)
    if k_ctx is not None:
        k_all = jnp.concatenate([k_ctx, k], axis=1)
        v_all = jnp.concatenate([v_ctx, v], axis=3)
    else:
        k_all, v_all = k, v
    attn = _attn_call(q, k_all, v_all)
    hv, hx1, hx2, po, gm = _local_call(p, lw)
    hy = _hyena(hv, hx1, hx2, kf, lw, fc)
    x = _merge_call(attn, hy, po, gm, p, x, g1, lw)
    x = _ffn_call(x, sh2, s2, lw["norm2_w"], g2, lw["ffn_w_gate"], lw["ffn_w_up"], lw["ffn_w_down"])
    return x, k, v


def kernel(x, c, ctx, c_ctx, w_mod, b_mod, norm1_w, norm2_w, w_in, q_norm_w, k_norm_w, hy_conv_w, hy_conv_b,
           hy_w1, hy_b1, hy_w2, hy_b2, hy_w3, hy_freq, hy_decay, hy_skip, pool_w, pool_scale, gm_norm_w, gm_ws,
           gm_bs, w_br_attn, w_br_hyena, w_br_pool, w_br_gmlp, w_out, ffn_w_gate, ffn_w_up, ffn_w_down,
           final_norm_w):
    a = dict(w_in=w_in, q_norm_w=q_norm_w, k_norm_w=k_norm_w, hy_conv_w=hy_conv_w, hy_conv_b=hy_conv_b,
             hy_w1=hy_w1, hy_b1=hy_b1, hy_w2=hy_w2, hy_b2=hy_b2, hy_w3=hy_w3, hy_freq=hy_freq, hy_decay=hy_decay,
             hy_skip=hy_skip, pool_w=pool_w, pool_scale=pool_scale, gm_norm_w=gm_norm_w, gm_ws=gm_ws, gm_bs=gm_bs,
             w_br_attn=w_br_attn, w_br_hyena=w_br_hyena, w_br_pool=w_br_pool, w_br_gmlp=w_br_gmlp, w_out=w_out,
             ffn_w_gate=ffn_w_gate, ffn_w_up=ffn_w_up, ffn_w_down=ffn_w_down, norm1_w=norm1_w, norm2_w=norm2_w)
    B, n, D = x.shape
    nc = ctx.shape[1]
    depth = w_mod.shape[0]

    R = ((B + 1 + 15) // 16) * 16
    act = jnp.zeros((R, D), F32).at[:B].set(c).at[B].set(c_ctx)
    mod = _mod_call(act, w_mod, b_mod)

    rope = _rope_tables(n)
    rope_ctx = _identity_tables(nc)
    fc = _fft_consts(2 * n)
    fc_ctx = _fft_consts(2 * nc)

    xc = ctx
    for l in range(depth):
        lw = _layer_weights(l, a)
        mods = [m.reshape(B, 1, D) for m in jnp.split(mod[l, :B], 6, axis=-1)]
        cmods = [jnp.broadcast_to(m.reshape(1, 1, D), (B, 1, D)) for m in jnp.split(mod[l, B], 6, axis=-1)]
        if l < depth - 1:
            kf_ctx = _filter_spectra(nc, lw, fc_ctx)
            xc, k_c, v_c = _mixer_block(xc, cmods, lw, rope_ctx, None, None, kf_ctx, fc_ctx)
        else:
            pc = _inproj_call(xc, cmods[0], cmods[1], lw["norm1_w"], lw["w_in"][:, P_KV:P_KV + 2 * KV_W], 2 * KV_W)
            k_c, v_c = _kvprep_ctx(pc, lw)
        kf = _filter_spectra(n, lw, fc)
        x, _, _ = _mixer_block(x, mods, lw, rope, k_c, v_c, kf, fc)
    return _rms_call(x, final_norm_w)


def _kvprep_only_kernel(kv_ref, kw_ref, bd_ref, ko_ref, vo_ref):
    t = kv_ref[:, :KV_W]
    sq_hi, sq_lo = _split(t * t)
    ms = _dot(sq_hi, bd_ref[...]) + _dot(sq_lo, bd_ref[...])
    ko_ref[...] = (t * lax.rsqrt(ms + RMS_EPS) * kw_ref[...]).astype(BF16)
    vt = kv_ref[:, KV_W:].T
    for g in range(N_KV_HEADS):
        vo_ref[g] = _value_rows(vt[g * HEAD_DIM:(g + 1) * HEAD_DIM])


def _kvprep_ctx(pc, lw):
    B, nc, _ = pc.shape
    bd = np.kron(np.eye(LANES // HEAD_DIM), np.full((HEAD_DIM, HEAD_DIM), 1.0 / HEAD_DIM))
    return pl.pallas_call(
        _kvprep_only_kernel,
        grid=(B,),
        in_specs=[pl.BlockSpec((None, nc, 2 * KV_W), lambda b: (b, 0, 0)),
                  pl.BlockSpec((1, KV_W), lambda b: (0, 0)), pl.BlockSpec((LANES, LANES), lambda b: (0, 0))],
        out_specs=[pl.BlockSpec((None, nc, KV_W), lambda b: (b, 0, 0)),
                   pl.BlockSpec((None, N_KV_HEADS, V_ROWS, nc), lambda b: (b, 0, 0, 0))],
        out_shape=[jax.ShapeDtypeStruct((B, nc, KV_W), BF16),
                   jax.ShapeDtypeStruct((B, N_KV_HEADS, V_ROWS, nc), BF16)],
        compiler_params=_cparams(("parallel",)),
        name="kvprep_ctx",
    )(pc, lw["k_norm_w"], jnp.asarray(bd, BF16))
```

```python
import functools
import math

import numpy as np
import jax
import jax.numpy as jnp
from jax import lax
from jax.experimental import pallas as pl
from jax.experimental.pallas import tpu as pltpu

F32 = jnp.float32
BF16 = jnp.bfloat16

D_MODEL = 1024
GRID_W = 64
RMS_EPS = 1e-6
HEAD_DIM = 64
N_Q_HEADS = 8
N_KV_HEADS = 2
ATTN_W = N_Q_HEADS * HEAD_DIM
KV_W = N_KV_HEADS * HEAD_DIM
ROPE_THETA = 10000.0
HYENA_W = 256
HYENA_ORDER = 2
HYENA_BANDS = 16
HYENA_HIDDEN = 64
HYENA_INNER = 2
POOL_W = 256
POOL_WINDOWS = (2, 4, 8, 16)
POOL_GROUP_W = POOL_W // len(POOL_WINDOWS)
POOL_HALO = 8
GMLP_W = 256
GMLP_CHUNK = 128
GMLP_GROUPS = 4
GMLP_GROUP_W = GMLP_W // GMLP_GROUPS
N_BRANCH = 4
FFN_HIDDEN = ((8 * D_MODEL // 3 + 255) // 256) * 256

Q_OFF = 0
K_OFF = Q_OFF + ATTN_W
V_OFF = K_OFF + KV_W
HY_OFF = V_OFF + KV_W
POOL_OFF = HY_OFF + 3 * HYENA_W
GM_OFF = POOL_OFF + POOL_W
GATE_OFF = GM_OFF + 2 * GMLP_W
IN_W = GATE_OFF + N_BRANCH * D_MODEL

P_Q = 0
P_KV = P_Q + ATTN_W
P_POOL = P_KV + 2 * KV_W
P_GMU = P_POOL + POOL_W
P_GMV = P_GMU + GMLP_W
P_HY = P_GMV + GMLP_W
P_W = P_HY + 3 * HYENA_W
P_TILE = 768

LANES = 128
VMEM_LIMIT = 52 * 1024 * 1024


def _cparams(sem):
    return pltpu.CompilerParams(dimension_semantics=sem, vmem_limit_bytes=VMEM_LIMIT)


def _tile(n, cap, mult=8):
    best = None
    for t in range(mult, min(n, cap) + 1, mult):
        if n % t == 0:
            best = t
    assert best is not None, (n, cap, mult)
    return best


def _split(x):
    hi = x.astype(BF16)
    lo = (x - hi.astype(F32)).astype(BF16)
    return hi, lo


def _dot(a, b):
    return jnp.dot(a, b, preferred_element_type=F32)


def _dot3(a_hi, a_lo, x):
    x_hi, x_lo = _split(x)
    return _dot(a_hi, x_hi) + (_dot(a_lo, x_hi) + _dot(a_hi, x_lo))


def _dot3x(x, b_hi, b_lo):
    x_hi, x_lo = _split(x)
    return _dot(x_hi, b_hi) + (_dot(x_lo, b_hi) + _dot(x_hi, b_lo))


def _sigmoid(x):
    return 1.0 / (1.0 + jnp.exp(-x))


def _silu(x):
    return x * _sigmoid(x)


def _modulated_norm(x, nw, scale, shift):
    ms = jnp.mean(x * x, axis=-1, keepdims=True)
    y = x * lax.rsqrt(ms + RMS_EPS) * nw
    return y * (1.0 + scale) + shift


def _mod_kernel(a_ref, w_ref, b_ref, o_ref):
    a = _silu(a_ref[...]).astype(BF16)
    o_ref[...] = _dot(a, w_ref[...].astype(BF16)) + b_ref[...]


def _mod_call(act, w_mod, b_mod):
    L, D, W6 = w_mod.shape
    R = act.shape[0]
    tn = 1024
    return pl.pallas_call(
        _mod_kernel,
        grid=(L, W6 // tn),
        in_specs=[
            pl.BlockSpec((R, D), lambda l, j: (0, 0)),
            pl.BlockSpec((None, D, tn), lambda l, j: (l, 0, j)),
            pl.BlockSpec((None, 1, tn), lambda l, j: (l, 0, j)),
        ],
        out_specs=pl.BlockSpec((None, R, tn), lambda l, j: (l, 0, j)),
        out_shape=jax.ShapeDtypeStruct((L, R, W6), F32),
        compiler_params=_cparams(("arbitrary", "arbitrary")),
        name="mod",
    )(act, w_mod, b_mod.reshape(L, 1, W6))


def _inproj_kernel(x_ref, sh_ref, sc_ref, nw_ref, w_ref, o_ref, h_ref):
    @pl.when(pl.program_id(2) == 0)
    def _():
        h_ref[...] = _modulated_norm(x_ref[...], nw_ref[...], sc_ref[...], sh_ref[...]).astype(BF16)

    o_ref[...] = _dot(h_ref[...], w_ref[...])


def _inproj_call(x, shift, scale, norm_w, w, tn):
    B, n, D = x.shape
    Nw = w.shape[1]
    tm = _tile(n, 1024)
    return pl.pallas_call(
        _inproj_kernel,
        grid=(B, n // tm, Nw // tn),
        in_specs=[
            pl.BlockSpec((None, tm, D), lambda b, i, j: (b, i, 0)),
            pl.BlockSpec((None, 1, D), lambda b, i, j: (b, 0, 0)),
            pl.BlockSpec((None, 1, D), lambda b, i, j: (b, 0, 0)),
            pl.BlockSpec((1, D), lambda b, i, j: (0, 0)),
            pl.BlockSpec((D, tn), lambda b, i, j: (0, j)),
        ],
        out_specs=pl.BlockSpec((None, tm, tn), lambda b, i, j: (b, i, j)),
        out_shape=jax.ShapeDtypeStruct((B, n, Nw), F32),
        scratch_shapes=[pltpu.VMEM((tm, D), BF16)],
        compiler_params=_cparams(("parallel", "parallel", "arbitrary")),
        name="inproj",
    )(x, shift, scale, norm_w.reshape(1, D), w)


def _head_norm_rope(t, w, cs, sn, bd_ref):
    sq_hi, sq_lo = _split(t * t)
    ms = _dot(sq_hi, bd_ref[...]) + _dot(sq_lo, bd_ref[...])
    y = t * lax.rsqrt(ms + RMS_EPS) * w
    lane = lax.broadcasted_iota(jnp.int32, y.shape, 1)
    swapped = jnp.where(lane % 2 == 0, pltpu.roll(y, LANES - 1, 1), pltpu.roll(y, 1, 1))
    return y * cs + swapped * sn


V_ROWS = 80
Q_PER_KV = N_Q_HEADS // N_KV_HEADS
LAG_INIT_KEYS = 128
LAG_LIMIT = 24.0


def _value_rows(vt_g):
    tail = lax.broadcasted_iota(jnp.int32, (V_ROWS - HEAD_DIM, vt_g.shape[1]), 0)
    return jnp.concatenate([vt_g, jnp.where(tail == 0, 1.0, 0.0)], axis=0).astype(BF16)


def _qkprep_kernel(q_ref, kv_ref, cs_ref, sn_ref, qw_ref, kw_ref, bd_ref, qo_ref, ko_ref, vo_ref, *, q_scale):
    cs = cs_ref[...]
    sn = sn_ref[...]
    tm = q_ref.shape[0]
    zeros = jnp.zeros((HEAD_DIM, tm), F32)
    for c in range(ATTN_W // LANES):
        sl = slice(c * LANES, (c + 1) * LANES)
        r = _head_norm_rope(q_ref[:, sl], qw_ref[:, sl], cs, sn, bd_ref) * q_scale
        rt = r.T
        for half in range(LANES // HEAD_DIM):
            h = c * (LANES // HEAD_DIM) + half
            blk = rt[half * HEAD_DIM:(half + 1) * HEAD_DIM]
            parts = [blk if g == h // Q_PER_KV else zeros for g in range(N_KV_HEADS)]
            qo_ref[h] = jnp.concatenate(parts, axis=0).astype(BF16)
    ko_ref[...] = _head_norm_rope(kv_ref[:, :KV_W], kw_ref[...], cs, sn, bd_ref).astype(BF16)
    vt = kv_ref[:, KV_W:].T
    for g in range(N_KV_HEADS):
        vo_ref[g] = _value_rows(vt[g * HEAD_DIM:(g + 1) * HEAD_DIM])


def _qkprep_call(p, cs, sn, qw, kw, q_scale):
    B, n, _ = p.shape
    tm = _tile(n, 512, LANES)
    bd = np.kron(np.eye(LANES // HEAD_DIM), np.full((HEAD_DIM, HEAD_DIM), 1.0 / HEAD_DIM))
    bd = jnp.asarray(bd, BF16)
    return pl.pallas_call(
        functools.partial(_qkprep_kernel, q_scale=q_scale),
        grid=(n // tm, B),
        in_specs=[
            pl.BlockSpec((None, tm, ATTN_W), lambda i, b: (b, i, P_Q // ATTN_W)),
            pl.BlockSpec((None, tm, 2 * KV_W), lambda i, b: (b, i, P_KV // (2 * KV_W))),
            pl.BlockSpec((tm, LANES), lambda i, b: (i, 0)),
            pl.BlockSpec((tm, LANES), lambda i, b: (i, 0)),
            pl.BlockSpec((1, ATTN_W), lambda i, b: (0, 0)),
            pl.BlockSpec((1, KV_W), lambda i, b: (0, 0)),
            pl.BlockSpec((LANES, LANES), lambda i, b: (0, 0)),
        ],
        out_specs=[
            pl.BlockSpec((None, N_Q_HEADS, KV_W, tm), lambda i, b: (b, 0, 0, i)),
            pl.BlockSpec((None, tm, KV_W), lambda i, b: (b, i, 0)),
            pl.BlockSpec((None, N_KV_HEADS, V_ROWS, tm), lambda i, b: (b, 0, 0, i)),
        ],
        out_shape=[
            jax.ShapeDtypeStruct((B, N_Q_HEADS, KV_W, n), BF16),
            jax.ShapeDtypeStruct((B, n, KV_W), BF16),
            jax.ShapeDtypeStruct((B, N_KV_HEADS, V_ROWS, n), BF16),
        ],
        compiler_params=_cparams(("parallel", "parallel")),
        name="qkprep",
    )(p, p, cs, sn, qw, kw, bd)


def _attn_kernel(qt_ref, k_ref, vt_ref, o_ref, m_ref, acc_ref, *, tk):
    nk = vt_ref.shape[0]
    tq = m_ref.shape[-1]

    def keys(j):
        return k_ref[pl.ds(pl.multiple_of(j * tk, tk), tk), :]

    def lagged_chunk(j, jump):
        kb = keys(j)
        s_next = _dot(kb, qt_ref[0])
        for h in range(N_Q_HEADS):
            s = s_next
            if h + 1 < N_Q_HEADS:
                s_next = _dot(kb, qt_ref[h + 1])
            m_old = m_ref[h]
            p = jnp.exp2(s - m_old).astype(BF16)
            cmax = jnp.max(s, axis=0, keepdims=True)
            m_new = jnp.maximum(m_old, cmax)
            jump = jnp.maximum(jump, cmax - m_old)
            acc_ref[h] = jnp.exp2(m_old - m_new) * (acc_ref[h] + _dot(vt_ref[j, h // Q_PER_KV], p))
            m_ref[h] = m_new
        return jump

    def exact_chunk(j, carry):
        kb = keys(j)
        for h in range(N_Q_HEADS):
            s = _dot(kb, qt_ref[h])
            m_old = m_ref[h]
            m_new = jnp.maximum(m_old, jnp.max(s, axis=0, keepdims=True))
            p = jnp.exp2(s - m_new).astype(BF16)
            acc_ref[h] = jnp.exp2(m_old - m_new) * acc_ref[h] + _dot(vt_ref[j, h // Q_PER_KV], p)
            m_ref[h] = m_new
        return carry

    k0 = k_ref[0:LAG_INIT_KEYS, :]
    for h in range(N_Q_HEADS):
        m_ref[h] = jnp.max(_dot(k0, qt_ref[h]), axis=0, keepdims=True)
    acc_ref[...] = jnp.zeros(acc_ref.shape, F32)
    jump = lax.fori_loop(0, nk, lagged_chunk, jnp.zeros((1, tq), F32))

    @pl.when(jnp.logical_not(jnp.max(jump) <= LAG_LIMIT))
    def _():
        m_ref[...] = jnp.full(m_ref.shape, -jnp.inf, F32)
        acc_ref[...] = jnp.zeros(acc_ref.shape, F32)
        lax.fori_loop(0, nk, exact_chunk, 0)

    for c in range(ATTN_W // LANES):
        heads = []
        for h in range(c * (LANES // HEAD_DIM), (c + 1) * (LANES // HEAD_DIM)):
            a = acc_ref[h]
            heads.append(a[:HEAD_DIM] / a[HEAD_DIM:HEAD_DIM + 1])
        o_ref[:, c * LANES:(c + 1) * LANES] = jnp.concatenate(heads, axis=0).T.astype(BF16)


def _attn_call(qt, k, vt):
    B, _, _, n = qt.shape
    Lk = k.shape[1]
    tq = _tile(n, 512, LANES)
    tk = _tile(Lk, 768, LANES)
    nk = Lk // tk
    vt = vt.reshape(B, N_KV_HEADS, V_ROWS, nk, tk).transpose(0, 3, 1, 2, 4)
    return pl.pallas_call(
        functools.partial(_attn_kernel, tk=tk),
        grid=(B, n // tq),
        in_specs=[
            pl.BlockSpec((None, N_Q_HEADS, KV_W, tq), lambda b, i: (b, 0, 0, i)),
            pl.BlockSpec((None, Lk, KV_W), lambda b, i: (b, 0, 0)),
            pl.BlockSpec((None, nk, N_KV_HEADS, V_ROWS, tk), lambda b, i: (b, 0, 0, 0, 0)),
        ],
        out_specs=pl.BlockSpec((None, tq, ATTN_W), lambda b, i: (b, i, 0)),
        out_shape=jax.ShapeDtypeStruct((B, n, ATTN_W), BF16),
        scratch_shapes=[
            pltpu.VMEM((N_Q_HEADS, 1, tq), F32),
            pltpu.VMEM((N_Q_HEADS, V_ROWS, tq), F32),
        ],
        compiler_params=_cparams(("parallel", "parallel")),
        name="attn",
    )(qt, k, vt)


def _shift_rows(cur, prev_row, next_row, rows):
    tm = cur.shape[0]
    up = jnp.where(rows == 0, prev_row, pltpu.roll(cur, 1, 0))
    dn = jnp.where(rows == tm - 1, next_row, pltpu.roll(cur, tm - 1, 0))
    return up, dn


def _local_kernel(hy_ref, hyp_ref, hyn_ref, po_ref, pop_ref, pon_ref, gu_ref, gv_ref,
                  cw_ref, cb_ref, pw_ref, ps_ref, gnw_ref, gws_ref, gb_ref,
                  v_ref, x1_ref, x2_ref, pl_ref, gm_ref, *, n_seq):
    i = pl.program_id(1)
    last = pl.num_programs(1) - 1
    tm = hy_ref.shape[0]
    has_prev = (i > 0).astype(F32)
    has_next = (i < last).astype(F32)

    cur = hy_ref[...]
    rows = lax.broadcasted_iota(jnp.int32, cur.shape, 0)
    prev_row = hyp_ref[POOL_HALO - 1:POOL_HALO, :] * has_prev
    next_row = hyn_ref[0:1, :] * has_next
    up, dn = _shift_rows(cur, prev_row, next_row, rows)
    z = up * cw_ref[0:1, :] + cur * cw_ref[1:2, :] + dn * cw_ref[2:3, :] + cb_ref[...]
    v_ref[...] = z[:, :HYENA_W]
    x1_ref[...] = z[:, HYENA_W:2 * HYENA_W]
    x2_ref[...] = z[:, 2 * HYENA_W:]

    zc = po_ref[...]
    ext = jnp.concatenate([pop_ref[...] * has_prev, zc, pon_ref[...] * has_next], axis=0)
    te = tm + 2 * POOL_HALO
    lane = lax.broadcasted_iota(jnp.int32, zc.shape, 1)
    prow = lax.broadcasted_iota(jnp.int32, zc.shape, 0) + i * tm
    half = jnp.left_shift(1, lane // POOL_GROUP_W)
    acc = jnp.zeros(zc.shape, F32)
    for d in range(-POOL_HALO, POOL_HALO):
        sh = pltpu.roll(ext, (te - (POOL_HALO + d)) % te, 0)[:tm]
        inside = (d >= -half) & (d < half)
        acc = acc + jnp.where(inside, sh, 0.0)
    cnt = jnp.minimum(prow + half, n_seq) - jnp.maximum(prow - half, 0)
    dlt = acc / cnt.astype(F32) - zc
    pl_ref[...] = (_dot(dlt.astype(BF16), pw_ref[...]) * ps_ref[...]).astype(BF16)

    gv = gv_ref[...]
    mu = jnp.mean(gv, axis=-1, keepdims=True)
    ctr = gv - mu
    var = jnp.mean(ctr * ctr, axis=-1, keepdims=True)
    vn = (ctr * lax.rsqrt(var + RMS_EPS) * gnw_ref[...]).astype(BF16)
    group = lax.broadcasted_iota(jnp.int32, (GMLP_CHUNK, GMLP_W), 1) // GMLP_GROUP_W
    for c in range(tm // GMLP_CHUNK):
        rs = slice(c * GMLP_CHUNK, (c + 1) * GMLP_CHUNK)
        vc = vn[rs]
        mixed = gb_ref[...]
        for g in range(GMLP_GROUPS):
            mg = _dot(gws_ref[g], vc)
            mixed = mixed + jnp.where(group == g, mg, 0.0)
        gm_ref[rs, :] = (gu_ref[rs, :] * mixed).astype(BF16)


def _local_call(p, lw):
    B, n, _ = p.shape
    tm = _tile(n, 512, GMLP_CHUNK)
    nb8 = n // POOL_HALO
    r8 = tm // POOL_HALO
    HW = 3 * HYENA_W

    def cur(w, off):
        return pl.BlockSpec((None, tm, w), lambda b, i: (b, i, off // w))

    def prev(w, off):
        return pl.BlockSpec((None, POOL_HALO, w), lambda b, i: (b, jnp.maximum(i * r8 - 1, 0), off // w))

    def nxt(w, off):
        return pl.BlockSpec((None, POOL_HALO, w), lambda b, i: (b, jnp.minimum((i + 1) * r8, nb8 - 1), off // w))

    def full(shape):
        return pl.BlockSpec(shape, lambda b, i: (0,) * len(shape))

    def out(w):
        return pl.BlockSpec((None, tm, w), lambda b, i: (b, i, 0))

    return pl.pallas_call(
        functools.partial(_local_kernel, n_seq=n),
        grid=(B, n // tm),
        in_specs=[
            cur(HW, P_HY), prev(HW, P_HY), nxt(HW, P_HY),
            cur(POOL_W, P_POOL), prev(POOL_W, P_POOL), nxt(POOL_W, P_POOL),
            cur(GMLP_W, P_GMU), cur(GMLP_W, P_GMV),
            full((3, HW)), full((1, HW)), full((POOL_W, POOL_W)), full((1, POOL_W)),
            full((1, GMLP_W)), full((GMLP_GROUPS, GMLP_CHUNK, GMLP_CHUNK)), full((GMLP_CHUNK, GMLP_W)),
        ],
        out_specs=[out(HYENA_W), out(HYENA_W), out(HYENA_W), out(POOL_W), out(GMLP_W)],
        out_shape=[
            jax.ShapeDtypeStruct((B, n, HYENA_W), F32),
            jax.ShapeDtypeStruct((B, n, HYENA_W), F32),
            jax.ShapeDtypeStruct((B, n, HYENA_W), F32),
            jax.ShapeDtypeStruct((B, n, POOL_W), BF16),
            jax.ShapeDtypeStruct((B, n, GMLP_W), BF16),
        ],
        compiler_params=_cparams(("parallel", "parallel")),
        name="local",
    )(p, p, p, p, p, p, p, p, lw["conv_w"], lw["conv_b"], lw["pool_bd"], lw["pool_scale"],
      lw["gm_norm_w"], lw["gm_ws"], lw["gm_bias"])


def _filt_kernel(z_ref, t_ref, w1h_ref, w1l_ref, b1_ref, w2h_ref, w2l_ref, b2_ref, fr_ref,
                 w3h_ref, w3l_ref, dec_ref, k_ref, s_ref, *, n_seq):
    i = pl.program_id(0)
    tb = z_ref.shape[0]
    fr = fr_ref[...]
    hdn = jnp.sin(fr * (_dot3x(z_ref[...], w1h_ref[...], w1l_ref[...]) + b1_ref[...]))
    for u in range(HYENA_INNER):
        hdn = jnp.sin(fr * (_dot3x(hdn, w2h_ref[u], w2l_ref[u]) + b2_ref[u:u + 1, :]))
    t = t_ref[...]
    tt = jnp.concatenate([t] * (HYENA_ORDER * HYENA_W // LANES), axis=1)
    window = jnp.exp(-tt * jnp.abs(dec_ref[...]))
    k = _dot3x(hdn, w3h_ref[...], w3l_ref[...]) * window
    rows = lax.broadcasted_iota(jnp.int32, k.shape, 0) + i * tb
    k = jnp.where(rows == n_seq, 0.0, k)
    k_ref[...] = k

    @pl.when(i == 0)
    def _():
        s_ref[...] = jnp.zeros(s_ref.shape, F32)

    s_ref[...] += jnp.sum(jnp.abs(k), axis=0, keepdims=True)


def _filt_call(n, lw):
    N = 2 * n
    tb = _tile(n, 512)
    nb_half = n // tb
    pos = np.arange(N)
    pos = np.where(pos < n, pos, N - pos).astype(np.float64)
    t = pos / (n - 1)
    bands = np.linspace(1e-4, HYENA_BANDS - 1, HYENA_BANDS)
    ang = (2.0 * math.pi / n) * pos[:, None] * bands
    z = np.concatenate([t[:, None], np.cos(ang), np.sin(ang)], axis=-1)
    zp = np.zeros((N, LANES), np.float32)
    zp[:, :z.shape[1]] = z
    tl = np.broadcast_to(t[:, None], (N, LANES)).astype(np.float32)
    CW = HYENA_ORDER * HYENA_W

    def full(shape):
        return pl.BlockSpec(shape, lambda i: (0,) * len(shape))

    def side(rows_):
        return pl.BlockSpec((None, rows_, CW), lambda i: (i // nb_half, 0, 0))

    return pl.pallas_call(
        functools.partial(_filt_kernel, n_seq=n),
        grid=(N // tb,),
        in_specs=[
            pl.BlockSpec((tb, LANES), lambda i: (i, 0)),
            pl.BlockSpec((tb, LANES), lambda i: (i, 0)),
            full((LANES, HYENA_HIDDEN)), full((LANES, HYENA_HIDDEN)), full((1, HYENA_HIDDEN)),
            full((HYENA_INNER, HYENA_HIDDEN, HYENA_HIDDEN)), full((HYENA_INNER, HYENA_HIDDEN, HYENA_HIDDEN)),
            full((HYENA_INNER, HYENA_HIDDEN)), full((1, HYENA_HIDDEN)),
            side(HYENA_HIDDEN), side(HYENA_HIDDEN), side(1),
        ],
        out_specs=[pl.BlockSpec((tb, CW), lambda i: (i, 0)), pl.BlockSpec((1, CW), lambda i: (0, 0))],
        out_shape=[jax.ShapeDtypeStruct((N, CW), F32), jax.ShapeDtypeStruct((1, CW), F32)],
        compiler_params=_cparams(("arbitrary",)),
        name="filt",
    )(jnp.asarray(zp), jnp.asarray(tl), lw["hy_w1h"], lw["hy_w1l"], lw["hy_b1"], lw["hy_w2h"], lw["hy_w2l"],
      lw["hy_b2"], lw["hy_freq"], lw["hy_w3h"], lw["hy_w3l"], lw["hy_decay"])


def _fft_dims(N):
    e = int(round(math.log2(N)))
    assert 2 ** e == N
    N1 = 2 ** ((e + 1) // 2)
    return N1, N // N1


def _hilo(a):
    a = np.asarray(a, np.float64)
    hi = jnp.asarray(a, F32).astype(BF16)
    lo = (jnp.asarray(a, F32) - hi.astype(F32)).astype(BF16)
    return hi, lo


@functools.lru_cache(maxsize=None)
def _fft_consts_np(N):
    N1, N2 = _fft_dims(N)
    h = N1 // 2
    k1 = np.arange(N1)
    F1 = np.exp(-2j * np.pi * np.outer(k1, k1) / N1)
    k2 = np.arange(N2)
    F2 = np.exp(-2j * np.pi * np.outer(k2, k2) / N2)
    tw = np.exp(-2j * np.pi * np.outer(k1, k2) / N)
    g1 = np.zeros((2 * N1, N1))
    g1[0::2, :h] = F1.real[:, :h]
    g1[0::2, h:] = -F1.imag[:, :h]
    g1[1::2, :h] = F1.imag[:, :h]
    g1[1::2, h:] = F1.real[:, :h]
    g1f = np.zeros((2 * N1, N1))
    g1f[0::2] = F1.real
    g1f[1::2] = F1.imag
    g2 = np.block([[F2.real, -F2.imag], [F2.imag, F2.real]])
    g2c = np.block([[F2.real, F2.imag], [-F2.imag, F2.real]])
    g3 = np.zeros((N1, 2 * N1))
    g3[:h, 0::2] = F1.real[:h]
    g3[:h, 1::2] = F1.imag[:h]
    g3[h:, 0::2] = -F1.imag[:h]
    g3[h:, 1::2] = F1.real[:h]
    g3 /= N
    twt = np.stack([tw.real, tw.imag], axis=1)
    return dict(g1=g1, g1f=g1f, g2=g2, g2c=g2c, g3=g3, tw=twt)


def _fft_consts(N):
    c = _fft_consts_np(N)
    out = {k: _hilo(c[k]) for k in ("g1", "g1f", "g2", "g2c", "g3")}
    N1, N2 = _fft_dims(N)
    tw = jnp.asarray(c["tw"], F32)
    out["tw"] = jnp.broadcast_to(tw[..., None], (N1, 2, N2, LANES))
    return out


def _lmul_kernel(gh_ref, gl_ref, x_ref, o_ref):
    o_ref[...] = _dot3(gh_ref[...], gl_ref[...], x_ref[...])


def _lmul_call(g, x):
    gh, gl = g
    M, K = gh.shape
    G, K2, L = x.shape
    assert K == K2
    tc = _tile(L, 2048, LANES)
    return pl.pallas_call(
        _lmul_kernel,
        grid=(G, L // tc),
        in_specs=[
            pl.BlockSpec((M, K), lambda b, j: (0, 0)),
            pl.BlockSpec((M, K), lambda b, j: (0, 0)),
            pl.BlockSpec((None, K, tc), lambda b, j: (b, 0, j)),
        ],
        out_specs=pl.BlockSpec((None, M, tc), lambda b, j: (b, 0, j)),
        out_shape=jax.ShapeDtypeStruct((G, M, L), F32),
        compiler_params=_cparams(("parallel", "parallel")),
        name="dft_outer",
    )(gh, gl, x)


def _cmul(ar, ai, br, bi):
    return ar * br - ai * bi, ar * bi + ai * br


def _lane_tile(t, c):
    return jnp.concatenate([t] * (c // LANES), axis=-1)


def _spec_kernel(a_ref, tw_ref, g2h_ref, g2l_ref, s_ref, o_ref):
    kb, _, N2, C = a_ref.shape
    inv = 1.0 / s_ref[...]
    for k in range(kb):
        twr = _lane_tile(tw_ref[k, 0], C)
        twi = _lane_tile(tw_ref[k, 1], C)
        br, bi = _cmul(a_ref[k, 0], a_ref[k, 1], twr, twi)
        zz = _dot3(g2h_ref[...], g2l_ref[...], jnp.concatenate([br, bi], axis=0))
        o_ref[k, 0] = zz[:N2] * inv
        o_ref[k, 1] = zz[N2:] * inv


def _conv_kernel(a_ref, tw_ref, kf_ref, g2h_ref, g2l_ref, g2ch_ref, g2cl_ref, o_ref):
    kb, _, N2, C = a_ref.shape
    for k in range(kb):
        twr = _lane_tile(tw_ref[k, 0], C)
        twi = _lane_tile(tw_ref[k, 1], C)
        br, bi = _cmul(a_ref[k, 0], a_ref[k, 1], twr, twi)
        zz = _dot3(g2h_ref[...], g2l_ref[...], jnp.concatenate([br, bi], axis=0))
        yr, yi = _cmul(zz[:N2], zz[N2:], kf_ref[k, 0], kf_ref[k, 1])
        cc = _dot3(g2ch_ref[...], g2cl_ref[...], jnp.concatenate([yr, yi], axis=0))
        dr, di = _cmul(cc[:N2], cc[N2:], twr, -twi)
        o_ref[k, 0] = dr
        o_ref[k, 1] = di


def _spec_call(a5, fc, s):
    G, N1, _, N2, C = a5.shape
    kb = _tile(N1, 8, 1)
    blk = pl.BlockSpec((None, kb, 2, N2, C), lambda i, b: (b, i, 0, 0, 0))
    mat = pl.BlockSpec((2 * N2, 2 * N2), lambda i, b: (0, 0))
    return pl.pallas_call(
        _spec_kernel,
        grid=(N1 // kb, G),
        in_specs=[blk, pl.BlockSpec((kb, 2, N2, LANES), lambda i, b: (i, 0, 0, 0)), mat, mat,
                  pl.BlockSpec((None, 1, C), lambda i, b: (b, 0, 0))],
        out_specs=blk,
        out_shape=jax.ShapeDtypeStruct(a5.shape, F32),
        compiler_params=_cparams(("parallel", "parallel")),
        name="filt_spec",
    )(a5, fc["tw"], fc["g2"][0], fc["g2"][1], s)


def _conv_call(a5, kf, fc):
    G, N1, _, N2, C = a5.shape
    kb = _tile(N1, 8, 1)
    blk = pl.BlockSpec((None, kb, 2, N2, C), lambda i, b: (b, i, 0, 0, 0))
    mat = pl.BlockSpec((2 * N2, 2 * N2), lambda i, b: (0, 0))
    return pl.pallas_call(
        _conv_kernel,
        grid=(N1 // kb, G),
        in_specs=[blk, pl.BlockSpec((kb, 2, N2, LANES), lambda i, b: (i, 0, 0, 0)),
                  pl.BlockSpec((kb, 2, N2, C), lambda i, b: (i, 0, 0, 0)), mat, mat, mat, mat],
        out_specs=blk,
        out_shape=jax.ShapeDtypeStruct(a5.shape, F32),
        compiler_params=_cparams(("parallel", "parallel")),
        name="spec_conv",
    )(a5, fc["tw"], kf, fc["g2"][0], fc["g2"][1], fc["g2c"][0], fc["g2c"][1])


def _filter_spectra(n, lw, fc):
    N = 2 * n
    N1, N2 = _fft_dims(N)
    k, s = _filt_call(n, lw)
    k = k.reshape(N1, N2, HYENA_ORDER, HYENA_W).transpose(2, 0, 1, 3).reshape(HYENA_ORDER, N1, N2 * HYENA_W)
    a = _lmul_call(fc["g1f"], k)
    a5 = a.reshape(HYENA_ORDER, N1, 2, N2, HYENA_W)
    return _spec_call(a5, fc, s.reshape(HYENA_ORDER, 1, HYENA_W))


def _long_conv(u, kf, fc):
    B, n, W = u.shape
    assert B % 2 == 0
    N1, N2 = _fft_dims(2 * n)
    x = u.reshape(B // 2, N1, N2 * W)
    a = _lmul_call(fc["g1"], x)
    d = _conv_call(a.reshape(B // 2, N1, 2, N2, W), kf, fc)
    y = _lmul_call(fc["g3"], d.reshape(B // 2, 2 * N1, N2 * W))
    return y.reshape(B, n, W)


def _gate_kernel(c_ref, u_ref, g_ref, sk_ref, o_ref):
    o_ref[...] = g_ref[...] * (c_ref[...] + u_ref[...] * sk_ref[...])


def _gate_call(conv, u, gate, skip):
    B, n, W = u.shape
    tm = _tile(n, 1024)
    blk = pl.BlockSpec((None, tm, W), lambda b, i: (b, i, 0))
    return pl.pallas_call(
        _gate_kernel,
        grid=(B, n // tm),
        in_specs=[blk, blk, blk, pl.BlockSpec((1, W), lambda b, i: (0, 0))],
        out_specs=blk,
        out_shape=jax.ShapeDtypeStruct((B, n, W), F32),
        compiler_params=_cparams(("parallel", "parallel")),
        name="hy_gate",
    )(conv, u, gate, skip)


def _hyena(v, x1, x2, kf, lw, fc):
    y = v
    for o, gate in enumerate((x1, x2)):
        y = _gate_call(_long_conv(y, kf[o], fc), y, gate, lw["hy_skip"][o:o + 1])
    return y


def _merge_kernel(at_ref, hy_ref, po_ref, gm_ref, x_ref, sh_ref, sc_ref, nw_ref, gr_ref,
                  wgate_ref, wa_ref, wh_ref, wp_ref, wg_ref, wo_ref, o_ref):
    x = x_ref[...]
    D = x.shape[1]
    h = _modulated_norm(x, nw_ref[...], sc_ref[...], sh_ref[...]).astype(BF16)
    branches = ((at_ref[...], wa_ref), (hy_ref[...].astype(BF16), wh_ref), (po_ref[...], wp_ref), (gm_ref[...], wg_ref))
    merged = None
    for i, (br, w_ref) in enumerate(branches):
        term = _sigmoid(_dot(h, wgate_ref[:, i * D:(i + 1) * D])) * _dot(br, w_ref[...])
        merged = term if merged is None else merged + term
    o_ref[...] = x + gr_ref[...] * _dot(merged.astype(BF16), wo_ref[...])


def _merge_call(attn, hy, po, gm, x, shift, scale, norm_w, gate_res, lw):
    B, n, D = x.shape
    tm = _tile(n, 512)

    def rows(w):
        return pl.BlockSpec((None, tm, w), lambda b, i: (b, i, 0))

    def full(shape):
        return pl.BlockSpec(shape, lambda b, i: (0,) * len(shape))

    vec = pl.BlockSpec((None, 1, D), lambda b, i: (b, 0, 0))
    return pl.pallas_call(
        _merge_kernel,
        grid=(B, n // tm),
        in_specs=[
            rows(ATTN_W), rows(HYENA_W), rows(POOL_W), rows(GMLP_W), rows(D),
            vec, vec, full((1, D)), vec,
            full((D, N_BRANCH * D)),
            full((ATTN_W, D)), full((HYENA_W, D)), full((POOL_W, D)), full((GMLP_W, D)), full((D, D)),
        ],
        out_specs=rows(D),
        out_shape=jax.ShapeDtypeStruct((B, n, D), F32),
        compiler_params=_cparams(("parallel", "parallel")),
        name="merge",
    )(attn, hy, po, gm, x, shift, scale, norm_w.reshape(1, D), gate_res, lw["w_gate"], lw["w_br_attn"],
      lw["w_br_hyena"], lw["w_br_pool"], lw["w_br_gmlp"], lw["w_out"])


def _ffn_kernel(x_ref, sh_ref, sc_ref, nw_ref, gr_ref, wg_ref, wu_ref, wd_ref, o_ref, h_ref, acc_ref):
    j = pl.program_id(2)

    @pl.when(j == 0)
    def _():
        h_ref[...] = _modulated_norm(x_ref[...], nw_ref[...], sc_ref[...], sh_ref[...]).astype(BF16)
        acc_ref[...] = jnp.zeros(acc_ref.shape, F32)

    h = h_ref[...]
    a = _silu(_dot(h, wg_ref[...])) * _dot(h, wu_ref[...])
    acc_ref[...] += _dot(a.astype(BF16), wd_ref[...])

    @pl.when(j == pl.num_programs(2) - 1)
    def _():
        o_ref[...] = x_ref[...] + gr_ref[...] * acc_ref[...]


def _ffn_call(x, shift, scale, norm_w, gate_res, wg, wu, wd):
    B, n, D = x.shape
    H = wg.shape[1]
    tm = _tile(n, 1024)
    th = 256
    vec = pl.BlockSpec((None, 1, D), lambda b, i, j: (b, 0, 0))
    return pl.pallas_call(
        _ffn_kernel,
        grid=(B, n // tm, H // th),
        in_specs=[
            pl.BlockSpec((None, tm, D), lambda b, i, j: (b, i, 0)),
            vec, vec, pl.BlockSpec((1, D), lambda b, i, j: (0, 0)), vec,
            pl.BlockSpec((D, th), lambda b, i, j: (0, j)),
            pl.BlockSpec((D, th), lambda b, i, j: (0, j)),
            pl.BlockSpec((th, D), lambda b, i, j: (j, 0)),
        ],
        out_specs=pl.BlockSpec((None, tm, D), lambda b, i, j: (b, i, 0)),
        out_shape=jax.ShapeDtypeStruct((B, n, D), F32),
        scratch_shapes=[pltpu.VMEM((tm, D), BF16), pltpu.VMEM((tm, D), F32)],
        compiler_params=_cparams(("parallel", "parallel", "arbitrary")),
        name="ffn",
    )(x, shift, scale, norm_w.reshape(1, D), gate_res, wg, wu, wd)


def _rms_kernel(x_ref, w_ref, o_ref):
    x = x_ref[...]
    ms = jnp.mean(x * x, axis=-1, keepdims=True)
    o_ref[...] = x * lax.rsqrt(ms + RMS_EPS) * w_ref[...]


def _rms_call(x, w):
    B, n, D = x.shape
    tm = _tile(n, 1024)
    blk = pl.BlockSpec((None, tm, D), lambda b, i: (b, i, 0))
    return pl.pallas_call(
        _rms_kernel,
        grid=(B, n // tm),
        in_specs=[blk, pl.BlockSpec((1, D), lambda b, i: (0, 0))],
        out_specs=blk,
        out_shape=jax.ShapeDtypeStruct((B, n, D), F32),
        compiler_params=_cparams(("parallel", "parallel")),
        name="final_norm",
    )(x, w.reshape(1, D))


def _rope_tables(n):
    tok = np.arange(n)
    axis_dim = HEAD_DIM // 2
    inv_freq = ROPE_THETA ** (-np.arange(0, axis_dim, 2, dtype=np.float64) / axis_dim)
    inv_freq = inv_freq.astype(np.float32).astype(np.float64)
    ang = np.concatenate([(tok // GRID_W)[:, None] * inv_freq, (tok % GRID_W)[:, None] * inv_freq], axis=-1)
    ang = ang.astype(np.float32)
    cos = np.repeat(np.cos(ang), 2, axis=-1)
    sin = np.repeat(np.sin(ang), 2, axis=-1)
    sin[:, 0::2] *= -1.0
    reps = LANES // HEAD_DIM
    return jnp.asarray(np.tile(cos, (1, reps)), F32), jnp.asarray(np.tile(sin, (1, reps)), F32)


def _identity_tables(n):
    return jnp.ones((n, LANES), F32), jnp.zeros((n, LANES), F32)


def _layer_weights(l, a):
    w_in = a["w_in"][l]
    w_perm = jnp.concatenate([
        w_in[:, Q_OFF:K_OFF], w_in[:, K_OFF:HY_OFF], w_in[:, POOL_OFF:GM_OFF], w_in[:, GM_OFF:GATE_OFF],
        w_in[:, HY_OFF:POOL_OFF]], axis=1).astype(BF16)
    w3 = a["hy_w3"][l].reshape(HYENA_HIDDEN, HYENA_ORDER, 2, HYENA_W).transpose(2, 0, 1, 3)
    w3 = w3.reshape(2, HYENA_HIDDEN, HYENA_ORDER * HYENA_W)
    dec = a["hy_decay"][l].reshape(HYENA_ORDER, 2, HYENA_W).transpose(1, 0, 2).reshape(2, 1, HYENA_ORDER * HYENA_W)
    w1 = jnp.zeros((LANES, HYENA_HIDDEN), F32).at[:a["hy_w1"].shape[1]].set(a["hy_w1"][l])
    w1h, w1l = _split(w1)
    w2h, w2l = _split(a["hy_w2"][l])
    w3h, w3l = _split(w3)
    eye = jnp.eye(len(POOL_WINDOWS), dtype=F32)
    pool_bd = jnp.einsum("gh,gcd->gchd", eye, a["pool_w"][l]).reshape(POOL_W, POOL_W).astype(BF16)
    gm_bias = jnp.repeat(a["gm_bs"][l].T, GMLP_GROUP_W, axis=1)
    return dict(
        w_in=w_perm, w_gate=w_in[:, GATE_OFF:].astype(BF16),
        q_norm_w=jnp.tile(a["q_norm_w"][l], N_Q_HEADS).reshape(1, ATTN_W),
        k_norm_w=jnp.tile(a["k_norm_w"][l], N_KV_HEADS).reshape(1, KV_W),
        conv_w=a["hy_conv_w"][l], conv_b=a["hy_conv_b"][l].reshape(1, -1),
        hy_w1h=w1h, hy_w1l=w1l, hy_b1=a["hy_b1"][l].reshape(1, -1), hy_w2h=w2h, hy_w2l=w2l, hy_b2=a["hy_b2"][l],
        hy_w3h=w3h, hy_w3l=w3l, hy_freq=a["hy_freq"][l].reshape(1, -1), hy_decay=dec, hy_skip=a["hy_skip"][l],
        pool_bd=pool_bd, pool_scale=a["pool_scale"][l].reshape(1, -1),
        gm_norm_w=a["gm_norm_w"][l].reshape(1, -1), gm_ws=a["gm_ws"][l].astype(BF16), gm_bias=gm_bias,
        w_br_attn=a["w_br_attn"][l].astype(BF16), w_br_hyena=a["w_br_hyena"][l].astype(BF16),
        w_br_pool=a["w_br_pool"][l].astype(BF16), w_br_gmlp=a["w_br_gmlp"][l].astype(BF16),
        w_out=a["w_out"][l].astype(BF16),
        ffn_w_gate=a["ffn_w_gate"][l].astype(BF16), ffn_w_up=a["ffn_w_up"][l].astype(BF16),
        ffn_w_down=a["ffn_w_down"][l].astype(BF16),
        norm1_w=a["norm1_w"][l], norm2_w=a["norm2_w"][l],
    )


Q_SCALE = (HEAD_DIM ** -0.5) * math.log2(math.e)


def _mixer_block(x, mods, lw, rope, k_ctx, v_ctx, kf, fc):
    sh1, s1, g1, sh2, s2, g2 = mods
    p = _inproj_call(x, sh1, s1, lw["norm1_w"], lw["w_in"], P_TILE)
    q, k, v = _qkprep_call(p, rope[0], rope[1], lw["q_norm_w"], lw["k_norm_w"], Q_SCALE)
    if k_ctx is not None:
        k_all = jnp.concatenate([k_ctx, k], axis=1)
        v_all = jnp.concatenate([v_ctx, v], axis=3)
    else:
        k_all, v_all = k, v
    attn = _attn_call(q, k_all, v_all)
    hv, hx1, hx2, po, gm = _local_call(p, lw)
    hy = _hyena(hv, hx1, hx2, kf, lw, fc)
    x = _merge_call(attn, hy, po, gm, x, sh1, s1, lw["norm1_w"], g1, lw)
    x = _ffn_call(x, sh2, s2, lw["norm2_w"], g2, lw["ffn_w_gate"], lw["ffn_w_up"], lw["ffn_w_down"])
    return x, k, v


def kernel(x, c, ctx, c_ctx, w_mod, b_mod, norm1_w, norm2_w, w_in, q_norm_w, k_norm_w, hy_conv_w, hy_conv_b,
           hy_w1, hy_b1, hy_w2, hy_b2, hy_w3, hy_freq, hy_decay, hy_skip, pool_w, pool_scale, gm_norm_w, gm_ws,
           gm_bs, w_br_attn, w_br_hyena, w_br_pool, w_br_gmlp, w_out, ffn_w_gate, ffn_w_up, ffn_w_down,
           final_norm_w):
    a = dict(w_in=w_in, q_norm_w=q_norm_w, k_norm_w=k_norm_w, hy_conv_w=hy_conv_w, hy_conv_b=hy_conv_b,
             hy_w1=hy_w1, hy_b1=hy_b1, hy_w2=hy_w2, hy_b2=hy_b2, hy_w3=hy_w3, hy_freq=hy_freq, hy_decay=hy_decay,
             hy_skip=hy_skip, pool_w=pool_w, pool_scale=pool_scale, gm_norm_w=gm_norm_w, gm_ws=gm_ws, gm_bs=gm_bs,
             w_br_attn=w_br_attn, w_br_hyena=w_br_hyena, w_br_pool=w_br_pool, w_br_gmlp=w_br_gmlp, w_out=w_out,
             ffn_w_gate=ffn_w_gate, ffn_w_up=ffn_w_up, ffn_w_down=ffn_w_down, norm1_w=norm1_w, norm2_w=norm2_w)
    B, n, D = x.shape
    nc = ctx.shape[1]
    depth = w_mod.shape[0]

    R = ((B + 1 + 15) // 16) * 16
    act = jnp.zeros((R, D), F32).at[:B].set(c).at[B].set(c_ctx)
    mod = _mod_call(act, w_mod, b_mod)

    rope = _rope_tables(n)
    rope_ctx = _identity_tables(nc)
    fc = _fft_consts(2 * n)
    fc_ctx = _fft_consts(2 * nc)

    xc = ctx
    for l in range(depth):
        lw = _layer_weights(l, a)
        mods = [m.reshape(B, 1, D) for m in jnp.split(mod[l, :B], 6, axis=-1)]
        cmods = [jnp.broadcast_to(m.reshape(1, 1, D), (B, 1, D)) for m in jnp.split(mod[l, B], 6, axis=-1)]
        if l < depth - 1:
            kf_ctx = _filter_spectra(nc, lw, fc_ctx)
            xc, k_c, v_c = _mixer_block(xc, cmods, lw, rope_ctx, None, None, kf_ctx, fc_ctx)
        else:
            pc = _inproj_call(xc, cmods[0], cmods[1], lw["norm1_w"], lw["w_in"][:, P_KV:P_KV + 2 * KV_W], 2 * KV_W)
            k_c, v_c = _kvprep_ctx(pc, lw)
        kf = _filter_spectra(n, lw, fc)
        x, _, _ = _mixer_block(x, mods, lw, rope, k_c, v_c, kf, fc)
    return _rms_call(x, final_norm_w)


def _kvprep_only_kernel(kv_ref, kw_ref, bd_ref, ko_ref, vo_ref):
    t = kv_ref[:, :KV_W]
    sq_hi, sq_lo = _split(t * t)
    ms = _dot(sq_hi, bd_ref[...]) + _dot(sq_lo, bd_ref[...])
    ko_ref[...] = (t * lax.rsqrt(ms + RMS_EPS) * kw_ref[...]).astype(BF16)
    vt = kv_ref[:, KV_W:].T
    for g in range(N_KV_HEADS):
        vo_ref[g] = _value_rows(vt[g * HEAD_DIM:(g + 1) * HEAD_DIM])


def _kvprep_ctx(pc, lw):
    B, nc, _ = pc.shape
    bd = np.kron(np.eye(LANES // HEAD_DIM), np.full((HEAD_DIM, HEAD_DIM), 1.0 / HEAD_DIM))
    return pl.pallas_call(
        _kvprep_only_kernel,
        grid=(B,),
        in_specs=[pl.BlockSpec((None, nc, 2 * KV_W), lambda b: (b, 0, 0)),
                  pl.BlockSpec((1, KV_W), lambda b: (0, 0)), pl.BlockSpec((LANES, LANES), lambda b: (0, 0))],
        out_specs=[pl.BlockSpec((None, nc, KV_W), lambda b: (b, 0, 0)),
                   pl.BlockSpec((None, N_KV_HEADS, V_ROWS, nc), lambda b: (b, 0, 0, 0))],
        out_shape=[jax.ShapeDtypeStruct((B, nc, KV_W), BF16),
                   jax.ShapeDtypeStruct((B, N_KV_HEADS, V_ROWS, nc), BF16)],
        compiler_params=_cparams(("parallel",)),
        name="kvprep_ctx",
    )(pc, lw["k_norm_w"], jnp.asarray(bd, BF16))
```

```python
import functools
import math

import numpy as np
import jax
import jax.numpy as jnp
from jax import lax
from jax.experimental import pallas as pl
from jax.experimental.pallas import tpu as pltpu

F32 = jnp.float32
BF16 = jnp.bfloat16

D_MODEL = 1024
GRID_W = 64
RMS_EPS = 1e-6
HEAD_DIM = 64
N_Q_HEADS = 8
N_KV_HEADS = 2
ATTN_W = N_Q_HEADS * HEAD_DIM
KV_W = N_KV_HEADS * HEAD_DIM
ROPE_THETA = 10000.0
HYENA_W = 256
HYENA_ORDER = 2
HYENA_BANDS = 16
HYENA_HIDDEN = 64
HYENA_INNER = 2
POOL_W = 256
POOL_WINDOWS = (2, 4, 8, 16)
POOL_GROUP_W = POOL_W // len(POOL_WINDOWS)
POOL_HALO = 8
GMLP_W = 256
GMLP_CHUNK = 128
GMLP_GROUPS = 4
GMLP_GROUP_W = GMLP_W // GMLP_GROUPS
N_BRANCH = 4
FFN_HIDDEN = ((8 * D_MODEL // 3 + 255) // 256) * 256

Q_OFF = 0
K_OFF = Q_OFF + ATTN_W
V_OFF = K_OFF + KV_W
HY_OFF = V_OFF + KV_W
POOL_OFF = HY_OFF + 3 * HYENA_W
GM_OFF = POOL_OFF + POOL_W
GATE_OFF = GM_OFF + 2 * GMLP_W
IN_W = GATE_OFF + N_BRANCH * D_MODEL

P_Q = 0
P_KV = P_Q + ATTN_W
P_POOL = P_KV + 2 * KV_W
P_GMU = P_POOL + POOL_W
P_GMV = P_GMU + GMLP_W
P_HY = P_GMV + GMLP_W
P_W = P_HY + 3 * HYENA_W
P_TILE = 768

LANES = 128
VMEM_LIMIT = 52 * 1024 * 1024


def _cparams(sem):
    return pltpu.CompilerParams(dimension_semantics=sem, vmem_limit_bytes=VMEM_LIMIT)


def _tile(n, cap, mult=8):
    best = None
    for t in range(mult, min(n, cap) + 1, mult):
        if n % t == 0:
            best = t
    assert best is not None, (n, cap, mult)
    return best


def _split(x):
    hi = x.astype(BF16)
    lo = (x - hi.astype(F32)).astype(BF16)
    return hi, lo


def _dot(a, b):
    return jnp.dot(a, b, preferred_element_type=F32)


def _dot3(a_hi, a_lo, x):
    x_hi, x_lo = _split(x)
    return _dot(a_hi, x_hi) + (_dot(a_lo, x_hi) + _dot(a_hi, x_lo))


def _dot3x(x, b_hi, b_lo):
    x_hi, x_lo = _split(x)
    return _dot(x_hi, b_hi) + (_dot(x_lo, b_hi) + _dot(x_hi, b_lo))


def _sigmoid(x):
    return 1.0 / (1.0 + jnp.exp(-x))


def _silu(x):
    return x * _sigmoid(x)


def _modulated_norm(x, nw, scale, shift):
    ms = jnp.mean(x * x, axis=-1, keepdims=True)
    y = x * lax.rsqrt(ms + RMS_EPS) * nw
    return y * (1.0 + scale) + shift


def _mod_kernel(a_ref, w_ref, b_ref, o_ref):
    a = _silu(a_ref[...]).astype(BF16)
    o_ref[...] = _dot(a, w_ref[...].astype(BF16)) + b_ref[...]


def _mod_call(act, w_mod, b_mod):
    L, D, W6 = w_mod.shape
    R = act.shape[0]
    tn = 1024
    return pl.pallas_call(
        _mod_kernel,
        grid=(L, W6 // tn),
        in_specs=[
            pl.BlockSpec((R, D), lambda l, j: (0, 0)),
            pl.BlockSpec((None, D, tn), lambda l, j: (l, 0, j)),
            pl.BlockSpec((None, 1, tn), lambda l, j: (l, 0, j)),
        ],
        out_specs=pl.BlockSpec((None, R, tn), lambda l, j: (l, 0, j)),
        out_shape=jax.ShapeDtypeStruct((L, R, W6), F32),
        compiler_params=_cparams(("arbitrary", "arbitrary")),
        name="mod",
    )(act, w_mod, b_mod.reshape(L, 1, W6))


def _inproj_kernel(x_ref, sh_ref, sc_ref, nw_ref, w_ref, o_ref, h_ref):
    @pl.when(pl.program_id(2) == 0)
    def _():
        h_ref[...] = _modulated_norm(x_ref[...], nw_ref[...], sc_ref[...], sh_ref[...]).astype(BF16)

    o_ref[...] = _dot(h_ref[...], w_ref[...])


def _inproj_call(x, shift, scale, norm_w, w, tn):
    B, n, D = x.shape
    Nw = w.shape[1]
    tm = _tile(n, 1024)
    return pl.pallas_call(
        _inproj_kernel,
        grid=(B, n // tm, Nw // tn),
        in_specs=[
            pl.BlockSpec((None, tm, D), lambda b, i, j: (b, i, 0)),
            pl.BlockSpec((None, 1, D), lambda b, i, j: (b, 0, 0)),
            pl.BlockSpec((None, 1, D), lambda b, i, j: (b, 0, 0)),
            pl.BlockSpec((1, D), lambda b, i, j: (0, 0)),
            pl.BlockSpec((D, tn), lambda b, i, j: (0, j)),
        ],
        out_specs=pl.BlockSpec((None, tm, tn), lambda b, i, j: (b, i, j)),
        out_shape=jax.ShapeDtypeStruct((B, n, Nw), F32),
        scratch_shapes=[pltpu.VMEM((tm, D), BF16)],
        compiler_params=_cparams(("parallel", "parallel", "arbitrary")),
        name="inproj",
    )(x, shift, scale, norm_w.reshape(1, D), w)


def _head_norm_rope(t, w, cs, sn, bd_ref):
    sq_hi, sq_lo = _split(t * t)
    ms = _dot(sq_hi, bd_ref[...]) + _dot(sq_lo, bd_ref[...])
    y = t * lax.rsqrt(ms + RMS_EPS) * w
    lane = lax.broadcasted_iota(jnp.int32, y.shape, 1)
    swapped = jnp.where(lane % 2 == 0, pltpu.roll(y, LANES - 1, 1), pltpu.roll(y, 1, 1))
    return y * cs + swapped * sn


V_ROWS = 80
Q_PER_KV = N_Q_HEADS // N_KV_HEADS
LAG_INIT_KEYS = 128
LAG_LIMIT = 24.0


def _value_rows(vt_g):
    tail = lax.broadcasted_iota(jnp.int32, (V_ROWS - HEAD_DIM, vt_g.shape[1]), 0)
    return jnp.concatenate([vt_g, jnp.where(tail == 0, 1.0, 0.0)], axis=0).astype(BF16)


def _qkprep_kernel(q_ref, kv_ref, cs_ref, sn_ref, qw_ref, kw_ref, bd_ref, qo_ref, ko_ref, vo_ref, *, q_scale):
    cs = cs_ref[...]
    sn = sn_ref[...]
    tm = q_ref.shape[0]
    zeros = jnp.zeros((HEAD_DIM, tm), F32)
    for c in range(ATTN_W // LANES):
        sl = slice(c * LANES, (c + 1) * LANES)
        r = _head_norm_rope(q_ref[:, sl], qw_ref[:, sl], cs, sn, bd_ref) * q_scale
        rt = r.T
        for half in range(LANES // HEAD_DIM):
            h = c * (LANES // HEAD_DIM) + half
            blk = rt[half * HEAD_DIM:(half + 1) * HEAD_DIM]
            parts = [blk if g == h // Q_PER_KV else zeros for g in range(N_KV_HEADS)]
            qo_ref[h] = jnp.concatenate(parts, axis=0).astype(BF16)
    ko_ref[...] = _head_norm_rope(kv_ref[:, :KV_W], kw_ref[...], cs, sn, bd_ref).astype(BF16)
    vt = kv_ref[:, KV_W:].T
    for g in range(N_KV_HEADS):
        vo_ref[g] = _value_rows(vt[g * HEAD_DIM:(g + 1) * HEAD_DIM])


def _qkprep_call(p, cs, sn, qw, kw, q_scale):
    B, n, _ = p.shape
    tm = _tile(n, 512, LANES)
    bd = np.kron(np.eye(LANES // HEAD_DIM), np.full((HEAD_DIM, HEAD_DIM), 1.0 / HEAD_DIM))
    bd = jnp.asarray(bd, BF16)
    return pl.pallas_call(
        functools.partial(_qkprep_kernel, q_scale=q_scale),
        grid=(n // tm, B),
        in_specs=[
            pl.BlockSpec((None, tm, ATTN_W), lambda i, b: (b, i, P_Q // ATTN_W)),
            pl.BlockSpec((None, tm, 2 * KV_W), lambda i, b: (b, i, P_KV // (2 * KV_W))),
            pl.BlockSpec((tm, LANES), lambda i, b: (i, 0)),
            pl.BlockSpec((tm, LANES), lambda i, b: (i, 0)),
            pl.BlockSpec((1, ATTN_W), lambda i, b: (0, 0)),
            pl.BlockSpec((1, KV_W), lambda i, b: (0, 0)),
            pl.BlockSpec((LANES, LANES), lambda i, b: (0, 0)),
        ],
        out_specs=[
            pl.BlockSpec((None, N_Q_HEADS, KV_W, tm), lambda i, b: (b, 0, 0, i)),
            pl.BlockSpec((None, tm, KV_W), lambda i, b: (b, i, 0)),
            pl.BlockSpec((None, N_KV_HEADS, V_ROWS, tm), lambda i, b: (b, 0, 0, i)),
        ],
        out_shape=[
            jax.ShapeDtypeStruct((B, N_Q_HEADS, KV_W, n), BF16),
            jax.ShapeDtypeStruct((B, n, KV_W), BF16),
            jax.ShapeDtypeStruct((B, N_KV_HEADS, V_ROWS, n), BF16),
        ],
        compiler_params=_cparams(("parallel", "parallel")),
        name="qkprep",
    )(p, p, cs, sn, qw, kw, bd)


def _attn_kernel(qt_ref, k_ref, vt_ref, o_ref, m_ref, acc_ref, *, tk):
    nk = vt_ref.shape[0]
    tq = m_ref.shape[-1]

    def keys(j):
        return k_ref[pl.ds(pl.multiple_of(j * tk, tk), tk), :]

    def lagged_chunk(j, jump):
        kb = keys(j)
        s_next = _dot(kb, qt_ref[0])
        for h in range(N_Q_HEADS):
            s = s_next
            if h + 1 < N_Q_HEADS:
                s_next = _dot(kb, qt_ref[h + 1])
            m_old = m_ref[h]
            p = jnp.exp2(s - m_old).astype(BF16)
            cmax = jnp.max(s, axis=0, keepdims=True)
            m_new = jnp.maximum(m_old, cmax)
            jump = jnp.maximum(jump, cmax - m_old)
            acc_ref[h] = jnp.exp2(m_old - m_new) * (acc_ref[h] + _dot(vt_ref[j, h // Q_PER_KV], p))
            m_ref[h] = m_new
        return jump

    def exact_chunk(j, carry):
        kb = keys(j)
        for h in range(N_Q_HEADS):
            s = _dot(kb, qt_ref[h])
            m_old = m_ref[h]
            m_new = jnp.maximum(m_old, jnp.max(s, axis=0, keepdims=True))
            p = jnp.exp2(s - m_new).astype(BF16)
            acc_ref[h] = jnp.exp2(m_old - m_new) * acc_ref[h] + _dot(vt_ref[j, h // Q_PER_KV], p)
            m_ref[h] = m_new
        return carry

    k0 = k_ref[0:LAG_INIT_KEYS, :]
    for h in range(N_Q_HEADS):
        m_ref[h] = jnp.max(_dot(k0, qt_ref[h]), axis=0, keepdims=True)
    acc_ref[...] = jnp.zeros(acc_ref.shape, F32)
    jump = lax.fori_loop(0, nk, lagged_chunk, jnp.zeros((1, tq), F32))

    @pl.when(jnp.logical_not(jnp.max(jump) <= LAG_LIMIT))
    def _():
        m_ref[...] = jnp.full(m_ref.shape, -jnp.inf, F32)
        acc_ref[...] = jnp.zeros(acc_ref.shape, F32)
        lax.fori_loop(0, nk, exact_chunk, 0)

    for c in range(ATTN_W // LANES):
        heads = []
        for h in range(c * (LANES // HEAD_DIM), (c + 1) * (LANES // HEAD_DIM)):
            a = acc_ref[h]
            heads.append(a[:HEAD_DIM] / a[HEAD_DIM:HEAD_DIM + 1])
        o_ref[:, c * LANES:(c + 1) * LANES] = jnp.concatenate(heads, axis=0).T.astype(BF16)


def _attn_call(qt, k, vt):
    B, _, _, n = qt.shape
    Lk = k.shape[1]
    tq = _tile(n, 512, LANES)
    tk = _tile(Lk, 768, LANES)
    nk = Lk // tk
    vt = vt.reshape(B, N_KV_HEADS, V_ROWS, nk, tk).transpose(0, 3, 1, 2, 4)
    return pl.pallas_call(
        functools.partial(_attn_kernel, tk=tk),
        grid=(B, n // tq),
        in_specs=[
            pl.BlockSpec((None, N_Q_HEADS, KV_W, tq), lambda b, i: (b, 0, 0, i)),
            pl.BlockSpec((None, Lk, KV_W), lambda b, i: (b, 0, 0)),
            pl.BlockSpec((None, nk, N_KV_HEADS, V_ROWS, tk), lambda b, i: (b, 0, 0, 0, 0)),
        ],
        out_specs=pl.BlockSpec((None, tq, ATTN_W), lambda b, i: (b, i, 0)),
        out_shape=jax.ShapeDtypeStruct((B, n, ATTN_W), BF16),
        scratch_shapes=[
            pltpu.VMEM((N_Q_HEADS, 1, tq), F32),
            pltpu.VMEM((N_Q_HEADS, V_ROWS, tq), F32),
        ],
        compiler_params=_cparams(("parallel", "parallel")),
        name="attn",
    )(qt, k, vt)


HALVES = HYENA_W // LANES


def _split_shape(B, n):
    assert B % 2 == 0
    return jax.ShapeDtypeStruct((B // 2, HALVES, 2, n, LANES), F32)


def _split_spec(tm):
    return pl.BlockSpec((None, HALVES, None, tm, LANES), lambda b, i: (b // 2, 0, b % 2, i, 0))


def _shift_rows(cur, prev_row, next_row, rows):
    tm = cur.shape[0]
    up = jnp.where(rows == 0, prev_row, pltpu.roll(cur, 1, 0))
    dn = jnp.where(rows == tm - 1, next_row, pltpu.roll(cur, tm - 1, 0))
    return up, dn


def _local_kernel(hy_ref, hyp_ref, hyn_ref, po_ref, pop_ref, pon_ref, gu_ref, gv_ref,
                  cw_ref, cb_ref, pw_ref, ps_ref, gnw_ref, gws_ref, gb_ref,
                  v_ref, x1_ref, x2_ref, pl_ref, gm_ref, *, n_seq):
    i = pl.program_id(1)
    last = pl.num_programs(1) - 1
    tm = hy_ref.shape[0]
    has_prev = (i > 0).astype(F32)
    has_next = (i < last).astype(F32)

    cur = hy_ref[...]
    rows = lax.broadcasted_iota(jnp.int32, cur.shape, 0)
    prev_row = hyp_ref[POOL_HALO - 1:POOL_HALO, :] * has_prev
    next_row = hyn_ref[0:1, :] * has_next
    up, dn = _shift_rows(cur, prev_row, next_row, rows)
    z = up * cw_ref[0:1, :] + cur * cw_ref[1:2, :] + dn * cw_ref[2:3, :] + cb_ref[...]
    for s, o_ref in enumerate((v_ref, x1_ref, x2_ref)):
        for hf in range(HYENA_W // LANES):
            lo = s * HYENA_W + hf * LANES
            o_ref[hf] = z[:, lo:lo + LANES]

    zc = po_ref[...]
    ext = jnp.concatenate([pop_ref[...] * has_prev, zc, pon_ref[...] * has_next], axis=0)
    te = tm + 2 * POOL_HALO
    lane = lax.broadcasted_iota(jnp.int32, zc.shape, 1)
    prow = lax.broadcasted_iota(jnp.int32, zc.shape, 0) + i * tm
    half = jnp.left_shift(1, lane // POOL_GROUP_W)
    acc = jnp.zeros(zc.shape, F32)
    for d in range(-POOL_HALO, POOL_HALO):
        sh = pltpu.roll(ext, (te - (POOL_HALO + d)) % te, 0)[:tm]
        inside = (d >= -half) & (d < half)
        acc = acc + jnp.where(inside, sh, 0.0)
    cnt = jnp.minimum(prow + half, n_seq) - jnp.maximum(prow - half, 0)
    dlt = acc / cnt.astype(F32) - zc
    pl_ref[...] = (_dot(dlt.astype(BF16), pw_ref[...]) * ps_ref[...]).astype(BF16)

    gv = gv_ref[...]
    mu = jnp.mean(gv, axis=-1, keepdims=True)
    ctr = gv - mu
    var = jnp.mean(ctr * ctr, axis=-1, keepdims=True)
    vn = (ctr * lax.rsqrt(var + RMS_EPS) * gnw_ref[...]).astype(BF16)
    group = lax.broadcasted_iota(jnp.int32, (GMLP_CHUNK, GMLP_W), 1) // GMLP_GROUP_W
    for c in range(tm // GMLP_CHUNK):
        rs = slice(c * GMLP_CHUNK, (c + 1) * GMLP_CHUNK)
        vc = vn[rs]
        mixed = gb_ref[...]
        for g in range(GMLP_GROUPS):
            mg = _dot(gws_ref[g], vc)
            mixed = mixed + jnp.where(group == g, mg, 0.0)
        gm_ref[rs, :] = (gu_ref[rs, :] * mixed).astype(BF16)


def _local_call(p, lw):
    B, n, _ = p.shape
    tm = _tile(n, 512, GMLP_CHUNK)
    nb8 = n // POOL_HALO
    r8 = tm // POOL_HALO
    HW = 3 * HYENA_W

    def cur(w, off):
        return pl.BlockSpec((None, tm, w), lambda b, i: (b, i, off // w))

    def prev(w, off):
        return pl.BlockSpec((None, POOL_HALO, w), lambda b, i: (b, jnp.maximum(i * r8 - 1, 0), off // w))

    def nxt(w, off):
        return pl.BlockSpec((None, POOL_HALO, w), lambda b, i: (b, jnp.minimum((i + 1) * r8, nb8 - 1), off // w))

    def full(shape):
        return pl.BlockSpec(shape, lambda b, i: (0,) * len(shape))

    def out(w):
        return pl.BlockSpec((None, tm, w), lambda b, i: (b, i, 0))

    return pl.pallas_call(
        functools.partial(_local_kernel, n_seq=n),
        grid=(B, n // tm),
        in_specs=[
            cur(HW, P_HY), prev(HW, P_HY), nxt(HW, P_HY),
            cur(POOL_W, P_POOL), prev(POOL_W, P_POOL), nxt(POOL_W, P_POOL),
            cur(GMLP_W, P_GMU), cur(GMLP_W, P_GMV),
            full((3, HW)), full((1, HW)), full((POOL_W, POOL_W)), full((1, POOL_W)),
            full((1, GMLP_W)), full((GMLP_GROUPS, GMLP_CHUNK, GMLP_CHUNK)), full((GMLP_CHUNK, GMLP_W)),
        ],
        out_specs=[_split_spec(tm), _split_spec(tm), _split_spec(tm), out(POOL_W), out(GMLP_W)],
        out_shape=[
            _split_shape(B, n), _split_shape(B, n), _split_shape(B, n),
            jax.ShapeDtypeStruct((B, n, POOL_W), BF16),
            jax.ShapeDtypeStruct((B, n, GMLP_W), BF16),
        ],
        compiler_params=_cparams(("parallel", "parallel")),
        name="local",
    )(p, p, p, p, p, p, p, p, lw["conv_w"], lw["conv_b"], lw["pool_bd"], lw["pool_scale"],
      lw["gm_norm_w"], lw["gm_ws"], lw["gm_bias"])


def _filt_kernel(z_ref, t_ref, w1h_ref, w1l_ref, b1_ref, w2h_ref, w2l_ref, b2_ref, fr_ref,
                 w3h_ref, w3l_ref, dec_ref, k_ref, s_ref, *, n_seq):
    i = pl.program_id(0)
    tb = z_ref.shape[0]
    fr = fr_ref[...]
    hdn = jnp.sin(fr * (_dot3x(z_ref[...], w1h_ref[...], w1l_ref[...]) + b1_ref[...]))
    for u in range(HYENA_INNER):
        hdn = jnp.sin(fr * (_dot3x(hdn, w2h_ref[u], w2l_ref[u]) + b2_ref[u:u + 1, :]))
    t = t_ref[...]
    tt = jnp.concatenate([t] * (HYENA_ORDER * HYENA_W // LANES), axis=1)
    window = jnp.exp(-tt * jnp.abs(dec_ref[...]))
    k = _dot3x(hdn, w3h_ref[...], w3l_ref[...]) * window
    rows = lax.broadcasted_iota(jnp.int32, k.shape, 0) + i * tb
    k = jnp.where(rows == n_seq, 0.0, k)
    for q in range(k_ref.shape[0]):
        k_ref[q] = k[:, q * LANES:(q + 1) * LANES]

    @pl.when(i == 0)
    def _():
        s_ref[...] = jnp.zeros(s_ref.shape, F32)

    s_ref[...] += jnp.sum(jnp.abs(k), axis=0, keepdims=True)


def _filt_call(n, lw):
    N = 2 * n
    tb = _tile(n, 512)
    nb_half = n // tb
    pos = np.arange(N)
    pos = np.where(pos < n, pos, N - pos).astype(np.float64)
    t = pos / (n - 1)
    bands = np.linspace(1e-4, HYENA_BANDS - 1, HYENA_BANDS)
    ang = (2.0 * math.pi / n) * pos[:, None] * bands
    z = np.concatenate([t[:, None], np.cos(ang), np.sin(ang)], axis=-1)
    zp = np.zeros((N, LANES), np.float32)
    zp[:, :z.shape[1]] = z
    tl = np.broadcast_to(t[:, None], (N, LANES)).astype(np.float32)
    CW = HYENA_ORDER * HYENA_W

    def full(shape):
        return pl.BlockSpec(shape, lambda i: (0,) * len(shape))

    def side(rows_):
        return pl.BlockSpec((None, rows_, CW), lambda i: (i // nb_half, 0, 0))

    return pl.pallas_call(
        functools.partial(_filt_kernel, n_seq=n),
        grid=(N // tb,),
        in_specs=[
            pl.BlockSpec((tb, LANES), lambda i: (i, 0)),
            pl.BlockSpec((tb, LANES), lambda i: (i, 0)),
            full((LANES, HYENA_HIDDEN)), full((LANES, HYENA_HIDDEN)), full((1, HYENA_HIDDEN)),
            full((HYENA_INNER, HYENA_HIDDEN, HYENA_HIDDEN)), full((HYENA_INNER, HYENA_HIDDEN, HYENA_HIDDEN)),
            full((HYENA_INNER, HYENA_HIDDEN)), full((1, HYENA_HIDDEN)),
            side(HYENA_HIDDEN), side(HYENA_HIDDEN), side(1),
        ],
        out_specs=[pl.BlockSpec((CW // LANES, tb, LANES), lambda i: (0, i, 0)), pl.BlockSpec((1, CW), lambda i: (0, 0))],
        out_shape=[jax.ShapeDtypeStruct((CW // LANES, N, LANES), F32), jax.ShapeDtypeStruct((1, CW), F32)],
        compiler_params=_cparams(("arbitrary",)),
        name="filt",
    )(jnp.asarray(zp), jnp.asarray(tl), lw["hy_w1h"], lw["hy_w1l"], lw["hy_b1"], lw["hy_w2h"], lw["hy_w2l"],
      lw["hy_b2"], lw["hy_freq"], lw["hy_w3h"], lw["hy_w3l"], lw["hy_decay"])


def _fft_dims(N):
    e = int(round(math.log2(N)))
    assert 2 ** e == N
    N1 = 2 ** ((e + 1) // 2)
    return N1, N // N1


def _hilo(a):
    a = np.asarray(a, np.float64)
    hi = jnp.asarray(a, F32).astype(BF16)
    lo = (jnp.asarray(a, F32) - hi.astype(F32)).astype(BF16)
    return hi, lo


@functools.lru_cache(maxsize=None)
def _fft_consts_np(N):
    N1, N2 = _fft_dims(N)
    h = N1 // 2
    k1 = np.arange(N1)
    F1 = np.exp(-2j * np.pi * np.outer(k1, k1) / N1)
    k2 = np.arange(N2)
    F2 = np.exp(-2j * np.pi * np.outer(k2, k2) / N2)
    tw = np.exp(-2j * np.pi * np.outer(k1, k2) / N)
    g1 = np.zeros((2 * N1, N1))
    g1[0::2, :h] = F1.real[:, :h]
    g1[0::2, h:] = -F1.imag[:, :h]
    g1[1::2, :h] = F1.imag[:, :h]
    g1[1::2, h:] = F1.real[:, :h]
    g1f = np.zeros((2 * N1, N1))
    g1f[0::2] = F1.real
    g1f[1::2] = F1.imag
    g2 = np.block([[F2.real, -F2.imag], [F2.imag, F2.real]])
    g2c = np.block([[F2.real, F2.imag], [-F2.imag, F2.real]])
    g3 = np.zeros((N1, 2 * N1))
    g3[:h, 0::2] = F1.real[:h]
    g3[:h, 1::2] = F1.imag[:h]
    g3[h:, 0::2] = -F1.imag[:h]
    g3[h:, 1::2] = F1.real[:h]
    g3 /= N
    twt = np.stack([tw.real, tw.imag], axis=1)
    return dict(g1=g1, g1f=g1f, g2=g2, g2c=g2c, g3=g3, tw=twt)


def _fft_consts(N):
    c = _fft_consts_np(N)
    out = {k: _hilo(c[k]) for k in ("g1", "g1f", "g2", "g2c", "g3")}
    N1, N2 = _fft_dims(N)
    tw = jnp.asarray(c["tw"], F32)
    out["tw"] = jnp.broadcast_to(tw[..., None], (N1, 2, N2, LANES))
    return out


T2_BLOCK = 8


def _load_8th(ref, j, rows):
    return ref.reshape(rows * T2_BLOCK, LANES)[pl.ds(j, rows, stride=T2_BLOCK), :]


def _store_8th(ref, j, rows, val):
    ref.reshape(rows * T2_BLOCK, LANES)[pl.ds(j, rows, stride=T2_BLOCK), :] = val


def _dft_in_kernel(gh_ref, gl_ref, x_ref, o_ref):
    N1 = x_ref.shape[1]
    for j in range(T2_BLOCK):
        xj = jnp.concatenate([_load_8th(x_ref.at[hf], j, N1) for hf in range(HALVES)], axis=1)
        a = _dot3(gh_ref[...], gl_ref[...], xj)
        for hf in range(HALVES):
            _store_8th(o_ref.at[hf], j, 2 * N1, a[:, hf * LANES:(hf + 1) * LANES])


def _dft_in_call(g, u5):
    gh, gl = g
    G, _, N1, N2, _ = u5.shape
    assert gh.shape == (2 * N1, N1)
    mat = pl.BlockSpec((2 * N1, N1), lambda b, i: (0, 0))
    return pl.pallas_call(
        _dft_in_kernel,
        grid=(G, N2 // T2_BLOCK),
        in_specs=[mat, mat, pl.BlockSpec((None, HALVES, N1, T2_BLOCK, LANES), lambda b, i: (b, 0, 0, i, 0))],
        out_specs=pl.BlockSpec((None, HALVES, N1, 2, T2_BLOCK, LANES), lambda b, i: (b, 0, 0, 0, i, 0)),
        out_shape=jax.ShapeDtypeStruct((G, HALVES, N1, 2, N2, LANES), F32),
        compiler_params=_cparams(("parallel", "parallel")),
        name="dft_in",
    )(gh, gl, u5)


def _dft_out_kernel(gh_ref, gl_ref, d_ref, gate_ref, u_ref, sk_ref, o_ref):
    N1 = gate_ref.shape[1]
    for j in range(T2_BLOCK):
        dj = jnp.concatenate([_load_8th(d_ref.at[hf], j, 2 * N1) for hf in range(HALVES)], axis=1)
        y = _dot3(gh_ref[...], gl_ref[...], dj)
        for hf in range(HALVES):
            conv = y[:, hf * LANES:(hf + 1) * LANES]
            u = _load_8th(u_ref.at[hf], j, N1)
            _store_8th(o_ref.at[hf], j, N1, _load_8th(gate_ref.at[hf], j, N1) * (conv + u * sk_ref[hf]))


def _dft_out_call(g, d6, gate5, u5, skip):
    gh, gl = g
    G, _, N1, N2, _ = u5.shape
    assert gh.shape == (N1, 2 * N1)
    mat = pl.BlockSpec((N1, 2 * N1), lambda b, i: (0, 0))
    blk = pl.BlockSpec((None, HALVES, N1, T2_BLOCK, LANES), lambda b, i: (b, 0, 0, i, 0))
    return pl.pallas_call(
        _dft_out_kernel,
        grid=(G, N2 // T2_BLOCK),
        in_specs=[mat, mat, pl.BlockSpec((None, HALVES, N1, 2, T2_BLOCK, LANES), lambda b, i: (b, 0, 0, 0, i, 0)),
                  blk, blk, pl.BlockSpec((HALVES, 1, LANES), lambda b, i: (0, 0, 0))],
        out_specs=blk,
        out_shape=jax.ShapeDtypeStruct(u5.shape, F32),
        compiler_params=_cparams(("parallel", "parallel")),
        name="dft_out",
    )(gh, gl, d6, gate5, u5, skip)


def _cmul(ar, ai, br, bi):
    return ar * br - ai * bi, ar * bi + ai * br


def _lane_tile(t, c):
    return jnp.concatenate([t] * (c // LANES), axis=-1)


def _both_halves(a_ref, k, part):
    return jnp.concatenate([a_ref[hf, k, part] for hf in range(HALVES)], axis=-1)


def _spec_kernel(a_ref, tw_ref, g2h_ref, g2l_ref, s_ref, o_ref):
    _, kb, _, N2, _ = a_ref.shape
    inv = 1.0 / s_ref[...]
    for k in range(kb):
        twr = _lane_tile(tw_ref[k, 0], HYENA_W)
        twi = _lane_tile(tw_ref[k, 1], HYENA_W)
        br, bi = _cmul(_both_halves(a_ref, k, 0), _both_halves(a_ref, k, 1), twr, twi)
        zz = _dot3(g2h_ref[...], g2l_ref[...], jnp.concatenate([br, bi], axis=0))
        o_ref[k, 0] = zz[:N2] * inv
        o_ref[k, 1] = zz[N2:] * inv


def _conv_kernel(a_ref, tw_ref, kf_ref, g2h_ref, g2l_ref, g2ch_ref, g2cl_ref, o_ref):
    _, kb, _, N2, _ = a_ref.shape
    for k in range(kb):
        twr = _lane_tile(tw_ref[k, 0], HYENA_W)
        twi = _lane_tile(tw_ref[k, 1], HYENA_W)
        br, bi = _cmul(_both_halves(a_ref, k, 0), _both_halves(a_ref, k, 1), twr, twi)
        zz = _dot3(g2h_ref[...], g2l_ref[...], jnp.concatenate([br, bi], axis=0))
        yr, yi = _cmul(zz[:N2], zz[N2:], kf_ref[k, 0], kf_ref[k, 1])
        cc = _dot3(g2ch_ref[...], g2cl_ref[...], jnp.concatenate([yr, yi], axis=0))
        dr, di = _cmul(cc[:N2], cc[N2:], twr, -twi)
        for hf in range(HALVES):
            o_ref[hf, k, 0] = dr[:, hf * LANES:(hf + 1) * LANES]
            o_ref[hf, k, 1] = di[:, hf * LANES:(hf + 1) * LANES]


def _spec_call(a6, fc, s):
    G, _, N1, _, N2, _ = a6.shape
    kb = _tile(N1, 8, 1)
    mat = pl.BlockSpec((2 * N2, 2 * N2), lambda i, b: (0, 0))
    return pl.pallas_call(
        _spec_kernel,
        grid=(N1 // kb, G),
        in_specs=[pl.BlockSpec((None, HALVES, kb, 2, N2, LANES), lambda i, b: (b, 0, i, 0, 0, 0)),
                  pl.BlockSpec((kb, 2, N2, LANES), lambda i, b: (i, 0, 0, 0)), mat, mat,
                  pl.BlockSpec((None, 1, HYENA_W), lambda i, b: (b, 0, 0))],
        out_specs=pl.BlockSpec((None, kb, 2, N2, HYENA_W), lambda i, b: (b, i, 0, 0, 0)),
        out_shape=jax.ShapeDtypeStruct((G, N1, 2, N2, HYENA_W), F32),
        compiler_params=_cparams(("parallel", "parallel")),
        name="filt_spec",
    )(a6, fc["tw"], fc["g2"][0], fc["g2"][1], s)


def _conv_call(a6, kf, fc):
    G, _, N1, _, N2, _ = a6.shape
    kb = _tile(N1, 8, 1)
    blk = pl.BlockSpec((None, HALVES, kb, 2, N2, LANES), lambda i, b: (b, 0, i, 0, 0, 0))
    mat = pl.BlockSpec((2 * N2, 2 * N2), lambda i, b: (0, 0))
    return pl.pallas_call(
        _conv_kernel,
        grid=(N1 // kb, G),
        in_specs=[blk, pl.BlockSpec((kb, 2, N2, LANES), lambda i, b: (i, 0, 0, 0)),
                  pl.BlockSpec((kb, 2, N2, HYENA_W), lambda i, b: (i, 0, 0, 0)), mat, mat, mat, mat],
        out_specs=blk,
        out_shape=jax.ShapeDtypeStruct(a6.shape, F32),
        compiler_params=_cparams(("parallel", "parallel")),
        name="spec_conv",
    )(a6, fc["tw"], kf, fc["g2"][0], fc["g2"][1], fc["g2c"][0], fc["g2c"][1])


def _filter_spectra(n, lw, fc):
    N1, N2 = _fft_dims(2 * n)
    k, s = _filt_call(n, lw)
    a6 = _dft_in_call(fc["g1f"], k.reshape(HYENA_ORDER, HALVES, N1, N2, LANES))
    return _spec_call(a6, fc, s.reshape(HYENA_ORDER, 1, HYENA_W))


def _hyena(v, x1, x2, kf, lw, fc):
    P, _, _, n, _ = v.shape
    N1, N2 = _fft_dims(2 * n)
    shape5 = (P, HALVES, N1, N2, LANES)
    y = v.reshape(shape5)
    for o, gate in enumerate((x1, x2)):
        d = _conv_call(_dft_in_call(fc["g1"], y), kf[o], fc)
        y = _dft_out_call(fc["g3"], d, gate.reshape(shape5), y, lw["hy_skip"][o].reshape(HALVES, 1, LANES))
    return y.reshape(v.shape)


def _merge_kernel(at_ref, hy_ref, po_ref, gm_ref, x_ref, sh_ref, sc_ref, nw_ref, gr_ref,
                  wgate_ref, wa_ref, wh_ref, wp_ref, wg_ref, wo_ref, o_ref):
    x = x_ref[...]
    D = x.shape[1]
    h = _modulated_norm(x, nw_ref[...], sc_ref[...], sh_ref[...]).astype(BF16)
    hy = jnp.concatenate([hy_ref[hf] for hf in range(HALVES)], axis=-1).astype(BF16)
    branches = ((at_ref[...], wa_ref), (hy, wh_ref), (po_ref[...], wp_ref), (gm_ref[...], wg_ref))
    merged = None
    for i, (br, w_ref) in enumerate(branches):
        term = _sigmoid(_dot(h, wgate_ref[:, i * D:(i + 1) * D])) * _dot(br, w_ref[...])
        merged = term if merged is None else merged + term
    o_ref[...] = x + gr_ref[...] * _dot(merged.astype(BF16), wo_ref[...])


def _merge_call(attn, hy, po, gm, x, shift, scale, norm_w, gate_res, lw):
    B, n, D = x.shape
    tm = _tile(n, 512)

    def rows(w):
        return pl.BlockSpec((None, tm, w), lambda b, i: (b, i, 0))

    def full(shape):
        return pl.BlockSpec(shape, lambda b, i: (0,) * len(shape))

    vec = pl.BlockSpec((None, 1, D), lambda b, i: (b, 0, 0))
    return pl.pallas_call(
        _merge_kernel,
        grid=(B, n // tm),
        in_specs=[
            rows(ATTN_W), _split_spec(tm), rows(POOL_W), rows(GMLP_W), rows(D),
            vec, vec, full((1, D)), vec,
            full((D, N_BRANCH * D)),
            full((ATTN_W, D)), full((HYENA_W, D)), full((POOL_W, D)), full((GMLP_W, D)), full((D, D)),
        ],
        out_specs=rows(D),
        out_shape=jax.ShapeDtypeStruct((B, n, D), F32),
        compiler_params=_cparams(("parallel", "parallel")),
        name="merge",
    )(attn, hy, po, gm, x, shift, scale, norm_w.reshape(1, D), gate_res, lw["w_gate"], lw["w_br_attn"],
      lw["w_br_hyena"], lw["w_br_pool"], lw["w_br_gmlp"], lw["w_out"])


def _ffn_kernel(x_ref, sh_ref, sc_ref, nw_ref, gr_ref, wg_ref, wu_ref, wd_ref, o_ref, h_ref, acc_ref):
    j = pl.program_id(2)

    @pl.when(j == 0)
    def _():
        h_ref[...] = _modulated_norm(x_ref[...], nw_ref[...], sc_ref[...], sh_ref[...]).astype(BF16)
        acc_ref[...] = jnp.zeros(acc_ref.shape, F32)

    h = h_ref[...]
    a = _silu(_dot(h, wg_ref[...])) * _dot(h, wu_ref[...])
    acc_ref[...] += _dot(a.astype(BF16), wd_ref[...])

    @pl.when(j == pl.num_programs(2) - 1)
    def _():
        o_ref[...] = x_ref[...] + gr_ref[...] * acc_ref[...]


def _ffn_call(x, shift, scale, norm_w, gate_res, wg, wu, wd):
    B, n, D = x.shape
    H = wg.shape[1]
    tm = _tile(n, 1024)
    th = 256
    vec = pl.BlockSpec((None, 1, D), lambda b, i, j: (b, 0, 0))
    return pl.pallas_call(
        _ffn_kernel,
        grid=(B, n // tm, H // th),
        in_specs=[
            pl.BlockSpec((None, tm, D), lambda b, i, j: (b, i, 0)),
            vec, vec, pl.BlockSpec((1, D), lambda b, i, j: (0, 0)), vec,
            pl.BlockSpec((D, th), lambda b, i, j: (0, j)),
            pl.BlockSpec((D, th), lambda b, i, j: (0, j)),
            pl.BlockSpec((th, D), lambda b, i, j: (j, 0)),
        ],
        out_specs=pl.BlockSpec((None, tm, D), lambda b, i, j: (b, i, 0)),
        out_shape=jax.ShapeDtypeStruct((B, n, D), F32),
        scratch_shapes=[pltpu.VMEM((tm, D), BF16), pltpu.VMEM((tm, D), F32)],
        compiler_params=_cparams(("parallel", "parallel", "arbitrary")),
        name="ffn",
    )(x, shift, scale, norm_w.reshape(1, D), gate_res, wg, wu, wd)


def _rms_kernel(x_ref, w_ref, o_ref):
    x = x_ref[...]
    ms = jnp.mean(x * x, axis=-1, keepdims=True)
    o_ref[...] = x * lax.rsqrt(ms + RMS_EPS) * w_ref[...]


def _rms_call(x, w):
    B, n, D = x.shape
    tm = _tile(n, 1024)
    blk = pl.BlockSpec((None, tm, D), lambda b, i: (b, i, 0))
    return pl.pallas_call(
        _rms_kernel,
        grid=(B, n // tm),
        in_specs=[blk, pl.BlockSpec((1, D), lambda b, i: (0, 0))],
        out_specs=blk,
        out_shape=jax.ShapeDtypeStruct((B, n, D), F32),
        compiler_params=_cparams(("parallel", "parallel")),
        name="final_norm",
    )(x, w.reshape(1, D))


def _rope_tables(n):
    tok = np.arange(n)
    axis_dim = HEAD_DIM // 2
    inv_freq = ROPE_THETA ** (-np.arange(0, axis_dim, 2, dtype=np.float64) / axis_dim)
    inv_freq = inv_freq.astype(np.float32).astype(np.float64)
    ang = np.concatenate([(tok // GRID_W)[:, None] * inv_freq, (tok % GRID_W)[:, None] * inv_freq], axis=-1)
    ang = ang.astype(np.float32)
    cos = np.repeat(np.cos(ang), 2, axis=-1)
    sin = np.repeat(np.sin(ang), 2, axis=-1)
    sin[:, 0::2] *= -1.0
    reps = LANES // HEAD_DIM
    return jnp.asarray(np.tile(cos, (1, reps)), F32), jnp.asarray(np.tile(sin, (1, reps)), F32)


def _identity_tables(n):
    return jnp.ones((n, LANES), F32), jnp.zeros((n, LANES), F32)


def _layer_weights(l, a):
    w_in = a["w_in"][l]
    w_perm = jnp.concatenate([
        w_in[:, Q_OFF:K_OFF], w_in[:, K_OFF:HY_OFF], w_in[:, POOL_OFF:GM_OFF], w_in[:, GM_OFF:GATE_OFF],
        w_in[:, HY_OFF:POOL_OFF]], axis=1).astype(BF16)
    w3 = a["hy_w3"][l].reshape(HYENA_HIDDEN, HYENA_ORDER, 2, HYENA_W).transpose(2, 0, 1, 3)
    w3 = w3.reshape(2, HYENA_HIDDEN, HYENA_ORDER * HYENA_W)
    dec = a["hy_decay"][l].reshape(HYENA_ORDER, 2, HYENA_W).transpose(1, 0, 2).reshape(2, 1, HYENA_ORDER * HYENA_W)
    w1 = jnp.zeros((LANES, HYENA_HIDDEN), F32).at[:a["hy_w1"].shape[1]].set(a["hy_w1"][l])
    w1h, w1l = _split(w1)
    w2h, w2l = _split(a["hy_w2"][l])
    w3h, w3l = _split(w3)
    eye = jnp.eye(len(POOL_WINDOWS), dtype=F32)
    pool_bd = jnp.einsum("gh,gcd->gchd", eye, a["pool_w"][l]).reshape(POOL_W, POOL_W).astype(BF16)
    gm_bias = jnp.repeat(a["gm_bs"][l].T, GMLP_GROUP_W, axis=1)
    return dict(
        w_in=w_perm, w_gate=w_in[:, GATE_OFF:].astype(BF16),
        q_norm_w=jnp.tile(a["q_norm_w"][l], N_Q_HEADS).reshape(1, ATTN_W),
        k_norm_w=jnp.tile(a["k_norm_w"][l], N_KV_HEADS).reshape(1, KV_W),
        conv_w=a["hy_conv_w"][l], conv_b=a["hy_conv_b"][l].reshape(1, -1),
        hy_w1h=w1h, hy_w1l=w1l, hy_b1=a["hy_b1"][l].reshape(1, -1), hy_w2h=w2h, hy_w2l=w2l, hy_b2=a["hy_b2"][l],
        hy_w3h=w3h, hy_w3l=w3l, hy_freq=a["hy_freq"][l].reshape(1, -1), hy_decay=dec, hy_skip=a["hy_skip"][l],
        pool_bd=pool_bd, pool_scale=a["pool_scale"][l].reshape(1, -1),
        gm_norm_w=a["gm_norm_w"][l].reshape(1, -1), gm_ws=a["gm_ws"][l].astype(BF16), gm_bias=gm_bias,
        w_br_attn=a["w_br_attn"][l].astype(BF16), w_br_hyena=a["w_br_hyena"][l].astype(BF16),
        w_br_pool=a["w_br_pool"][l].astype(BF16), w_br_gmlp=a["w_br_gmlp"][l].astype(BF16),
        w_out=a["w_out"][l].astype(BF16),
        ffn_w_gate=a["ffn_w_gate"][l].astype(BF16), ffn_w_up=a["ffn_w_up"][l].astype(BF16),
        ffn_w_down=a["ffn_w_down"][l].astype(BF16),
        norm1_w=a["norm1_w"][l], norm2_w=a["norm2_w"][l],
    )


Q_SCALE = (HEAD_DIM ** -0.5) * math.log2(math.e)


def _mixer_block(x, mods, lw, rope, k_ctx, v_ctx, kf, fc):
    sh1, s1, g1, sh2, s2, g2 = mods
    p = _inproj_call(x, sh1, s1, lw["norm1_w"], lw["w_in"], P_TILE)
    q, k, v = _qkprep_call(p, rope[0], rope[1], lw["q_norm_w"], lw["k_norm_w"], Q_SCALE)
    if k_ctx is not None:
        k_all = jnp.concatenate([k_ctx, k], axis=1)
        v_all = jnp.concatenate([v_ctx, v], axis=3)
    else:
        k_all, v_all = k, v
    attn = _attn_call(q, k_all, v_all)
    hv, hx1, hx2, po, gm = _local_call(p, lw)
    hy = _hyena(hv, hx1, hx2, kf, lw, fc)
    x = _merge_call(attn, hy, po, gm, x, sh1, s1, lw["norm1_w"], g1, lw)
    x = _ffn_call(x, sh2, s2, lw["norm2_w"], g2, lw["ffn_w_gate"], lw["ffn_w_up"], lw["ffn_w_down"])
    return x, k, v


def kernel(x, c, ctx, c_ctx, w_mod, b_mod, norm1_w, norm2_w, w_in, q_norm_w, k_norm_w, hy_conv_w, hy_conv_b,
           hy_w1, hy_b1, hy_w2, hy_b2, hy_w3, hy_freq, hy_decay, hy_skip, pool_w, pool_scale, gm_norm_w, gm_ws,
           gm_bs, w_br_attn, w_br_hyena, w_br_pool, w_br_gmlp, w_out, ffn_w_gate, ffn_w_up, ffn_w_down,
           final_norm_w):
    a = dict(w_in=w_in, q_norm_w=q_norm_w, k_norm_w=k_norm_w, hy_conv_w=hy_conv_w, hy_conv_b=hy_conv_b,
             hy_w1=hy_w1, hy_b1=hy_b1, hy_w2=hy_w2, hy_b2=hy_b2, hy_w3=hy_w3, hy_freq=hy_freq, hy_decay=hy_decay,
             hy_skip=hy_skip, pool_w=pool_w, pool_scale=pool_scale, gm_norm_w=gm_norm_w, gm_ws=gm_ws, gm_bs=gm_bs,
             w_br_attn=w_br_attn, w_br_hyena=w_br_hyena, w_br_pool=w_br_pool, w_br_gmlp=w_br_gmlp, w_out=w_out,
             ffn_w_gate=ffn_w_gate, ffn_w_up=ffn_w_up, ffn_w_down=ffn_w_down, norm1_w=norm1_w, norm2_w=norm2_w)
    B, n, D = x.shape
    nc = ctx.shape[1]
    depth = w_mod.shape[0]

    R = ((B + 1 + 15) // 16) * 16
    act = jnp.zeros((R, D), F32).at[:B].set(c).at[B].set(c_ctx)
    mod = _mod_call(act, w_mod, b_mod)

    rope = _rope_tables(n)
    rope_ctx = _identity_tables(nc)
    fc = _fft_consts(2 * n)
    fc_ctx = _fft_consts(2 * nc)

    xc = ctx
    for l in range(depth):
        lw = _layer_weights(l, a)
        mods = [m.reshape(B, 1, D) for m in jnp.split(mod[l, :B], 6, axis=-1)]
        cmods = [jnp.broadcast_to(m.reshape(1, 1, D), (B, 1, D)) for m in jnp.split(mod[l, B], 6, axis=-1)]
        if l < depth - 1:
            kf_ctx = _filter_spectra(nc, lw, fc_ctx)
            xc, k_c, v_c = _mixer_block(xc, cmods, lw, rope_ctx, None, None, kf_ctx, fc_ctx)
        else:
            pc = _inproj_call(xc, cmods[0], cmods[1], lw["norm1_w"], lw["w_in"][:, P_KV:P_KV + 2 * KV_W], 2 * KV_W)
            k_c, v_c = _kvprep_ctx(pc, lw)
        kf = _filter_spectra(n, lw, fc)
        x, _, _ = _mixer_block(x, mods, lw, rope, k_c, v_c, kf, fc)
    return _rms_call(x, final_norm_w)


def _kvprep_only_kernel(kv_ref, kw_ref, bd_ref, ko_ref, vo_ref):
    t = kv_ref[:, :KV_W]
    sq_hi, sq_lo = _split(t * t)
    ms = _dot(sq_hi, bd_ref[...]) + _dot(sq_lo, bd_ref[...])
    ko_ref[...] = (t * lax.rsqrt(ms + RMS_EPS) * kw_ref[...]).astype(BF16)
    vt = kv_ref[:, KV_W:].T
    for g in range(N_KV_HEADS):
        vo_ref[g] = _value_rows(vt[g * HEAD_DIM:(g + 1) * HEAD_DIM])


def _kvprep_ctx(pc, lw):
    B, nc, _ = pc.shape
    bd = np.kron(np.eye(LANES // HEAD_DIM), np.full((HEAD_DIM, HEAD_DIM), 1.0 / HEAD_DIM))
    return pl.pallas_call(
        _kvprep_only_kernel,
        grid=(B,),
        in_specs=[pl.BlockSpec((None, nc, 2 * KV_W), lambda b: (b, 0, 0)),
                  pl.BlockSpec((1, KV_W), lambda b: (0, 0)), pl.BlockSpec((LANES, LANES), lambda b: (0, 0))],
        out_specs=[pl.BlockSpec((None, nc, KV_W), lambda b: (b, 0, 0)),
                   pl.BlockSpec((None, N_KV_HEADS, V_ROWS, nc), lambda b: (b, 0, 0, 0))],
        out_shape=[jax.ShapeDtypeStruct((B, nc, KV_W), BF16),
                   jax.ShapeDtypeStruct((B, N_KV_HEADS, V_ROWS, nc), BF16)],
        compiler_params=_cparams(("parallel",)),
        name="kvprep_ctx",
    )(pc, lw["k_norm_w"], jnp.asarray(bd, BF16))
```

```python
import functools
import math

import numpy as np
import jax
import jax.numpy as jnp
from jax import lax
from jax.experimental import pallas as pl
from jax.experimental.pallas import tpu as pltpu

F32 = jnp.float32
BF16 = jnp.bfloat16

D_MODEL = 1024
GRID_W = 64
RMS_EPS = 1e-6
HEAD_DIM = 64
N_Q_HEADS = 8
N_KV_HEADS = 2
ATTN_W = N_Q_HEADS * HEAD_DIM
KV_W = N_KV_HEADS * HEAD_DIM
ROPE_THETA = 10000.0
HYENA_W = 256
HYENA_ORDER = 2
HYENA_BANDS = 16
HYENA_HIDDEN = 64
HYENA_INNER = 2
POOL_W = 256
POOL_WINDOWS = (2, 4, 8, 16)
POOL_GROUP_W = POOL_W // len(POOL_WINDOWS)
POOL_HALO = 8
GMLP_W = 256
GMLP_CHUNK = 128
GMLP_GROUPS = 4
GMLP_GROUP_W = GMLP_W // GMLP_GROUPS
N_BRANCH = 4
FFN_HIDDEN = ((8 * D_MODEL // 3 + 255) // 256) * 256

Q_OFF = 0
K_OFF = Q_OFF + ATTN_W
V_OFF = K_OFF + KV_W
HY_OFF = V_OFF + KV_W
POOL_OFF = HY_OFF + 3 * HYENA_W
GM_OFF = POOL_OFF + POOL_W
GATE_OFF = GM_OFF + 2 * GMLP_W
IN_W = GATE_OFF + N_BRANCH * D_MODEL

P_Q = 0
P_KV = P_Q + ATTN_W
P_POOL = P_KV + 2 * KV_W
P_GMU = P_POOL + POOL_W
P_GMV = P_GMU + GMLP_W
P_HY = P_GMV + GMLP_W
P_W = P_HY + 3 * HYENA_W
P_TILE = 768

LANES = 128
VMEM_LIMIT = 52 * 1024 * 1024


def _cparams(sem):
    return pltpu.CompilerParams(dimension_semantics=sem, vmem_limit_bytes=VMEM_LIMIT)


def _tile(n, cap, mult=8):
    best = None
    for t in range(mult, min(n, cap) + 1, mult):
        if n % t == 0:
            best = t
    assert best is not None, (n, cap, mult)
    return best


def _split(x):
    hi = x.astype(BF16)
    lo = (x - hi.astype(F32)).astype(BF16)
    return hi, lo


def _dot(a, b):
    return jnp.dot(a, b, preferred_element_type=F32)


def _dot3(a_hi, a_lo, x):
    x_hi, x_lo = _split(x)
    return _dot(a_hi, x_hi) + (_dot(a_lo, x_hi) + _dot(a_hi, x_lo))


def _dot3x(x, b_hi, b_lo):
    x_hi, x_lo = _split(x)
    return _dot(x_hi, b_hi) + (_dot(x_lo, b_hi) + _dot(x_hi, b_lo))


def _sigmoid(x):
    return 1.0 / (1.0 + jnp.exp(-x))


def _silu(x):
    return x * _sigmoid(x)


def _modulated_norm(x, nw, scale, shift):
    ms = jnp.mean(x * x, axis=-1, keepdims=True)
    y = x * lax.rsqrt(ms + RMS_EPS) * nw
    return y * (1.0 + scale) + shift


def _mod_kernel(a_ref, w_ref, b_ref, o_ref):
    a = _silu(a_ref[...]).astype(BF16)
    o_ref[...] = _dot(a, w_ref[...].astype(BF16)) + b_ref[...]


def _mod_call(act, w_mod, b_mod):
    L, D, W6 = w_mod.shape
    R = act.shape[0]
    tn = 1024
    return pl.pallas_call(
        _mod_kernel,
        grid=(L, W6 // tn),
        in_specs=[
            pl.BlockSpec((R, D), lambda l, j: (0, 0)),
            pl.BlockSpec((None, D, tn), lambda l, j: (l, 0, j)),
            pl.BlockSpec((None, 1, tn), lambda l, j: (l, 0, j)),
        ],
        out_specs=pl.BlockSpec((None, R, tn), lambda l, j: (l, 0, j)),
        out_shape=jax.ShapeDtypeStruct((L, R, W6), F32),
        compiler_params=_cparams(("arbitrary", "arbitrary")),
        name="mod",
    )(act, w_mod, b_mod.reshape(L, 1, W6))


def _inproj_kernel(x_ref, sh_ref, sc_ref, nw_ref, w_ref, o_ref, h_ref):
    @pl.when(pl.program_id(2) == 0)
    def _():
        h_ref[...] = _modulated_norm(x_ref[...], nw_ref[...], sc_ref[...], sh_ref[...]).astype(BF16)

    o_ref[...] = _dot(h_ref[...], w_ref[...])


def _inproj_call(x, shift, scale, norm_w, w, tn):
    B, n, D = x.shape
    Nw = w.shape[1]
    tm = _tile(n, 1024)
    return pl.pallas_call(
        _inproj_kernel,
        grid=(B, n // tm, Nw // tn),
        in_specs=[
            pl.BlockSpec((None, tm, D), lambda b, i, j: (b, i, 0)),
            pl.BlockSpec((None, 1, D), lambda b, i, j: (b, 0, 0)),
            pl.BlockSpec((None, 1, D), lambda b, i, j: (b, 0, 0)),
            pl.BlockSpec((1, D), lambda b, i, j: (0, 0)),
            pl.BlockSpec((D, tn), lambda b, i, j: (0, j)),
        ],
        out_specs=pl.BlockSpec((None, tm, tn), lambda b, i, j: (b, i, j)),
        out_shape=jax.ShapeDtypeStruct((B, n, Nw), F32),
        scratch_shapes=[pltpu.VMEM((tm, D), BF16)],
        compiler_params=_cparams(("parallel", "parallel", "arbitrary")),
        name="inproj",
    )(x, shift, scale, norm_w.reshape(1, D), w)


def _head_norm_rope(t, w, cs, sn, bd_ref):
    sq_hi, sq_lo = _split(t * t)
    ms = _dot(sq_hi, bd_ref[...]) + _dot(sq_lo, bd_ref[...])
    y = t * lax.rsqrt(ms + RMS_EPS) * w
    lane = lax.broadcasted_iota(jnp.int32, y.shape, 1)
    swapped = jnp.where(lane % 2 == 0, pltpu.roll(y, LANES - 1, 1), pltpu.roll(y, 1, 1))
    return y * cs + swapped * sn


V_ROWS = 80
Q_PER_KV = N_Q_HEADS // N_KV_HEADS
LAG_INIT_KEYS = 128
LAG_LIMIT = 24.0


def _value_rows(vt_g):
    tail = lax.broadcasted_iota(jnp.int32, (V_ROWS - HEAD_DIM, vt_g.shape[1]), 0)
    return jnp.concatenate([vt_g, jnp.where(tail == 0, 1.0, 0.0)], axis=0).astype(BF16)


FP8 = jnp.float8_e4m3fn
FP8_MAX_WEIGHT = 32.0


def _split8(x):
    hi = x.astype(FP8).astype(F32)
    return hi, (x - hi).astype(FP8).astype(F32)


def _qkprep_kernel(q_ref, kv_ref, cs_ref, sn_ref, qw_ref, kw_ref, bd_ref, kp_ref, qo_ref, ko_ref, vo_ref, *, fp8):
    cs = cs_ref[...]
    sn = sn_ref[...]
    tm = q_ref.shape[0]
    zeros = jnp.zeros((HEAD_DIM, tm), F32)
    heads_per_chunk = LANES // HEAD_DIM
    for c in range(ATTN_W // LANES):
        sl = slice(c * LANES, (c + 1) * LANES)
        r = _head_norm_rope(q_ref[:, sl], qw_ref[:, sl], cs, sn, bd_ref)
        if fp8:
            hi, lo = _split8(r)
            hi_t, lo_t = hi.T, lo.T
            for half in range(heads_per_chunk):
                rows = slice(half * HEAD_DIM, (half + 1) * HEAD_DIM)
                qo_ref[c * heads_per_chunk + half] = jnp.concatenate(
                    [hi_t[rows], lo_t[rows], hi_t[rows], lo_t[rows]], axis=0).astype(FP8)
        else:
            rt = r.T
            for half in range(heads_per_chunk):
                h = c * heads_per_chunk + half
                blk = rt[half * HEAD_DIM:(half + 1) * HEAD_DIM]
                parts = [blk if g == h // Q_PER_KV else zeros for g in range(N_KV_HEADS)]
                qo_ref[h] = jnp.concatenate(parts, axis=0).astype(BF16)
    k = _head_norm_rope(kv_ref[:, :KV_W], kw_ref[...], cs, sn, bd_ref)
    if fp8:
        hi, lo = _split8(k)
        hi, lo = hi.astype(BF16), lo.astype(BF16)
        for g in range(N_KV_HEADS):
            ko_ref[g] = (_dot(hi, kp_ref[g, 0]) + _dot(lo, kp_ref[g, 1])).astype(FP8)
    else:
        ko_ref[0] = k.astype(BF16)
    vt = kv_ref[:, KV_W:].T
    for g in range(N_KV_HEADS):
        vo_ref[g] = _value_rows(vt[g * HEAD_DIM:(g + 1) * HEAD_DIM])


def _k_placement():
    pm = np.zeros((N_KV_HEADS, 2, KV_W, 4 * HEAD_DIM), np.float32)
    d = np.arange(HEAD_DIM)
    for g in range(N_KV_HEADS):
        for part in range(2):
            for rep in range(2):
                pm[g, part, g * HEAD_DIM + d, (2 * part + rep) * HEAD_DIM + d] = 1.0
    return jnp.asarray(pm, BF16)


def _qkprep_call(p, cs, sn, qw, kw, fp8):
    B, n, _ = p.shape
    tm = _tile(n, 512, LANES)
    bd = np.kron(np.eye(LANES // HEAD_DIM), np.full((HEAD_DIM, HEAD_DIM), 1.0 / HEAD_DIM))
    bd = jnp.asarray(bd, BF16)
    kd, kg, dt = (4 * HEAD_DIM, N_KV_HEADS, FP8) if fp8 else (KV_W, 1, BF16)
    return pl.pallas_call(
        functools.partial(_qkprep_kernel, fp8=fp8),
        grid=(n // tm, B),
        in_specs=[
            pl.BlockSpec((None, tm, ATTN_W), lambda i, b: (b, i, P_Q // ATTN_W)),
            pl.BlockSpec((None, tm, 2 * KV_W), lambda i, b: (b, i, P_KV // (2 * KV_W))),
            pl.BlockSpec((tm, LANES), lambda i, b: (i, 0)),
            pl.BlockSpec((tm, LANES), lambda i, b: (i, 0)),
            pl.BlockSpec((1, ATTN_W), lambda i, b: (0, 0)),
            pl.BlockSpec((1, KV_W), lambda i, b: (0, 0)),
            pl.BlockSpec((LANES, LANES), lambda i, b: (0, 0)),
            pl.BlockSpec((N_KV_HEADS, 2, KV_W, 4 * HEAD_DIM), lambda i, b: (0, 0, 0, 0)),
        ],
        out_specs=[
            pl.BlockSpec((None, N_Q_HEADS, kd, tm), lambda i, b: (b, 0, 0, i)),
            pl.BlockSpec((None, kg, tm, kd), lambda i, b: (b, 0, i, 0)),
            pl.BlockSpec((None, N_KV_HEADS, V_ROWS, tm), lambda i, b: (b, 0, 0, i)),
        ],
        out_shape=[
            jax.ShapeDtypeStruct((B, N_Q_HEADS, kd, n), dt),
            jax.ShapeDtypeStruct((B, kg, n, kd), dt),
            jax.ShapeDtypeStruct((B, N_KV_HEADS, V_ROWS, n), BF16),
        ],
        compiler_params=_cparams(("parallel", "parallel")),
        name="qkprep",
    )(p, p, cs, sn, qw, kw, bd, _k_placement())


def _attn_kernel(qt_ref, k_ref, vt_ref, o_ref, m_ref, acc_ref, *, tk):
    nk = vt_ref.shape[0]
    tq = m_ref.shape[-1]
    per_kv_head = k_ref.shape[0] > 1

    def keys(j, h):
        g = h // Q_PER_KV if per_kv_head else 0
        return k_ref[g, pl.ds(pl.multiple_of(j * tk, tk), tk), :]

    def lagged_chunk(j, jump):
        s_next = _dot(keys(j, 0), qt_ref[0])
        for h in range(N_Q_HEADS):
            s = s_next
            if h + 1 < N_Q_HEADS:
                s_next = _dot(keys(j, h + 1), qt_ref[h + 1])
            m_old = m_ref[h]
            p = jnp.exp2((s - m_old).astype(BF16))
            cmax = jnp.max(s, axis=0, keepdims=True)
            m_new = jnp.maximum(m_old, cmax)
            jump = jnp.maximum(jump, cmax - m_old)
            acc_ref[h] = jnp.exp2(m_old - m_new) * (acc_ref[h] + _dot(vt_ref[j, h // Q_PER_KV], p))
            m_ref[h] = m_new
        return jump

    def exact_chunk(j, carry):
        for h in range(N_Q_HEADS):
            s = _dot(keys(j, h), qt_ref[h])
            m_old = m_ref[h]
            m_new = jnp.maximum(m_old, jnp.max(s, axis=0, keepdims=True))
            p = jnp.exp2(s - m_new).astype(BF16)
            acc_ref[h] = jnp.exp2(m_old - m_new) * acc_ref[h] + _dot(vt_ref[j, h // Q_PER_KV], p)
            m_ref[h] = m_new
        return carry

    for h in range(N_Q_HEADS):
        k0 = k_ref[h // Q_PER_KV if per_kv_head else 0, 0:LAG_INIT_KEYS, :]
        m_ref[h] = jnp.max(_dot(k0, qt_ref[h]), axis=0, keepdims=True)
    acc_ref[...] = jnp.zeros(acc_ref.shape, F32)
    jump = lax.fori_loop(0, nk, lagged_chunk, jnp.zeros((1, tq), F32))

    @pl.when(jnp.logical_not(jnp.max(jump) <= LAG_LIMIT))
    def _():
        m_ref[...] = jnp.full(m_ref.shape, -jnp.inf, F32)
        acc_ref[...] = jnp.zeros(acc_ref.shape, F32)
        lax.fori_loop(0, nk, exact_chunk, 0)

    for c in range(ATTN_W // LANES):
        heads = []
        for h in range(c * (LANES // HEAD_DIM), (c + 1) * (LANES // HEAD_DIM)):
            a = acc_ref[h]
            heads.append(a[:HEAD_DIM] / a[HEAD_DIM:HEAD_DIM + 1])
        o_ref[:, c * LANES:(c + 1) * LANES] = jnp.concatenate(heads, axis=0).T.astype(BF16)


def _attn_call(qt, k, vt):
    B, _, kd, n = qt.shape
    _, kg, Lk, _ = k.shape
    tq = _tile(n, 512, LANES)
    tk = _tile(Lk, 768, LANES)
    nk = Lk // tk
    vt = vt.reshape(B, N_KV_HEADS, V_ROWS, nk, tk).transpose(0, 3, 1, 2, 4)
    return pl.pallas_call(
        functools.partial(_attn_kernel, tk=tk),
        grid=(B, n // tq),
        in_specs=[
            pl.BlockSpec((None, N_Q_HEADS, kd, tq), lambda b, i: (b, 0, 0, i)),
            pl.BlockSpec((None, kg, Lk, kd), lambda b, i: (b, 0, 0, 0)),
            pl.BlockSpec((None, nk, N_KV_HEADS, V_ROWS, tk), lambda b, i: (b, 0, 0, 0, 0)),
        ],
        out_specs=pl.BlockSpec((None, tq, ATTN_W), lambda b, i: (b, i, 0)),
        out_shape=jax.ShapeDtypeStruct((B, n, ATTN_W), BF16),
        scratch_shapes=[
            pltpu.VMEM((N_Q_HEADS, 1, tq), F32),
            pltpu.VMEM((N_Q_HEADS, V_ROWS, tq), F32),
        ],
        compiler_params=_cparams(("parallel", "parallel")),
        name="attn",
    )(qt, k, vt)


HALVES = HYENA_W // LANES


def _split_shape(B, n):
    assert B % 2 == 0
    return jax.ShapeDtypeStruct((B // 2, HALVES, 2, n, LANES), F32)


def _split_spec(tm):
    return pl.BlockSpec((None, HALVES, None, tm, LANES), lambda b, i: (b // 2, 0, b % 2, i, 0))


def _shift_rows(cur, prev_row, next_row, rows):
    tm = cur.shape[0]
    up = jnp.where(rows == 0, prev_row, pltpu.roll(cur, 1, 0))
    dn = jnp.where(rows == tm - 1, next_row, pltpu.roll(cur, tm - 1, 0))
    return up, dn


def _local_kernel(hy_ref, hyp_ref, hyn_ref, po_ref, pop_ref, pon_ref, gu_ref, gv_ref,
                  cw_ref, cb_ref, pw_ref, ps_ref, gnw_ref, gws_ref, gb_ref,
                  v_ref, x1_ref, x2_ref, pl_ref, gm_ref, *, n_seq):
    i = pl.program_id(1)
    last = pl.num_programs(1) - 1
    tm = hy_ref.shape[0]
    has_prev = (i > 0).astype(F32)
    has_next = (i < last).astype(F32)

    cur = hy_ref[...]
    rows = lax.broadcasted_iota(jnp.int32, cur.shape, 0)
    prev_row = hyp_ref[POOL_HALO - 1:POOL_HALO, :] * has_prev
    next_row = hyn_ref[0:1, :] * has_next
    up, dn = _shift_rows(cur, prev_row, next_row, rows)
    z = up * cw_ref[0:1, :] + cur * cw_ref[1:2, :] + dn * cw_ref[2:3, :] + cb_ref[...]
    for s, o_ref in enumerate((v_ref, x1_ref, x2_ref)):
        for hf in range(HYENA_W // LANES):
            lo = s * HYENA_W + hf * LANES
            o_ref[hf] = z[:, lo:lo + LANES]

    zc = po_ref[...]
    ext = jnp.concatenate([pop_ref[...] * has_prev, zc, pon_ref[...] * has_next], axis=0)
    te = tm + 2 * POOL_HALO
    lane = lax.broadcasted_iota(jnp.int32, zc.shape, 1)
    prow = lax.broadcasted_iota(jnp.int32, zc.shape, 0) + i * tm
    half = jnp.left_shift(1, lane // POOL_GROUP_W)
    def ahead(a, k):
        return pltpu.roll(a, (te - k) % te, 0)

    assert POOL_WINDOWS == (2, 4, 8, 16)
    sums = [ext + ahead(ext, -1)]
    for win in POOL_WINDOWS[:-1]:
        sums.append(ahead(sums[-1], win // 2) + ahead(sums[-1], -(win // 2)))
    group = lane // POOL_GROUP_W
    acc = sums[-1][POOL_HALO:POOL_HALO + tm]
    for g in range(len(POOL_WINDOWS) - 2, -1, -1):
        acc = jnp.where(group == g, sums[g][POOL_HALO:POOL_HALO + tm], acc)
    cnt = jnp.minimum(prow + half, n_seq) - jnp.maximum(prow - half, 0)
    dlt = acc / cnt.astype(F32) - zc
    pl_ref[...] = (_dot(dlt.astype(BF16), pw_ref[...]) * ps_ref[...]).astype(BF16)

    gv = gv_ref[...]
    mu = jnp.mean(gv, axis=-1, keepdims=True)
    ctr = gv - mu
    var = jnp.mean(ctr * ctr, axis=-1, keepdims=True)
    vn = (ctr * lax.rsqrt(var + RMS_EPS) * gnw_ref[...]).astype(BF16)
    group = lax.broadcasted_iota(jnp.int32, (GMLP_CHUNK, GMLP_W), 1) // GMLP_GROUP_W
    for c in range(tm // GMLP_CHUNK):
        rs = slice(c * GMLP_CHUNK, (c + 1) * GMLP_CHUNK)
        vc = vn[rs]
        mixed = gb_ref[...]
        for g in range(GMLP_GROUPS):
            mg = _dot(gws_ref[g], vc)
            mixed = mixed + jnp.where(group == g, mg, 0.0)
        gm_ref[rs, :] = (gu_ref[rs, :] * mixed).astype(BF16)


def _local_call(p, lw):
    B, n, _ = p.shape
    tm = _tile(n, 512, GMLP_CHUNK)
    nb8 = n // POOL_HALO
    r8 = tm // POOL_HALO
    HW = 3 * HYENA_W

    def cur(w, off):
        return pl.BlockSpec((None, tm, w), lambda b, i: (b, i, off // w))

    def prev(w, off):
        return pl.BlockSpec((None, POOL_HALO, w), lambda b, i: (b, jnp.maximum(i * r8 - 1, 0), off // w))

    def nxt(w, off):
        return pl.BlockSpec((None, POOL_HALO, w), lambda b, i: (b, jnp.minimum((i + 1) * r8, nb8 - 1), off // w))

    def full(shape):
        return pl.BlockSpec(shape, lambda b, i: (0,) * len(shape))

    def out(w):
        return pl.BlockSpec((None, tm, w), lambda b, i: (b, i, 0))

    return pl.pallas_call(
        functools.partial(_local_kernel, n_seq=n),
        grid=(B, n // tm),
        in_specs=[
            cur(HW, P_HY), prev(HW, P_HY), nxt(HW, P_HY),
            cur(POOL_W, P_POOL), prev(POOL_W, P_POOL), nxt(POOL_W, P_POOL),
            cur(GMLP_W, P_GMU), cur(GMLP_W, P_GMV),
            full((3, HW)), full((1, HW)), full((POOL_W, POOL_W)), full((1, POOL_W)),
            full((1, GMLP_W)), full((GMLP_GROUPS, GMLP_CHUNK, GMLP_CHUNK)), full((GMLP_CHUNK, GMLP_W)),
        ],
        out_specs=[_split_spec(tm), _split_spec(tm), _split_spec(tm), out(POOL_W), out(GMLP_W)],
        out_shape=[
            _split_shape(B, n), _split_shape(B, n), _split_shape(B, n),
            jax.ShapeDtypeStruct((B, n, POOL_W), BF16),
            jax.ShapeDtypeStruct((B, n, GMLP_W), BF16),
        ],
        compiler_params=_cparams(("parallel", "parallel")),
        name="local",
    )(p, p, p, p, p, p, p, p, lw["conv_w"], lw["conv_b"], lw["pool_bd"], lw["pool_scale"],
      lw["gm_norm_w"], lw["gm_ws"], lw["gm_bias"])


def _filt_kernel(z_ref, t_ref, w1h_ref, w1l_ref, b1_ref, w2h_ref, w2l_ref, b2_ref, fr_ref,
                 w3h_ref, w3l_ref, dec_ref, k_ref, s_ref, *, n_seq):
    i = pl.program_id(0)
    tb = z_ref.shape[0]
    fr = fr_ref[...]
    hdn = jnp.sin(fr * (_dot3x(z_ref[...], w1h_ref[...], w1l_ref[...]) + b1_ref[...]))
    for u in range(HYENA_INNER):
        hdn = jnp.sin(fr * (_dot3x(hdn, w2h_ref[u], w2l_ref[u]) + b2_ref[u:u + 1, :]))
    t = t_ref[...]
    tt = jnp.concatenate([t] * (HYENA_ORDER * HYENA_W // LANES), axis=1)
    window = jnp.exp(-tt * jnp.abs(dec_ref[...]))
    k = _dot3x(hdn, w3h_ref[...], w3l_ref[...]) * window
    rows = lax.broadcasted_iota(jnp.int32, k.shape, 0) + i * tb
    k = jnp.where(rows == n_seq, 0.0, k)
    for q in range(k_ref.shape[0]):
        k_ref[q] = k[:, q * LANES:(q + 1) * LANES]

    @pl.when(i == 0)
    def _():
        s_ref[...] = jnp.zeros(s_ref.shape, F32)

    s_ref[...] += jnp.sum(jnp.abs(k), axis=0, keepdims=True)


def _filt_call(n, lw):
    N = 2 * n
    tb = _tile(n, 512)
    nb_half = n // tb
    pos = np.arange(N)
    pos = np.where(pos < n, pos, N - pos).astype(np.float64)
    t = pos / (n - 1)
    bands = np.linspace(1e-4, HYENA_BANDS - 1, HYENA_BANDS)
    ang = (2.0 * math.pi / n) * pos[:, None] * bands
    z = np.concatenate([t[:, None], np.cos(ang), np.sin(ang)], axis=-1)
    zp = np.zeros((N, LANES), np.float32)
    zp[:, :z.shape[1]] = z
    tl = np.broadcast_to(t[:, None], (N, LANES)).astype(np.float32)
    CW = HYENA_ORDER * HYENA_W

    def full(shape):
        return pl.BlockSpec(shape, lambda i: (0,) * len(shape))

    def side(rows_):
        return pl.BlockSpec((None, rows_, CW), lambda i: (i // nb_half, 0, 0))

    return pl.pallas_call(
        functools.partial(_filt_kernel, n_seq=n),
        grid=(N // tb,),
        in_specs=[
            pl.BlockSpec((tb, LANES), lambda i: (i, 0)),
            pl.BlockSpec((tb, LANES), lambda i: (i, 0)),
            full((LANES, HYENA_HIDDEN)), full((LANES, HYENA_HIDDEN)), full((1, HYENA_HIDDEN)),
            full((HYENA_INNER, HYENA_HIDDEN, HYENA_HIDDEN)), full((HYENA_INNER, HYENA_HIDDEN, HYENA_HIDDEN)),
            full((HYENA_INNER, HYENA_HIDDEN)), full((1, HYENA_HIDDEN)),
            side(HYENA_HIDDEN), side(HYENA_HIDDEN), side(1),
        ],
        out_specs=[pl.BlockSpec((CW // LANES, tb, LANES), lambda i: (0, i, 0)), pl.BlockSpec((1, CW), lambda i: (0, 0))],
        out_shape=[jax.ShapeDtypeStruct((CW // LANES, N, LANES), F32), jax.ShapeDtypeStruct((1, CW), F32)],
        compiler_params=_cparams(("arbitrary",)),
        name="filt",
    )(jnp.asarray(zp), jnp.asarray(tl), lw["hy_w1h"], lw["hy_w1l"], lw["hy_b1"], lw["hy_w2h"], lw["hy_w2l"],
      lw["hy_b2"], lw["hy_freq"], lw["hy_w3h"], lw["hy_w3l"], lw["hy_decay"])


def _fft_dims(N):
    e = int(round(math.log2(N)))
    assert 2 ** e == N
    N1 = 2 ** ((e + 1) // 2)
    return N1, N // N1


def _hilo(a):
    a = np.asarray(a, np.float64)
    hi = jnp.asarray(a, F32).astype(BF16)
    lo = (jnp.asarray(a, F32) - hi.astype(F32)).astype(BF16)
    return hi, lo


@functools.lru_cache(maxsize=None)
def _fft_consts_np(N):
    N1, N2 = _fft_dims(N)
    h = N1 // 2
    k1 = np.arange(N1)
    F1 = np.exp(-2j * np.pi * np.outer(k1, k1) / N1)
    k2 = np.arange(N2)
    F2 = np.exp(-2j * np.pi * np.outer(k2, k2) / N2)
    tw = np.exp(-2j * np.pi * np.outer(k1, k2) / N)
    g1 = np.zeros((2 * N1, N1))
    g1[0::2, :h] = F1.real[:, :h]
    g1[0::2, h:] = -F1.imag[:, :h]
    g1[1::2, :h] = F1.imag[:, :h]
    g1[1::2, h:] = F1.real[:, :h]
    g1f = np.zeros((2 * N1, N1))
    g1f[0::2] = F1.real
    g1f[1::2] = F1.imag
    g2 = np.block([[F2.real, -F2.imag], [F2.imag, F2.real]])
    g2c = np.block([[F2.real, F2.imag], [-F2.imag, F2.real]])
    g3 = np.zeros((N1, 2 * N1))
    g3[:h, 0::2] = F1.real[:h]
    g3[:h, 1::2] = F1.imag[:h]
    g3[h:, 0::2] = -F1.imag[:h]
    g3[h:, 1::2] = F1.real[:h]
    g3 /= N
    twt = np.stack([tw.real, tw.imag], axis=1)
    return dict(g1=g1, g1f=g1f, g2=g2, g2c=g2c, g3=g3, tw=twt)


def _fft_consts(N):
    c = _fft_consts_np(N)
    out = {k: _hilo(c[k]) for k in ("g1", "g1f", "g2", "g2c", "g3")}
    N1, N2 = _fft_dims(N)
    tw = jnp.asarray(c["tw"], F32)
    out["tw"] = jnp.broadcast_to(tw[..., None], (N1, 2, N2, LANES))
    return out


T2_BLOCK = 8


def _load_8th(ref, j, rows):
    return ref.reshape(rows * T2_BLOCK, LANES)[pl.ds(j, rows, stride=T2_BLOCK), :]


def _store_8th(ref, j, rows, val):
    ref.reshape(rows * T2_BLOCK, LANES)[pl.ds(j, rows, stride=T2_BLOCK), :] = val


def _dft_in_kernel(gh_ref, gl_ref, x_ref, o_ref):
    N1 = x_ref.shape[1]
    for j in range(T2_BLOCK):
        xj = jnp.concatenate([_load_8th(x_ref.at[hf], j, N1) for hf in range(HALVES)], axis=1)
        a = _dot3(gh_ref[...], gl_ref[...], xj)
        for hf in range(HALVES):
            _store_8th(o_ref.at[hf], j, 2 * N1, a[:, hf * LANES:(hf + 1) * LANES])


def _dft_in_call(g, u5):
    gh, gl = g
    G, _, N1, N2, _ = u5.shape
    assert gh.shape == (2 * N1, N1)
    mat = pl.BlockSpec((2 * N1, N1), lambda b, i: (0, 0))
    return pl.pallas_call(
        _dft_in_kernel,
        grid=(G, N2 // T2_BLOCK),
        in_specs=[mat, mat, pl.BlockSpec((None, HALVES, N1, T2_BLOCK, LANES), lambda b, i: (b, 0, 0, i, 0))],
        out_specs=pl.BlockSpec((None, HALVES, N1, 2, T2_BLOCK, LANES), lambda b, i: (b, 0, 0, 0, i, 0)),
        out_shape=jax.ShapeDtypeStruct((G, HALVES, N1, 2, N2, LANES), F32),
        compiler_params=_cparams(("parallel", "parallel")),
        name="dft_in",
    )(gh, gl, u5)


def _dft_out_kernel(gh_ref, gl_ref, d_ref, gate_ref, u_ref, sk_ref, o_ref):
    N1 = gate_ref.shape[1]
    for j in range(T2_BLOCK):
        dj = jnp.concatenate([_load_8th(d_ref.at[hf], j, 2 * N1) for hf in range(HALVES)], axis=1)
        y = _dot3(gh_ref[...], gl_ref[...], dj)
        for hf in range(HALVES):
            conv = y[:, hf * LANES:(hf + 1) * LANES]
            u = _load_8th(u_ref.at[hf], j, N1)
            _store_8th(o_ref.at[hf], j, N1, _load_8th(gate_ref.at[hf], j, N1) * (conv + u * sk_ref[hf]))


def _dft_out_call(g, d6, gate5, u5, skip):
    gh, gl = g
    G, _, N1, N2, _ = u5.shape
    assert gh.shape == (N1, 2 * N1)
    mat = pl.BlockSpec((N1, 2 * N1), lambda b, i: (0, 0))
    blk = pl.BlockSpec((None, HALVES, N1, T2_BLOCK, LANES), lambda b, i: (b, 0, 0, i, 0))
    return pl.pallas_call(
        _dft_out_kernel,
        grid=(G, N2 // T2_BLOCK),
        in_specs=[mat, mat, pl.BlockSpec((None, HALVES, N1, 2, T2_BLOCK, LANES), lambda b, i: (b, 0, 0, 0, i, 0)),
                  blk, blk, pl.BlockSpec((HALVES, 1, LANES), lambda b, i: (0, 0, 0))],
        out_specs=blk,
        out_shape=jax.ShapeDtypeStruct(u5.shape, F32),
        compiler_params=_cparams(("parallel", "parallel")),
        name="dft_out",
    )(gh, gl, d6, gate5, u5, skip)


def _cmul(ar, ai, br, bi):
    return ar * br - ai * bi, ar * bi + ai * br


def _lane_tile(t, c):
    return jnp.concatenate([t] * (c // LANES), axis=-1)


def _both_halves(a_ref, k, part):
    return jnp.concatenate([a_ref[hf, k, part] for hf in range(HALVES)], axis=-1)


def _spec_kernel(a_ref, tw_ref, g2h_ref, g2l_ref, s_ref, o_ref):
    _, kb, _, N2, _ = a_ref.shape
    inv = 1.0 / s_ref[...]
    for k in range(kb):
        twr = _lane_tile(tw_ref[k, 0], HYENA_W)
        twi = _lane_tile(tw_ref[k, 1], HYENA_W)
        br, bi = _cmul(_both_halves(a_ref, k, 0), _both_halves(a_ref, k, 1), twr, twi)
        zz = _dot3(g2h_ref[...], g2l_ref[...], jnp.concatenate([br, bi], axis=0))
        o_ref[k, 0] = zz[:N2] * inv
        o_ref[k, 1] = zz[N2:] * inv


def _conv_kernel(a_ref, tw_ref, kf_ref, g2h_ref, g2l_ref, g2ch_ref, g2cl_ref, o_ref):
    _, kb, _, N2, _ = a_ref.shape
    for k in range(kb):
        twr = _lane_tile(tw_ref[k, 0], HYENA_W)
        twi = _lane_tile(tw_ref[k, 1], HYENA_W)
        br, bi = _cmul(_both_halves(a_ref, k, 0), _both_halves(a_ref, k, 1), twr, twi)
        zz = _dot3(g2h_ref[...], g2l_ref[...], jnp.concatenate([br, bi], axis=0))
        yr, yi = _cmul(zz[:N2], zz[N2:], kf_ref[k, 0], kf_ref[k, 1])
        cc = _dot3(g2ch_ref[...], g2cl_ref[...], jnp.concatenate([yr, yi], axis=0))
        dr, di = _cmul(cc[:N2], cc[N2:], twr, -twi)
        for hf in range(HALVES):
            o_ref[hf, k, 0] = dr[:, hf * LANES:(hf + 1) * LANES]
            o_ref[hf, k, 1] = di[:, hf * LANES:(hf + 1) * LANES]


def _spec_call(a6, fc, s):
    G, _, N1, _, N2, _ = a6.shape
    kb = _tile(N1, 8, 1)
    mat = pl.BlockSpec((2 * N2, 2 * N2), lambda i, b: (0, 0))
    return pl.pallas_call(
        _spec_kernel,
        grid=(N1 // kb, G),
        in_specs=[pl.BlockSpec((None, HALVES, kb, 2, N2, LANES), lambda i, b: (b, 0, i, 0, 0, 0)),
                  pl.BlockSpec((kb, 2, N2, LANES), lambda i, b: (i, 0, 0, 0)), mat, mat,
                  pl.BlockSpec((None, 1, HYENA_W), lambda i, b: (b, 0, 0))],
        out_specs=pl.BlockSpec((None, kb, 2, N2, HYENA_W), lambda i, b: (b, i, 0, 0, 0)),
        out_shape=jax.ShapeDtypeStruct((G, N1, 2, N2, HYENA_W), F32),
        compiler_params=_cparams(("parallel", "parallel")),
        name="filt_spec",
    )(a6, fc["tw"], fc["g2"][0], fc["g2"][1], s)


def _conv_call(a6, kf, fc):
    G, _, N1, _, N2, _ = a6.shape
    kb = _tile(N1, 8, 1)
    blk = pl.BlockSpec((None, HALVES, kb, 2, N2, LANES), lambda i, b: (b, 0, i, 0, 0, 0))
    mat = pl.BlockSpec((2 * N2, 2 * N2), lambda i, b: (0, 0))
    return pl.pallas_call(
        _conv_kernel,
        grid=(N1 // kb, G),
        in_specs=[blk, pl.BlockSpec((kb, 2, N2, LANES), lambda i, b: (i, 0, 0, 0)),
                  pl.BlockSpec((kb, 2, N2, HYENA_W), lambda i, b: (i, 0, 0, 0)), mat, mat, mat, mat],
        out_specs=blk,
        out_shape=jax.ShapeDtypeStruct(a6.shape, F32),
        compiler_params=_cparams(("parallel", "parallel")),
        name="spec_conv",
    )(a6, fc["tw"], kf, fc["g2"][0], fc["g2"][1], fc["g2c"][0], fc["g2c"][1])


def _filter_spectra(n, lw, fc):
    N1, N2 = _fft_dims(2 * n)
    k, s = _filt_call(n, lw)
    a6 = _dft_in_call(fc["g1f"], k.reshape(HYENA_ORDER, HALVES, N1, N2, LANES))
    return _spec_call(a6, fc, s.reshape(HYENA_ORDER, 1, HYENA_W))


def _hyena(v, x1, x2, kf, lw, fc):
    P, _, _, n, _ = v.shape
    N1, N2 = _fft_dims(2 * n)
    shape5 = (P, HALVES, N1, N2, LANES)
    y = v.reshape(shape5)
    for o, gate in enumerate((x1, x2)):
        d = _conv_call(_dft_in_call(fc["g1"], y), kf[o], fc)
        y = _dft_out_call(fc["g3"], d, gate.reshape(shape5), y, lw["hy_skip"][o].reshape(HALVES, 1, LANES))
    return y.reshape(v.shape)


def _merge_kernel(at_ref, hy_ref, po_ref, gm_ref, x_ref, sh_ref, sc_ref, nw_ref, gr_ref,
                  wgate_ref, wa_ref, wh_ref, wp_ref, wg_ref, wo_ref, o_ref):
    x = x_ref[...]
    D = x.shape[1]
    h = _modulated_norm(x, nw_ref[...], sc_ref[...], sh_ref[...]).astype(BF16)
    hy = jnp.concatenate([hy_ref[hf] for hf in range(HALVES)], axis=-1).astype(BF16)
    branches = ((at_ref[...], wa_ref), (hy, wh_ref), (po_ref[...], wp_ref), (gm_ref[...], wg_ref))
    merged = None
    for i, (br, w_ref) in enumerate(branches):
        term = _sigmoid(_dot(h, wgate_ref[:, i * D:(i + 1) * D])) * _dot(br, w_ref[...])
        merged = term if merged is None else merged + term
    o_ref[...] = x + gr_ref[...] * _dot(merged.astype(BF16), wo_ref[...])


def _merge_call(attn, hy, po, gm, x, shift, scale, norm_w, gate_res, lw):
    B, n, D = x.shape
    tm = _tile(n, 512)

    def rows(w):
        return pl.BlockSpec((None, tm, w), lambda b, i: (b, i, 0))

    def full(shape):
        return pl.BlockSpec(shape, lambda b, i: (0,) * len(shape))

    vec = pl.BlockSpec((None, 1, D), lambda b, i: (b, 0, 0))
    return pl.pallas_call(
        _merge_kernel,
        grid=(B, n // tm),
        in_specs=[
            rows(ATTN_W), _split_spec(tm), rows(POOL_W), rows(GMLP_W), rows(D),
            vec, vec, full((1, D)), vec,
            full((D, N_BRANCH * D)),
            full((ATTN_W, D)), full((HYENA_W, D)), full((POOL_W, D)), full((GMLP_W, D)), full((D, D)),
        ],
        out_specs=rows(D),
        out_shape=jax.ShapeDtypeStruct((B, n, D), F32),
        compiler_params=_cparams(("parallel", "parallel")),
        name="merge",
    )(attn, hy, po, gm, x, shift, scale, norm_w.reshape(1, D), gate_res, lw["w_gate"], lw["w_br_attn"],
      lw["w_br_hyena"], lw["w_br_pool"], lw["w_br_gmlp"], lw["w_out"])


FFN_ROWS = 1024
FFN_HIDDEN_TILE = 256


def _ffn_kernel(x_ref, sh_ref, sc_ref, nw_ref, gr_ref, wg_ref, wu_ref, wd_ref, fw_ref, o_ref, h_ref, acc_ref, *, final):
    j = pl.program_id(2)

    @pl.when(j == 0)
    def _():
        h_ref[...] = _modulated_norm(x_ref[...], nw_ref[...], sc_ref[...], sh_ref[...]).astype(BF16)
        acc_ref[...] = jnp.zeros(acc_ref.shape, F32)

    h = h_ref[...]
    a = _silu(_dot(h, wg_ref[...])) * _dot(h, wu_ref[...])
    acc_ref[...] += _dot(a.astype(BF16), wd_ref[...])

    @pl.when(j == pl.num_programs(2) - 1)
    def _():
        y = x_ref[...] + gr_ref[...] * acc_ref[...]
        if final:
            y = y * lax.rsqrt(jnp.mean(y * y, axis=-1, keepdims=True) + RMS_EPS) * fw_ref[...]
        o_ref[...] = y


def _ffn_call(x, shift, scale, norm_w, gate_res, wg, wu, wd, final_w=None):
    B, n, D = x.shape
    H = wg.shape[1]
    tm = _tile(n, FFN_ROWS)
    th = _tile(H, FFN_HIDDEN_TILE, LANES)
    vec = pl.BlockSpec((None, 1, D), lambda b, i, j: (b, 0, 0))
    final = final_w is not None
    fw = (final_w if final else norm_w).reshape(1, D)
    return pl.pallas_call(
        functools.partial(_ffn_kernel, final=final),
        grid=(B, n // tm, H // th),
        in_specs=[
            pl.BlockSpec((None, tm, D), lambda b, i, j: (b, i, 0)),
            vec, vec, pl.BlockSpec((1, D), lambda b, i, j: (0, 0)), vec,
            pl.BlockSpec((D, th), lambda b, i, j: (0, j)),
            pl.BlockSpec((D, th), lambda b, i, j: (0, j)),
            pl.BlockSpec((th, D), lambda b, i, j: (j, 0)),
            pl.BlockSpec((1, D), lambda b, i, j: (0, 0)),
        ],
        out_specs=pl.BlockSpec((None, tm, D), lambda b, i, j: (b, i, 0)),
        out_shape=jax.ShapeDtypeStruct((B, n, D), F32),
        scratch_shapes=[pltpu.VMEM((tm, D), BF16), pltpu.VMEM((tm, D), F32)],
        compiler_params=_cparams(("parallel", "parallel", "arbitrary")),
        name="ffn",
    )(x, shift, scale, norm_w.reshape(1, D), gate_res, wg, wu, wd, fw)


def _rope_tables(n):
    tok = np.arange(n)
    axis_dim = HEAD_DIM // 2
    inv_freq = ROPE_THETA ** (-np.arange(0, axis_dim, 2, dtype=np.float64) / axis_dim)
    inv_freq = inv_freq.astype(np.float32).astype(np.float64)
    ang = np.concatenate([(tok // GRID_W)[:, None] * inv_freq, (tok % GRID_W)[:, None] * inv_freq], axis=-1)
    ang = ang.astype(np.float32)
    cos = np.repeat(np.cos(ang), 2, axis=-1)
    sin = np.repeat(np.sin(ang), 2, axis=-1)
    sin[:, 0::2] *= -1.0
    reps = LANES // HEAD_DIM
    return jnp.asarray(np.tile(cos, (1, reps)), F32), jnp.asarray(np.tile(sin, (1, reps)), F32)


def _identity_tables(n):
    return jnp.ones((n, LANES), F32), jnp.zeros((n, LANES), F32)


def _layer_weights(l, a):
    w_in = a["w_in"][l]
    w_perm = jnp.concatenate([
        w_in[:, Q_OFF:K_OFF], w_in[:, K_OFF:HY_OFF], w_in[:, POOL_OFF:GM_OFF], w_in[:, GM_OFF:GATE_OFF],
        w_in[:, HY_OFF:POOL_OFF]], axis=1).astype(BF16)
    w3 = a["hy_w3"][l].reshape(HYENA_HIDDEN, HYENA_ORDER, 2, HYENA_W).transpose(2, 0, 1, 3)
    w3 = w3.reshape(2, HYENA_HIDDEN, HYENA_ORDER * HYENA_W)
    dec = a["hy_decay"][l].reshape(HYENA_ORDER, 2, HYENA_W).transpose(1, 0, 2).reshape(2, 1, HYENA_ORDER * HYENA_W)
    w1 = jnp.zeros((LANES, HYENA_HIDDEN), F32).at[:a["hy_w1"].shape[1]].set(a["hy_w1"][l])
    w1h, w1l = _split(w1)
    w2h, w2l = _split(a["hy_w2"][l])
    w3h, w3l = _split(w3)
    qw16 = jnp.tile(a["q_norm_w"][l], N_Q_HEADS).reshape(1, ATTN_W) * Q_SCALE
    kw16 = jnp.tile(a["k_norm_w"][l], N_KV_HEADS).reshape(1, KV_W)
    rms_q = jnp.sqrt(jnp.mean(qw16 * qw16))
    rms_k = jnp.sqrt(jnp.mean(kw16 * kw16))
    bal = jnp.exp2(jnp.clip(jnp.round(0.5 * jnp.log2(rms_k / rms_q)), -20.0, 20.0))
    bal = jnp.where(jnp.isfinite(bal), bal, 1.0)
    fp8_ok = (jnp.max(jnp.abs(qw16 * bal)) <= FP8_MAX_WEIGHT) & (jnp.max(jnp.abs(kw16 / bal)) <= FP8_MAX_WEIGHT)
    eye = jnp.eye(len(POOL_WINDOWS), dtype=F32)
    pool_bd = jnp.einsum("gh,gcd->gchd", eye, a["pool_w"][l]).reshape(POOL_W, POOL_W).astype(BF16)
    gm_bias = jnp.repeat(a["gm_bs"][l].T, GMLP_GROUP_W, axis=1)
    return dict(
        w_in=w_perm, w_gate=w_in[:, GATE_OFF:].astype(BF16),
        qw16=qw16, kw16=kw16, qw8=qw16 * bal, kw8=kw16 / bal, fp8_ok=fp8_ok,
        conv_w=a["hy_conv_w"][l], conv_b=a["hy_conv_b"][l].reshape(1, -1),
        hy_w1h=w1h, hy_w1l=w1l, hy_b1=a["hy_b1"][l].reshape(1, -1), hy_w2h=w2h, hy_w2l=w2l, hy_b2=a["hy_b2"][l],
        hy_w3h=w3h, hy_w3l=w3l, hy_freq=a["hy_freq"][l].reshape(1, -1), hy_decay=dec, hy_skip=a["hy_skip"][l],
        pool_bd=pool_bd, pool_scale=a["pool_scale"][l].reshape(1, -1),
        gm_norm_w=a["gm_norm_w"][l].reshape(1, -1), gm_ws=a["gm_ws"][l].astype(BF16), gm_bias=gm_bias,
        w_br_attn=a["w_br_attn"][l].astype(BF16), w_br_hyena=a["w_br_hyena"][l].astype(BF16),
        w_br_pool=a["w_br_pool"][l].astype(BF16), w_br_gmlp=a["w_br_gmlp"][l].astype(BF16),
        w_out=a["w_out"][l].astype(BF16),
        ffn_w_gate=a["ffn_w_gate"][l].astype(BF16), ffn_w_up=a["ffn_w_up"][l].astype(BF16),
        ffn_w_down=a["ffn_w_down"][l].astype(BF16),
        norm1_w=a["norm1_w"][l], norm2_w=a["norm2_w"][l],
    )


Q_SCALE = (HEAD_DIM ** -0.5) * math.log2(math.e)


def _attend(p, p_ctx, rope, rope_ctx, lw):
    def run(fp8):
        qw, kw = (lw["qw8"], lw["kw8"]) if fp8 else (lw["qw16"], lw["kw16"])
        qt, k, vt = _qkprep_call(p, rope[0], rope[1], qw, kw, fp8)
        if p_ctx is not None:
            _, k_c, vt_c = _qkprep_call(p_ctx, rope_ctx[0], rope_ctx[1], qw, kw, fp8)
            k = jnp.concatenate([k_c, k], axis=2)
            vt = jnp.concatenate([vt_c, vt], axis=3)
        return _attn_call(qt, k, vt)

    return lax.cond(lw["fp8_ok"], lambda: run(True), lambda: run(False))


def _mixer_block(x, mods, lw, rope, p_ctx, rope_ctx, kf, fc, final_w=None):
    sh1, s1, g1, sh2, s2, g2 = mods
    p = _inproj_call(x, sh1, s1, lw["norm1_w"], lw["w_in"], P_TILE)
    attn = _attend(p, p_ctx, rope, rope_ctx, lw)
    hv, hx1, hx2, po, gm = _local_call(p, lw)
    hy = _hyena(hv, hx1, hx2, kf, lw, fc)
    x = _merge_call(attn, hy, po, gm, x, sh1, s1, lw["norm1_w"], g1, lw)
    x = _ffn_call(x, sh2, s2, lw["norm2_w"], g2, lw["ffn_w_gate"], lw["ffn_w_up"], lw["ffn_w_down"], final_w)
    return x, p


def kernel(x, c, ctx, c_ctx, w_mod, b_mod, norm1_w, norm2_w, w_in, q_norm_w, k_norm_w, hy_conv_w, hy_conv_b,
           hy_w1, hy_b1, hy_w2, hy_b2, hy_w3, hy_freq, hy_decay, hy_skip, pool_w, pool_scale, gm_norm_w, gm_ws,
           gm_bs, w_br_attn, w_br_hyena, w_br_pool, w_br_gmlp, w_out, ffn_w_gate, ffn_w_up, ffn_w_down,
           final_norm_w):
    a = dict(w_in=w_in, q_norm_w=q_norm_w, k_norm_w=k_norm_w, hy_conv_w=hy_conv_w, hy_conv_b=hy_conv_b,
             hy_w1=hy_w1, hy_b1=hy_b1, hy_w2=hy_w2, hy_b2=hy_b2, hy_w3=hy_w3, hy_freq=hy_freq, hy_decay=hy_decay,
             hy_skip=hy_skip, pool_w=pool_w, pool_scale=pool_scale, gm_norm_w=gm_norm_w, gm_ws=gm_ws, gm_bs=gm_bs,
             w_br_attn=w_br_attn, w_br_hyena=w_br_hyena, w_br_pool=w_br_pool, w_br_gmlp=w_br_gmlp, w_out=w_out,
             ffn_w_gate=ffn_w_gate, ffn_w_up=ffn_w_up, ffn_w_down=ffn_w_down, norm1_w=norm1_w, norm2_w=norm2_w)
    B, n, D = x.shape
    nc = ctx.shape[1]
    depth = w_mod.shape[0]

    R = ((B + 1 + 15) // 16) * 16
    act = jnp.zeros((R, D), F32).at[:B].set(c).at[B].set(c_ctx)
    mod = _mod_call(act, w_mod, b_mod)

    rope = _rope_tables(n)
    rope_ctx = _identity_tables(nc)
    fc = _fft_consts(2 * n)
    fc_ctx = _fft_consts(2 * nc)

    xc = ctx
    for l in range(depth):
        lw = _layer_weights(l, a)
        mods = [m.reshape(B, 1, D) for m in jnp.split(mod[l, :B], 6, axis=-1)]
        cmods = [jnp.broadcast_to(m.reshape(1, 1, D), (B, 1, D)) for m in jnp.split(mod[l, B], 6, axis=-1)]
        if l < depth - 1:
            kf_ctx = _filter_spectra(nc, lw, fc_ctx)
            xc, pc = _mixer_block(xc, cmods, lw, rope_ctx, None, None, kf_ctx, fc_ctx)
        else:
            pc = _inproj_call(xc, cmods[0], cmods[1], lw["norm1_w"], lw["w_in"][:, :P_TILE], P_TILE)
        kf = _filter_spectra(n, lw, fc)
        x, _ = _mixer_block(x, mods, lw, rope, pc, rope_ctx, kf, fc, final_norm_w if l == depth - 1 else None)
    return x
```

```python
import functools
import math

import numpy as np
import jax
import jax.numpy as jnp
from jax import lax
from jax.experimental import pallas as pl
from jax.experimental.pallas import tpu as pltpu

F32 = jnp.float32
BF16 = jnp.bfloat16

D_MODEL = 1024
GRID_W = 64
RMS_EPS = 1e-6
HEAD_DIM = 64
N_Q_HEADS = 8
N_KV_HEADS = 2
ATTN_W = N_Q_HEADS * HEAD_DIM
KV_W = N_KV_HEADS * HEAD_DIM
ROPE_THETA = 10000.0
HYENA_W = 256
HYENA_ORDER = 2
HYENA_BANDS = 16
HYENA_HIDDEN = 64
HYENA_INNER = 2
POOL_W = 256
POOL_WINDOWS = (2, 4, 8, 16)
POOL_GROUP_W = POOL_W // len(POOL_WINDOWS)
POOL_HALO = 8
GMLP_W = 256
GMLP_CHUNK = 128
GMLP_GROUPS = 4
GMLP_GROUP_W = GMLP_W // GMLP_GROUPS
N_BRANCH = 4
FFN_HIDDEN = ((8 * D_MODEL // 3 + 255) // 256) * 256

Q_OFF = 0
K_OFF = Q_OFF + ATTN_W
V_OFF = K_OFF + KV_W
HY_OFF = V_OFF + KV_W
POOL_OFF = HY_OFF + 3 * HYENA_W
GM_OFF = POOL_OFF + POOL_W
GATE_OFF = GM_OFF + 2 * GMLP_W
IN_W = GATE_OFF + N_BRANCH * D_MODEL

P_Q = 0
P_KV = P_Q + ATTN_W
P_POOL = P_KV + 2 * KV_W
P_GMU = P_POOL + POOL_W
P_GMV = P_GMU + GMLP_W
P_HY = P_GMV + GMLP_W
P_W = P_HY + 3 * HYENA_W
P_TILE = 768

LANES = 128
VMEM_LIMIT = 52 * 1024 * 1024


def _cparams(sem):
    return pltpu.CompilerParams(dimension_semantics=sem, vmem_limit_bytes=VMEM_LIMIT)


def _tile(n, cap, mult=8):
    best = None
    for t in range(mult, min(n, cap) + 1, mult):
        if n % t == 0:
            best = t
    assert best is not None, (n, cap, mult)
    return best


def _split(x):
    hi = x.astype(BF16)
    lo = (x - hi.astype(F32)).astype(BF16)
    return hi, lo


def _dot(a, b):
    return jnp.dot(a, b, preferred_element_type=F32)


def _dot3(a_hi, a_lo, x):
    x_hi, x_lo = _split(x)
    return _dot(a_hi, x_hi) + (_dot(a_lo, x_hi) + _dot(a_hi, x_lo))


def _dot3x(x, b_hi, b_lo):
    x_hi, x_lo = _split(x)
    return _dot(x_hi, b_hi) + (_dot(x_lo, b_hi) + _dot(x_hi, b_lo))


def _sigmoid(x):
    return 1.0 / (1.0 + jnp.exp(-x))


def _silu(x):
    return x * _sigmoid(x)


def _modulated_norm(x, nw, scale, shift):
    ms = jnp.mean(x * x, axis=-1, keepdims=True)
    y = x * lax.rsqrt(ms + RMS_EPS) * nw
    return y * (1.0 + scale) + shift


def _mod_kernel(a_ref, w_ref, b_ref, o_ref):
    a = _silu(a_ref[...]).astype(BF16)
    o_ref[...] = _dot(a, w_ref[...].astype(BF16)) + b_ref[...]


def _mod_call(act, w_mod, b_mod):
    L, D, W6 = w_mod.shape
    R = act.shape[0]
    tn = 1024
    return pl.pallas_call(
        _mod_kernel,
        grid=(L, W6 // tn),
        in_specs=[
            pl.BlockSpec((R, D), lambda l, j: (0, 0)),
            pl.BlockSpec((None, D, tn), lambda l, j: (l, 0, j)),
            pl.BlockSpec((None, 1, tn), lambda l, j: (l, 0, j)),
        ],
        out_specs=pl.BlockSpec((None, R, tn), lambda l, j: (l, 0, j)),
        out_shape=jax.ShapeDtypeStruct((L, R, W6), F32),
        compiler_params=_cparams(("arbitrary", "arbitrary")),
        name="mod",
    )(act, w_mod, b_mod.reshape(L, 1, W6))


def _inproj_kernel(x_ref, sh_ref, sc_ref, nw_ref, w_ref, o_ref, h_ref):
    @pl.when(pl.program_id(2) == 0)
    def _():
        h_ref[...] = _modulated_norm(x_ref[...], nw_ref[...], sc_ref[...], sh_ref[...]).astype(BF16)

    o_ref[...] = _dot(h_ref[...], w_ref[...])


INPROJ_ROWS = 1024


def _inproj_call(x, shift, scale, norm_w, w, tn):
    B, n, D = x.shape
    Nw = w.shape[1]
    tm = _tile(n, INPROJ_ROWS)
    return pl.pallas_call(
        _inproj_kernel,
        grid=(B, n // tm, Nw // tn),
        in_specs=[
            pl.BlockSpec((None, tm, D), lambda b, i, j: (b, i, 0)),
            pl.BlockSpec((None, 1, D), lambda b, i, j: (b, 0, 0)),
            pl.BlockSpec((None, 1, D), lambda b, i, j: (b, 0, 0)),
            pl.BlockSpec((1, D), lambda b, i, j: (0, 0)),
            pl.BlockSpec((D, tn), lambda b, i, j: (0, j)),
        ],
        out_specs=pl.BlockSpec((None, tm, tn), lambda b, i, j: (b, i, j)),
        out_shape=jax.ShapeDtypeStruct((B, n, Nw), F32),
        scratch_shapes=[pltpu.VMEM((tm, D), BF16)],
        compiler_params=_cparams(("parallel", "parallel", "arbitrary")),
        name="inproj",
    )(x, shift, scale, norm_w.reshape(1, D), w)


def _head_norm_rope(t, w, cs, sn, bd_ref):
    sq_hi, sq_lo = _split(t * t)
    ms = _dot(sq_hi, bd_ref[...]) + _dot(sq_lo, bd_ref[...])
    y = t * lax.rsqrt(ms + RMS_EPS) * w
    lane = lax.broadcasted_iota(jnp.int32, y.shape, 1)
    swapped = jnp.where(lane % 2 == 0, pltpu.roll(y, LANES - 1, 1), pltpu.roll(y, 1, 1))
    return y * cs + swapped * sn


V_ROWS = 80
Q_PER_KV = N_Q_HEADS // N_KV_HEADS
LAG_INIT_KEYS = 128
LAG_LIMIT = 24.0
ATTN_QUERIES = 512
ATTN_KEYS = 768


def _value_rows(vt_g):
    tail = lax.broadcasted_iota(jnp.int32, (V_ROWS - HEAD_DIM, vt_g.shape[1]), 0)
    return jnp.concatenate([vt_g, jnp.where(tail == 0, 1.0, 0.0)], axis=0).astype(BF16)


FP8 = jnp.float8_e4m3fn
FP8_MAX_WEIGHT = 32.0


def _split8(x):
    hi = x.astype(FP8).astype(F32)
    return hi, (x - hi).astype(FP8).astype(F32)


def _qkprep_kernel(q_ref, kv_ref, cs_ref, sn_ref, qw_ref, kw_ref, bd_ref, kp_ref, qo_ref, ko_ref, vo_ref, *, fp8):
    cs = cs_ref[...]
    sn = sn_ref[...]
    tm = q_ref.shape[0]
    zeros = jnp.zeros((HEAD_DIM, tm), F32)
    heads_per_chunk = LANES // HEAD_DIM
    for c in range(ATTN_W // LANES):
        sl = slice(c * LANES, (c + 1) * LANES)
        r = _head_norm_rope(q_ref[:, sl], qw_ref[:, sl], cs, sn, bd_ref)
        if fp8:
            hi, lo = _split8(r)
            hi_t, lo_t = hi.T, lo.T
            for half in range(heads_per_chunk):
                rows = slice(half * HEAD_DIM, (half + 1) * HEAD_DIM)
                qo_ref[c * heads_per_chunk + half] = jnp.concatenate(
                    [hi_t[rows], lo_t[rows], hi_t[rows], lo_t[rows]], axis=0).astype(FP8)
        else:
            rt = r.T
            for half in range(heads_per_chunk):
                h = c * heads_per_chunk + half
                blk = rt[half * HEAD_DIM:(half + 1) * HEAD_DIM]
                parts = [blk if g == h // Q_PER_KV else zeros for g in range(N_KV_HEADS)]
                qo_ref[h] = jnp.concatenate(parts, axis=0).astype(BF16)
    k = _head_norm_rope(kv_ref[:, :KV_W], kw_ref[...], cs, sn, bd_ref)
    if fp8:
        hi, lo = _split8(k)
        hi, lo = hi.astype(BF16), lo.astype(BF16)
        for g in range(N_KV_HEADS):
            ko_ref[g] = (_dot(hi, kp_ref[g, 0]) + _dot(lo, kp_ref[g, 1])).astype(FP8)
    else:
        ko_ref[0] = k.astype(BF16)
    vt = kv_ref[:, KV_W:].T
    for g in range(N_KV_HEADS):
        vo_ref[g] = _value_rows(vt[g * HEAD_DIM:(g + 1) * HEAD_DIM])


def _k_placement():
    pm = np.zeros((N_KV_HEADS, 2, KV_W, 4 * HEAD_DIM), np.float32)
    d = np.arange(HEAD_DIM)
    for g in range(N_KV_HEADS):
        for part in range(2):
            for rep in range(2):
                pm[g, part, g * HEAD_DIM + d, (2 * part + rep) * HEAD_DIM + d] = 1.0
    return jnp.asarray(pm, BF16)


def _qkprep_call(p, cs, sn, qw, kw, fp8):
    B, n, _ = p.shape
    tm = _tile(n, 512, LANES)
    bd = np.kron(np.eye(LANES // HEAD_DIM), np.full((HEAD_DIM, HEAD_DIM), 1.0 / HEAD_DIM))
    bd = jnp.asarray(bd, BF16)
    kd, kg, dt = (4 * HEAD_DIM, N_KV_HEADS, FP8) if fp8 else (KV_W, 1, BF16)
    return pl.pallas_call(
        functools.partial(_qkprep_kernel, fp8=fp8),
        grid=(n // tm, B),
        in_specs=[
            pl.BlockSpec((None, tm, ATTN_W), lambda i, b: (b, i, P_Q // ATTN_W)),
            pl.BlockSpec((None, tm, 2 * KV_W), lambda i, b: (b, i, P_KV // (2 * KV_W))),
            pl.BlockSpec((tm, LANES), lambda i, b: (i, 0)),
            pl.BlockSpec((tm, LANES), lambda i, b: (i, 0)),
            pl.BlockSpec((1, ATTN_W), lambda i, b: (0, 0)),
            pl.BlockSpec((1, KV_W), lambda i, b: (0, 0)),
            pl.BlockSpec((LANES, LANES), lambda i, b: (0, 0)),
            pl.BlockSpec((N_KV_HEADS, 2, KV_W, 4 * HEAD_DIM), lambda i, b: (0, 0, 0, 0)),
        ],
        out_specs=[
            pl.BlockSpec((None, N_Q_HEADS, kd, tm), lambda i, b: (b, 0, 0, i)),
            pl.BlockSpec((None, kg, tm, kd), lambda i, b: (b, 0, i, 0)),
            pl.BlockSpec((None, N_KV_HEADS, V_ROWS, tm), lambda i, b: (b, 0, 0, i)),
        ],
        out_shape=[
            jax.ShapeDtypeStruct((B, N_Q_HEADS, kd, n), dt),
            jax.ShapeDtypeStruct((B, kg, n, kd), dt),
            jax.ShapeDtypeStruct((B, N_KV_HEADS, V_ROWS, n), BF16),
        ],
        compiler_params=_cparams(("parallel", "parallel")),
        name="qkprep",
    )(p, p, cs, sn, qw, kw, bd, _k_placement())


def _attn_kernel(qt_ref, k_ref, vt_ref, o_ref, m_ref, acc_ref, *, tk):
    nk = vt_ref.shape[0]
    tq = m_ref.shape[-1]
    per_kv_head = k_ref.shape[0] > 1

    def keys(j, h):
        g = h // Q_PER_KV if per_kv_head else 0
        return k_ref[g, pl.ds(pl.multiple_of(j * tk, tk), tk), :]

    def lagged_chunk(j, jump):
        s_next = _dot(keys(j, 0), qt_ref[0])
        for h in range(N_Q_HEADS):
            s = s_next
            if h + 1 < N_Q_HEADS:
                s_next = _dot(keys(j, h + 1), qt_ref[h + 1])
            m_old = m_ref[h]
            p = jnp.exp2((s - m_old).astype(BF16))
            cmax = jnp.max(s, axis=0, keepdims=True)
            m_new = jnp.maximum(m_old, cmax)
            jump = jnp.maximum(jump, cmax - m_old)
            acc_ref[h] = jnp.exp2(m_old - m_new) * (acc_ref[h] + _dot(vt_ref[j, h // Q_PER_KV], p))
            m_ref[h] = m_new
        return jump

    def exact_chunk(j, carry):
        for h in range(N_Q_HEADS):
            s = _dot(keys(j, h), qt_ref[h])
            m_old = m_ref[h]
            m_new = jnp.maximum(m_old, jnp.max(s, axis=0, keepdims=True))
            p = jnp.exp2(s - m_new).astype(BF16)
            acc_ref[h] = jnp.exp2(m_old - m_new) * acc_ref[h] + _dot(vt_ref[j, h // Q_PER_KV], p)
            m_ref[h] = m_new
        return carry

    for h in range(N_Q_HEADS):
        k0 = k_ref[h // Q_PER_KV if per_kv_head else 0, 0:LAG_INIT_KEYS, :]
        m_ref[h] = jnp.max(_dot(k0, qt_ref[h]), axis=0, keepdims=True)
    acc_ref[...] = jnp.zeros(acc_ref.shape, F32)
    jump = lax.fori_loop(0, nk, lagged_chunk, jnp.zeros((1, tq), F32))

    @pl.when(jnp.logical_not(jnp.max(jump) <= LAG_LIMIT))
    def _():
        m_ref[...] = jnp.full(m_ref.shape, -jnp.inf, F32)
        acc_ref[...] = jnp.zeros(acc_ref.shape, F32)
        lax.fori_loop(0, nk, exact_chunk, 0)

    for c in range(ATTN_W // LANES):
        heads = []
        for h in range(c * (LANES // HEAD_DIM), (c + 1) * (LANES // HEAD_DIM)):
            a = acc_ref[h]
            heads.append(a[:HEAD_DIM] / a[HEAD_DIM:HEAD_DIM + 1])
        o_ref[:, c * LANES:(c + 1) * LANES] = jnp.concatenate(heads, axis=0).T.astype(BF16)


def _attn_call(qt, k, vt):
    B, _, kd, n = qt.shape
    _, kg, Lk, _ = k.shape
    tq = _tile(n, ATTN_QUERIES, LANES)
    tk = _tile(Lk, ATTN_KEYS, LANES)
    nk = Lk // tk
    vt = vt.reshape(B, N_KV_HEADS, V_ROWS, nk, tk).transpose(0, 3, 1, 2, 4)
    return pl.pallas_call(
        functools.partial(_attn_kernel, tk=tk),
        grid=(B, n // tq),
        in_specs=[
            pl.BlockSpec((None, N_Q_HEADS, kd, tq), lambda b, i: (b, 0, 0, i)),
            pl.BlockSpec((None, kg, Lk, kd), lambda b, i: (b, 0, 0, 0)),
            pl.BlockSpec((None, nk, N_KV_HEADS, V_ROWS, tk), lambda b, i: (b, 0, 0, 0, 0)),
        ],
        out_specs=pl.BlockSpec((None, tq, ATTN_W), lambda b, i: (b, i, 0)),
        out_shape=jax.ShapeDtypeStruct((B, n, ATTN_W), BF16),
        scratch_shapes=[
            pltpu.VMEM((N_Q_HEADS, 1, tq), F32),
            pltpu.VMEM((N_Q_HEADS, V_ROWS, tq), F32),
        ],
        compiler_params=_cparams(("parallel", "parallel")),
        name="attn",
    )(qt, k, vt)


HALVES = HYENA_W // LANES


def _split_shape(B, n):
    assert B % 2 == 0
    return jax.ShapeDtypeStruct((B // 2, HALVES, 2, n, LANES), F32)


def _split_spec(tm):
    return pl.BlockSpec((None, HALVES, None, tm, LANES), lambda b, i: (b // 2, 0, b % 2, i, 0))


def _shift_rows(cur, prev_row, next_row, rows):
    tm = cur.shape[0]
    up = jnp.where(rows == 0, prev_row, pltpu.roll(cur, 1, 0))
    dn = jnp.where(rows == tm - 1, next_row, pltpu.roll(cur, tm - 1, 0))
    return up, dn


def _local_kernel(hy_ref, hyp_ref, hyn_ref, po_ref, pop_ref, pon_ref, gu_ref, gv_ref,
                  cw_ref, cb_ref, pw_ref, ps_ref, gnw_ref, gws_ref, gb_ref,
                  v_ref, x1_ref, x2_ref, pl_ref, gm_ref, *, n_seq):
    i = pl.program_id(1)
    last = pl.num_programs(1) - 1
    tm = hy_ref.shape[0]
    has_prev = (i > 0).astype(F32)
    has_next = (i < last).astype(F32)

    cur = hy_ref[...]
    rows = lax.broadcasted_iota(jnp.int32, cur.shape, 0)
    prev_row = hyp_ref[POOL_HALO - 1:POOL_HALO, :] * has_prev
    next_row = hyn_ref[0:1, :] * has_next
    up, dn = _shift_rows(cur, prev_row, next_row, rows)
    z = up * cw_ref[0:1, :] + cur * cw_ref[1:2, :] + dn * cw_ref[2:3, :] + cb_ref[...]
    for s, o_ref in enumerate((v_ref, x1_ref, x2_ref)):
        for hf in range(HYENA_W // LANES):
            lo = s * HYENA_W + hf * LANES
            o_ref[hf] = z[:, lo:lo + LANES]

    zc = po_ref[...]
    ext = jnp.concatenate([pop_ref[...] * has_prev, zc, pon_ref[...] * has_next], axis=0)
    te = tm + 2 * POOL_HALO
    lane = lax.broadcasted_iota(jnp.int32, zc.shape, 1)
    prow = lax.broadcasted_iota(jnp.int32, zc.shape, 0) + i * tm
    half = jnp.left_shift(1, lane // POOL_GROUP_W)
    def ahead(a, k):
        return pltpu.roll(a, (te - k) % te, 0)

    assert POOL_WINDOWS == (2, 4, 8, 16)
    sums = [ext + ahead(ext, -1)]
    for win in POOL_WINDOWS[:-1]:
        sums.append(ahead(sums[-1], win // 2) + ahead(sums[-1], -(win // 2)))
    group = lane // POOL_GROUP_W
    acc = sums[-1][POOL_HALO:POOL_HALO + tm]
    for g in range(len(POOL_WINDOWS) - 2, -1, -1):
        acc = jnp.where(group == g, sums[g][POOL_HALO:POOL_HALO + tm], acc)
    cnt = jnp.minimum(prow + half, n_seq) - jnp.maximum(prow - half, 0)
    dlt = acc / cnt.astype(F32) - zc
    pl_ref[...] = (_dot(dlt.astype(BF16), pw_ref[...]) * ps_ref[...]).astype(BF16)

    gv = gv_ref[...]
    mu = jnp.mean(gv, axis=-1, keepdims=True)
    ctr = gv - mu
    var = jnp.mean(ctr * ctr, axis=-1, keepdims=True)
    vn = (ctr * lax.rsqrt(var + RMS_EPS) * gnw_ref[...]).astype(BF16)
    group = lax.broadcasted_iota(jnp.int32, (GMLP_CHUNK, GMLP_W), 1) // GMLP_GROUP_W
    for c in range(tm // GMLP_CHUNK):
        rs = slice(c * GMLP_CHUNK, (c + 1) * GMLP_CHUNK)
        vc = vn[rs]
        mixed = gb_ref[...]
        for g in range(GMLP_GROUPS):
            mg = _dot(gws_ref[g], vc)
            mixed = mixed + jnp.where(group == g, mg, 0.0)
        gm_ref[rs, :] = (gu_ref[rs, :] * mixed).astype(BF16)


def _local_call(p, lw):
    B, n, _ = p.shape
    tm = _tile(n, 512, GMLP_CHUNK)
    nb8 = n // POOL_HALO
    r8 = tm // POOL_HALO
    HW = 3 * HYENA_W

    def cur(w, off):
        return pl.BlockSpec((None, tm, w), lambda b, i: (b, i, off // w))

    def prev(w, off):
        return pl.BlockSpec((None, POOL_HALO, w), lambda b, i: (b, jnp.maximum(i * r8 - 1, 0), off // w))

    def nxt(w, off):
        return pl.BlockSpec((None, POOL_HALO, w), lambda b, i: (b, jnp.minimum((i + 1) * r8, nb8 - 1), off // w))

    def full(shape):
        return pl.BlockSpec(shape, lambda b, i: (0,) * len(shape))

    def out(w):
        return pl.BlockSpec((None, tm, w), lambda b, i: (b, i, 0))

    return pl.pallas_call(
        functools.partial(_local_kernel, n_seq=n),
        grid=(B, n // tm),
        in_specs=[
            cur(HW, P_HY), prev(HW, P_HY), nxt(HW, P_HY),
            cur(POOL_W, P_POOL), prev(POOL_W, P_POOL), nxt(POOL_W, P_POOL),
            cur(GMLP_W, P_GMU), cur(GMLP_W, P_GMV),
            full((3, HW)), full((1, HW)), full((POOL_W, POOL_W)), full((1, POOL_W)),
            full((1, GMLP_W)), full((GMLP_GROUPS, GMLP_CHUNK, GMLP_CHUNK)), full((GMLP_CHUNK, GMLP_W)),
        ],
        out_specs=[_split_spec(tm), _split_spec(tm), _split_spec(tm), out(POOL_W), out(GMLP_W)],
        out_shape=[
            _split_shape(B, n), _split_shape(B, n), _split_shape(B, n),
            jax.ShapeDtypeStruct((B, n, POOL_W), BF16),
            jax.ShapeDtypeStruct((B, n, GMLP_W), BF16),
        ],
        compiler_params=_cparams(("parallel", "parallel")),
        name="local",
    )(p, p, p, p, p, p, p, p, lw["conv_w"], lw["conv_b"], lw["pool_bd"], lw["pool_scale"],
      lw["gm_norm_w"], lw["gm_ws"], lw["gm_bias"])


def _filt_kernel(z_ref, t_ref, w1h_ref, w1l_ref, b1_ref, w2h_ref, w2l_ref, b2_ref, fr_ref,
                 w3h_ref, w3l_ref, dec_ref, k_ref, s_ref, *, n_seq):
    i = pl.program_id(0)
    tb = z_ref.shape[0]
    fr = fr_ref[...]
    hdn = jnp.sin(fr * (_dot3x(z_ref[...], w1h_ref[...], w1l_ref[...]) + b1_ref[...]))
    for u in range(HYENA_INNER):
        hdn = jnp.sin(fr * (_dot3x(hdn, w2h_ref[u], w2l_ref[u]) + b2_ref[u:u + 1, :]))
    t = t_ref[...]
    tt = jnp.concatenate([t] * (HYENA_ORDER * HYENA_W // LANES), axis=1)
    window = jnp.exp(-tt * jnp.abs(dec_ref[...]))
    k = _dot3x(hdn, w3h_ref[...], w3l_ref[...]) * window
    rows = lax.broadcasted_iota(jnp.int32, k.shape, 0) + i * tb
    k = jnp.where(rows == n_seq, 0.0, k)
    for q in range(k_ref.shape[0]):
        k_ref[q] = k[:, q * LANES:(q + 1) * LANES]

    @pl.when(i == 0)
    def _():
        s_ref[...] = jnp.zeros(s_ref.shape, F32)

    s_ref[...] += jnp.sum(jnp.abs(k), axis=0, keepdims=True)


def _filt_call(n, lw):
    N = 2 * n
    tb = _tile(n, 512)
    nb_half = n // tb
    pos = np.arange(N)
    pos = np.where(pos < n, pos, N - pos).astype(np.float64)
    t = pos / (n - 1)
    bands = np.linspace(1e-4, HYENA_BANDS - 1, HYENA_BANDS)
    ang = (2.0 * math.pi / n) * pos[:, None] * bands
    z = np.concatenate([t[:, None], np.cos(ang), np.sin(ang)], axis=-1)
    zp = np.zeros((N, LANES), np.float32)
    zp[:, :z.shape[1]] = z
    tl = np.broadcast_to(t[:, None], (N, LANES)).astype(np.float32)
    CW = HYENA_ORDER * HYENA_W

    def full(shape):
        return pl.BlockSpec(shape, lambda i: (0,) * len(shape))

    def side(rows_):
        return pl.BlockSpec((None, rows_, CW), lambda i: (i // nb_half, 0, 0))

    return pl.pallas_call(
        functools.partial(_filt_kernel, n_seq=n),
        grid=(N // tb,),
        in_specs=[
            pl.BlockSpec((tb, LANES), lambda i: (i, 0)),
            pl.BlockSpec((tb, LANES), lambda i: (i, 0)),
            full((LANES, HYENA_HIDDEN)), full((LANES, HYENA_HIDDEN)), full((1, HYENA_HIDDEN)),
            full((HYENA_INNER, HYENA_HIDDEN, HYENA_HIDDEN)), full((HYENA_INNER, HYENA_HIDDEN, HYENA_HIDDEN)),
            full((HYENA_INNER, HYENA_HIDDEN)), full((1, HYENA_HIDDEN)),
            side(HYENA_HIDDEN), side(HYENA_HIDDEN), side(1),
        ],
        out_specs=[pl.BlockSpec((CW // LANES, tb, LANES), lambda i: (0, i, 0)), pl.BlockSpec((1, CW), lambda i: (0, 0))],
        out_shape=[jax.ShapeDtypeStruct((CW // LANES, N, LANES), F32), jax.ShapeDtypeStruct((1, CW), F32)],
        compiler_params=_cparams(("arbitrary",)),
        name="filt",
    )(jnp.asarray(zp), jnp.asarray(tl), lw["hy_w1h"], lw["hy_w1l"], lw["hy_b1"], lw["hy_w2h"], lw["hy_w2l"],
      lw["hy_b2"], lw["hy_freq"], lw["hy_w3h"], lw["hy_w3l"], lw["hy_decay"])


def _fft_dims(N):
    e = int(round(math.log2(N)))
    assert 2 ** e == N
    N1 = 2 ** ((e + 1) // 2)
    return N1, N // N1


def _hilo(a):
    a = np.asarray(a, np.float64)
    hi = jnp.asarray(a, F32).astype(BF16)
    lo = (jnp.asarray(a, F32) - hi.astype(F32)).astype(BF16)
    return hi, lo


@functools.lru_cache(maxsize=None)
def _fft_consts_np(N):
    N1, N2 = _fft_dims(N)
    h = N1 // 2
    k1 = np.arange(N1)
    F1 = np.exp(-2j * np.pi * np.outer(k1, k1) / N1)
    k2 = np.arange(N2)
    F2 = np.exp(-2j * np.pi * np.outer(k2, k2) / N2)
    tw = np.exp(-2j * np.pi * np.outer(k1, k2) / N)
    g1 = np.zeros((2 * N1, N1))
    g1[0::2, :h] = F1.real[:, :h]
    g1[0::2, h:] = -F1.imag[:, :h]
    g1[1::2, :h] = F1.imag[:, :h]
    g1[1::2, h:] = F1.real[:, :h]
    g1f = np.zeros((2 * N1, N1))
    g1f[0::2] = F1.real
    g1f[1::2] = F1.imag
    g2 = np.block([[F2.real, -F2.imag], [F2.imag, F2.real]])
    g2c = np.block([[F2.real, F2.imag], [-F2.imag, F2.real]])
    g3 = np.zeros((N1, 2 * N1))
    g3[:h, 0::2] = F1.real[:h]
    g3[:h, 1::2] = F1.imag[:h]
    g3[h:, 0::2] = -F1.imag[:h]
    g3[h:, 1::2] = F1.real[:h]
    g3 /= N
    twt = np.stack([tw.real, tw.imag], axis=1)
    return dict(g1=g1, g1f=g1f, g2=g2, g2c=g2c, g3=g3, tw=twt)


def _fft_consts(N):
    c = _fft_consts_np(N)
    out = {k: _hilo(c[k]) for k in ("g1", "g1f", "g2", "g2c", "g3")}
    N1, N2 = _fft_dims(N)
    tw = jnp.asarray(c["tw"], F32)
    out["tw"] = jnp.broadcast_to(tw[..., None], (N1, 2, N2, LANES))
    return out


T2_BLOCK = 8


def _load_8th(ref, j, rows):
    return ref.reshape(rows * T2_BLOCK, LANES)[pl.ds(j, rows, stride=T2_BLOCK), :]


def _store_8th(ref, j, rows, val):
    ref.reshape(rows * T2_BLOCK, LANES)[pl.ds(j, rows, stride=T2_BLOCK), :] = val


def _dft_in_kernel(gh_ref, gl_ref, x_ref, o_ref):
    N1 = x_ref.shape[1]
    for j in range(T2_BLOCK):
        xj = jnp.concatenate([_load_8th(x_ref.at[hf], j, N1) for hf in range(HALVES)], axis=1)
        a = _dot3(gh_ref[...], gl_ref[...], xj)
        for hf in range(HALVES):
            _store_8th(o_ref.at[hf], j, 2 * N1, a[:, hf * LANES:(hf + 1) * LANES])


def _dft_in_call(g, u5):
    gh, gl = g
    G, _, N1, N2, _ = u5.shape
    assert gh.shape == (2 * N1, N1)
    mat = pl.BlockSpec((2 * N1, N1), lambda b, i: (0, 0))
    return pl.pallas_call(
        _dft_in_kernel,
        grid=(G, N2 // T2_BLOCK),
        in_specs=[mat, mat, pl.BlockSpec((None, HALVES, N1, T2_BLOCK, LANES), lambda b, i: (b, 0, 0, i, 0))],
        out_specs=pl.BlockSpec((None, HALVES, N1, 2, T2_BLOCK, LANES), lambda b, i: (b, 0, 0, 0, i, 0)),
        out_shape=jax.ShapeDtypeStruct((G, HALVES, N1, 2, N2, LANES), F32),
        compiler_params=_cparams(("parallel", "parallel")),
        name="dft_in",
    )(gh, gl, u5)


def _dft_out_kernel(gh_ref, gl_ref, d_ref, gate_ref, u_ref, sk_ref, o_ref):
    N1 = gate_ref.shape[1]
    for j in range(T2_BLOCK):
        dj = jnp.concatenate([_load_8th(d_ref.at[hf], j, 2 * N1) for hf in range(HALVES)], axis=1)
        y = _dot3(gh_ref[...], gl_ref[...], dj)
        for hf in range(HALVES):
            conv = y[:, hf * LANES:(hf + 1) * LANES]
            u = _load_8th(u_ref.at[hf], j, N1)
            _store_8th(o_ref.at[hf], j, N1, _load_8th(gate_ref.at[hf], j, N1) * (conv + u * sk_ref[hf]))


def _dft_out_call(g, d6, gate5, u5, skip):
    gh, gl = g
    G, _, N1, N2, _ = u5.shape
    assert gh.shape == (N1, 2 * N1)
    mat = pl.BlockSpec((N1, 2 * N1), lambda b, i: (0, 0))
    blk = pl.BlockSpec((None, HALVES, N1, T2_BLOCK, LANES), lambda b, i: (b, 0, 0, i, 0))
    return pl.pallas_call(
        _dft_out_kernel,
        grid=(G, N2 // T2_BLOCK),
        in_specs=[mat, mat, pl.BlockSpec((None, HALVES, N1, 2, T2_BLOCK, LANES), lambda b, i: (b, 0, 0, 0, i, 0)),
                  blk, blk, pl.BlockSpec((HALVES, 1, LANES), lambda b, i: (0, 0, 0))],
        out_specs=blk,
        out_shape=jax.ShapeDtypeStruct(u5.shape, F32),
        compiler_params=_cparams(("parallel", "parallel")),
        name="dft_out",
    )(gh, gl, d6, gate5, u5, skip)


def _cmul(ar, ai, br, bi):
    return ar * br - ai * bi, ar * bi + ai * br


def _lane_tile(t, c):
    return jnp.concatenate([t] * (c // LANES), axis=-1)


def _both_halves(a_ref, k, part):
    return jnp.concatenate([a_ref[hf, k, part] for hf in range(HALVES)], axis=-1)


def _spec_kernel(a_ref, tw_ref, g2h_ref, g2l_ref, s_ref, o_ref):
    _, kb, _, N2, _ = a_ref.shape
    inv = 1.0 / s_ref[...]
    for k in range(kb):
        twr = _lane_tile(tw_ref[k, 0], HYENA_W)
        twi = _lane_tile(tw_ref[k, 1], HYENA_W)
        br, bi = _cmul(_both_halves(a_ref, k, 0), _both_halves(a_ref, k, 1), twr, twi)
        zz = _dot3(g2h_ref[...], g2l_ref[...], jnp.concatenate([br, bi], axis=0))
        o_ref[k, 0] = zz[:N2] * inv
        o_ref[k, 1] = zz[N2:] * inv


def _conv_kernel(a_ref, tw_ref, kf_ref, g2h_ref, g2l_ref, g2ch_ref, g2cl_ref, o_ref):
    _, kb, _, N2, _ = a_ref.shape
    for k in range(kb):
        twr = _lane_tile(tw_ref[k, 0], HYENA_W)
        twi = _lane_tile(tw_ref[k, 1], HYENA_W)
        br, bi = _cmul(_both_halves(a_ref, k, 0), _both_halves(a_ref, k, 1), twr, twi)
        zz = _dot3(g2h_ref[...], g2l_ref[...], jnp.concatenate([br, bi], axis=0))
        yr, yi = _cmul(zz[:N2], zz[N2:], kf_ref[k, 0], kf_ref[k, 1])
        cc = _dot3(g2ch_ref[...], g2cl_ref[...], jnp.concatenate([yr, yi], axis=0))
        dr, di = _cmul(cc[:N2], cc[N2:], twr, -twi)
        for hf in range(HALVES):
            o_ref[hf, k, 0] = dr[:, hf * LANES:(hf + 1) * LANES]
            o_ref[hf, k, 1] = di[:, hf * LANES:(hf + 1) * LANES]


def _spec_call(a6, fc, s):
    G, _, N1, _, N2, _ = a6.shape
    kb = _tile(N1, 8, 1)
    mat = pl.BlockSpec((2 * N2, 2 * N2), lambda i, b: (0, 0))
    return pl.pallas_call(
        _spec_kernel,
        grid=(N1 // kb, G),
        in_specs=[pl.BlockSpec((None, HALVES, kb, 2, N2, LANES), lambda i, b: (b, 0, i, 0, 0, 0)),
                  pl.BlockSpec((kb, 2, N2, LANES), lambda i, b: (i, 0, 0, 0)), mat, mat,
                  pl.BlockSpec((None, 1, HYENA_W), lambda i, b: (b, 0, 0))],
        out_specs=pl.BlockSpec((None, kb, 2, N2, HYENA_W), lambda i, b: (b, i, 0, 0, 0)),
        out_shape=jax.ShapeDtypeStruct((G, N1, 2, N2, HYENA_W), F32),
        compiler_params=_cparams(("parallel", "parallel")),
        name="filt_spec",
    )(a6, fc["tw"], fc["g2"][0], fc["g2"][1], s)


def _conv_call(a6, kf, fc):
    G, _, N1, _, N2, _ = a6.shape
    kb = _tile(N1, 8, 1)
    blk = pl.BlockSpec((None, HALVES, kb, 2, N2, LANES), lambda i, b: (b, 0, i, 0, 0, 0))
    mat = pl.BlockSpec((2 * N2, 2 * N2), lambda i, b: (0, 0))
    return pl.pallas_call(
        _conv_kernel,
        grid=(N1 // kb, G),
        in_specs=[blk, pl.BlockSpec((kb, 2, N2, LANES), lambda i, b: (i, 0, 0, 0)),
                  pl.BlockSpec((kb, 2, N2, HYENA_W), lambda i, b: (i, 0, 0, 0)), mat, mat, mat, mat],
        out_specs=blk,
        out_shape=jax.ShapeDtypeStruct(a6.shape, F32),
        compiler_params=_cparams(("parallel", "parallel")),
        name="spec_conv",
    )(a6, fc["tw"], kf, fc["g2"][0], fc["g2"][1], fc["g2c"][0], fc["g2c"][1])


def _filter_spectra(n, lw, fc):
    N1, N2 = _fft_dims(2 * n)
    k, s = _filt_call(n, lw)
    a6 = _dft_in_call(fc["g1f"], k.reshape(HYENA_ORDER, HALVES, N1, N2, LANES))
    return _spec_call(a6, fc, s.reshape(HYENA_ORDER, 1, HYENA_W))


def _hyena(v, x1, x2, kf, lw, fc):
    P, _, _, n, _ = v.shape
    N1, N2 = _fft_dims(2 * n)
    shape5 = (P, HALVES, N1, N2, LANES)
    y = v.reshape(shape5)
    for o, gate in enumerate((x1, x2)):
        d = _conv_call(_dft_in_call(fc["g1"], y), kf[o], fc)
        y = _dft_out_call(fc["g3"], d, gate.reshape(shape5), y, lw["hy_skip"][o].reshape(HALVES, 1, LANES))
    return y.reshape(v.shape)


def _merge_kernel(at_ref, hy_ref, po_ref, gm_ref, x_ref, sh_ref, sc_ref, nw_ref, gr_ref,
                  wgate_ref, wa_ref, wh_ref, wp_ref, wg_ref, wo_ref, o_ref):
    x = x_ref[...]
    D = x.shape[1]
    h = _modulated_norm(x, nw_ref[...], sc_ref[...], sh_ref[...]).astype(BF16)
    hy = jnp.concatenate([hy_ref[hf] for hf in range(HALVES)], axis=-1).astype(BF16)
    branches = ((at_ref[...], wa_ref), (hy, wh_ref), (po_ref[...], wp_ref), (gm_ref[...], wg_ref))
    merged = None
    for i, (br, w_ref) in enumerate(branches):
        term = _sigmoid(_dot(h, wgate_ref[:, i * D:(i + 1) * D])) * _dot(br, w_ref[...])
        merged = term if merged is None else merged + term
    o_ref[...] = x + gr_ref[...] * _dot(merged.astype(BF16), wo_ref[...])


def _merge_call(attn, hy, po, gm, x, shift, scale, norm_w, gate_res, lw):
    B, n, D = x.shape
    tm = _tile(n, 512)

    def rows(w):
        return pl.BlockSpec((None, tm, w), lambda b, i: (b, i, 0))

    def full(shape):
        return pl.BlockSpec(shape, lambda b, i: (0,) * len(shape))

    vec = pl.BlockSpec((None, 1, D), lambda b, i: (b, 0, 0))
    return pl.pallas_call(
        _merge_kernel,
        grid=(B, n // tm),
        in_specs=[
            rows(ATTN_W), _split_spec(tm), rows(POOL_W), rows(GMLP_W), rows(D),
            vec, vec, full((1, D)), vec,
            full((D, N_BRANCH * D)),
            full((ATTN_W, D)), full((HYENA_W, D)), full((POOL_W, D)), full((GMLP_W, D)), full((D, D)),
        ],
        out_specs=rows(D),
        out_shape=jax.ShapeDtypeStruct((B, n, D), F32),
        compiler_params=_cparams(("parallel", "parallel")),
        name="merge",
    )(attn, hy, po, gm, x, shift, scale, norm_w.reshape(1, D), gate_res, lw["w_gate"], lw["w_br_attn"],
      lw["w_br_hyena"], lw["w_br_pool"], lw["w_br_gmlp"], lw["w_out"])


FFN_ROWS = 512
FFN_HIDDEN_TILE = 2816


def _ffn_kernel(x_ref, sh_ref, sc_ref, nw_ref, gr_ref, wg_ref, wu_ref, wd_ref, fw_ref, o_ref, h_ref, acc_ref, *, final):
    j = pl.program_id(2)

    @pl.when(j == 0)
    def _():
        h_ref[...] = _modulated_norm(x_ref[...], nw_ref[...], sc_ref[...], sh_ref[...]).astype(BF16)
        acc_ref[...] = jnp.zeros(acc_ref.shape, F32)

    h = h_ref[...]
    a = _silu(_dot(h, wg_ref[...])) * _dot(h, wu_ref[...])
    acc_ref[...] += _dot(a.astype(BF16), wd_ref[...])

    @pl.when(j == pl.num_programs(2) - 1)
    def _():
        y = x_ref[...] + gr_ref[...] * acc_ref[...]
        if final:
            y = y * lax.rsqrt(jnp.mean(y * y, axis=-1, keepdims=True) + RMS_EPS) * fw_ref[...]
        o_ref[...] = y


def _ffn_call(x, shift, scale, norm_w, gate_res, wg, wu, wd, final_w=None):
    B, n, D = x.shape
    H = wg.shape[1]
    tm = _tile(n, FFN_ROWS)
    th = _tile(H, FFN_HIDDEN_TILE, LANES)
    vec = pl.BlockSpec((None, 1, D), lambda b, i, j: (b, 0, 0))
    final = final_w is not None
    fw = (final_w if final else norm_w).reshape(1, D)
    return pl.pallas_call(
        functools.partial(_ffn_kernel, final=final),
        grid=(B, n // tm, H // th),
        in_specs=[
            pl.BlockSpec((None, tm, D), lambda b, i, j: (b, i, 0)),
            vec, vec, pl.BlockSpec((1, D), lambda b, i, j: (0, 0)), vec,
            pl.BlockSpec((D, th), lambda b, i, j: (0, j)),
            pl.BlockSpec((D, th), lambda b, i, j: (0, j)),
            pl.BlockSpec((th, D), lambda b, i, j: (j, 0)),
            pl.BlockSpec((1, D), lambda b, i, j: (0, 0)),
        ],
        out_specs=pl.BlockSpec((None, tm, D), lambda b, i, j: (b, i, 0)),
        out_shape=jax.ShapeDtypeStruct((B, n, D), F32),
        scratch_shapes=[pltpu.VMEM((tm, D), BF16), pltpu.VMEM((tm, D), F32)],
        compiler_params=_cparams(("parallel", "parallel", "arbitrary")),
        name="ffn",
    )(x, shift, scale, norm_w.reshape(1, D), gate_res, wg, wu, wd, fw)


def _rope_tables(n):
    tok = np.arange(n)
    axis_dim = HEAD_DIM // 2
    inv_freq = ROPE_THETA ** (-np.arange(0, axis_dim, 2, dtype=np.float64) / axis_dim)
    inv_freq = inv_freq.astype(np.float32).astype(np.float64)
    ang = np.concatenate([(tok // GRID_W)[:, None] * inv_freq, (tok % GRID_W)[:, None] * inv_freq], axis=-1)
    ang = ang.astype(np.float32)
    cos = np.repeat(np.cos(ang), 2, axis=-1)
    sin = np.repeat(np.sin(ang), 2, axis=-1)
    sin[:, 0::2] *= -1.0
    reps = LANES // HEAD_DIM
    return jnp.asarray(np.tile(cos, (1, reps)), F32), jnp.asarray(np.tile(sin, (1, reps)), F32)


def _identity_tables(n):
    return jnp.ones((n, LANES), F32), jnp.zeros((n, LANES), F32)


def _layer_weights(l, a):
    w_in = a["w_in"][l]
    w_perm = jnp.concatenate([
        w_in[:, Q_OFF:K_OFF], w_in[:, K_OFF:HY_OFF], w_in[:, POOL_OFF:GM_OFF], w_in[:, GM_OFF:GATE_OFF],
        w_in[:, HY_OFF:POOL_OFF]], axis=1).astype(BF16)
    w3 = a["hy_w3"][l].reshape(HYENA_HIDDEN, HYENA_ORDER, 2, HYENA_W).transpose(2, 0, 1, 3)
    w3 = w3.reshape(2, HYENA_HIDDEN, HYENA_ORDER * HYENA_W)
    dec = a["hy_decay"][l].reshape(HYENA_ORDER, 2, HYENA_W).transpose(1, 0, 2).reshape(2, 1, HYENA_ORDER * HYENA_W)
    w1 = jnp.zeros((LANES, HYENA_HIDDEN), F32).at[:a["hy_w1"].shape[1]].set(a["hy_w1"][l])
    w1h, w1l = _split(w1)
    w2h, w2l = _split(a["hy_w2"][l])
    w3h, w3l = _split(w3)
    qw16 = jnp.tile(a["q_norm_w"][l], N_Q_HEADS).reshape(1, ATTN_W) * Q_SCALE
    kw16 = jnp.tile(a["k_norm_w"][l], N_KV_HEADS).reshape(1, KV_W)
    rms_q = jnp.sqrt(jnp.mean(qw16 * qw16))
    rms_k = jnp.sqrt(jnp.mean(kw16 * kw16))
    bal = jnp.exp2(jnp.clip(jnp.round(0.5 * jnp.log2(rms_k / rms_q)), -20.0, 20.0))
    bal = jnp.where(jnp.isfinite(bal), bal, 1.0)
    fp8_ok = (jnp.max(jnp.abs(qw16 * bal)) <= FP8_MAX_WEIGHT) & (jnp.max(jnp.abs(kw16 / bal)) <= FP8_MAX_WEIGHT)
    eye = jnp.eye(len(POOL_WINDOWS), dtype=F32)
    pool_bd = jnp.einsum("gh,gcd->gchd", eye, a["pool_w"][l]).reshape(POOL_W, POOL_W).astype(BF16)
    gm_bias = jnp.repeat(a["gm_bs"][l].T, GMLP_GROUP_W, axis=1)
    return dict(
        w_in=w_perm, w_gate=w_in[:, GATE_OFF:].astype(BF16),
        qw16=qw16, kw16=kw16, qw8=qw16 * bal, kw8=kw16 / bal, fp8_ok=fp8_ok,
        conv_w=a["hy_conv_w"][l], conv_b=a["hy_conv_b"][l].reshape(1, -1),
        hy_w1h=w1h, hy_w1l=w1l, hy_b1=a["hy_b1"][l].reshape(1, -1), hy_w2h=w2h, hy_w2l=w2l, hy_b2=a["hy_b2"][l],
        hy_w3h=w3h, hy_w3l=w3l, hy_freq=a["hy_freq"][l].reshape(1, -1), hy_decay=dec, hy_skip=a["hy_skip"][l],
        pool_bd=pool_bd, pool_scale=a["pool_scale"][l].reshape(1, -1),
        gm_norm_w=a["gm_norm_w"][l].reshape(1, -1), gm_ws=a["gm_ws"][l].astype(BF16), gm_bias=gm_bias,
        w_br_attn=a["w_br_attn"][l].astype(BF16), w_br_hyena=a["w_br_hyena"][l].astype(BF16),
        w_br_pool=a["w_br_pool"][l].astype(BF16), w_br_gmlp=a["w_br_gmlp"][l].astype(BF16),
        w_out=a["w_out"][l].astype(BF16),
        ffn_w_gate=a["ffn_w_gate"][l].astype(BF16), ffn_w_up=a["ffn_w_up"][l].astype(BF16),
        ffn_w_down=a["ffn_w_down"][l].astype(BF16),
        norm1_w=a["norm1_w"][l], norm2_w=a["norm2_w"][l],
    )


Q_SCALE = (HEAD_DIM ** -0.5) * math.log2(math.e)


def _attend(p, p_ctx, rope, rope_ctx, lw):
    def run(fp8):
        qw, kw = (lw["qw8"], lw["kw8"]) if fp8 else (lw["qw16"], lw["kw16"])
        qt, k, vt = _qkprep_call(p, rope[0], rope[1], qw, kw, fp8)
        if p_ctx is not None:
            _, k_c, vt_c = _qkprep_call(p_ctx, rope_ctx[0], rope_ctx[1], qw, kw, fp8)
            k = jnp.concatenate([k_c, k], axis=2)
            vt = jnp.concatenate([vt_c, vt], axis=3)
        return _attn_call(qt, k, vt)

    return lax.cond(lw["fp8_ok"], lambda: run(True), lambda: run(False))


def _mixer_block(x, mods, lw, rope, p_ctx, rope_ctx, kf, fc, final_w=None):
    sh1, s1, g1, sh2, s2, g2 = mods
    p = _inproj_call(x, sh1, s1, lw["norm1_w"], lw["w_in"], P_W)
    attn = _attend(p, p_ctx, rope, rope_ctx, lw)
    hv, hx1, hx2, po, gm = _local_call(p, lw)
    hy = _hyena(hv, hx1, hx2, kf, lw, fc)
    x = _merge_call(attn, hy, po, gm, x, sh1, s1, lw["norm1_w"], g1, lw)
    x = _ffn_call(x, sh2, s2, lw["norm2_w"], g2, lw["ffn_w_gate"], lw["ffn_w_up"], lw["ffn_w_down"], final_w)
    return x, p


def kernel(x, c, ctx, c_ctx, w_mod, b_mod, norm1_w, norm2_w, w_in, q_norm_w, k_norm_w, hy_conv_w, hy_conv_b,
           hy_w1, hy_b1, hy_w2, hy_b2, hy_w3, hy_freq, hy_decay, hy_skip, pool_w, pool_scale, gm_norm_w, gm_ws,
           gm_bs, w_br_attn, w_br_hyena, w_br_pool, w_br_gmlp, w_out, ffn_w_gate, ffn_w_up, ffn_w_down,
           final_norm_w):
    a = dict(w_in=w_in, q_norm_w=q_norm_w, k_norm_w=k_norm_w, hy_conv_w=hy_conv_w, hy_conv_b=hy_conv_b,
             hy_w1=hy_w1, hy_b1=hy_b1, hy_w2=hy_w2, hy_b2=hy_b2, hy_w3=hy_w3, hy_freq=hy_freq, hy_decay=hy_decay,
             hy_skip=hy_skip, pool_w=pool_w, pool_scale=pool_scale, gm_norm_w=gm_norm_w, gm_ws=gm_ws, gm_bs=gm_bs,
             w_br_attn=w_br_attn, w_br_hyena=w_br_hyena, w_br_pool=w_br_pool, w_br_gmlp=w_br_gmlp, w_out=w_out,
             ffn_w_gate=ffn_w_gate, ffn_w_up=ffn_w_up, ffn_w_down=ffn_w_down, norm1_w=norm1_w, norm2_w=norm2_w)
    B, n, D = x.shape
    nc = ctx.shape[1]
    depth = w_mod.shape[0]

    R = ((B + 1 + 15) // 16) * 16
    act = jnp.zeros((R, D), F32).at[:B].set(c).at[B].set(c_ctx)
    mod = _mod_call(act, w_mod, b_mod)

    rope = _rope_tables(n)
    rope_ctx = _identity_tables(nc)
    fc = _fft_consts(2 * n)
    fc_ctx = _fft_consts(2 * nc)

    xc = ctx
    for l in range(depth):
        lw = _layer_weights(l, a)
        mods = [m.reshape(B, 1, D) for m in jnp.split(mod[l, :B], 6, axis=-1)]
        cmods = [jnp.broadcast_to(m.reshape(1, 1, D), (B, 1, D)) for m in jnp.split(mod[l, B], 6, axis=-1)]
        if l < depth - 1:
            kf_ctx = _filter_spectra(nc, lw, fc_ctx)
            xc, pc = _mixer_block(xc, cmods, lw, rope_ctx, None, None, kf_ctx, fc_ctx)
        else:
            pc = _inproj_call(xc, cmods[0], cmods[1], lw["norm1_w"], lw["w_in"][:, :P_TILE], P_TILE)
        kf = _filter_spectra(n, lw, fc)
        x, _ = _mixer_block(x, mods, lw, rope, pc, rope_ctx, kf, fc, final_norm_w if l == depth - 1 else None)
    return x
```

```python
import functools
import math

import numpy as np
import jax
import jax.numpy as jnp
from jax import lax
from jax.experimental import pallas as pl
from jax.experimental.pallas import tpu as pltpu

F32 = jnp.float32
BF16 = jnp.bfloat16

D_MODEL = 1024
GRID_W = 64
RMS_EPS = 1e-6
HEAD_DIM = 64
N_Q_HEADS = 8
N_KV_HEADS = 2
ATTN_W = N_Q_HEADS * HEAD_DIM
KV_W = N_KV_HEADS * HEAD_DIM
ROPE_THETA = 10000.0
HYENA_W = 256
HYENA_ORDER = 2
HYENA_BANDS = 16
HYENA_HIDDEN = 64
HYENA_INNER = 2
POOL_W = 256
POOL_WINDOWS = (2, 4, 8, 16)
POOL_GROUP_W = POOL_W // len(POOL_WINDOWS)
POOL_HALO = 8
GMLP_W = 256
GMLP_CHUNK = 128
GMLP_GROUPS = 4
GMLP_GROUP_W = GMLP_W // GMLP_GROUPS
N_BRANCH = 4
FFN_HIDDEN = ((8 * D_MODEL // 3 + 255) // 256) * 256

Q_OFF = 0
K_OFF = Q_OFF + ATTN_W
V_OFF = K_OFF + KV_W
HY_OFF = V_OFF + KV_W
POOL_OFF = HY_OFF + 3 * HYENA_W
GM_OFF = POOL_OFF + POOL_W
GATE_OFF = GM_OFF + 2 * GMLP_W
IN_W = GATE_OFF + N_BRANCH * D_MODEL

P_Q = 0
P_KV = P_Q + ATTN_W
P_POOL = P_KV + 2 * KV_W
P_GMU = P_POOL + POOL_W
P_GMV = P_GMU + GMLP_W
P_HY = P_GMV + GMLP_W
P_W = P_HY + 3 * HYENA_W
P_QKV = P_POOL

LANES = 128
VMEM_LIMIT = 52 * 1024 * 1024


def _cparams(sem):
    return pltpu.CompilerParams(dimension_semantics=sem, vmem_limit_bytes=VMEM_LIMIT)


def _tile(n, cap, mult=8):
    best = None
    for t in range(mult, min(n, cap) + 1, mult):
        if n % t == 0:
            best = t
    assert best is not None, (n, cap, mult)
    return best


def _split(x):
    hi = x.astype(BF16)
    lo = (x - hi.astype(F32)).astype(BF16)
    return hi, lo


def _dot(a, b):
    return jnp.dot(a, b, preferred_element_type=F32)


def _dot3(a_hi, a_lo, x):
    x_hi, x_lo = _split(x)
    return _dot(a_hi, x_hi) + (_dot(a_lo, x_hi) + _dot(a_hi, x_lo))


def _dot3x(x, b_hi, b_lo):
    x_hi, x_lo = _split(x)
    return _dot(x_hi, b_hi) + (_dot(x_lo, b_hi) + _dot(x_hi, b_lo))


def _sigmoid(x):
    return 1.0 / (1.0 + jnp.exp(-x))


def _silu(x):
    return x * _sigmoid(x)


def _modulated_norm(x, nw, scale, shift):
    ms = jnp.mean(x * x, axis=-1, keepdims=True)
    y = x * lax.rsqrt(ms + RMS_EPS) * nw
    return y * (1.0 + scale) + shift


def _mod_kernel(a_ref, w_ref, b_ref, o_ref):
    a = _silu(a_ref[...]).astype(BF16)
    o_ref[...] = _dot(a, w_ref[...].astype(BF16)) + b_ref[...]


def _mod_call(act, w_mod, b_mod):
    L, D, W6 = w_mod.shape
    R = act.shape[0]
    tn = 1024
    return pl.pallas_call(
        _mod_kernel,
        grid=(L, W6 // tn),
        in_specs=[
            pl.BlockSpec((R, D), lambda l, j: (0, 0)),
            pl.BlockSpec((None, D, tn), lambda l, j: (l, 0, j)),
            pl.BlockSpec((None, 1, tn), lambda l, j: (l, 0, j)),
        ],
        out_specs=pl.BlockSpec((None, R, tn), lambda l, j: (l, 0, j)),
        out_shape=jax.ShapeDtypeStruct((L, R, W6), F32),
        compiler_params=_cparams(("arbitrary", "arbitrary")),
        name="mod",
    )(act, w_mod, b_mod.reshape(L, 1, W6))


ROW_SPLIT = 2
INPROJ_ROWS = 1024


def _row_blocks(tm):
    sub = tm // ROW_SPLIT if tm % (ROW_SPLIT * 16) == 0 else tm
    return [slice(r, r + sub) for r in range(0, tm, sub)]


def _inproj_kernel(x_ref, sh_ref, sc_ref, nw_ref, w_ref, o_ref):
    for rows in _row_blocks(x_ref.shape[0]):
        h = _modulated_norm(x_ref[rows, :], nw_ref[...], sc_ref[...], sh_ref[...]).astype(BF16)
        o_ref[rows, :] = _dot(h, w_ref[...])


def _inproj_call(x, shift, scale, norm_w, w):
    B, n, D = x.shape
    Nw = w.shape[1]
    tm = _tile(n, INPROJ_ROWS)
    return pl.pallas_call(
        _inproj_kernel,
        grid=(B, n // tm),
        in_specs=[
            pl.BlockSpec((None, tm, D), lambda b, i: (b, i, 0)),
            pl.BlockSpec((None, 1, D), lambda b, i: (b, 0, 0)),
            pl.BlockSpec((None, 1, D), lambda b, i: (b, 0, 0)),
            pl.BlockSpec((1, D), lambda b, i: (0, 0)),
            pl.BlockSpec((D, Nw), lambda b, i: (0, 0)),
        ],
        out_specs=pl.BlockSpec((None, tm, Nw), lambda b, i: (b, i, 0)),
        out_shape=jax.ShapeDtypeStruct((B, n, Nw), F32),
        compiler_params=_cparams(("parallel", "parallel")),
        name="inproj",
    )(x, shift, scale, norm_w.reshape(1, D), w)


def _head_norm_rope(t, w, cs, sn, bd_ref):
    sq_hi, sq_lo = _split(t * t)
    ms = _dot(sq_hi, bd_ref[...]) + _dot(sq_lo, bd_ref[...])
    y = t * lax.rsqrt(ms + RMS_EPS) * w
    lane = lax.broadcasted_iota(jnp.int32, y.shape, 1)
    swapped = jnp.where(lane % 2 == 0, pltpu.roll(y, LANES - 1, 1), pltpu.roll(y, 1, 1))
    return y * cs + swapped * sn


V_ROWS = 80
Q_PER_KV = N_Q_HEADS // N_KV_HEADS
LAG_INIT_KEYS = 128
LAG_LIMIT = 24.0
ATTN_QUERIES = 512
ATTN_KEYS = 768
ATTN_UNROLL = 2


def _value_rows(vt_g):
    tail = lax.broadcasted_iota(jnp.int32, (V_ROWS - HEAD_DIM, vt_g.shape[1]), 0)
    return jnp.concatenate([vt_g, jnp.where(tail == 0, 1.0, 0.0)], axis=0).astype(BF16)


FP8 = jnp.float8_e4m3fn
FP8_MAX_WEIGHT = 32.0


def _split8(x):
    hi = x.astype(FP8).astype(F32)
    return hi, (x - hi).astype(FP8).astype(F32)


def _qkprep_kernel(q_ref, kv_ref, cs_ref, sn_ref, qw_ref, kw_ref, bd_ref, kp_ref, qo_ref, ko_ref, vo_ref, *, fp8):
    cs = cs_ref[...]
    sn = sn_ref[...]
    tm = q_ref.shape[0]
    zeros = jnp.zeros((HEAD_DIM, tm), F32)
    heads_per_chunk = LANES // HEAD_DIM
    for c in range(ATTN_W // LANES):
        sl = slice(c * LANES, (c + 1) * LANES)
        r = _head_norm_rope(q_ref[:, sl], qw_ref[:, sl], cs, sn, bd_ref)
        if fp8:
            hi, lo = _split8(r)
            hi_t, lo_t = hi.T, lo.T
            for half in range(heads_per_chunk):
                rows = slice(half * HEAD_DIM, (half + 1) * HEAD_DIM)
                qo_ref[c * heads_per_chunk + half] = jnp.concatenate(
                    [hi_t[rows], lo_t[rows], hi_t[rows], lo_t[rows]], axis=0).astype(FP8)
        else:
            rt = r.T
            for half in range(heads_per_chunk):
                h = c * heads_per_chunk + half
                blk = rt[half * HEAD_DIM:(half + 1) * HEAD_DIM]
                parts = [blk if g == h // Q_PER_KV else zeros for g in range(N_KV_HEADS)]
                qo_ref[h] = jnp.concatenate(parts, axis=0).astype(BF16)
    k = _head_norm_rope(kv_ref[:, :KV_W], kw_ref[...], cs, sn, bd_ref)
    if fp8:
        hi, lo = _split8(k)
        hi, lo = hi.astype(BF16), lo.astype(BF16)
        for g in range(N_KV_HEADS):
            ko_ref[g] = (_dot(hi, kp_ref[g, 0]) + _dot(lo, kp_ref[g, 1])).astype(FP8)
    else:
        ko_ref[0] = k.astype(BF16)
    vt = kv_ref[:, KV_W:].T
    for g in range(N_KV_HEADS):
        vo_ref[g] = _value_rows(vt[g * HEAD_DIM:(g + 1) * HEAD_DIM])


def _k_placement():
    pm = np.zeros((N_KV_HEADS, 2, KV_W, 4 * HEAD_DIM), np.float32)
    d = np.arange(HEAD_DIM)
    for g in range(N_KV_HEADS):
        for part in range(2):
            for rep in range(2):
                pm[g, part, g * HEAD_DIM + d, (2 * part + rep) * HEAD_DIM + d] = 1.0
    return jnp.asarray(pm, BF16)


def _qkprep_call(p, cs, sn, qw, kw, fp8):
    B, n, _ = p.shape
    tm = _tile(n, 512, LANES)
    bd = np.kron(np.eye(LANES // HEAD_DIM), np.full((HEAD_DIM, HEAD_DIM), 1.0 / HEAD_DIM))
    bd = jnp.asarray(bd, BF16)
    kd, kg, dt = (4 * HEAD_DIM, N_KV_HEADS, FP8) if fp8 else (KV_W, 1, BF16)
    return pl.pallas_call(
        functools.partial(_qkprep_kernel, fp8=fp8),
        grid=(n // tm, B),
        in_specs=[
            pl.BlockSpec((None, tm, ATTN_W), lambda i, b: (b, i, P_Q // ATTN_W)),
            pl.BlockSpec((None, tm, 2 * KV_W), lambda i, b: (b, i, P_KV // (2 * KV_W))),
            pl.BlockSpec((tm, LANES), lambda i, b: (i, 0)),
            pl.BlockSpec((tm, LANES), lambda i, b: (i, 0)),
            pl.BlockSpec((1, ATTN_W), lambda i, b: (0, 0)),
            pl.BlockSpec((1, KV_W), lambda i, b: (0, 0)),
            pl.BlockSpec((LANES, LANES), lambda i, b: (0, 0)),
            pl.BlockSpec((N_KV_HEADS, 2, KV_W, 4 * HEAD_DIM), lambda i, b: (0, 0, 0, 0)),
        ],
        out_specs=[
            pl.BlockSpec((None, N_Q_HEADS, kd, tm), lambda i, b: (b, 0, 0, i)),
            pl.BlockSpec((None, kg, tm, kd), lambda i, b: (b, 0, i, 0)),
            pl.BlockSpec((None, N_KV_HEADS, V_ROWS, tm), lambda i, b: (b, 0, 0, i)),
        ],
        out_shape=[
            jax.ShapeDtypeStruct((B, N_Q_HEADS, kd, n), dt),
            jax.ShapeDtypeStruct((B, kg, n, kd), dt),
            jax.ShapeDtypeStruct((B, N_KV_HEADS, V_ROWS, n), BF16),
        ],
        compiler_params=_cparams(("parallel", "parallel")),
        name="qkprep",
    )(p, p, cs, sn, qw, kw, bd, _k_placement())


def _attn_kernel(qt_ref, k_ref, vt_ref, o_ref, m_ref, acc_ref, *, tk):
    nk = vt_ref.shape[0]
    tq = m_ref.shape[-1]
    per_kv_head = k_ref.shape[0] > 1

    def keys(j, h):
        g = h // Q_PER_KV if per_kv_head else 0
        return k_ref[g, pl.ds(pl.multiple_of(j * tk, tk), tk), :]

    def lagged_chunk(j, jump):
        s_next = _dot(keys(j, 0), qt_ref[0])
        for h in range(N_Q_HEADS):
            s = s_next
            if h + 1 < N_Q_HEADS:
                s_next = _dot(keys(j, h + 1), qt_ref[h + 1])
            m_old = m_ref[h]
            p = jnp.exp2((s - m_old).astype(BF16))
            cmax = jnp.max(s, axis=0, keepdims=True)
            m_new = jnp.maximum(m_old, cmax)
            jump = jnp.maximum(jump, cmax - m_old)
            acc_ref[h] = jnp.exp2(m_old - m_new) * (acc_ref[h] + _dot(vt_ref[j, h // Q_PER_KV], p))
            m_ref[h] = m_new
        return jump

    def exact_chunk(j, carry):
        for h in range(N_Q_HEADS):
            s = _dot(keys(j, h), qt_ref[h])
            m_old = m_ref[h]
            m_new = jnp.maximum(m_old, jnp.max(s, axis=0, keepdims=True))
            p = jnp.exp2(s - m_new).astype(BF16)
            acc_ref[h] = jnp.exp2(m_old - m_new) * acc_ref[h] + _dot(vt_ref[j, h // Q_PER_KV], p)
            m_ref[h] = m_new
        return carry

    for h in range(N_Q_HEADS):
        k0 = k_ref[h // Q_PER_KV if per_kv_head else 0, 0:LAG_INIT_KEYS, :]
        m_ref[h] = jnp.max(_dot(k0, qt_ref[h]), axis=0, keepdims=True)
    acc_ref[...] = jnp.zeros(acc_ref.shape, F32)
    jump = lax.fori_loop(0, nk, lagged_chunk, jnp.zeros((1, tq), F32), unroll=ATTN_UNROLL)

    @pl.when(jnp.logical_not(jnp.max(jump) <= LAG_LIMIT))
    def _():
        m_ref[...] = jnp.full(m_ref.shape, -jnp.inf, F32)
        acc_ref[...] = jnp.zeros(acc_ref.shape, F32)
        lax.fori_loop(0, nk, exact_chunk, 0)

    for c in range(ATTN_W // LANES):
        heads = []
        for h in range(c * (LANES // HEAD_DIM), (c + 1) * (LANES // HEAD_DIM)):
            a = acc_ref[h]
            heads.append(a[:HEAD_DIM] / a[HEAD_DIM:HEAD_DIM + 1])
        o_ref[:, c * LANES:(c + 1) * LANES] = jnp.concatenate(heads, axis=0).T.astype(BF16)


def _attn_call(qt, k, vt):
    B, _, kd, n = qt.shape
    _, kg, Lk, _ = k.shape
    tq = _tile(n, ATTN_QUERIES, LANES)
    tk = _tile(Lk, ATTN_KEYS, LANES)
    nk = Lk // tk
    vt = vt.reshape(B, N_KV_HEADS, V_ROWS, nk, tk).transpose(0, 3, 1, 2, 4)
    return pl.pallas_call(
        functools.partial(_attn_kernel, tk=tk),
        grid=(B, n // tq),
        in_specs=[
            pl.BlockSpec((None, N_Q_HEADS, kd, tq), lambda b, i: (b, 0, 0, i)),
            pl.BlockSpec((None, kg, Lk, kd), lambda b, i: (b, 0, 0, 0)),
            pl.BlockSpec((None, nk, N_KV_HEADS, V_ROWS, tk), lambda b, i: (b, 0, 0, 0, 0)),
        ],
        out_specs=pl.BlockSpec((None, tq, ATTN_W), lambda b, i: (b, i, 0)),
        out_shape=jax.ShapeDtypeStruct((B, n, ATTN_W), BF16),
        scratch_shapes=[
            pltpu.VMEM((N_Q_HEADS, 1, tq), F32),
            pltpu.VMEM((N_Q_HEADS, V_ROWS, tq), F32),
        ],
        compiler_params=_cparams(("parallel", "parallel")),
        name="attn",
    )(qt, k, vt)


HALVES = HYENA_W // LANES


def _split_shape(B, n):
    assert B % 2 == 0
    return jax.ShapeDtypeStruct((B // 2, HALVES, 2, n, LANES), F32)


def _split_spec(tm):
    return pl.BlockSpec((None, HALVES, None, tm, LANES), lambda b, i: (b // 2, 0, b % 2, i, 0))


def _shift_rows(cur, prev_row, next_row, rows):
    tm = cur.shape[0]
    up = jnp.where(rows == 0, prev_row, pltpu.roll(cur, 1, 0))
    dn = jnp.where(rows == tm - 1, next_row, pltpu.roll(cur, tm - 1, 0))
    return up, dn


def _local_kernel(hy_ref, hyp_ref, hyn_ref, po_ref, pop_ref, pon_ref, gu_ref, gv_ref,
                  cw_ref, cb_ref, pw_ref, ps_ref, gnw_ref, gws_ref, gb_ref,
                  v_ref, x1_ref, x2_ref, pl_ref, gm_ref, *, n_seq):
    i = pl.program_id(1)
    last = pl.num_programs(1) - 1
    tm = hy_ref.shape[0]
    has_prev = (i > 0).astype(F32)
    has_next = (i < last).astype(F32)

    cur = hy_ref[...]
    rows = lax.broadcasted_iota(jnp.int32, cur.shape, 0)
    prev_row = hyp_ref[POOL_HALO - 1:POOL_HALO, :] * has_prev
    next_row = hyn_ref[0:1, :] * has_next
    up, dn = _shift_rows(cur, prev_row, next_row, rows)
    z = up * cw_ref[0:1, :] + cur * cw_ref[1:2, :] + dn * cw_ref[2:3, :] + cb_ref[...]
    for s, o_ref in enumerate((v_ref, x1_ref, x2_ref)):
        for hf in range(HYENA_W // LANES):
            lo = s * HYENA_W + hf * LANES
            o_ref[hf] = z[:, lo:lo + LANES]

    zc = po_ref[...]
    ext = jnp.concatenate([pop_ref[...] * has_prev, zc, pon_ref[...] * has_next], axis=0)
    te = tm + 2 * POOL_HALO
    lane = lax.broadcasted_iota(jnp.int32, zc.shape, 1)
    prow = lax.broadcasted_iota(jnp.int32, zc.shape, 0) + i * tm
    half = jnp.left_shift(1, lane // POOL_GROUP_W)
    def ahead(a, k):
        return pltpu.roll(a, (te - k) % te, 0)

    assert POOL_WINDOWS == (2, 4, 8, 16)
    sums = [ext + ahead(ext, -1)]
    for win in POOL_WINDOWS[:-1]:
        sums.append(ahead(sums[-1], win // 2) + ahead(sums[-1], -(win // 2)))
    group = lane // POOL_GROUP_W
    acc = sums[-1][POOL_HALO:POOL_HALO + tm]
    for g in range(len(POOL_WINDOWS) - 2, -1, -1):
        acc = jnp.where(group == g, sums[g][POOL_HALO:POOL_HALO + tm], acc)
    cnt = jnp.minimum(prow + half, n_seq) - jnp.maximum(prow - half, 0)
    dlt = acc / cnt.astype(F32) - zc
    pl_ref[...] = (_dot(dlt.astype(BF16), pw_ref[...]) * ps_ref[...]).astype(BF16)

    gv = gv_ref[...]
    mu = jnp.mean(gv, axis=-1, keepdims=True)
    ctr = gv - mu
    var = jnp.mean(ctr * ctr, axis=-1, keepdims=True)
    vn = (ctr * lax.rsqrt(var + RMS_EPS) * gnw_ref[...]).astype(BF16)
    group = lax.broadcasted_iota(jnp.int32, (GMLP_CHUNK, GMLP_W), 1) // GMLP_GROUP_W
    for c in range(tm // GMLP_CHUNK):
        rs = slice(c * GMLP_CHUNK, (c + 1) * GMLP_CHUNK)
        vc = vn[rs]
        mixed = gb_ref[...]
        for g in range(GMLP_GROUPS):
            mg = _dot(gws_ref[g], vc)
            mixed = mixed + jnp.where(group == g, mg, 0.0)
        gm_ref[rs, :] = (gu_ref[rs, :] * mixed).astype(BF16)


def _local_call(p, lw):
    B, n, _ = p.shape
    tm = _tile(n, 512, GMLP_CHUNK)
    nb8 = n // POOL_HALO
    r8 = tm // POOL_HALO
    HW = 3 * HYENA_W

    def cur(w, off):
        return pl.BlockSpec((None, tm, w), lambda b, i: (b, i, off // w))

    def prev(w, off):
        return pl.BlockSpec((None, POOL_HALO, w), lambda b, i: (b, jnp.maximum(i * r8 - 1, 0), off // w))

    def nxt(w, off):
        return pl.BlockSpec((None, POOL_HALO, w), lambda b, i: (b, jnp.minimum((i + 1) * r8, nb8 - 1), off // w))

    def full(shape):
        return pl.BlockSpec(shape, lambda b, i: (0,) * len(shape))

    def out(w):
        return pl.BlockSpec((None, tm, w), lambda b, i: (b, i, 0))

    return pl.pallas_call(
        functools.partial(_local_kernel, n_seq=n),
        grid=(B, n // tm),
        in_specs=[
            cur(HW, P_HY), prev(HW, P_HY), nxt(HW, P_HY),
            cur(POOL_W, P_POOL), prev(POOL_W, P_POOL), nxt(POOL_W, P_POOL),
            cur(GMLP_W, P_GMU), cur(GMLP_W, P_GMV),
            full((3, HW)), full((1, HW)), full((POOL_W, POOL_W)), full((1, POOL_W)),
            full((1, GMLP_W)), full((GMLP_GROUPS, GMLP_CHUNK, GMLP_CHUNK)), full((GMLP_CHUNK, GMLP_W)),
        ],
        out_specs=[_split_spec(tm), _split_spec(tm), _split_spec(tm), out(POOL_W), out(GMLP_W)],
        out_shape=[
            _split_shape(B, n), _split_shape(B, n), _split_shape(B, n),
            jax.ShapeDtypeStruct((B, n, POOL_W), BF16),
            jax.ShapeDtypeStruct((B, n, GMLP_W), BF16),
        ],
        compiler_params=_cparams(("parallel", "parallel")),
        name="local",
    )(p, p, p, p, p, p, p, p, lw["conv_w"], lw["conv_b"], lw["pool_bd"], lw["pool_scale"],
      lw["gm_norm_w"], lw["gm_ws"], lw["gm_bias"])


def _filt_kernel(z_ref, t_ref, w1h_ref, w1l_ref, b1_ref, w2h_ref, w2l_ref, b2_ref, fr_ref,
                 w3h_ref, w3l_ref, dec_ref, k_ref, s_ref, *, n_seq):
    i = pl.program_id(0)
    tb = z_ref.shape[0]
    fr = fr_ref[...]
    hdn = jnp.sin(fr * (_dot3x(z_ref[...], w1h_ref[...], w1l_ref[...]) + b1_ref[...]))
    for u in range(HYENA_INNER):
        hdn = jnp.sin(fr * (_dot3x(hdn, w2h_ref[u], w2l_ref[u]) + b2_ref[u:u + 1, :]))
    t = t_ref[...]
    tt = jnp.concatenate([t] * (HYENA_ORDER * HYENA_W // LANES), axis=1)
    window = jnp.exp(-tt * jnp.abs(dec_ref[...]))
    k = _dot3x(hdn, w3h_ref[...], w3l_ref[...]) * window
    rows = lax.broadcasted_iota(jnp.int32, k.shape, 0) + i * tb
    k = jnp.where(rows == n_seq, 0.0, k)
    for q in range(k_ref.shape[0]):
        k_ref[q] = k[:, q * LANES:(q + 1) * LANES]

    @pl.when(i == 0)
    def _():
        s_ref[...] = jnp.zeros(s_ref.shape, F32)

    s_ref[...] += jnp.sum(jnp.abs(k), axis=0, keepdims=True)


def _filt_call(n, lw):
    N = 2 * n
    tb = _tile(n, 512)
    nb_half = n // tb
    pos = np.arange(N)
    pos = np.where(pos < n, pos, N - pos).astype(np.float64)
    t = pos / (n - 1)
    bands = np.linspace(1e-4, HYENA_BANDS - 1, HYENA_BANDS)
    ang = (2.0 * math.pi / n) * pos[:, None] * bands
    z = np.concatenate([t[:, None], np.cos(ang), np.sin(ang)], axis=-1)
    zp = np.zeros((N, LANES), np.float32)
    zp[:, :z.shape[1]] = z
    tl = np.broadcast_to(t[:, None], (N, LANES)).astype(np.float32)
    CW = HYENA_ORDER * HYENA_W

    def full(shape):
        return pl.BlockSpec(shape, lambda i: (0,) * len(shape))

    def side(rows_):
        return pl.BlockSpec((None, rows_, CW), lambda i: (i // nb_half, 0, 0))

    return pl.pallas_call(
        functools.partial(_filt_kernel, n_seq=n),
        grid=(N // tb,),
        in_specs=[
            pl.BlockSpec((tb, LANES), lambda i: (i, 0)),
            pl.BlockSpec((tb, LANES), lambda i: (i, 0)),
            full((LANES, HYENA_HIDDEN)), full((LANES, HYENA_HIDDEN)), full((1, HYENA_HIDDEN)),
            full((HYENA_INNER, HYENA_HIDDEN, HYENA_HIDDEN)), full((HYENA_INNER, HYENA_HIDDEN, HYENA_HIDDEN)),
            full((HYENA_INNER, HYENA_HIDDEN)), full((1, HYENA_HIDDEN)),
            side(HYENA_HIDDEN), side(HYENA_HIDDEN), side(1),
        ],
        out_specs=[pl.BlockSpec((CW // LANES, tb, LANES), lambda i: (0, i, 0)), pl.BlockSpec((1, CW), lambda i: (0, 0))],
        out_shape=[jax.ShapeDtypeStruct((CW // LANES, N, LANES), F32), jax.ShapeDtypeStruct((1, CW), F32)],
        compiler_params=_cparams(("arbitrary",)),
        name="filt",
    )(jnp.asarray(zp), jnp.asarray(tl), lw["hy_w1h"], lw["hy_w1l"], lw["hy_b1"], lw["hy_w2h"], lw["hy_w2l"],
      lw["hy_b2"], lw["hy_freq"], lw["hy_w3h"], lw["hy_w3l"], lw["hy_decay"])


def _fft_dims(N):
    e = int(round(math.log2(N)))
    assert 2 ** e == N
    N1 = 2 ** ((e + 1) // 2)
    return N1, N // N1


def _hilo(a):
    a = np.asarray(a, np.float64)
    hi = jnp.asarray(a, F32).astype(BF16)
    lo = (jnp.asarray(a, F32) - hi.astype(F32)).astype(BF16)
    return hi, lo


@functools.lru_cache(maxsize=None)
def _fft_consts_np(N):
    N1, N2 = _fft_dims(N)
    h = N1 // 2
    k1 = np.arange(N1)
    F1 = np.exp(-2j * np.pi * np.outer(k1, k1) / N1)
    k2 = np.arange(N2)
    F2 = np.exp(-2j * np.pi * np.outer(k2, k2) / N2)
    tw = np.exp(-2j * np.pi * np.outer(k1, k2) / N)
    g1 = np.zeros((2 * N1, N1))
    g1[0::2, :h] = F1.real[:, :h]
    g1[0::2, h:] = -F1.imag[:, :h]
    g1[1::2, :h] = F1.imag[:, :h]
    g1[1::2, h:] = F1.real[:, :h]
    g1f = np.zeros((2 * N1, N1))
    g1f[0::2] = F1.real
    g1f[1::2] = F1.imag
    g2 = np.block([[F2.real, -F2.imag], [F2.imag, F2.real]])
    g2c = np.block([[F2.real, F2.imag], [-F2.imag, F2.real]])
    g3 = np.zeros((N1, 2 * N1))
    g3[:h, 0::2] = F1.real[:h]
    g3[:h, 1::2] = F1.imag[:h]
    g3[h:, 0::2] = -F1.imag[:h]
    g3[h:, 1::2] = F1.real[:h]
    g3 /= N
    twt = np.stack([tw.real, tw.imag], axis=1)
    return dict(g1=g1, g1f=g1f, g2=g2, g2c=g2c, g3=g3, tw=twt)


def _fft_consts(N):
    c = _fft_consts_np(N)
    out = {k: _hilo(c[k]) for k in ("g1", "g1f", "g2", "g2c", "g3")}
    N1, N2 = _fft_dims(N)
    tw = jnp.asarray(c["tw"], F32)
    out["tw"] = jnp.broadcast_to(tw[..., None], (N1, 2, N2, LANES))
    return out


T2_BLOCK = 8


def _load_8th(ref, j, rows):
    return ref.reshape(rows * T2_BLOCK, LANES)[pl.ds(j, rows, stride=T2_BLOCK), :]


def _store_8th(ref, j, rows, val):
    ref.reshape(rows * T2_BLOCK, LANES)[pl.ds(j, rows, stride=T2_BLOCK), :] = val


def _dft_in_kernel(gh_ref, gl_ref, x_ref, o_ref):
    N1 = x_ref.shape[1]
    for j in range(T2_BLOCK):
        xj = jnp.concatenate([_load_8th(x_ref.at[hf], j, N1) for hf in range(HALVES)], axis=1)
        a = _dot3(gh_ref[...], gl_ref[...], xj)
        for hf in range(HALVES):
            _store_8th(o_ref.at[hf], j, 2 * N1, a[:, hf * LANES:(hf + 1) * LANES])


def _dft_in_call(g, u5):
    gh, gl = g
    G, _, N1, N2, _ = u5.shape
    assert gh.shape == (2 * N1, N1)
    mat = pl.BlockSpec((2 * N1, N1), lambda b, i: (0, 0))
    return pl.pallas_call(
        _dft_in_kernel,
        grid=(G, N2 // T2_BLOCK),
        in_specs=[mat, mat, pl.BlockSpec((None, HALVES, N1, T2_BLOCK, LANES), lambda b, i: (b, 0, 0, i, 0))],
        out_specs=pl.BlockSpec((None, HALVES, N1, 2, T2_BLOCK, LANES), lambda b, i: (b, 0, 0, 0, i, 0)),
        out_shape=jax.ShapeDtypeStruct((G, HALVES, N1, 2, N2, LANES), F32),
        compiler_params=_cparams(("parallel", "parallel")),
        name="dft_in",
    )(gh, gl, u5)


def _dft_out_kernel(gh_ref, gl_ref, d_ref, gate_ref, u_ref, sk_ref, o_ref):
    N1 = gate_ref.shape[1]
    for j in range(T2_BLOCK):
        dj = jnp.concatenate([_load_8th(d_ref.at[hf], j, 2 * N1) for hf in range(HALVES)], axis=1)
        y = _dot3(gh_ref[...], gl_ref[...], dj)
        for hf in range(HALVES):
            conv = y[:, hf * LANES:(hf + 1) * LANES]
            u = _load_8th(u_ref.at[hf], j, N1)
            _store_8th(o_ref.at[hf], j, N1, _load_8th(gate_ref.at[hf], j, N1) * (conv + u * sk_ref[hf]))


def _dft_out_call(g, d6, gate5, u5, skip):
    gh, gl = g
    G, _, N1, N2, _ = u5.shape
    assert gh.shape == (N1, 2 * N1)
    mat = pl.BlockSpec((N1, 2 * N1), lambda b, i: (0, 0))
    blk = pl.BlockSpec((None, HALVES, N1, T2_BLOCK, LANES), lambda b, i: (b, 0, 0, i, 0))
    return pl.pallas_call(
        _dft_out_kernel,
        grid=(G, N2 // T2_BLOCK),
        in_specs=[mat, mat, pl.BlockSpec((None, HALVES, N1, 2, T2_BLOCK, LANES), lambda b, i: (b, 0, 0, 0, i, 0)),
                  blk, blk, pl.BlockSpec((HALVES, 1, LANES), lambda b, i: (0, 0, 0))],
        out_specs=blk,
        out_shape=jax.ShapeDtypeStruct(u5.shape, F32),
        compiler_params=_cparams(("parallel", "parallel")),
        name="dft_out",
    )(gh, gl, d6, gate5, u5, skip)


def _cmul(ar, ai, br, bi):
    return ar * br - ai * bi, ar * bi + ai * br


def _lane_tile(t, c):
    return jnp.concatenate([t] * (c // LANES), axis=-1)


def _both_halves(a_ref, k, part):
    return jnp.concatenate([a_ref[hf, k, part] for hf in range(HALVES)], axis=-1)


def _spec_kernel(a_ref, tw_ref, g2h_ref, g2l_ref, s_ref, o_ref):
    _, kb, _, N2, _ = a_ref.shape
    inv = 1.0 / s_ref[...]
    for k in range(kb):
        twr = _lane_tile(tw_ref[k, 0], HYENA_W)
        twi = _lane_tile(tw_ref[k, 1], HYENA_W)
        br, bi = _cmul(_both_halves(a_ref, k, 0), _both_halves(a_ref, k, 1), twr, twi)
        zz = _dot3(g2h_ref[...], g2l_ref[...], jnp.concatenate([br, bi], axis=0))
        o_ref[k, 0] = zz[:N2] * inv
        o_ref[k, 1] = zz[N2:] * inv


def _conv_kernel(a_ref, tw_ref, kf_ref, g2h_ref, g2l_ref, g2ch_ref, g2cl_ref, o_ref):
    _, kb, _, N2, _ = a_ref.shape
    for k in range(kb):
        twr = _lane_tile(tw_ref[k, 0], HYENA_W)
        twi = _lane_tile(tw_ref[k, 1], HYENA_W)
        br, bi = _cmul(_both_halves(a_ref, k, 0), _both_halves(a_ref, k, 1), twr, twi)
        zz = _dot3(g2h_ref[...], g2l_ref[...], jnp.concatenate([br, bi], axis=0))
        yr, yi = _cmul(zz[:N2], zz[N2:], kf_ref[k, 0], kf_ref[k, 1])
        cc = _dot3(g2ch_ref[...], g2cl_ref[...], jnp.concatenate([yr, yi], axis=0))
        dr, di = _cmul(cc[:N2], cc[N2:], twr, -twi)
        for hf in range(HALVES):
            o_ref[hf, k, 0] = dr[:, hf * LANES:(hf + 1) * LANES]
            o_ref[hf, k, 1] = di[:, hf * LANES:(hf + 1) * LANES]


def _spec_call(a6, fc, s):
    G, _, N1, _, N2, _ = a6.shape
    kb = _tile(N1, 8, 1)
    mat = pl.BlockSpec((2 * N2, 2 * N2), lambda i, b: (0, 0))
    return pl.pallas_call(
        _spec_kernel,
        grid=(N1 // kb, G),
        in_specs=[pl.BlockSpec((None, HALVES, kb, 2, N2, LANES), lambda i, b: (b, 0, i, 0, 0, 0)),
                  pl.BlockSpec((kb, 2, N2, LANES), lambda i, b: (i, 0, 0, 0)), mat, mat,
                  pl.BlockSpec((None, 1, HYENA_W), lambda i, b: (b, 0, 0))],
        out_specs=pl.BlockSpec((None, kb, 2, N2, HYENA_W), lambda i, b: (b, i, 0, 0, 0)),
        out_shape=jax.ShapeDtypeStruct((G, N1, 2, N2, HYENA_W), F32),
        compiler_params=_cparams(("parallel", "parallel")),
        name="filt_spec",
    )(a6, fc["tw"], fc["g2"][0], fc["g2"][1], s)


def _conv_call(a6, kf, fc):
    G, _, N1, _, N2, _ = a6.shape
    kb = _tile(N1, 8, 1)
    blk = pl.BlockSpec((None, HALVES, kb, 2, N2, LANES), lambda i, b: (b, 0, i, 0, 0, 0))
    mat = pl.BlockSpec((2 * N2, 2 * N2), lambda i, b: (0, 0))
    return pl.pallas_call(
        _conv_kernel,
        grid=(N1 // kb, G),
        in_specs=[blk, pl.BlockSpec((kb, 2, N2, LANES), lambda i, b: (i, 0, 0, 0)),
                  pl.BlockSpec((kb, 2, N2, HYENA_W), lambda i, b: (i, 0, 0, 0)), mat, mat, mat, mat],
        out_specs=blk,
        out_shape=jax.ShapeDtypeStruct(a6.shape, F32),
        compiler_params=_cparams(("parallel", "parallel")),
        name="spec_conv",
    )(a6, fc["tw"], kf, fc["g2"][0], fc["g2"][1], fc["g2c"][0], fc["g2c"][1])


def _filter_spectra(n, lw, fc):
    N1, N2 = _fft_dims(2 * n)
    k, s = _filt_call(n, lw)
    a6 = _dft_in_call(fc["g1f"], k.reshape(HYENA_ORDER, HALVES, N1, N2, LANES))
    return _spec_call(a6, fc, s.reshape(HYENA_ORDER, 1, HYENA_W))


def _hyena(v, x1, x2, kf, lw, fc):
    P, _, _, n, _ = v.shape
    N1, N2 = _fft_dims(2 * n)
    shape5 = (P, HALVES, N1, N2, LANES)
    y = v.reshape(shape5)
    for o, gate in enumerate((x1, x2)):
        d = _conv_call(_dft_in_call(fc["g1"], y), kf[o], fc)
        y = _dft_out_call(fc["g3"], d, gate.reshape(shape5), y, lw["hy_skip"][o].reshape(HALVES, 1, LANES))
    return y.reshape(v.shape)


def _merge_kernel(at_ref, hy_ref, po_ref, gm_ref, x_ref, sh_ref, sc_ref, nw_ref, gr_ref,
                  wgate_ref, wa_ref, wh_ref, wp_ref, wg_ref, wo_ref, o_ref):
    D = x_ref.shape[1]
    for rows in _row_blocks(x_ref.shape[0]):
        x = x_ref[rows, :]
        h = _modulated_norm(x, nw_ref[...], sc_ref[...], sh_ref[...]).astype(BF16)
        hy = jnp.concatenate([hy_ref[hf, rows, :] for hf in range(HALVES)], axis=-1).astype(BF16)
        branches = ((at_ref[rows, :], wa_ref), (hy, wh_ref), (po_ref[rows, :], wp_ref), (gm_ref[rows, :], wg_ref))
        merged = None
        for i, (br, w_ref) in enumerate(branches):
            term = _sigmoid(_dot(h, wgate_ref[:, i * D:(i + 1) * D])) * _dot(br, w_ref[...])
            merged = term if merged is None else merged + term
        o_ref[rows, :] = x + gr_ref[...] * _dot(merged.astype(BF16), wo_ref[...])


def _merge_call(attn, hy, po, gm, x, shift, scale, norm_w, gate_res, lw):
    B, n, D = x.shape
    tm = _tile(n, 512)

    def rows(w):
        return pl.BlockSpec((None, tm, w), lambda b, i: (b, i, 0))

    def full(shape):
        return pl.BlockSpec(shape, lambda b, i: (0,) * len(shape))

    vec = pl.BlockSpec((None, 1, D), lambda b, i: (b, 0, 0))
    return pl.pallas_call(
        _merge_kernel,
        grid=(B, n // tm),
        in_specs=[
            rows(ATTN_W), _split_spec(tm), rows(POOL_W), rows(GMLP_W), rows(D),
            vec, vec, full((1, D)), vec,
            full((D, N_BRANCH * D)),
            full((ATTN_W, D)), full((HYENA_W, D)), full((POOL_W, D)), full((GMLP_W, D)), full((D, D)),
        ],
        out_specs=rows(D),
        out_shape=jax.ShapeDtypeStruct((B, n, D), F32),
        compiler_params=_cparams(("parallel", "parallel")),
        name="merge",
    )(attn, hy, po, gm, x, shift, scale, norm_w.reshape(1, D), gate_res, lw["w_gate"], lw["w_br_attn"],
      lw["w_br_hyena"], lw["w_br_pool"], lw["w_br_gmlp"], lw["w_out"])


FFN_ROWS = 512


def _ffn_kernel(x_ref, sh_ref, sc_ref, nw_ref, gr_ref, wg_ref, wu_ref, wd_ref, fw_ref, o_ref, *, final):
    for rows in _row_blocks(x_ref.shape[0]):
        x = x_ref[rows, :]
        h = _modulated_norm(x, nw_ref[...], sc_ref[...], sh_ref[...]).astype(BF16)
        a = _silu(_dot(h, wg_ref[...])) * _dot(h, wu_ref[...])
        y = x + gr_ref[...] * _dot(a.astype(BF16), wd_ref[...])
        if final:
            y = y * lax.rsqrt(jnp.mean(y * y, axis=-1, keepdims=True) + RMS_EPS) * fw_ref[...]
        o_ref[rows, :] = y


def _ffn_call(x, shift, scale, norm_w, gate_res, wg, wu, wd, final_w=None):
    B, n, D = x.shape
    H = wg.shape[1]
    tm = _tile(n, FFN_ROWS)
    vec = pl.BlockSpec((None, 1, D), lambda b, i: (b, 0, 0))
    one = pl.BlockSpec((1, D), lambda b, i: (0, 0))
    final = final_w is not None
    fw = (final_w if final else norm_w).reshape(1, D)
    return pl.pallas_call(
        functools.partial(_ffn_kernel, final=final),
        grid=(B, n // tm),
        in_specs=[
            pl.BlockSpec((None, tm, D), lambda b, i: (b, i, 0)),
            vec, vec, one, vec,
            pl.BlockSpec((D, H), lambda b, i: (0, 0)),
            pl.BlockSpec((D, H), lambda b, i: (0, 0)),
            pl.BlockSpec((H, D), lambda b, i: (0, 0)),
            one,
        ],
        out_specs=pl.BlockSpec((None, tm, D), lambda b, i: (b, i, 0)),
        out_shape=jax.ShapeDtypeStruct((B, n, D), F32),
        compiler_params=_cparams(("parallel", "parallel")),
        name="ffn",
    )(x, shift, scale, norm_w.reshape(1, D), gate_res, wg, wu, wd, fw)


def _rope_tables(n):
    tok = np.arange(n)
    axis_dim = HEAD_DIM // 2
    inv_freq = ROPE_THETA ** (-np.arange(0, axis_dim, 2, dtype=np.float64) / axis_dim)
    inv_freq = inv_freq.astype(np.float32).astype(np.float64)
    ang = np.concatenate([(tok // GRID_W)[:, None] * inv_freq, (tok % GRID_W)[:, None] * inv_freq], axis=-1)
    ang = ang.astype(np.float32)
    cos = np.repeat(np.cos(ang), 2, axis=-1)
    sin = np.repeat(np.sin(ang), 2, axis=-1)
    sin[:, 0::2] *= -1.0
    reps = LANES // HEAD_DIM
    return jnp.asarray(np.tile(cos, (1, reps)), F32), jnp.asarray(np.tile(sin, (1, reps)), F32)


def _identity_tables(n):
    return jnp.ones((n, LANES), F32), jnp.zeros((n, LANES), F32)


def _layer_weights(l, a):
    w_in = a["w_in"][l]
    w_perm = jnp.concatenate([
        w_in[:, Q_OFF:K_OFF], w_in[:, K_OFF:HY_OFF], w_in[:, POOL_OFF:GM_OFF], w_in[:, GM_OFF:GATE_OFF],
        w_in[:, HY_OFF:POOL_OFF]], axis=1).astype(BF16)
    w3 = a["hy_w3"][l].reshape(HYENA_HIDDEN, HYENA_ORDER, 2, HYENA_W).transpose(2, 0, 1, 3)
    w3 = w3.reshape(2, HYENA_HIDDEN, HYENA_ORDER * HYENA_W)
    dec = a["hy_decay"][l].reshape(HYENA_ORDER, 2, HYENA_W).transpose(1, 0, 2).reshape(2, 1, HYENA_ORDER * HYENA_W)
    w1 = jnp.zeros((LANES, HYENA_HIDDEN), F32).at[:a["hy_w1"].shape[1]].set(a["hy_w1"][l])
    w1h, w1l = _split(w1)
    w2h, w2l = _split(a["hy_w2"][l])
    w3h, w3l = _split(w3)
    qw16 = jnp.tile(a["q_norm_w"][l], N_Q_HEADS).reshape(1, ATTN_W) * Q_SCALE
    kw16 = jnp.tile(a["k_norm_w"][l], N_KV_HEADS).reshape(1, KV_W)
    rms_q = jnp.sqrt(jnp.mean(qw16 * qw16))
    rms_k = jnp.sqrt(jnp.mean(kw16 * kw16))
    bal = jnp.exp2(jnp.clip(jnp.round(0.5 * jnp.log2(rms_k / rms_q)), -20.0, 20.0))
    bal = jnp.where(jnp.isfinite(bal), bal, 1.0)
    fp8_ok = (jnp.max(jnp.abs(qw16 * bal)) <= FP8_MAX_WEIGHT) & (jnp.max(jnp.abs(kw16 / bal)) <= FP8_MAX_WEIGHT)
    eye = jnp.eye(len(POOL_WINDOWS), dtype=F32)
    pool_bd = jnp.einsum("gh,gcd->gchd", eye, a["pool_w"][l]).reshape(POOL_W, POOL_W).astype(BF16)
    gm_bias = jnp.repeat(a["gm_bs"][l].T, GMLP_GROUP_W, axis=1)
    return dict(
        w_in=w_perm, w_gate=w_in[:, GATE_OFF:].astype(BF16),
        qw16=qw16, kw16=kw16, qw8=qw16 * bal, kw8=kw16 / bal, fp8_ok=fp8_ok,
        conv_w=a["hy_conv_w"][l], conv_b=a["hy_conv_b"][l].reshape(1, -1),
        hy_w1h=w1h, hy_w1l=w1l, hy_b1=a["hy_b1"][l].reshape(1, -1), hy_w2h=w2h, hy_w2l=w2l, hy_b2=a["hy_b2"][l],
        hy_w3h=w3h, hy_w3l=w3l, hy_freq=a["hy_freq"][l].reshape(1, -1), hy_decay=dec, hy_skip=a["hy_skip"][l],
        pool_bd=pool_bd, pool_scale=a["pool_scale"][l].reshape(1, -1),
        gm_norm_w=a["gm_norm_w"][l].reshape(1, -1), gm_ws=a["gm_ws"][l].astype(BF16), gm_bias=gm_bias,
        w_br_attn=a["w_br_attn"][l].astype(BF16), w_br_hyena=a["w_br_hyena"][l].astype(BF16),
        w_br_pool=a["w_br_pool"][l].astype(BF16), w_br_gmlp=a["w_br_gmlp"][l].astype(BF16),
        w_out=a["w_out"][l].astype(BF16),
        ffn_w_gate=a["ffn_w_gate"][l].astype(BF16), ffn_w_up=a["ffn_w_up"][l].astype(BF16),
        ffn_w_down=a["ffn_w_down"][l].astype(BF16),
        norm1_w=a["norm1_w"][l], norm2_w=a["norm2_w"][l],
    )


Q_SCALE = (HEAD_DIM ** -0.5) * math.log2(math.e)


def _attend(p, p_ctx, rope, rope_ctx, lw):
    def run(fp8):
        qw, kw = (lw["qw8"], lw["kw8"]) if fp8 else (lw["qw16"], lw["kw16"])
        qt, k, vt = _qkprep_call(p, rope[0], rope[1], qw, kw, fp8)
        if p_ctx is not None:
            _, k_c, vt_c = _qkprep_call(p_ctx, rope_ctx[0], rope_ctx[1], qw, kw, fp8)
            k = jnp.concatenate([k_c, k], axis=2)
            vt = jnp.concatenate([vt_c, vt], axis=3)
        return _attn_call(qt, k, vt)

    return lax.cond(lw["fp8_ok"], lambda: run(True), lambda: run(False))


def _mixer_block(x, mods, lw, rope, p_ctx, rope_ctx, kf, fc, final_w=None):
    sh1, s1, g1, sh2, s2, g2 = mods
    p = _inproj_call(x, sh1, s1, lw["norm1_w"], lw["w_in"])
    attn = _attend(p, p_ctx, rope, rope_ctx, lw)
    hv, hx1, hx2, po, gm = _local_call(p, lw)
    hy = _hyena(hv, hx1, hx2, kf, lw, fc)
    x = _merge_call(attn, hy, po, gm, x, sh1, s1, lw["norm1_w"], g1, lw)
    x = _ffn_call(x, sh2, s2, lw["norm2_w"], g2, lw["ffn_w_gate"], lw["ffn_w_up"], lw["ffn_w_down"], final_w)
    return x, p


def kernel(x, c, ctx, c_ctx, w_mod, b_mod, norm1_w, norm2_w, w_in, q_norm_w, k_norm_w, hy_conv_w, hy_conv_b,
           hy_w1, hy_b1, hy_w2, hy_b2, hy_w3, hy_freq, hy_decay, hy_skip, pool_w, pool_scale, gm_norm_w, gm_ws,
           gm_bs, w_br_attn, w_br_hyena, w_br_pool, w_br_gmlp, w_out, ffn_w_gate, ffn_w_up, ffn_w_down,
           final_norm_w):
    a = dict(w_in=w_in, q_norm_w=q_norm_w, k_norm_w=k_norm_w, hy_conv_w=hy_conv_w, hy_conv_b=hy_conv_b,
             hy_w1=hy_w1, hy_b1=hy_b1, hy_w2=hy_w2, hy_b2=hy_b2, hy_w3=hy_w3, hy_freq=hy_freq, hy_decay=hy_decay,
             hy_skip=hy_skip, pool_w=pool_w, pool_scale=pool_scale, gm_norm_w=gm_norm_w, gm_ws=gm_ws, gm_bs=gm_bs,
             w_br_attn=w_br_attn, w_br_hyena=w_br_hyena, w_br_pool=w_br_pool, w_br_gmlp=w_br_gmlp, w_out=w_out,
             ffn_w_gate=ffn_w_gate, ffn_w_up=ffn_w_up, ffn_w_down=ffn_w_down, norm1_w=norm1_w, norm2_w=norm2_w)
    B, n, D = x.shape
    nc = ctx.shape[1]
    depth = w_mod.shape[0]

    R = ((B + 1 + 15) // 16) * 16
    act = jnp.zeros((R, D), F32).at[:B].set(c).at[B].set(c_ctx)
    mod = _mod_call(act, w_mod, b_mod)

    rope = _rope_tables(n)
    rope_ctx = _identity_tables(nc)
    fc = _fft_consts(2 * n)
    fc_ctx = _fft_consts(2 * nc)

    xc = ctx
    for l in range(depth):
        lw = _layer_weights(l, a)
        mods = [m.reshape(B, 1, D) for m in jnp.split(mod[l, :B], 6, axis=-1)]
        cmods = [jnp.broadcast_to(m.reshape(1, 1, D), (B, 1, D)) for m in jnp.split(mod[l, B], 6, axis=-1)]
        if l < depth - 1:
            kf_ctx = _filter_spectra(nc, lw, fc_ctx)
            xc, pc = _mixer_block(xc, cmods, lw, rope_ctx, None, None, kf_ctx, fc_ctx)
        else:
            pc = _inproj_call(xc, cmods[0], cmods[1], lw["norm1_w"], lw["w_in"][:, :P_QKV])
        kf = _filter_spectra(n, lw, fc)
        x, _ = _mixer_block(x, mods, lw, rope, pc, rope_ctx, kf, fc, final_norm_w if l == depth - 1 else None)
    return x
```

```python
import functools
import math

import numpy as np
import jax
import jax.numpy as jnp
from jax import lax
from jax.experimental import pallas as pl
from jax.experimental.pallas import tpu as pltpu

F32 = jnp.float32
BF16 = jnp.bfloat16

D_MODEL = 1024
GRID_W = 64
RMS_EPS = 1e-6
HEAD_DIM = 64
N_Q_HEADS = 8
N_KV_HEADS = 2
ATTN_W = N_Q_HEADS * HEAD_DIM
KV_W = N_KV_HEADS * HEAD_DIM
ROPE_THETA = 10000.0
HYENA_W = 256
HYENA_ORDER = 2
HYENA_BANDS = 16
HYENA_HIDDEN = 64
HYENA_INNER = 2
POOL_W = 256
POOL_WINDOWS = (2, 4, 8, 16)
POOL_GROUP_W = POOL_W // len(POOL_WINDOWS)
POOL_HALO = 8
GMLP_W = 256
GMLP_CHUNK = 128
GMLP_GROUPS = 4
GMLP_GROUP_W = GMLP_W // GMLP_GROUPS
N_BRANCH = 4
FFN_HIDDEN = ((8 * D_MODEL // 3 + 255) // 256) * 256

Q_OFF = 0
K_OFF = Q_OFF + ATTN_W
V_OFF = K_OFF + KV_W
HY_OFF = V_OFF + KV_W
POOL_OFF = HY_OFF + 3 * HYENA_W
GM_OFF = POOL_OFF + POOL_W
GATE_OFF = GM_OFF + 2 * GMLP_W
IN_W = GATE_OFF + N_BRANCH * D_MODEL

P_Q = 0
P_KV = P_Q + ATTN_W
P_POOL = P_KV + 2 * KV_W
P_GMU = P_POOL + POOL_W
P_GMV = P_GMU + GMLP_W
P_HY = P_GMV + GMLP_W
P_W = P_HY + 3 * HYENA_W
P_QKV = P_POOL

LANES = 128
VMEM_LIMIT = 52 * 1024 * 1024


def _cparams(sem):
    return pltpu.CompilerParams(dimension_semantics=sem, vmem_limit_bytes=VMEM_LIMIT)


def _tile(n, cap, mult=8):
    best = None
    for t in range(mult, min(n, cap) + 1, mult):
        if n % t == 0:
            best = t
    assert best is not None, (n, cap, mult)
    return best


def _split(x):
    hi = x.astype(BF16)
    lo = (x - hi.astype(F32)).astype(BF16)
    return hi, lo


def _dot(a, b):
    return jnp.dot(a, b, preferred_element_type=F32)


def _dot3(a_hi, a_lo, x):
    x_hi, x_lo = _split(x)
    return _dot(a_hi, x_hi) + (_dot(a_lo, x_hi) + _dot(a_hi, x_lo))


def _dot3x(x, b_hi, b_lo):
    x_hi, x_lo = _split(x)
    return _dot(x_hi, b_hi) + (_dot(x_lo, b_hi) + _dot(x_hi, b_lo))


def _sigmoid(x):
    return 1.0 / (1.0 + jnp.exp(-x))


def _silu(x):
    return x * _sigmoid(x)


def _modulated_norm(x, nw, scale, shift):
    ms = jnp.mean(x * x, axis=-1, keepdims=True)
    y = x * lax.rsqrt(ms + RMS_EPS) * nw
    return y * (1.0 + scale) + shift


def _mod_kernel(a_ref, w_ref, b_ref, o_ref):
    a = _silu(a_ref[...]).astype(BF16)
    o_ref[...] = _dot(a, w_ref[...].astype(BF16)) + b_ref[...]


def _mod_call(act, w_mod, b_mod):
    L, D, W6 = w_mod.shape
    R = act.shape[0]
    tn = 1024
    return pl.pallas_call(
        _mod_kernel,
        grid=(L, W6 // tn),
        in_specs=[
            pl.BlockSpec((R, D), lambda l, j: (0, 0)),
            pl.BlockSpec((None, D, tn), lambda l, j: (l, 0, j)),
            pl.BlockSpec((None, 1, tn), lambda l, j: (l, 0, j)),
        ],
        out_specs=pl.BlockSpec((None, R, tn), lambda l, j: (l, 0, j)),
        out_shape=jax.ShapeDtypeStruct((L, R, W6), F32),
        compiler_params=_cparams(("arbitrary", "arbitrary")),
        name="mod",
    )(act, w_mod, b_mod.reshape(L, 1, W6))


ROW_BLOCK = 256
INPROJ_ROWS = 1024


def _row_blocks(tm):
    sub = ROW_BLOCK if tm % ROW_BLOCK == 0 else tm
    return [slice(r, r + sub) for r in range(0, tm, sub)]


def _inproj_kernel(x_ref, sh_ref, sc_ref, nw_ref, w_ref, o_ref):
    for rows in _row_blocks(x_ref.shape[0]):
        h = _modulated_norm(x_ref[rows, :], nw_ref[...], sc_ref[...], sh_ref[...]).astype(BF16)
        o_ref[rows, :] = _dot(h, w_ref[...])


def _inproj_call(x, shift, scale, norm_w, w):
    B, n, D = x.shape
    Nw = w.shape[1]
    tm = _tile(n, INPROJ_ROWS)
    return pl.pallas_call(
        _inproj_kernel,
        grid=(B, n // tm),
        in_specs=[
            pl.BlockSpec((None, tm, D), lambda b, i: (b, i, 0)),
            pl.BlockSpec((None, 1, D), lambda b, i: (b, 0, 0)),
            pl.BlockSpec((None, 1, D), lambda b, i: (b, 0, 0)),
            pl.BlockSpec((1, D), lambda b, i: (0, 0)),
            pl.BlockSpec((D, Nw), lambda b, i: (0, 0)),
        ],
        out_specs=pl.BlockSpec((None, tm, Nw), lambda b, i: (b, i, 0)),
        out_shape=jax.ShapeDtypeStruct((B, n, Nw), F32),
        compiler_params=_cparams(("parallel", "parallel")),
        name="inproj",
    )(x, shift, scale, norm_w.reshape(1, D), w)


def _head_norm_rope(t, w, cs, sn, bd_ref):
    sq_hi, sq_lo = _split(t * t)
    ms = _dot(sq_hi, bd_ref[...]) + _dot(sq_lo, bd_ref[...])
    y = t * lax.rsqrt(ms + RMS_EPS) * w
    lane = lax.broadcasted_iota(jnp.int32, y.shape, 1)
    swapped = jnp.where(lane % 2 == 0, pltpu.roll(y, LANES - 1, 1), pltpu.roll(y, 1, 1))
    return y * cs + swapped * sn


V_ROWS = 80
Q_PER_KV = N_Q_HEADS // N_KV_HEADS
LAG_INIT_KEYS = 128
LAG_LIMIT = 24.0
ATTN_QUERIES = 512
ATTN_KEYS = 768
ATTN_UNROLL = 2


def _value_rows(vt_g):
    tail = lax.broadcasted_iota(jnp.int32, (V_ROWS - HEAD_DIM, vt_g.shape[1]), 0)
    return jnp.concatenate([vt_g, jnp.where(tail == 0, 1.0, 0.0)], axis=0).astype(BF16)


FP8 = jnp.float8_e4m3fn
FP8_MAX_WEIGHT = 32.0


def _split8(x):
    hi = x.astype(FP8).astype(F32)
    return hi, (x - hi).astype(FP8).astype(F32)


def _qkprep_kernel(q_ref, kv_ref, cs_ref, sn_ref, qw_ref, kw_ref, bd_ref, kp_ref, qo_ref, ko_ref, vo_ref, *, fp8):
    cs = cs_ref[...]
    sn = sn_ref[...]
    tm = q_ref.shape[0]
    zeros = jnp.zeros((HEAD_DIM, tm), F32)
    heads_per_chunk = LANES // HEAD_DIM
    for c in range(ATTN_W // LANES):
        sl = slice(c * LANES, (c + 1) * LANES)
        r = _head_norm_rope(q_ref[:, sl], qw_ref[:, sl], cs, sn, bd_ref)
        if fp8:
            hi_t, lo_t = _split8(r.T)
            for half in range(heads_per_chunk):
                rows = slice(half * HEAD_DIM, (half + 1) * HEAD_DIM)
                qo_ref[c * heads_per_chunk + half] = jnp.concatenate(
                    [hi_t[rows], lo_t[rows], hi_t[rows], lo_t[rows]], axis=0).astype(FP8)
        else:
            rt = r.T
            for half in range(heads_per_chunk):
                h = c * heads_per_chunk + half
                blk = rt[half * HEAD_DIM:(half + 1) * HEAD_DIM]
                parts = [blk if g == h // Q_PER_KV else zeros for g in range(N_KV_HEADS)]
                qo_ref[h] = jnp.concatenate(parts, axis=0).astype(BF16)
    k = _head_norm_rope(kv_ref[:, :KV_W], kw_ref[...], cs, sn, bd_ref)
    if fp8:
        hi, lo = _split8(k)
        hi, lo = hi.astype(BF16), lo.astype(BF16)
        for g in range(N_KV_HEADS):
            ko_ref[g] = (_dot(hi, kp_ref[g, 0]) + _dot(lo, kp_ref[g, 1])).astype(FP8)
    else:
        ko_ref[0] = k.astype(BF16)
    vt = kv_ref[:, KV_W:].T
    for g in range(N_KV_HEADS):
        vo_ref[g] = _value_rows(vt[g * HEAD_DIM:(g + 1) * HEAD_DIM])


def _k_placement():
    pm = np.zeros((N_KV_HEADS, 2, KV_W, 4 * HEAD_DIM), np.float32)
    d = np.arange(HEAD_DIM)
    for g in range(N_KV_HEADS):
        for part in range(2):
            for rep in range(2):
                pm[g, part, g * HEAD_DIM + d, (2 * part + rep) * HEAD_DIM + d] = 1.0
    return jnp.asarray(pm, BF16)


def _qkprep_call(p, cs, sn, qw, kw, fp8):
    B, n, _ = p.shape
    tm = _tile(n, 512, LANES)
    bd = np.kron(np.eye(LANES // HEAD_DIM), np.full((HEAD_DIM, HEAD_DIM), 1.0 / HEAD_DIM))
    bd = jnp.asarray(bd, BF16)
    kd, kg, dt = (4 * HEAD_DIM, N_KV_HEADS, FP8) if fp8 else (KV_W, 1, BF16)
    return pl.pallas_call(
        functools.partial(_qkprep_kernel, fp8=fp8),
        grid=(n // tm, B),
        in_specs=[
            pl.BlockSpec((None, tm, ATTN_W), lambda i, b: (b, i, P_Q // ATTN_W)),
            pl.BlockSpec((None, tm, 2 * KV_W), lambda i, b: (b, i, P_KV // (2 * KV_W))),
            pl.BlockSpec((tm, LANES), lambda i, b: (i, 0)),
            pl.BlockSpec((tm, LANES), lambda i, b: (i, 0)),
            pl.BlockSpec((1, ATTN_W), lambda i, b: (0, 0)),
            pl.BlockSpec((1, KV_W), lambda i, b: (0, 0)),
            pl.BlockSpec((LANES, LANES), lambda i, b: (0, 0)),
            pl.BlockSpec((N_KV_HEADS, 2, KV_W, 4 * HEAD_DIM), lambda i, b: (0, 0, 0, 0)),
        ],
        out_specs=[
            pl.BlockSpec((None, N_Q_HEADS, kd, tm), lambda i, b: (b, 0, 0, i)),
            pl.BlockSpec((None, kg, tm, kd), lambda i, b: (b, 0, i, 0)),
            pl.BlockSpec((None, N_KV_HEADS, V_ROWS, tm), lambda i, b: (b, 0, 0, i)),
        ],
        out_shape=[
            jax.ShapeDtypeStruct((B, N_Q_HEADS, kd, n), dt),
            jax.ShapeDtypeStruct((B, kg, n, kd), dt),
            jax.ShapeDtypeStruct((B, N_KV_HEADS, V_ROWS, n), BF16),
        ],
        compiler_params=_cparams(("parallel", "parallel")),
        name="qkprep",
    )(p, p, cs, sn, qw, kw, bd, _k_placement())


def _attn_kernel(qt_ref, k_ref, vt_ref, o_ref, m_ref, acc_ref, *, tk):
    nk = vt_ref.shape[0]
    tq = m_ref.shape[-1]
    per_kv_head = k_ref.shape[0] > 1

    def keys(j, h):
        g = h // Q_PER_KV if per_kv_head else 0
        return k_ref[g, pl.ds(pl.multiple_of(j * tk, tk), tk), :]

    def lagged_chunk(j, jump):
        s_next = _dot(keys(j, 0), qt_ref[0])
        for h in range(N_Q_HEADS):
            s = s_next
            if h + 1 < N_Q_HEADS:
                s_next = _dot(keys(j, h + 1), qt_ref[h + 1])
            m_old = m_ref[h]
            p = jnp.exp2((s - m_old).astype(BF16))
            cmax = jnp.max(s, axis=0, keepdims=True)
            m_new = jnp.maximum(m_old, cmax)
            jump = jnp.maximum(jump, cmax - m_old)
            acc_ref[h] = jnp.exp2(m_old - m_new) * (acc_ref[h] + _dot(vt_ref[j, h // Q_PER_KV], p))
            m_ref[h] = m_new
        return jump

    def exact_chunk(j, carry):
        for h in range(N_Q_HEADS):
            s = _dot(keys(j, h), qt_ref[h])
            m_old = m_ref[h]
            m_new = jnp.maximum(m_old, jnp.max(s, axis=0, keepdims=True))
            p = jnp.exp2(s - m_new).astype(BF16)
            acc_ref[h] = jnp.exp2(m_old - m_new) * acc_ref[h] + _dot(vt_ref[j, h // Q_PER_KV], p)
            m_ref[h] = m_new
        return carry

    for h in range(N_Q_HEADS):
        k0 = k_ref[h // Q_PER_KV if per_kv_head else 0, 0:LAG_INIT_KEYS, :]
        m_ref[h] = jnp.max(_dot(k0, qt_ref[h]), axis=0, keepdims=True)
    acc_ref[...] = jnp.zeros(acc_ref.shape, F32)
    jump = lax.fori_loop(0, nk, lagged_chunk, jnp.zeros((1, tq), F32), unroll=ATTN_UNROLL)

    @pl.when(jnp.logical_not(jnp.max(jump) <= LAG_LIMIT))
    def _():
        m_ref[...] = jnp.full(m_ref.shape, -jnp.inf, F32)
        acc_ref[...] = jnp.zeros(acc_ref.shape, F32)
        lax.fori_loop(0, nk, exact_chunk, 0)

    for c in range(ATTN_W // LANES):
        heads = []
        for h in range(c * (LANES // HEAD_DIM), (c + 1) * (LANES // HEAD_DIM)):
            a = acc_ref[h]
            heads.append(a[:HEAD_DIM] / a[HEAD_DIM:HEAD_DIM + 1])
        o_ref[:, c * LANES:(c + 1) * LANES] = jnp.concatenate(heads, axis=0).T.astype(BF16)


def _attn_call(qt, k, vt):
    B, _, kd, n = qt.shape
    _, kg, Lk, _ = k.shape
    tq = _tile(n, ATTN_QUERIES, LANES)
    tk = _tile(Lk, ATTN_KEYS, LANES)
    nk = Lk // tk
    vt = vt.reshape(B, N_KV_HEADS, V_ROWS, nk, tk).transpose(0, 3, 1, 2, 4)
    return pl.pallas_call(
        functools.partial(_attn_kernel, tk=tk),
        grid=(B, n // tq),
        in_specs=[
            pl.BlockSpec((None, N_Q_HEADS, kd, tq), lambda b, i: (b, 0, 0, i)),
            pl.BlockSpec((None, kg, Lk, kd), lambda b, i: (b, 0, 0, 0)),
            pl.BlockSpec((None, nk, N_KV_HEADS, V_ROWS, tk), lambda b, i: (b, 0, 0, 0, 0)),
        ],
        out_specs=pl.BlockSpec((None, tq, ATTN_W), lambda b, i: (b, i, 0)),
        out_shape=jax.ShapeDtypeStruct((B, n, ATTN_W), BF16),
        scratch_shapes=[
            pltpu.VMEM((N_Q_HEADS, 1, tq), F32),
            pltpu.VMEM((N_Q_HEADS, V_ROWS, tq), F32),
        ],
        compiler_params=_cparams(("parallel", "parallel")),
        name="attn",
    )(qt, k, vt)


HALVES = HYENA_W // LANES


def _split_shape(B, n):
    assert B % 2 == 0
    return jax.ShapeDtypeStruct((B // 2, HALVES, 2, n, LANES), F32)


def _split_spec(tm):
    return pl.BlockSpec((None, HALVES, None, tm, LANES), lambda b, i: (b // 2, 0, b % 2, i, 0))


def _shift_rows(cur, prev_row, next_row, rows):
    tm = cur.shape[0]
    up = jnp.where(rows == 0, prev_row, pltpu.roll(cur, 1, 0))
    dn = jnp.where(rows == tm - 1, next_row, pltpu.roll(cur, tm - 1, 0))
    return up, dn


def _local_kernel(hy_ref, hyp_ref, hyn_ref, po_ref, pop_ref, pon_ref, gu_ref, gv_ref,
                  cw_ref, cb_ref, pw_ref, ps_ref, gnw_ref, gws_ref, gb_ref,
                  v_ref, x1_ref, x2_ref, pl_ref, gm_ref, *, n_seq):
    i = pl.program_id(1)
    last = pl.num_programs(1) - 1
    tm = hy_ref.shape[0]
    has_prev = (i > 0).astype(F32)
    has_next = (i < last).astype(F32)

    cur = hy_ref[...]
    rows = lax.broadcasted_iota(jnp.int32, cur.shape, 0)
    prev_row = hyp_ref[POOL_HALO - 1:POOL_HALO, :] * has_prev
    next_row = hyn_ref[0:1, :] * has_next
    up, dn = _shift_rows(cur, prev_row, next_row, rows)
    z = up * cw_ref[0:1, :] + cur * cw_ref[1:2, :] + dn * cw_ref[2:3, :] + cb_ref[...]
    for s, o_ref in enumerate((v_ref, x1_ref, x2_ref)):
        for hf in range(HYENA_W // LANES):
            lo = s * HYENA_W + hf * LANES
            o_ref[hf] = z[:, lo:lo + LANES]

    zc = po_ref[...]
    ext = jnp.concatenate([pop_ref[...] * has_prev, zc, pon_ref[...] * has_next], axis=0)
    te = tm + 2 * POOL_HALO
    lane = lax.broadcasted_iota(jnp.int32, zc.shape, 1)
    prow = lax.broadcasted_iota(jnp.int32, zc.shape, 0) + i * tm
    half = jnp.left_shift(1, lane // POOL_GROUP_W)
    def ahead(a, k):
        return pltpu.roll(a, (te - k) % te, 0)

    assert POOL_WINDOWS == (2, 4, 8, 16)
    sums = [ext + ahead(ext, -1)]
    for win in POOL_WINDOWS[:-1]:
        sums.append(ahead(sums[-1], win // 2) + ahead(sums[-1], -(win // 2)))
    group = lane // POOL_GROUP_W
    acc = sums[-1][POOL_HALO:POOL_HALO + tm]
    for g in range(len(POOL_WINDOWS) - 2, -1, -1):
        acc = jnp.where(group == g, sums[g][POOL_HALO:POOL_HALO + tm], acc)
    cnt = jnp.minimum(prow + half, n_seq) - jnp.maximum(prow - half, 0)
    dlt = acc / cnt.astype(F32) - zc
    pl_ref[...] = (_dot(dlt.astype(BF16), pw_ref[...]) * ps_ref[...]).astype(BF16)

    gv = gv_ref[...]
    mu = jnp.mean(gv, axis=-1, keepdims=True)
    ctr = gv - mu
    var = jnp.mean(ctr * ctr, axis=-1, keepdims=True)
    vn = (ctr * lax.rsqrt(var + RMS_EPS) * gnw_ref[...]).astype(BF16)
    group = lax.broadcasted_iota(jnp.int32, (GMLP_CHUNK, GMLP_W), 1) // GMLP_GROUP_W
    for c in range(tm // GMLP_CHUNK):
        rs = slice(c * GMLP_CHUNK, (c + 1) * GMLP_CHUNK)
        vc = vn[rs]
        mixed = gb_ref[...]
        for g in range(GMLP_GROUPS):
            mg = _dot(gws_ref[g], vc)
            mixed = mixed + jnp.where(group == g, mg, 0.0)
        gm_ref[rs, :] = (gu_ref[rs, :] * mixed).astype(BF16)


def _local_call(p, lw):
    B, n, _ = p.shape
    tm = _tile(n, 512, GMLP_CHUNK)
    nb8 = n // POOL_HALO
    r8 = tm // POOL_HALO
    HW = 3 * HYENA_W

    def cur(w, off):
        return pl.BlockSpec((None, tm, w), lambda b, i: (b, i, off // w))

    def prev(w, off):
        return pl.BlockSpec((None, POOL_HALO, w), lambda b, i: (b, jnp.maximum(i * r8 - 1, 0), off // w))

    def nxt(w, off):
        return pl.BlockSpec((None, POOL_HALO, w), lambda b, i: (b, jnp.minimum((i + 1) * r8, nb8 - 1), off // w))

    def full(shape):
        return pl.BlockSpec(shape, lambda b, i: (0,) * len(shape))

    def out(w):
        return pl.BlockSpec((None, tm, w), lambda b, i: (b, i, 0))

    return pl.pallas_call(
        functools.partial(_local_kernel, n_seq=n),
        grid=(B, n // tm),
        in_specs=[
            cur(HW, P_HY), prev(HW, P_HY), nxt(HW, P_HY),
            cur(POOL_W, P_POOL), prev(POOL_W, P_POOL), nxt(POOL_W, P_POOL),
            cur(GMLP_W, P_GMU), cur(GMLP_W, P_GMV),
            full((3, HW)), full((1, HW)), full((POOL_W, POOL_W)), full((1, POOL_W)),
            full((1, GMLP_W)), full((GMLP_GROUPS, GMLP_CHUNK, GMLP_CHUNK)), full((GMLP_CHUNK, GMLP_W)),
        ],
        out_specs=[_split_spec(tm), _split_spec(tm), _split_spec(tm), out(POOL_W), out(GMLP_W)],
        out_shape=[
            _split_shape(B, n), _split_shape(B, n), _split_shape(B, n),
            jax.ShapeDtypeStruct((B, n, POOL_W), BF16),
            jax.ShapeDtypeStruct((B, n, GMLP_W), BF16),
        ],
        compiler_params=_cparams(("parallel", "parallel")),
        name="local",
    )(p, p, p, p, p, p, p, p, lw["conv_w"], lw["conv_b"], lw["pool_bd"], lw["pool_scale"],
      lw["gm_norm_w"], lw["gm_ws"], lw["gm_bias"])


def _filt_kernel(z_ref, t_ref, w1h_ref, w1l_ref, b1_ref, w2h_ref, w2l_ref, b2_ref, fr_ref,
                 w3h_ref, w3l_ref, dec_ref, k_ref, s_ref, *, n_seq):
    i = pl.program_id(0)
    tb = z_ref.shape[0]
    fr = fr_ref[...]
    hdn = jnp.sin(fr * (_dot3x(z_ref[...], w1h_ref[...], w1l_ref[...]) + b1_ref[...]))
    for u in range(HYENA_INNER):
        hdn = jnp.sin(fr * (_dot3x(hdn, w2h_ref[u], w2l_ref[u]) + b2_ref[u:u + 1, :]))
    t = t_ref[...]
    tt = jnp.concatenate([t] * (HYENA_ORDER * HYENA_W // LANES), axis=1)
    window = jnp.exp(-tt * jnp.abs(dec_ref[...]))
    k = _dot3x(hdn, w3h_ref[...], w3l_ref[...]) * window
    rows = lax.broadcasted_iota(jnp.int32, k.shape, 0) + i * tb
    k = jnp.where(rows == n_seq, 0.0, k)
    for q in range(k_ref.shape[0]):
        k_ref[q] = k[:, q * LANES:(q + 1) * LANES]

    @pl.when(i == 0)
    def _():
        s_ref[...] = jnp.zeros(s_ref.shape, F32)

    s_ref[...] += jnp.sum(jnp.abs(k), axis=0, keepdims=True)


def _filt_call(n, lw):
    N = 2 * n
    tb = _tile(n, 512)
    nb_half = n // tb
    pos = np.arange(N)
    pos = np.where(pos < n, pos, N - pos).astype(np.float64)
    t = pos / (n - 1)
    bands = np.linspace(1e-4, HYENA_BANDS - 1, HYENA_BANDS)
    ang = (2.0 * math.pi / n) * pos[:, None] * bands
    z = np.concatenate([t[:, None], np.cos(ang), np.sin(ang)], axis=-1)
    zp = np.zeros((N, LANES), np.float32)
    zp[:, :z.shape[1]] = z
    tl = np.broadcast_to(t[:, None], (N, LANES)).astype(np.float32)
    CW = HYENA_ORDER * HYENA_W

    def full(shape):
        return pl.BlockSpec(shape, lambda i: (0,) * len(shape))

    def side(rows_):
        return pl.BlockSpec((None, rows_, CW), lambda i: (i // nb_half, 0, 0))

    return pl.pallas_call(
        functools.partial(_filt_kernel, n_seq=n),
        grid=(N // tb,),
        in_specs=[
            pl.BlockSpec((tb, LANES), lambda i: (i, 0)),
            pl.BlockSpec((tb, LANES), lambda i: (i, 0)),
            full((LANES, HYENA_HIDDEN)), full((LANES, HYENA_HIDDEN)), full((1, HYENA_HIDDEN)),
            full((HYENA_INNER, HYENA_HIDDEN, HYENA_HIDDEN)), full((HYENA_INNER, HYENA_HIDDEN, HYENA_HIDDEN)),
            full((HYENA_INNER, HYENA_HIDDEN)), full((1, HYENA_HIDDEN)),
            side(HYENA_HIDDEN), side(HYENA_HIDDEN), side(1),
        ],
        out_specs=[pl.BlockSpec((CW // LANES, tb, LANES), lambda i: (0, i, 0)), pl.BlockSpec((1, CW), lambda i: (0, 0))],
        out_shape=[jax.ShapeDtypeStruct((CW // LANES, N, LANES), F32), jax.ShapeDtypeStruct((1, CW), F32)],
        compiler_params=_cparams(("arbitrary",)),
        name="filt",
    )(jnp.asarray(zp), jnp.asarray(tl), lw["hy_w1h"], lw["hy_w1l"], lw["hy_b1"], lw["hy_w2h"], lw["hy_w2l"],
      lw["hy_b2"], lw["hy_freq"], lw["hy_w3h"], lw["hy_w3l"], lw["hy_decay"])


def _fft_dims(N):
    e = int(round(math.log2(N)))
    assert 2 ** e == N
    N1 = 2 ** ((e + 1) // 2)
    return N1, N // N1


def _hilo(a):
    a = np.asarray(a, np.float64)
    hi = jnp.asarray(a, F32).astype(BF16)
    lo = (jnp.asarray(a, F32) - hi.astype(F32)).astype(BF16)
    return hi, lo


@functools.lru_cache(maxsize=None)
def _fft_consts_np(N):
    N1, N2 = _fft_dims(N)
    h = N1 // 2
    k1 = np.arange(N1)
    F1 = np.exp(-2j * np.pi * np.outer(k1, k1) / N1)
    k2 = np.arange(N2)
    F2 = np.exp(-2j * np.pi * np.outer(k2, k2) / N2)
    tw = np.exp(-2j * np.pi * np.outer(k1, k2) / N)
    g1 = np.zeros((2 * N1, N1))
    g1[0::2, :h] = F1.real[:, :h]
    g1[0::2, h:] = -F1.imag[:, :h]
    g1[1::2, :h] = F1.imag[:, :h]
    g1[1::2, h:] = F1.real[:, :h]
    g1f = np.zeros((2 * N1, N1))
    g1f[0::2] = F1.real
    g1f[1::2] = F1.imag
    g2 = np.block([[F2.real, -F2.imag], [F2.imag, F2.real]])
    g2c = np.block([[F2.real, F2.imag], [-F2.imag, F2.real]])
    g3 = np.zeros((N1, 2 * N1))
    g3[:h, 0::2] = F1.real[:h]
    g3[:h, 1::2] = F1.imag[:h]
    g3[h:, 0::2] = -F1.imag[:h]
    g3[h:, 1::2] = F1.real[:h]
    g3 /= N
    twt = np.stack([tw.real, tw.imag], axis=1)
    return dict(g1=g1, g1f=g1f, g2=g2, g2c=g2c, g3=g3, tw=twt)


def _fft_consts(N):
    c = _fft_consts_np(N)
    out = {k: _hilo(c[k]) for k in ("g1", "g1f", "g2", "g2c", "g3")}
    N1, N2 = _fft_dims(N)
    tw = jnp.asarray(c["tw"], F32)
    out["tw"] = jnp.broadcast_to(tw[..., None], (N1, 2, N2, LANES))
    return out


T2_BLOCK = 8


def _load_8th(ref, j, rows):
    return ref.reshape(rows * T2_BLOCK, LANES)[pl.ds(j, rows, stride=T2_BLOCK), :]


def _store_8th(ref, j, rows, val):
    ref.reshape(rows * T2_BLOCK, LANES)[pl.ds(j, rows, stride=T2_BLOCK), :] = val


def _dft_in_kernel(gh_ref, gl_ref, x_ref, o_ref):
    N1 = x_ref.shape[1]
    for j in range(T2_BLOCK):
        xj = jnp.concatenate([_load_8th(x_ref.at[hf], j, N1) for hf in range(HALVES)], axis=1)
        a = _dot3(gh_ref[...], gl_ref[...], xj)
        for hf in range(HALVES):
            _store_8th(o_ref.at[hf], j, 2 * N1, a[:, hf * LANES:(hf + 1) * LANES])


def _dft_in_call(g, u5):
    gh, gl = g
    G, _, N1, N2, _ = u5.shape
    assert gh.shape == (2 * N1, N1)
    mat = pl.BlockSpec((2 * N1, N1), lambda b, i: (0, 0))
    return pl.pallas_call(
        _dft_in_kernel,
        grid=(G, N2 // T2_BLOCK),
        in_specs=[mat, mat, pl.BlockSpec((None, HALVES, N1, T2_BLOCK, LANES), lambda b, i: (b, 0, 0, i, 0))],
        out_specs=pl.BlockSpec((None, HALVES, N1, 2, T2_BLOCK, LANES), lambda b, i: (b, 0, 0, 0, i, 0)),
        out_shape=jax.ShapeDtypeStruct((G, HALVES, N1, 2, N2, LANES), F32),
        compiler_params=_cparams(("parallel", "parallel")),
        name="dft_in",
    )(gh, gl, u5)


def _dft_out_kernel(gh_ref, gl_ref, d_ref, gate_ref, u_ref, sk_ref, o_ref):
    N1 = gate_ref.shape[1]
    for j in range(T2_BLOCK):
        dj = jnp.concatenate([_load_8th(d_ref.at[hf], j, 2 * N1) for hf in range(HALVES)], axis=1)
        y = _dot3(gh_ref[...], gl_ref[...], dj)
        for hf in range(HALVES):
            conv = y[:, hf * LANES:(hf + 1) * LANES]
            u = _load_8th(u_ref.at[hf], j, N1)
            _store_8th(o_ref.at[hf], j, N1, _load_8th(gate_ref.at[hf], j, N1) * (conv + u * sk_ref[hf]))


def _dft_out_call(g, d6, gate5, u5, skip):
    gh, gl = g
    G, _, N1, N2, _ = u5.shape
    assert gh.shape == (N1, 2 * N1)
    mat = pl.BlockSpec((N1, 2 * N1), lambda b, i: (0, 0))
    blk = pl.BlockSpec((None, HALVES, N1, T2_BLOCK, LANES), lambda b, i: (b, 0, 0, i, 0))
    return pl.pallas_call(
        _dft_out_kernel,
        grid=(G, N2 // T2_BLOCK),
        in_specs=[mat, mat, pl.BlockSpec((None, HALVES, N1, 2, T2_BLOCK, LANES), lambda b, i: (b, 0, 0, 0, i, 0)),
                  blk, blk, pl.BlockSpec((HALVES, 1, LANES), lambda b, i: (0, 0, 0))],
        out_specs=blk,
        out_shape=jax.ShapeDtypeStruct(u5.shape, F32),
        compiler_params=_cparams(("parallel", "parallel")),
        name="dft_out",
    )(gh, gl, d6, gate5, u5, skip)


def _cmul(ar, ai, br, bi):
    return ar * br - ai * bi, ar * bi + ai * br


def _lane_tile(t, c):
    return jnp.concatenate([t] * (c // LANES), axis=-1)


def _both_halves(a_ref, k, part):
    return jnp.concatenate([a_ref[hf, k, part] for hf in range(HALVES)], axis=-1)


def _spec_kernel(a_ref, tw_ref, g2h_ref, g2l_ref, s_ref, o_ref):
    _, kb, _, N2, _ = a_ref.shape
    inv = 1.0 / s_ref[...]
    for k in range(kb):
        twr = _lane_tile(tw_ref[k, 0], HYENA_W)
        twi = _lane_tile(tw_ref[k, 1], HYENA_W)
        br, bi = _cmul(_both_halves(a_ref, k, 0), _both_halves(a_ref, k, 1), twr, twi)
        zz = _dot3(g2h_ref[...], g2l_ref[...], jnp.concatenate([br, bi], axis=0))
        o_ref[k, 0] = zz[:N2] * inv
        o_ref[k, 1] = zz[N2:] * inv


def _conv_kernel(a_ref, tw_ref, kf_ref, g2h_ref, g2l_ref, g2ch_ref, g2cl_ref, o_ref):
    _, kb, _, N2, _ = a_ref.shape
    for k in range(kb):
        twr = _lane_tile(tw_ref[k, 0], HYENA_W)
        twi = _lane_tile(tw_ref[k, 1], HYENA_W)
        br, bi = _cmul(_both_halves(a_ref, k, 0), _both_halves(a_ref, k, 1), twr, twi)
        zz = _dot3(g2h_ref[...], g2l_ref[...], jnp.concatenate([br, bi], axis=0))
        yr, yi = _cmul(zz[:N2], zz[N2:], kf_ref[k, 0], kf_ref[k, 1])
        cc = _dot3(g2ch_ref[...], g2cl_ref[...], jnp.concatenate([yr, yi], axis=0))
        dr, di = _cmul(cc[:N2], cc[N2:], twr, -twi)
        for hf in range(HALVES):
            o_ref[hf, k, 0] = dr[:, hf * LANES:(hf + 1) * LANES]
            o_ref[hf, k, 1] = di[:, hf * LANES:(hf + 1) * LANES]


def _spec_call(a6, fc, s):
    G, _, N1, _, N2, _ = a6.shape
    kb = _tile(N1, 8, 1)
    mat = pl.BlockSpec((2 * N2, 2 * N2), lambda i, b: (0, 0))
    return pl.pallas_call(
        _spec_kernel,
        grid=(N1 // kb, G),
        in_specs=[pl.BlockSpec((None, HALVES, kb, 2, N2, LANES), lambda i, b: (b, 0, i, 0, 0, 0)),
                  pl.BlockSpec((kb, 2, N2, LANES), lambda i, b: (i, 0, 0, 0)), mat, mat,
                  pl.BlockSpec((None, 1, HYENA_W), lambda i, b: (b, 0, 0))],
        out_specs=pl.BlockSpec((None, kb, 2, N2, HYENA_W), lambda i, b: (b, i, 0, 0, 0)),
        out_shape=jax.ShapeDtypeStruct((G, N1, 2, N2, HYENA_W), F32),
        compiler_params=_cparams(("parallel", "parallel")),
        name="filt_spec",
    )(a6, fc["tw"], fc["g2"][0], fc["g2"][1], s)


def _conv_call(a6, kf, fc):
    G, _, N1, _, N2, _ = a6.shape
    kb = _tile(N1, 8, 1)
    blk = pl.BlockSpec((None, HALVES, kb, 2, N2, LANES), lambda i, b: (b, 0, i, 0, 0, 0))
    mat = pl.BlockSpec((2 * N2, 2 * N2), lambda i, b: (0, 0))
    return pl.pallas_call(
        _conv_kernel,
        grid=(N1 // kb, G),
        in_specs=[blk, pl.BlockSpec((kb, 2, N2, LANES), lambda i, b: (i, 0, 0, 0)),
                  pl.BlockSpec((kb, 2, N2, HYENA_W), lambda i, b: (i, 0, 0, 0)), mat, mat, mat, mat],
        out_specs=blk,
        out_shape=jax.ShapeDtypeStruct(a6.shape, F32),
        compiler_params=_cparams(("parallel", "parallel")),
        name="spec_conv",
    )(a6, fc["tw"], kf, fc["g2"][0], fc["g2"][1], fc["g2c"][0], fc["g2c"][1])


def _filter_spectra(n, lw, fc):
    N1, N2 = _fft_dims(2 * n)
    k, s = _filt_call(n, lw)
    a6 = _dft_in_call(fc["g1f"], k.reshape(HYENA_ORDER, HALVES, N1, N2, LANES))
    return _spec_call(a6, fc, s.reshape(HYENA_ORDER, 1, HYENA_W))


def _hyena(v, x1, x2, kf, lw, fc):
    P, _, _, n, _ = v.shape
    N1, N2 = _fft_dims(2 * n)
    shape5 = (P, HALVES, N1, N2, LANES)
    y = v.reshape(shape5)
    for o, gate in enumerate((x1, x2)):
        d = _conv_call(_dft_in_call(fc["g1"], y), kf[o], fc)
        y = _dft_out_call(fc["g3"], d, gate.reshape(shape5), y, lw["hy_skip"][o].reshape(HALVES, 1, LANES))
    return y.reshape(v.shape)


def _merge_kernel(at_ref, hy_ref, po_ref, gm_ref, x_ref, sh_ref, sc_ref, nw_ref, gr_ref,
                  wgate_ref, wa_ref, wh_ref, wp_ref, wg_ref, wo_ref, o_ref):
    D = x_ref.shape[1]
    for rows in _row_blocks(x_ref.shape[0]):
        x = x_ref[rows, :]
        h = _modulated_norm(x, nw_ref[...], sc_ref[...], sh_ref[...]).astype(BF16)
        hy = jnp.concatenate([hy_ref[hf, rows, :] for hf in range(HALVES)], axis=-1).astype(BF16)
        branches = ((at_ref[rows, :], wa_ref), (hy, wh_ref), (po_ref[rows, :], wp_ref), (gm_ref[rows, :], wg_ref))
        merged = None
        for i, (br, w_ref) in enumerate(branches):
            term = _sigmoid(_dot(h, wgate_ref[:, i * D:(i + 1) * D])) * _dot(br, w_ref[...])
            merged = term if merged is None else merged + term
        o_ref[rows, :] = x + gr_ref[...] * _dot(merged.astype(BF16), wo_ref[...])


MERGE_ROWS = 1024


def _merge_call(attn, hy, po, gm, x, shift, scale, norm_w, gate_res, lw):
    B, n, D = x.shape
    tm = _tile(n, MERGE_ROWS)

    def rows(w):
        return pl.BlockSpec((None, tm, w), lambda b, i: (b, i, 0))

    def full(shape):
        return pl.BlockSpec(shape, lambda b, i: (0,) * len(shape))

    vec = pl.BlockSpec((None, 1, D), lambda b, i: (b, 0, 0))
    return pl.pallas_call(
        _merge_kernel,
        grid=(B, n // tm),
        in_specs=[
            rows(ATTN_W), _split_spec(tm), rows(POOL_W), rows(GMLP_W), rows(D),
            vec, vec, full((1, D)), vec,
            full((D, N_BRANCH * D)),
            full((ATTN_W, D)), full((HYENA_W, D)), full((POOL_W, D)), full((GMLP_W, D)), full((D, D)),
        ],
        out_specs=rows(D),
        out_shape=jax.ShapeDtypeStruct((B, n, D), F32),
        compiler_params=_cparams(("parallel", "parallel")),
        name="merge",
    )(attn, hy, po, gm, x, shift, scale, norm_w.reshape(1, D), gate_res, lw["w_gate"], lw["w_br_attn"],
      lw["w_br_hyena"], lw["w_br_pool"], lw["w_br_gmlp"], lw["w_out"])


FFN_ROWS = 1024


def _ffn_kernel(x_ref, sh_ref, sc_ref, nw_ref, gr_ref, wg_ref, wu_ref, wd_ref, fw_ref, o_ref, *, final):
    for rows in _row_blocks(x_ref.shape[0]):
        x = x_ref[rows, :]
        h = _modulated_norm(x, nw_ref[...], sc_ref[...], sh_ref[...]).astype(BF16)
        a = _silu(_dot(h, wg_ref[...])) * _dot(h, wu_ref[...])
        y = x + gr_ref[...] * _dot(a.astype(BF16), wd_ref[...])
        if final:
            y = y * lax.rsqrt(jnp.mean(y * y, axis=-1, keepdims=True) + RMS_EPS) * fw_ref[...]
        o_ref[rows, :] = y


def _ffn_call(x, shift, scale, norm_w, gate_res, wg, wu, wd, final_w=None):
    B, n, D = x.shape
    H = wg.shape[1]
    tm = _tile(n, FFN_ROWS)
    vec = pl.BlockSpec((None, 1, D), lambda b, i: (b, 0, 0))
    one = pl.BlockSpec((1, D), lambda b, i: (0, 0))
    final = final_w is not None
    fw = (final_w if final else norm_w).reshape(1, D)
    return pl.pallas_call(
        functools.partial(_ffn_kernel, final=final),
        grid=(B, n // tm),
        in_specs=[
            pl.BlockSpec((None, tm, D), lambda b, i: (b, i, 0)),
            vec, vec, one, vec,
            pl.BlockSpec((D, H), lambda b, i: (0, 0)),
            pl.BlockSpec((D, H), lambda b, i: (0, 0)),
            pl.BlockSpec((H, D), lambda b, i: (0, 0)),
            one,
        ],
        out_specs=pl.BlockSpec((None, tm, D), lambda b, i: (b, i, 0)),
        out_shape=jax.ShapeDtypeStruct((B, n, D), F32),
        compiler_params=_cparams(("parallel", "parallel")),
        name="ffn",
    )(x, shift, scale, norm_w.reshape(1, D), gate_res, wg, wu, wd, fw)


def _rope_tables(n):
    tok = np.arange(n)
    axis_dim = HEAD_DIM // 2
    inv_freq = ROPE_THETA ** (-np.arange(0, axis_dim, 2, dtype=np.float64) / axis_dim)
    inv_freq = inv_freq.astype(np.float32).astype(np.float64)
    ang = np.concatenate([(tok // GRID_W)[:, None] * inv_freq, (tok % GRID_W)[:, None] * inv_freq], axis=-1)
    ang = ang.astype(np.float32)
    cos = np.repeat(np.cos(ang), 2, axis=-1)
    sin = np.repeat(np.sin(ang), 2, axis=-1)
    sin[:, 0::2] *= -1.0
    reps = LANES // HEAD_DIM
    return jnp.asarray(np.tile(cos, (1, reps)), F32), jnp.asarray(np.tile(sin, (1, reps)), F32)


def _identity_tables(n):
    return jnp.ones((n, LANES), F32), jnp.zeros((n, LANES), F32)


def _layer_weights(l, a):
    w_in = a["w_in"][l]
    w_perm = jnp.concatenate([
        w_in[:, Q_OFF:K_OFF], w_in[:, K_OFF:HY_OFF], w_in[:, POOL_OFF:GM_OFF], w_in[:, GM_OFF:GATE_OFF],
        w_in[:, HY_OFF:POOL_OFF]], axis=1).astype(BF16)
    w3 = a["hy_w3"][l].reshape(HYENA_HIDDEN, HYENA_ORDER, 2, HYENA_W).transpose(2, 0, 1, 3)
    w3 = w3.reshape(2, HYENA_HIDDEN, HYENA_ORDER * HYENA_W)
    dec = a["hy_decay"][l].reshape(HYENA_ORDER, 2, HYENA_W).transpose(1, 0, 2).reshape(2, 1, HYENA_ORDER * HYENA_W)
    w1 = jnp.zeros((LANES, HYENA_HIDDEN), F32).at[:a["hy_w1"].shape[1]].set(a["hy_w1"][l])
    w1h, w1l = _split(w1)
    w2h, w2l = _split(a["hy_w2"][l])
    w3h, w3l = _split(w3)
    qw16 = jnp.tile(a["q_norm_w"][l], N_Q_HEADS).reshape(1, ATTN_W) * Q_SCALE
    kw16 = jnp.tile(a["k_norm_w"][l], N_KV_HEADS).reshape(1, KV_W)
    rms_q = jnp.sqrt(jnp.mean(qw16 * qw16))
    rms_k = jnp.sqrt(jnp.mean(kw16 * kw16))
    bal = jnp.exp2(jnp.clip(jnp.round(0.5 * jnp.log2(rms_k / rms_q)), -20.0, 20.0))
    bal = jnp.where(jnp.isfinite(bal), bal, 1.0)
    fp8_ok = (jnp.max(jnp.abs(qw16 * bal)) <= FP8_MAX_WEIGHT) & (jnp.max(jnp.abs(kw16 / bal)) <= FP8_MAX_WEIGHT)
    eye = jnp.eye(len(POOL_WINDOWS), dtype=F32)
    pool_bd = jnp.einsum("gh,gcd->gchd", eye, a["pool_w"][l]).reshape(POOL_W, POOL_W).astype(BF16)
    gm_bias = jnp.repeat(a["gm_bs"][l].T, GMLP_GROUP_W, axis=1)
    return dict(
        w_in=w_perm, w_gate=w_in[:, GATE_OFF:].astype(BF16),
        qw16=qw16, kw16=kw16, qw8=qw16 * bal, kw8=kw16 / bal, fp8_ok=fp8_ok,
        conv_w=a["hy_conv_w"][l], conv_b=a["hy_conv_b"][l].reshape(1, -1),
        hy_w1h=w1h, hy_w1l=w1l, hy_b1=a["hy_b1"][l].reshape(1, -1), hy_w2h=w2h, hy_w2l=w2l, hy_b2=a["hy_b2"][l],
        hy_w3h=w3h, hy_w3l=w3l, hy_freq=a["hy_freq"][l].reshape(1, -1), hy_decay=dec, hy_skip=a["hy_skip"][l],
        pool_bd=pool_bd, pool_scale=a["pool_scale"][l].reshape(1, -1),
        gm_norm_w=a["gm_norm_w"][l].reshape(1, -1), gm_ws=a["gm_ws"][l].astype(BF16), gm_bias=gm_bias,
        w_br_attn=a["w_br_attn"][l].astype(BF16), w_br_hyena=a["w_br_hyena"][l].astype(BF16),
        w_br_pool=a["w_br_pool"][l].astype(BF16), w_br_gmlp=a["w_br_gmlp"][l].astype(BF16),
        w_out=a["w_out"][l].astype(BF16),
        ffn_w_gate=a["ffn_w_gate"][l].astype(BF16), ffn_w_up=a["ffn_w_up"][l].astype(BF16),
        ffn_w_down=a["ffn_w_down"][l].astype(BF16),
        norm1_w=a["norm1_w"][l], norm2_w=a["norm2_w"][l],
    )


Q_SCALE = (HEAD_DIM ** -0.5) * math.log2(math.e)


def _attend(p, p_ctx, rope, rope_ctx, lw):
    def run(fp8):
        qw, kw = (lw["qw8"], lw["kw8"]) if fp8 else (lw["qw16"], lw["kw16"])
        qt, k, vt = _qkprep_call(p, rope[0], rope[1], qw, kw, fp8)
        if p_ctx is not None:
            _, k_c, vt_c = _qkprep_call(p_ctx, rope_ctx[0], rope_ctx[1], qw, kw, fp8)
            k = jnp.concatenate([k_c, k], axis=2)
            vt = jnp.concatenate([vt_c, vt], axis=3)
        return _attn_call(qt, k, vt)

    return lax.cond(lw["fp8_ok"], lambda: run(True), lambda: run(False))


def _mixer_block(x, mods, lw, rope, p_ctx, rope_ctx, kf, fc, final_w=None):
    sh1, s1, g1, sh2, s2, g2 = mods
    p = _inproj_call(x, sh1, s1, lw["norm1_w"], lw["w_in"])
    attn = _attend(p, p_ctx, rope, rope_ctx, lw)
    hv, hx1, hx2, po, gm = _local_call(p, lw)
    hy = _hyena(hv, hx1, hx2, kf, lw, fc)
    x = _merge_call(attn, hy, po, gm, x, sh1, s1, lw["norm1_w"], g1, lw)
    x = _ffn_call(x, sh2, s2, lw["norm2_w"], g2, lw["ffn_w_gate"], lw["ffn_w_up"], lw["ffn_w_down"], final_w)
    return x, p


def kernel(x, c, ctx, c_ctx, w_mod, b_mod, norm1_w, norm2_w, w_in, q_norm_w, k_norm_w, hy_conv_w, hy_conv_b,
           hy_w1, hy_b1, hy_w2, hy_b2, hy_w3, hy_freq, hy_decay, hy_skip, pool_w, pool_scale, gm_norm_w, gm_ws,
           gm_bs, w_br_attn, w_br_hyena, w_br_pool, w_br_gmlp, w_out, ffn_w_gate, ffn_w_up, ffn_w_down,
           final_norm_w):
    a = dict(w_in=w_in, q_norm_w=q_norm_w, k_norm_w=k_norm_w, hy_conv_w=hy_conv_w, hy_conv_b=hy_conv_b,
             hy_w1=hy_w1, hy_b1=hy_b1, hy_w2=hy_w2, hy_b2=hy_b2, hy_w3=hy_w3, hy_freq=hy_freq, hy_decay=hy_decay,
             hy_skip=hy_skip, pool_w=pool_w, pool_scale=pool_scale, gm_norm_w=gm_norm_w, gm_ws=gm_ws, gm_bs=gm_bs,
             w_br_attn=w_br_attn, w_br_hyena=w_br_hyena, w_br_pool=w_br_pool, w_br_gmlp=w_br_gmlp, w_out=w_out,
             ffn_w_gate=ffn_w_gate, ffn_w_up=ffn_w_up, ffn_w_down=ffn_w_down, norm1_w=norm1_w, norm2_w=norm2_w)
    B, n, D = x.shape
    nc = ctx.shape[1]
    depth = w_mod.shape[0]

    R = ((B + 1 + 15) // 16) * 16
    act = jnp.zeros((R, D), F32).at[:B].set(c).at[B].set(c_ctx)
    mod = _mod_call(act, w_mod, b_mod)

    rope = _rope_tables(n)
    rope_ctx = _identity_tables(nc)
    fc = _fft_consts(2 * n)
    fc_ctx = _fft_consts(2 * nc)

    xc = ctx
    for l in range(depth):
        lw = _layer_weights(l, a)
        mods = [m.reshape(B, 1, D) for m in jnp.split(mod[l, :B], 6, axis=-1)]
        cmods = [jnp.broadcast_to(m.reshape(1, 1, D), (B, 1, D)) for m in jnp.split(mod[l, B], 6, axis=-1)]
        if l < depth - 1:
            kf_ctx = _filter_spectra(nc, lw, fc_ctx)
            xc, pc = _mixer_block(xc, cmods, lw, rope_ctx, None, None, kf_ctx, fc_ctx)
        else:
            pc = _inproj_call(xc, cmods[0], cmods[1], lw["norm1_w"], lw["w_in"][:, :P_QKV])
        kf = _filter_spectra(n, lw, fc)
        x, _ = _mixer_block(x, mods, lw, rope, pc, rope_ctx, kf, fc, final_norm_w if l == depth - 1 else None)
    return x
```

```python
import functools
import math

import numpy as np
import jax
import jax.numpy as jnp
from jax import lax
from jax.experimental import pallas as pl
from jax.experimental.pallas import tpu as pltpu

F32 = jnp.float32
BF16 = jnp.bfloat16

D_MODEL = 1024
GRID_W = 64
RMS_EPS = 1e-6
HEAD_DIM = 64
N_Q_HEADS = 8
N_KV_HEADS = 2
ATTN_W = N_Q_HEADS * HEAD_DIM
KV_W = N_KV_HEADS * HEAD_DIM
ROPE_THETA = 10000.0
HYENA_W = 256
HYENA_ORDER = 2
HYENA_BANDS = 16
HYENA_HIDDEN = 64
HYENA_INNER = 2
POOL_W = 256
POOL_WINDOWS = (2, 4, 8, 16)
POOL_GROUP_W = POOL_W // len(POOL_WINDOWS)
POOL_HALO = 8
GMLP_W = 256
GMLP_CHUNK = 128
GMLP_GROUPS = 4
GMLP_GROUP_W = GMLP_W // GMLP_GROUPS
N_BRANCH = 4
FFN_HIDDEN = ((8 * D_MODEL // 3 + 255) // 256) * 256

Q_OFF = 0
K_OFF = Q_OFF + ATTN_W
V_OFF = K_OFF + KV_W
HY_OFF = V_OFF + KV_W
POOL_OFF = HY_OFF + 3 * HYENA_W
GM_OFF = POOL_OFF + POOL_W
GATE_OFF = GM_OFF + 2 * GMLP_W
IN_W = GATE_OFF + N_BRANCH * D_MODEL

P_Q = 0
P_KV = P_Q + ATTN_W
P_POOL = P_KV + 2 * KV_W
P_GMU = P_POOL + POOL_W
P_GMV = P_GMU + GMLP_W
P_HY = P_GMV + GMLP_W
P_W = P_HY + 3 * HYENA_W
P_QKV = P_POOL

LANES = 128
VMEM_LIMIT = 52 * 1024 * 1024


def _cparams(sem):
    return pltpu.CompilerParams(dimension_semantics=sem, vmem_limit_bytes=VMEM_LIMIT)


def _tile(n, cap, mult=8):
    best = None
    for t in range(mult, min(n, cap) + 1, mult):
        if n % t == 0:
            best = t
    assert best is not None, (n, cap, mult)
    return best


def _split(x):
    hi = x.astype(BF16)
    lo = (x - hi.astype(F32)).astype(BF16)
    return hi, lo


def _dot(a, b):
    return jnp.dot(a, b, preferred_element_type=F32)


def _dot3(a_hi, a_lo, x):
    x_hi, x_lo = _split(x)
    return _dot(a_hi, x_hi) + (_dot(a_lo, x_hi) + _dot(a_hi, x_lo))


def _dot3x(x, b_hi, b_lo):
    x_hi, x_lo = _split(x)
    return _dot(x_hi, b_hi) + (_dot(x_lo, b_hi) + _dot(x_hi, b_lo))


def _sigmoid(x):
    return 1.0 / (1.0 + jnp.exp(-x))


def _silu(x):
    return x * _sigmoid(x)


def _modulated_norm(x, nw, scale, shift):
    ms = jnp.mean(x * x, axis=-1, keepdims=True)
    y = x * lax.rsqrt(ms + RMS_EPS) * nw
    return y * (1.0 + scale) + shift


def _mod_kernel(a_ref, w_ref, b_ref, o_ref):
    a = _silu(a_ref[...]).astype(BF16)
    o_ref[...] = _dot(a, w_ref[...].astype(BF16)) + b_ref[...]


def _mod_call(act, w_mod, b_mod):
    L, D, W6 = w_mod.shape
    R = act.shape[0]
    tn = 1024
    return pl.pallas_call(
        _mod_kernel,
        grid=(L, W6 // tn),
        in_specs=[
            pl.BlockSpec((R, D), lambda l, j: (0, 0)),
            pl.BlockSpec((None, D, tn), lambda l, j: (l, 0, j)),
            pl.BlockSpec((None, 1, tn), lambda l, j: (l, 0, j)),
        ],
        out_specs=pl.BlockSpec((None, R, tn), lambda l, j: (l, 0, j)),
        out_shape=jax.ShapeDtypeStruct((L, R, W6), F32),
        compiler_params=_cparams(("arbitrary", "arbitrary")),
        name="mod",
    )(act, w_mod, b_mod.reshape(L, 1, W6))


ROW_BLOCK = 256
INPROJ_ROWS = 1024


def _row_blocks(tm):
    sub = ROW_BLOCK if tm % ROW_BLOCK == 0 else tm
    return [slice(r, r + sub) for r in range(0, tm, sub)]


def _inproj_kernel(x_ref, sh_ref, sc_ref, nw_ref, w_ref, o_ref):
    for rows in _row_blocks(x_ref.shape[0]):
        h = _modulated_norm(x_ref[rows, :], nw_ref[...], sc_ref[...], sh_ref[...]).astype(BF16)
        o_ref[rows, :] = _dot(h, w_ref[...])


def _inproj_call(x, shift, scale, norm_w, w):
    B, n, D = x.shape
    Nw = w.shape[1]
    tm = _tile(n, INPROJ_ROWS)
    return pl.pallas_call(
        _inproj_kernel,
        grid=(B, n // tm),
        in_specs=[
            pl.BlockSpec((None, tm, D), lambda b, i: (b, i, 0)),
            pl.BlockSpec((None, 1, D), lambda b, i: (b, 0, 0)),
            pl.BlockSpec((None, 1, D), lambda b, i: (b, 0, 0)),
            pl.BlockSpec((1, D), lambda b, i: (0, 0)),
            pl.BlockSpec((D, Nw), lambda b, i: (0, 0)),
        ],
        out_specs=pl.BlockSpec((None, tm, Nw), lambda b, i: (b, i, 0)),
        out_shape=jax.ShapeDtypeStruct((B, n, Nw), F32),
        compiler_params=_cparams(("parallel", "parallel")),
        name="inproj",
    )(x, shift, scale, norm_w.reshape(1, D), w)


def _head_norm_rope(t, w, cs, sn, bd_ref):
    sq_hi, sq_lo = _split(t * t)
    ms = _dot(sq_hi, bd_ref[...]) + _dot(sq_lo, bd_ref[...])
    y = t * lax.rsqrt(ms + RMS_EPS) * w
    lane = lax.broadcasted_iota(jnp.int32, y.shape, 1)
    swapped = jnp.where(lane % 2 == 0, pltpu.roll(y, LANES - 1, 1), pltpu.roll(y, 1, 1))
    return y * cs + swapped * sn


V_ROWS = 80
Q_PER_KV = N_Q_HEADS // N_KV_HEADS
LAG_INIT_KEYS = 128
LAG_LIMIT = 24.0
ATTN_QUERIES = 512
ATTN_KEYS = 768
ATTN_UNROLL = 2


def _value_rows(vt_g):
    tail = lax.broadcasted_iota(jnp.int32, (V_ROWS - HEAD_DIM, vt_g.shape[1]), 0)
    return jnp.concatenate([vt_g, jnp.where(tail == 0, 1.0, 0.0)], axis=0).astype(BF16)


FP8 = jnp.float8_e4m3fn
FP8_MAX_WEIGHT = 32.0


def _split8(x):
    hi = x.astype(FP8).astype(F32)
    return hi, (x - hi).astype(FP8).astype(F32)


def _qkprep_kernel(q_ref, kv_ref, cs_ref, sn_ref, qw_ref, kw_ref, bd_ref, kp_ref, qo_ref, ko_ref, vo_ref, *, fp8):
    cs = cs_ref[...]
    sn = sn_ref[...]
    tm = q_ref.shape[0]
    zeros = jnp.zeros((HEAD_DIM, tm), F32)
    heads_per_chunk = LANES // HEAD_DIM
    for c in range(ATTN_W // LANES):
        sl = slice(c * LANES, (c + 1) * LANES)
        r = _head_norm_rope(q_ref[:, sl], qw_ref[:, sl], cs, sn, bd_ref)
        if fp8:
            hi_t, lo_t = _split8(r.T)
            for half in range(heads_per_chunk):
                rows = slice(half * HEAD_DIM, (half + 1) * HEAD_DIM)
                qo_ref[c * heads_per_chunk + half] = jnp.concatenate(
                    [hi_t[rows], lo_t[rows], hi_t[rows], lo_t[rows]], axis=0).astype(FP8)
        else:
            rt = r.T
            for half in range(heads_per_chunk):
                h = c * heads_per_chunk + half
                blk = rt[half * HEAD_DIM:(half + 1) * HEAD_DIM]
                parts = [blk if g == h // Q_PER_KV else zeros for g in range(N_KV_HEADS)]
                qo_ref[h] = jnp.concatenate(parts, axis=0).astype(BF16)
    k = _head_norm_rope(kv_ref[:, :KV_W], kw_ref[...], cs, sn, bd_ref)
    if fp8:
        hi, lo = _split8(k)
        hi, lo = hi.astype(BF16), lo.astype(BF16)
        for g in range(N_KV_HEADS):
            ko_ref[g] = (_dot(hi, kp_ref[g, 0]) + _dot(lo, kp_ref[g, 1])).astype(FP8)
    else:
        ko_ref[0] = k.astype(BF16)
    vt = kv_ref[:, KV_W:].T
    for g in range(N_KV_HEADS):
        vo_ref[g] = _value_rows(vt[g * HEAD_DIM:(g + 1) * HEAD_DIM])


def _k_placement():
    pm = np.zeros((N_KV_HEADS, 2, KV_W, 4 * HEAD_DIM), np.float32)
    d = np.arange(HEAD_DIM)
    for g in range(N_KV_HEADS):
        for part in range(2):
            for rep in range(2):
                pm[g, part, g * HEAD_DIM + d, (2 * part + rep) * HEAD_DIM + d] = 1.0
    return jnp.asarray(pm, BF16)


def _qkprep_call(p, cs, sn, qw, kw, fp8):
    B, n, _ = p.shape
    tm = _tile(n, 512, LANES)
    bd = np.kron(np.eye(LANES // HEAD_DIM), np.full((HEAD_DIM, HEAD_DIM), 1.0 / HEAD_DIM))
    bd = jnp.asarray(bd, BF16)
    kd, kg, dt = (4 * HEAD_DIM, N_KV_HEADS, FP8) if fp8 else (KV_W, 1, BF16)
    return pl.pallas_call(
        functools.partial(_qkprep_kernel, fp8=fp8),
        grid=(n // tm, B),
        in_specs=[
            pl.BlockSpec((None, tm, ATTN_W), lambda i, b: (b, i, P_Q // ATTN_W)),
            pl.BlockSpec((None, tm, 2 * KV_W), lambda i, b: (b, i, P_KV // (2 * KV_W))),
            pl.BlockSpec((tm, LANES), lambda i, b: (i, 0)),
            pl.BlockSpec((tm, LANES), lambda i, b: (i, 0)),
            pl.BlockSpec((1, ATTN_W), lambda i, b: (0, 0)),
            pl.BlockSpec((1, KV_W), lambda i, b: (0, 0)),
            pl.BlockSpec((LANES, LANES), lambda i, b: (0, 0)),
            pl.BlockSpec((N_KV_HEADS, 2, KV_W, 4 * HEAD_DIM), lambda i, b: (0, 0, 0, 0)),
        ],
        out_specs=[
            pl.BlockSpec((None, N_Q_HEADS, kd, tm), lambda i, b: (b, 0, 0, i)),
            pl.BlockSpec((None, kg, tm, kd), lambda i, b: (b, 0, i, 0)),
            pl.BlockSpec((None, N_KV_HEADS, V_ROWS, tm), lambda i, b: (b, 0, 0, i)),
        ],
        out_shape=[
            jax.ShapeDtypeStruct((B, N_Q_HEADS, kd, n), dt),
            jax.ShapeDtypeStruct((B, kg, n, kd), dt),
            jax.ShapeDtypeStruct((B, N_KV_HEADS, V_ROWS, n), BF16),
        ],
        compiler_params=_cparams(("parallel", "parallel")),
        name="qkprep",
    )(p, p, cs, sn, qw, kw, bd, _k_placement())


def _attn_kernel(qt_ref, k_ref, vt_ref, o_ref, m_ref, acc_ref, *, tk):
    nk = vt_ref.shape[0]
    tq = m_ref.shape[-1]
    per_kv_head = k_ref.shape[0] > 1

    def keys(j, h):
        g = h // Q_PER_KV if per_kv_head else 0
        return k_ref[g, pl.ds(pl.multiple_of(j * tk, tk), tk), :]

    def lagged_chunk(j, jump):
        s_next = _dot(keys(j, 0), qt_ref[0])
        for h in range(N_Q_HEADS):
            s = s_next
            if h + 1 < N_Q_HEADS:
                s_next = _dot(keys(j, h + 1), qt_ref[h + 1])
            m_old = m_ref[h]
            p = jnp.exp2((s - m_old).astype(BF16))
            cmax = jnp.max(s, axis=0, keepdims=True)
            m_new = jnp.maximum(m_old, cmax)
            jump = jnp.maximum(jump, cmax - m_old)
            acc_ref[h] = jnp.exp2(m_old - m_new) * (acc_ref[h] + _dot(vt_ref[j, h // Q_PER_KV], p))
            m_ref[h] = m_new
        return jump

    def exact_chunk(j, carry):
        for h in range(N_Q_HEADS):
            s = _dot(keys(j, h), qt_ref[h])
            m_old = m_ref[h]
            m_new = jnp.maximum(m_old, jnp.max(s, axis=0, keepdims=True))
            p = jnp.exp2(s - m_new).astype(BF16)
            acc_ref[h] = jnp.exp2(m_old - m_new) * acc_ref[h] + _dot(vt_ref[j, h // Q_PER_KV], p)
            m_ref[h] = m_new
        return carry

    for h in range(N_Q_HEADS):
        k0 = k_ref[h // Q_PER_KV if per_kv_head else 0, 0:LAG_INIT_KEYS, :]
        m_ref[h] = jnp.max(_dot(k0, qt_ref[h]), axis=0, keepdims=True)
    acc_ref[...] = jnp.zeros(acc_ref.shape, F32)
    jump = lax.fori_loop(0, nk, lagged_chunk, jnp.zeros((1, tq), F32), unroll=ATTN_UNROLL)

    @pl.when(jnp.logical_not(jnp.max(jump) <= LAG_LIMIT))
    def _():
        m_ref[...] = jnp.full(m_ref.shape, -jnp.inf, F32)
        acc_ref[...] = jnp.zeros(acc_ref.shape, F32)
        lax.fori_loop(0, nk, exact_chunk, 0)

    for c in range(ATTN_W // LANES):
        heads = []
        for h in range(c * (LANES // HEAD_DIM), (c + 1) * (LANES // HEAD_DIM)):
            a = acc_ref[h]
            heads.append(a[:HEAD_DIM] / a[HEAD_DIM:HEAD_DIM + 1])
        o_ref[:, c * LANES:(c + 1) * LANES] = jnp.concatenate(heads, axis=0).T.astype(BF16)


def _attn_call(qt, k, vt):
    B, _, kd, n = qt.shape
    _, kg, Lk, _ = k.shape
    tq = _tile(n, ATTN_QUERIES, LANES)
    tk = _tile(Lk, ATTN_KEYS, LANES)
    nk = Lk // tk
    vt = vt.reshape(B, N_KV_HEADS, V_ROWS, nk, tk).transpose(0, 3, 1, 2, 4)
    return pl.pallas_call(
        functools.partial(_attn_kernel, tk=tk),
        grid=(B, n // tq),
        in_specs=[
            pl.BlockSpec((None, N_Q_HEADS, kd, tq), lambda b, i: (b, 0, 0, i)),
            pl.BlockSpec((None, kg, Lk, kd), lambda b, i: (b, 0, 0, 0)),
            pl.BlockSpec((None, nk, N_KV_HEADS, V_ROWS, tk), lambda b, i: (b, 0, 0, 0, 0)),
        ],
        out_specs=pl.BlockSpec((None, tq, ATTN_W), lambda b, i: (b, i, 0)),
        out_shape=jax.ShapeDtypeStruct((B, n, ATTN_W), BF16),
        scratch_shapes=[
            pltpu.VMEM((N_Q_HEADS, 1, tq), F32),
            pltpu.VMEM((N_Q_HEADS, V_ROWS, tq), F32),
        ],
        compiler_params=_cparams(("parallel", "parallel")),
        name="attn",
    )(qt, k, vt)


HALVES = HYENA_W // LANES


def _split_shape(B, n):
    assert B % 2 == 0
    return jax.ShapeDtypeStruct((B // 2, HALVES, 2, n, LANES), F32)


def _split_spec(tm):
    return pl.BlockSpec((None, HALVES, None, tm, LANES), lambda b, i: (b // 2, 0, b % 2, i, 0))


def _shift_rows(cur, prev_row, next_row, rows):
    tm = cur.shape[0]
    up = jnp.where(rows == 0, prev_row, pltpu.roll(cur, 1, 0))
    dn = jnp.where(rows == tm - 1, next_row, pltpu.roll(cur, tm - 1, 0))
    return up, dn


def _local_kernel(hy_ref, hyp_ref, hyn_ref, po_ref, pop_ref, pon_ref, gu_ref, gv_ref,
                  cw_ref, cb_ref, pw_ref, ps_ref, gnw_ref, gws_ref, gb_ref,
                  v_ref, x1_ref, x2_ref, pl_ref, gm_ref, *, n_seq):
    i = pl.program_id(1)
    last = pl.num_programs(1) - 1
    tm = hy_ref.shape[0]
    has_prev = (i > 0).astype(F32)
    has_next = (i < last).astype(F32)

    cur = hy_ref[...]
    rows = lax.broadcasted_iota(jnp.int32, cur.shape, 0)
    prev_row = hyp_ref[POOL_HALO - 1:POOL_HALO, :] * has_prev
    next_row = hyn_ref[0:1, :] * has_next
    up, dn = _shift_rows(cur, prev_row, next_row, rows)
    z = up * cw_ref[0:1, :] + cur * cw_ref[1:2, :] + dn * cw_ref[2:3, :] + cb_ref[...]
    for s, o_ref in enumerate((v_ref, x1_ref, x2_ref)):
        for hf in range(HYENA_W // LANES):
            lo = s * HYENA_W + hf * LANES
            o_ref[hf] = z[:, lo:lo + LANES]

    zc = po_ref[...]
    ext = jnp.concatenate([pop_ref[...] * has_prev, zc, pon_ref[...] * has_next], axis=0)
    te = tm + 2 * POOL_HALO
    lane = lax.broadcasted_iota(jnp.int32, zc.shape, 1)
    prow = lax.broadcasted_iota(jnp.int32, zc.shape, 0) + i * tm
    half = jnp.left_shift(1, lane // POOL_GROUP_W)
    def ahead(a, k):
        return pltpu.roll(a, (te - k) % te, 0)

    assert POOL_WINDOWS == (2, 4, 8, 16)
    sums = [ext + ahead(ext, -1)]
    for win in POOL_WINDOWS[:-1]:
        sums.append(ahead(sums[-1], win // 2) + ahead(sums[-1], -(win // 2)))
    group = lane // POOL_GROUP_W
    acc = sums[-1][POOL_HALO:POOL_HALO + tm]
    for g in range(len(POOL_WINDOWS) - 2, -1, -1):
        acc = jnp.where(group == g, sums[g][POOL_HALO:POOL_HALO + tm], acc)
    cnt = jnp.minimum(prow + half, n_seq) - jnp.maximum(prow - half, 0)
    dlt = acc / cnt.astype(F32) - zc
    pl_ref[...] = (_dot(dlt.astype(BF16), pw_ref[...]) * ps_ref[...]).astype(BF16)

    gv = gv_ref[...]
    mu = jnp.mean(gv, axis=-1, keepdims=True)
    ctr = gv - mu
    var = jnp.mean(ctr * ctr, axis=-1, keepdims=True)
    vn = (ctr * lax.rsqrt(var + RMS_EPS) * gnw_ref[...]).astype(BF16)
    group = lax.broadcasted_iota(jnp.int32, (GMLP_CHUNK, GMLP_W), 1) // GMLP_GROUP_W
    for c in range(tm // GMLP_CHUNK):
        rs = slice(c * GMLP_CHUNK, (c + 1) * GMLP_CHUNK)
        vc = vn[rs]
        mixed = gb_ref[...]
        for g in range(GMLP_GROUPS):
            mg = _dot(gws_ref[g], vc)
            mixed = mixed + jnp.where(group == g, mg, 0.0)
        gm_ref[rs, :] = (gu_ref[rs, :] * mixed).astype(BF16)


def _local_call(p, lw):
    B, n, _ = p.shape
    tm = _tile(n, 512, GMLP_CHUNK)
    nb8 = n // POOL_HALO
    r8 = tm // POOL_HALO
    HW = 3 * HYENA_W

    def cur(w, off):
        return pl.BlockSpec((None, tm, w), lambda b, i: (b, i, off // w))

    def prev(w, off):
        return pl.BlockSpec((None, POOL_HALO, w), lambda b, i: (b, jnp.maximum(i * r8 - 1, 0), off // w))

    def nxt(w, off):
        return pl.BlockSpec((None, POOL_HALO, w), lambda b, i: (b, jnp.minimum((i + 1) * r8, nb8 - 1), off // w))

    def full(shape):
        return pl.BlockSpec(shape, lambda b, i: (0,) * len(shape))

    def out(w):
        return pl.BlockSpec((None, tm, w), lambda b, i: (b, i, 0))

    return pl.pallas_call(
        functools.partial(_local_kernel, n_seq=n),
        grid=(B, n // tm),
        in_specs=[
            cur(HW, P_HY), prev(HW, P_HY), nxt(HW, P_HY),
            cur(POOL_W, P_POOL), prev(POOL_W, P_POOL), nxt(POOL_W, P_POOL),
            cur(GMLP_W, P_GMU), cur(GMLP_W, P_GMV),
            full((3, HW)), full((1, HW)), full((POOL_W, POOL_W)), full((1, POOL_W)),
            full((1, GMLP_W)), full((GMLP_GROUPS, GMLP_CHUNK, GMLP_CHUNK)), full((GMLP_CHUNK, GMLP_W)),
        ],
        out_specs=[_split_spec(tm), _split_spec(tm), _split_spec(tm), out(POOL_W), out(GMLP_W)],
        out_shape=[
            _split_shape(B, n), _split_shape(B, n), _split_shape(B, n),
            jax.ShapeDtypeStruct((B, n, POOL_W), BF16),
            jax.ShapeDtypeStruct((B, n, GMLP_W), BF16),
        ],
        compiler_params=_cparams(("parallel", "parallel")),
        name="local",
    )(p, p, p, p, p, p, p, p, lw["conv_w"], lw["conv_b"], lw["pool_bd"], lw["pool_scale"],
      lw["gm_norm_w"], lw["gm_ws"], lw["gm_bias"])


def _filt_kernel(z_ref, t_ref, w1h_ref, w1l_ref, b1_ref, w2h_ref, w2l_ref, b2_ref, fr_ref,
                 w3h_ref, w3l_ref, dec_ref, k_ref, s_ref, *, n_seq):
    i = pl.program_id(0)
    tb = z_ref.shape[1]
    fr = _lane_tile(fr_ref[...], tb)
    hdn = jnp.sin(fr * (_dot3(w1h_ref[...], w1l_ref[...], z_ref[...]) + _lane_tile(b1_ref[...], tb)))
    for u in range(HYENA_INNER):
        hdn = jnp.sin(fr * (_dot3(w2h_ref[u], w2l_ref[u], hdn) + _lane_tile(b2_ref[u], tb)))
    hdn = hdn.T
    t = t_ref[...]
    tt = jnp.concatenate([t] * (HYENA_ORDER * HYENA_W // LANES), axis=1)
    window = jnp.exp(-tt * jnp.abs(dec_ref[...]))
    k = _dot3x(hdn, w3h_ref[...], w3l_ref[...]) * window
    rows = lax.broadcasted_iota(jnp.int32, k.shape, 0) + i * tb
    k = jnp.where(rows == n_seq, 0.0, k)
    for q in range(k_ref.shape[0]):
        k_ref[q] = k[:, q * LANES:(q + 1) * LANES]

    @pl.when(i == 0)
    def _():
        s_ref[...] = jnp.zeros(s_ref.shape, F32)

    s_ref[...] += jnp.sum(jnp.abs(k), axis=0, keepdims=True)


def _filt_call(n, lw):
    N = 2 * n
    tb = _tile(n, 512)
    nb_half = n // tb
    pos = np.arange(N)
    pos = np.where(pos < n, pos, N - pos).astype(np.float64)
    t = pos / (n - 1)
    bands = np.linspace(1e-4, HYENA_BANDS - 1, HYENA_BANDS)
    ang = (2.0 * math.pi / n) * pos[:, None] * bands
    z = np.concatenate([t[:, None], np.cos(ang), np.sin(ang)], axis=-1)
    zp = np.zeros((LANES, N), np.float32)
    zp[:z.shape[1]] = z.T
    tl = np.broadcast_to(t[:, None], (N, LANES)).astype(np.float32)
    CW = HYENA_ORDER * HYENA_W

    def full(shape):
        return pl.BlockSpec(shape, lambda i: (0,) * len(shape))

    def side(rows_):
        return pl.BlockSpec((None, rows_, CW), lambda i: (i // nb_half, 0, 0))

    return pl.pallas_call(
        functools.partial(_filt_kernel, n_seq=n),
        grid=(N // tb,),
        in_specs=[
            pl.BlockSpec((LANES, tb), lambda i: (0, i)),
            pl.BlockSpec((tb, LANES), lambda i: (i, 0)),
            full((HYENA_HIDDEN, LANES)), full((HYENA_HIDDEN, LANES)), full((HYENA_HIDDEN, LANES)),
            full((HYENA_INNER, HYENA_HIDDEN, HYENA_HIDDEN)), full((HYENA_INNER, HYENA_HIDDEN, HYENA_HIDDEN)),
            full((HYENA_INNER, HYENA_HIDDEN, LANES)), full((HYENA_HIDDEN, LANES)),
            side(HYENA_HIDDEN), side(HYENA_HIDDEN), side(1),
        ],
        out_specs=[pl.BlockSpec((CW // LANES, tb, LANES), lambda i: (0, i, 0)), pl.BlockSpec((1, CW), lambda i: (0, 0))],
        out_shape=[jax.ShapeDtypeStruct((CW // LANES, N, LANES), F32), jax.ShapeDtypeStruct((1, CW), F32)],
        compiler_params=_cparams(("arbitrary",)),
        name="filt",
    )(jnp.asarray(zp), jnp.asarray(tl), lw["hy_w1h"], lw["hy_w1l"], lw["hy_b1"], lw["hy_w2h"], lw["hy_w2l"],
      lw["hy_b2"], lw["hy_freq"], lw["hy_w3h"], lw["hy_w3l"], lw["hy_decay"])


def _fft_dims(N):
    e = int(round(math.log2(N)))
    assert 2 ** e == N
    N1 = 2 ** ((e + 1) // 2)
    return N1, N // N1


def _hilo(a):
    a = np.asarray(a, np.float64)
    hi = jnp.asarray(a, F32).astype(BF16)
    lo = (jnp.asarray(a, F32) - hi.astype(F32)).astype(BF16)
    return hi, lo


@functools.lru_cache(maxsize=None)
def _fft_consts_np(N):
    N1, N2 = _fft_dims(N)
    h = N1 // 2
    k1 = np.arange(N1)
    F1 = np.exp(-2j * np.pi * np.outer(k1, k1) / N1)
    k2 = np.arange(N2)
    F2 = np.exp(-2j * np.pi * np.outer(k2, k2) / N2)
    tw = np.exp(-2j * np.pi * np.outer(k1, k2) / N)
    g1 = np.zeros((2 * N1, N1))
    g1[0::2, :h] = F1.real[:, :h]
    g1[0::2, h:] = -F1.imag[:, :h]
    g1[1::2, :h] = F1.imag[:, :h]
    g1[1::2, h:] = F1.real[:, :h]
    g1f = np.zeros((2 * N1, N1))
    g1f[0::2] = F1.real
    g1f[1::2] = F1.imag
    g2 = np.block([[F2.real, -F2.imag], [F2.imag, F2.real]])
    g2c = np.block([[F2.real, F2.imag], [-F2.imag, F2.real]])
    g3 = np.zeros((N1, 2 * N1))
    g3[:h, 0::2] = F1.real[:h]
    g3[:h, 1::2] = F1.imag[:h]
    g3[h:, 0::2] = -F1.imag[:h]
    g3[h:, 1::2] = F1.real[:h]
    g3 /= N
    twt = np.stack([tw.real, tw.imag], axis=1)
    return dict(g1=g1, g1f=g1f, g2=g2, g2c=g2c, g3=g3, tw=twt)


def _fft_consts(N):
    c = _fft_consts_np(N)
    out = {k: _hilo(c[k]) for k in ("g1", "g1f", "g2", "g2c", "g3")}
    N1, N2 = _fft_dims(N)
    tw = jnp.asarray(c["tw"], F32)
    out["tw"] = jnp.broadcast_to(tw[..., None], (N1, 2, N2, LANES))
    return out


T2_BLOCK = 8


def _load_8th(ref, j, rows):
    return ref.reshape(rows * T2_BLOCK, LANES)[pl.ds(j, rows, stride=T2_BLOCK), :]


def _store_8th(ref, j, rows, val):
    ref.reshape(rows * T2_BLOCK, LANES)[pl.ds(j, rows, stride=T2_BLOCK), :] = val


def _dft_in_kernel(gh_ref, gl_ref, x_ref, o_ref):
    N1 = x_ref.shape[1]
    for j in range(T2_BLOCK):
        xj = jnp.concatenate([_load_8th(x_ref.at[hf], j, N1) for hf in range(HALVES)], axis=1)
        a = _dot3(gh_ref[...], gl_ref[...], xj)
        for hf in range(HALVES):
            _store_8th(o_ref.at[hf], j, 2 * N1, a[:, hf * LANES:(hf + 1) * LANES])


def _dft_in_call(g, u5):
    gh, gl = g
    G, _, N1, N2, _ = u5.shape
    assert gh.shape == (2 * N1, N1)
    mat = pl.BlockSpec((2 * N1, N1), lambda b, i: (0, 0))
    return pl.pallas_call(
        _dft_in_kernel,
        grid=(G, N2 // T2_BLOCK),
        in_specs=[mat, mat, pl.BlockSpec((None, HALVES, N1, T2_BLOCK, LANES), lambda b, i: (b, 0, 0, i, 0))],
        out_specs=pl.BlockSpec((None, HALVES, N1, 2, T2_BLOCK, LANES), lambda b, i: (b, 0, 0, 0, i, 0)),
        out_shape=jax.ShapeDtypeStruct((G, HALVES, N1, 2, N2, LANES), F32),
        compiler_params=_cparams(("parallel", "parallel")),
        name="dft_in",
    )(gh, gl, u5)


def _dft_out_kernel(gh_ref, gl_ref, d_ref, gate_ref, u_ref, sk_ref, o_ref):
    N1 = gate_ref.shape[1]
    for j in range(T2_BLOCK):
        dj = jnp.concatenate([_load_8th(d_ref.at[hf], j, 2 * N1) for hf in range(HALVES)], axis=1)
        y = _dot3(gh_ref[...], gl_ref[...], dj)
        for hf in range(HALVES):
            conv = y[:, hf * LANES:(hf + 1) * LANES]
            u = _load_8th(u_ref.at[hf], j, N1)
            _store_8th(o_ref.at[hf], j, N1, _load_8th(gate_ref.at[hf], j, N1) * (conv + u * sk_ref[hf]))


def _dft_out_call(g, d6, gate5, u5, skip):
    gh, gl = g
    G, _, N1, N2, _ = u5.shape
    assert gh.shape == (N1, 2 * N1)
    mat = pl.BlockSpec((N1, 2 * N1), lambda b, i: (0, 0))
    blk = pl.BlockSpec((None, HALVES, N1, T2_BLOCK, LANES), lambda b, i: (b, 0, 0, i, 0))
    return pl.pallas_call(
        _dft_out_kernel,
        grid=(G, N2 // T2_BLOCK),
        in_specs=[mat, mat, pl.BlockSpec((None, HALVES, N1, 2, T2_BLOCK, LANES), lambda b, i: (b, 0, 0, 0, i, 0)),
                  blk, blk, pl.BlockSpec((HALVES, 1, LANES), lambda b, i: (0, 0, 0))],
        out_specs=blk,
        out_shape=jax.ShapeDtypeStruct(u5.shape, F32),
        compiler_params=_cparams(("parallel", "parallel")),
        name="dft_out",
    )(gh, gl, d6, gate5, u5, skip)


def _cmul(ar, ai, br, bi):
    return ar * br - ai * bi, ar * bi + ai * br


def _lane_tile(t, c):
    return jnp.concatenate([t] * (c // LANES), axis=-1)


def _both_halves(a_ref, k, part):
    return jnp.concatenate([a_ref[hf, k, part] for hf in range(HALVES)], axis=-1)


def _spec_kernel(a_ref, tw_ref, g2h_ref, g2l_ref, s_ref, o_ref):
    _, kb, _, N2, _ = a_ref.shape
    inv = 1.0 / s_ref[...]
    for k in range(kb):
        twr = _lane_tile(tw_ref[k, 0], HYENA_W)
        twi = _lane_tile(tw_ref[k, 1], HYENA_W)
        br, bi = _cmul(_both_halves(a_ref, k, 0), _both_halves(a_ref, k, 1), twr, twi)
        zz = _dot3(g2h_ref[...], g2l_ref[...], jnp.concatenate([br, bi], axis=0))
        o_ref[k, 0] = zz[:N2] * inv
        o_ref[k, 1] = zz[N2:] * inv


def _conv_kernel(a_ref, tw_ref, kf_ref, g2h_ref, g2l_ref, g2ch_ref, g2cl_ref, o_ref):
    _, kb, _, N2, _ = a_ref.shape
    for k in range(kb):
        twr = _lane_tile(tw_ref[k, 0], HYENA_W)
        twi = _lane_tile(tw_ref[k, 1], HYENA_W)
        br, bi = _cmul(_both_halves(a_ref, k, 0), _both_halves(a_ref, k, 1), twr, twi)
        zz = _dot3(g2h_ref[...], g2l_ref[...], jnp.concatenate([br, bi], axis=0))
        yr, yi = _cmul(zz[:N2], zz[N2:], kf_ref[k, 0], kf_ref[k, 1])
        cc = _dot3(g2ch_ref[...], g2cl_ref[...], jnp.concatenate([yr, yi], axis=0))
        dr, di = _cmul(cc[:N2], cc[N2:], twr, -twi)
        for hf in range(HALVES):
            o_ref[hf, k, 0] = dr[:, hf * LANES:(hf + 1) * LANES]
            o_ref[hf, k, 1] = di[:, hf * LANES:(hf + 1) * LANES]


def _spec_call(a6, fc, s):
    G, _, N1, _, N2, _ = a6.shape
    kb = _tile(N1, 8, 1)
    mat = pl.BlockSpec((2 * N2, 2 * N2), lambda i, b: (0, 0))
    return pl.pallas_call(
        _spec_kernel,
        grid=(N1 // kb, G),
        in_specs=[pl.BlockSpec((None, HALVES, kb, 2, N2, LANES), lambda i, b: (b, 0, i, 0, 0, 0)),
                  pl.BlockSpec((kb, 2, N2, LANES), lambda i, b: (i, 0, 0, 0)), mat, mat,
                  pl.BlockSpec((None, 1, HYENA_W), lambda i, b: (b, 0, 0))],
        out_specs=pl.BlockSpec((None, kb, 2, N2, HYENA_W), lambda i, b: (b, i, 0, 0, 0)),
        out_shape=jax.ShapeDtypeStruct((G, N1, 2, N2, HYENA_W), F32),
        compiler_params=_cparams(("parallel", "parallel")),
        name="filt_spec",
    )(a6, fc["tw"], fc["g2"][0], fc["g2"][1], s)


def _conv_call(a6, kf, order, fc):
    G, _, N1, _, N2, _ = a6.shape
    kb = _tile(N1, 8, 1)
    blk = pl.BlockSpec((None, HALVES, kb, 2, N2, LANES), lambda i, b: (b, 0, i, 0, 0, 0))
    mat = pl.BlockSpec((2 * N2, 2 * N2), lambda i, b: (0, 0))
    return pl.pallas_call(
        _conv_kernel,
        grid=(N1 // kb, G),
        in_specs=[blk, pl.BlockSpec((kb, 2, N2, LANES), lambda i, b: (i, 0, 0, 0)),
                  pl.BlockSpec((None, kb, 2, N2, HYENA_W), lambda i, b: (order, i, 0, 0, 0)), mat, mat, mat, mat],
        out_specs=blk,
        out_shape=jax.ShapeDtypeStruct(a6.shape, F32),
        compiler_params=_cparams(("parallel", "parallel")),
        name="spec_conv",
    )(a6, fc["tw"], kf, fc["g2"][0], fc["g2"][1], fc["g2c"][0], fc["g2c"][1])


def _filter_spectra(n, lw, fc):
    N1, N2 = _fft_dims(2 * n)
    k, s = _filt_call(n, lw)
    a6 = _dft_in_call(fc["g1f"], k.reshape(HYENA_ORDER, HALVES, N1, N2, LANES))
    return _spec_call(a6, fc, s.reshape(HYENA_ORDER, 1, HYENA_W))


def _hyena(v, x1, x2, kf, lw, fc):
    P, _, _, n, _ = v.shape
    N1, N2 = _fft_dims(2 * n)
    shape5 = (P, HALVES, N1, N2, LANES)
    y = v.reshape(shape5)
    for o, gate in enumerate((x1, x2)):
        d = _conv_call(_dft_in_call(fc["g1"], y), kf, o, fc)
        y = _dft_out_call(fc["g3"], d, gate.reshape(shape5), y, lw["hy_skip"][o].reshape(HALVES, 1, LANES))
    return y.reshape(v.shape)


def _merge_kernel(at_ref, hy_ref, po_ref, gm_ref, x_ref, sh_ref, sc_ref, nw_ref, gr_ref,
                  wgate_ref, wa_ref, wh_ref, wp_ref, wg_ref, wo_ref, o_ref):
    D = x_ref.shape[1]
    for rows in _row_blocks(x_ref.shape[0]):
        x = x_ref[rows, :]
        h = _modulated_norm(x, nw_ref[...], sc_ref[...], sh_ref[...]).astype(BF16)
        hy = jnp.concatenate([hy_ref[hf, rows, :] for hf in range(HALVES)], axis=-1).astype(BF16)
        branches = ((at_ref[rows, :], wa_ref), (hy, wh_ref), (po_ref[rows, :], wp_ref), (gm_ref[rows, :], wg_ref))
        merged = None
        for i, (br, w_ref) in enumerate(branches):
            term = _sigmoid(_dot(h, wgate_ref[:, i * D:(i + 1) * D])) * _dot(br, w_ref[...])
            merged = term if merged is None else merged + term
        o_ref[rows, :] = x + gr_ref[...] * _dot(merged.astype(BF16), wo_ref[...])


MERGE_ROWS = 1024


def _merge_call(attn, hy, po, gm, x, shift, scale, norm_w, gate_res, lw):
    B, n, D = x.shape
    tm = _tile(n, MERGE_ROWS)

    def rows(w):
        return pl.BlockSpec((None, tm, w), lambda b, i: (b, i, 0))

    def full(shape):
        return pl.BlockSpec(shape, lambda b, i: (0,) * len(shape))

    vec = pl.BlockSpec((None, 1, D), lambda b, i: (b, 0, 0))
    return pl.pallas_call(
        _merge_kernel,
        grid=(B, n // tm),
        in_specs=[
            rows(ATTN_W), _split_spec(tm), rows(POOL_W), rows(GMLP_W), rows(D),
            vec, vec, full((1, D)), vec,
            full((D, N_BRANCH * D)),
            full((ATTN_W, D)), full((HYENA_W, D)), full((POOL_W, D)), full((GMLP_W, D)), full((D, D)),
        ],
        out_specs=rows(D),
        out_shape=jax.ShapeDtypeStruct((B, n, D), F32),
        compiler_params=_cparams(("parallel", "parallel")),
        name="merge",
    )(attn, hy, po, gm, x, shift, scale, norm_w.reshape(1, D), gate_res, lw["w_gate"], lw["w_br_attn"],
      lw["w_br_hyena"], lw["w_br_pool"], lw["w_br_gmlp"], lw["w_out"])


FFN_ROWS = 1024


def _ffn_kernel(x_ref, sh_ref, sc_ref, nw_ref, gr_ref, wg_ref, wu_ref, wd_ref, fw_ref, o_ref, *, final):
    for rows in _row_blocks(x_ref.shape[0]):
        x = x_ref[rows, :]
        h = _modulated_norm(x, nw_ref[...], sc_ref[...], sh_ref[...]).astype(BF16)
        a = _silu(_dot(h, wg_ref[...])) * _dot(h, wu_ref[...])
        y = x + gr_ref[...] * _dot(a.astype(BF16), wd_ref[...])
        if final:
            y = y * lax.rsqrt(jnp.mean(y * y, axis=-1, keepdims=True) + RMS_EPS) * fw_ref[...]
        o_ref[rows, :] = y


def _ffn_call(x, shift, scale, norm_w, gate_res, wg, wu, wd, final_w=None):
    B, n, D = x.shape
    H = wg.shape[1]
    tm = _tile(n, FFN_ROWS)
    vec = pl.BlockSpec((None, 1, D), lambda b, i: (b, 0, 0))
    one = pl.BlockSpec((1, D), lambda b, i: (0, 0))
    final = final_w is not None
    fw = (final_w if final else norm_w).reshape(1, D)
    return pl.pallas_call(
        functools.partial(_ffn_kernel, final=final),
        grid=(B, n // tm),
        in_specs=[
            pl.BlockSpec((None, tm, D), lambda b, i: (b, i, 0)),
            vec, vec, one, vec,
            pl.BlockSpec((D, H), lambda b, i: (0, 0)),
            pl.BlockSpec((D, H), lambda b, i: (0, 0)),
            pl.BlockSpec((H, D), lambda b, i: (0, 0)),
            one,
        ],
        out_specs=pl.BlockSpec((None, tm, D), lambda b, i: (b, i, 0)),
        out_shape=jax.ShapeDtypeStruct((B, n, D), F32),
        compiler_params=_cparams(("parallel", "parallel")),
        name="ffn",
    )(x, shift, scale, norm_w.reshape(1, D), gate_res, wg, wu, wd, fw)


def _rope_tables(n):
    tok = np.arange(n)
    axis_dim = HEAD_DIM // 2
    inv_freq = ROPE_THETA ** (-np.arange(0, axis_dim, 2, dtype=np.float64) / axis_dim)
    inv_freq = inv_freq.astype(np.float32).astype(np.float64)
    ang = np.concatenate([(tok // GRID_W)[:, None] * inv_freq, (tok % GRID_W)[:, None] * inv_freq], axis=-1)
    ang = ang.astype(np.float32)
    cos = np.repeat(np.cos(ang), 2, axis=-1)
    sin = np.repeat(np.sin(ang), 2, axis=-1)
    sin[:, 0::2] *= -1.0
    reps = LANES // HEAD_DIM
    return jnp.asarray(np.tile(cos, (1, reps)), F32), jnp.asarray(np.tile(sin, (1, reps)), F32)


def _identity_tables(n):
    return jnp.ones((n, LANES), F32), jnp.zeros((n, LANES), F32)


def _layer_weights(l, a):
    w_in = a["w_in"][l]
    w_perm = jnp.concatenate([
        w_in[:, Q_OFF:K_OFF], w_in[:, K_OFF:HY_OFF], w_in[:, POOL_OFF:GM_OFF], w_in[:, GM_OFF:GATE_OFF],
        w_in[:, HY_OFF:POOL_OFF]], axis=1).astype(BF16)
    w3 = a["hy_w3"][l].reshape(HYENA_HIDDEN, HYENA_ORDER, 2, HYENA_W).transpose(2, 0, 1, 3)
    w3 = w3.reshape(2, HYENA_HIDDEN, HYENA_ORDER * HYENA_W)
    dec = a["hy_decay"][l].reshape(HYENA_ORDER, 2, HYENA_W).transpose(1, 0, 2).reshape(2, 1, HYENA_ORDER * HYENA_W)
    w1 = jnp.zeros((HYENA_HIDDEN, LANES), F32).at[:, :a["hy_w1"].shape[1]].set(a["hy_w1"][l].T)
    w1h, w1l = _split(w1)
    w2h, w2l = _split(jnp.swapaxes(a["hy_w2"][l], 1, 2))

    def lanes_of(v):
        return jnp.broadcast_to(v[..., None], v.shape + (LANES,))
    w3h, w3l = _split(w3)
    qw16 = jnp.tile(a["q_norm_w"][l], N_Q_HEADS).reshape(1, ATTN_W) * Q_SCALE
    kw16 = jnp.tile(a["k_norm_w"][l], N_KV_HEADS).reshape(1, KV_W)
    rms_q = jnp.sqrt(jnp.mean(qw16 * qw16))
    rms_k = jnp.sqrt(jnp.mean(kw16 * kw16))
    bal = jnp.exp2(jnp.clip(jnp.round(0.5 * jnp.log2(rms_k / rms_q)), -20.0, 20.0))
    bal = jnp.where(jnp.isfinite(bal), bal, 1.0)
    fp8_ok = (jnp.max(jnp.abs(qw16 * bal)) <= FP8_MAX_WEIGHT) & (jnp.max(jnp.abs(kw16 / bal)) <= FP8_MAX_WEIGHT)
    eye = jnp.eye(len(POOL_WINDOWS), dtype=F32)
    pool_bd = jnp.einsum("gh,gcd->gchd", eye, a["pool_w"][l]).reshape(POOL_W, POOL_W).astype(BF16)
    gm_bias = jnp.repeat(a["gm_bs"][l].T, GMLP_GROUP_W, axis=1)
    return dict(
        w_in=w_perm, w_gate=w_in[:, GATE_OFF:].astype(BF16),
        qw16=qw16, kw16=kw16, qw8=qw16 * bal, kw8=kw16 / bal, fp8_ok=fp8_ok,
        conv_w=a["hy_conv_w"][l], conv_b=a["hy_conv_b"][l].reshape(1, -1),
        hy_w1h=w1h, hy_w1l=w1l, hy_b1=lanes_of(a["hy_b1"][l]), hy_w2h=w2h, hy_w2l=w2l, hy_b2=lanes_of(a["hy_b2"][l]),
        hy_w3h=w3h, hy_w3l=w3l, hy_freq=lanes_of(a["hy_freq"][l]), hy_decay=dec, hy_skip=a["hy_skip"][l],
        pool_bd=pool_bd, pool_scale=a["pool_scale"][l].reshape(1, -1),
        gm_norm_w=a["gm_norm_w"][l].reshape(1, -1), gm_ws=a["gm_ws"][l].astype(BF16), gm_bias=gm_bias,
        w_br_attn=a["w_br_attn"][l].astype(BF16), w_br_hyena=a["w_br_hyena"][l].astype(BF16),
        w_br_pool=a["w_br_pool"][l].astype(BF16), w_br_gmlp=a["w_br_gmlp"][l].astype(BF16),
        w_out=a["w_out"][l].astype(BF16),
        ffn_w_gate=a["ffn_w_gate"][l].astype(BF16), ffn_w_up=a["ffn_w_up"][l].astype(BF16),
        ffn_w_down=a["ffn_w_down"][l].astype(BF16),
        norm1_w=a["norm1_w"][l], norm2_w=a["norm2_w"][l],
    )


Q_SCALE = (HEAD_DIM ** -0.5) * math.log2(math.e)


def _attend(p, p_ctx, rope, rope_ctx, lw):
    def run(fp8):
        qw, kw = (lw["qw8"], lw["kw8"]) if fp8 else (lw["qw16"], lw["kw16"])
        qt, k, vt = _qkprep_call(p, rope[0], rope[1], qw, kw, fp8)
        if p_ctx is not None:
            _, k_c, vt_c = _qkprep_call(p_ctx, rope_ctx[0], rope_ctx[1], qw, kw, fp8)
            k = jnp.concatenate([k_c, k], axis=2)
            vt = jnp.concatenate([vt_c, vt], axis=3)
        return _attn_call(qt, k, vt)

    return lax.cond(lw["fp8_ok"], lambda: run(True), lambda: run(False))


def _mixer_block(x, mods, lw, rope, p_ctx, rope_ctx, kf, fc, final_w=None):
    sh1, s1, g1, sh2, s2, g2 = mods
    p = _inproj_call(x, sh1, s1, lw["norm1_w"], lw["w_in"])
    attn = _attend(p, p_ctx, rope, rope_ctx, lw)
    hv, hx1, hx2, po, gm = _local_call(p, lw)
    hy = _hyena(hv, hx1, hx2, kf, lw, fc)
    x = _merge_call(attn, hy, po, gm, x, sh1, s1, lw["norm1_w"], g1, lw)
    x = _ffn_call(x, sh2, s2, lw["norm2_w"], g2, lw["ffn_w_gate"], lw["ffn_w_up"], lw["ffn_w_down"], final_w)
    return x, p


def kernel(x, c, ctx, c_ctx, w_mod, b_mod, norm1_w, norm2_w, w_in, q_norm_w, k_norm_w, hy_conv_w, hy_conv_b,
           hy_w1, hy_b1, hy_w2, hy_b2, hy_w3, hy_freq, hy_decay, hy_skip, pool_w, pool_scale, gm_norm_w, gm_ws,
           gm_bs, w_br_attn, w_br_hyena, w_br_pool, w_br_gmlp, w_out, ffn_w_gate, ffn_w_up, ffn_w_down,
           final_norm_w):
    a = dict(w_in=w_in, q_norm_w=q_norm_w, k_norm_w=k_norm_w, hy_conv_w=hy_conv_w, hy_conv_b=hy_conv_b,
             hy_w1=hy_w1, hy_b1=hy_b1, hy_w2=hy_w2, hy_b2=hy_b2, hy_w3=hy_w3, hy_freq=hy_freq, hy_decay=hy_decay,
             hy_skip=hy_skip, pool_w=pool_w, pool_scale=pool_scale, gm_norm_w=gm_norm_w, gm_ws=gm_ws, gm_bs=gm_bs,
             w_br_attn=w_br_attn, w_br_hyena=w_br_hyena, w_br_pool=w_br_pool, w_br_gmlp=w_br_gmlp, w_out=w_out,
             ffn_w_gate=ffn_w_gate, ffn_w_up=ffn_w_up, ffn_w_down=ffn_w_down, norm1_w=norm1_w, norm2_w=norm2_w)
    B, n, D = x.shape
    nc = ctx.shape[1]
    depth = w_mod.shape[0]

    R = ((B + 1 + 15) // 16) * 16
    act = jnp.zeros((R, D), F32).at[:B].set(c).at[B].set(c_ctx)
    mod = _mod_call(act, w_mod, b_mod)

    rope = _rope_tables(n)
    rope_ctx = _identity_tables(nc)
    fc = _fft_consts(2 * n)
    fc_ctx = _fft_consts(2 * nc)

    xc = ctx
    for l in range(depth):
        lw = _layer_weights(l, a)
        mods = [m.reshape(B, 1, D) for m in jnp.split(mod[l, :B], 6, axis=-1)]
        cmods = [jnp.broadcast_to(m.reshape(1, 1, D), (B, 1, D)) for m in jnp.split(mod[l, B], 6, axis=-1)]
        if l < depth - 1:
            kf_ctx = _filter_spectra(nc, lw, fc_ctx)
            xc, pc = _mixer_block(xc, cmods, lw, rope_ctx, None, None, kf_ctx, fc_ctx)
        else:
            pc = _inproj_call(xc, cmods[0], cmods[1], lw["norm1_w"], lw["w_in"][:, :P_QKV])
        kf = _filter_spectra(n, lw, fc)
        x, _ = _mixer_block(x, mods, lw, rope, pc, rope_ctx, kf, fc, final_norm_w if l == depth - 1 else None)
    return x
```

```python
import functools
import math

import numpy as np
import jax
import jax.numpy as jnp
from jax import lax
from jax.experimental import pallas as pl
from jax.experimental.pallas import tpu as pltpu

F32 = jnp.float32
BF16 = jnp.bfloat16

D_MODEL = 1024
GRID_W = 64
RMS_EPS = 1e-6
HEAD_DIM = 64
N_Q_HEADS = 8
N_KV_HEADS = 2
ATTN_W = N_Q_HEADS * HEAD_DIM
KV_W = N_KV_HEADS * HEAD_DIM
ROPE_THETA = 10000.0
HYENA_W = 256
HYENA_ORDER = 2
HYENA_BANDS = 16
HYENA_HIDDEN = 64
HYENA_INNER = 2
POOL_W = 256
POOL_WINDOWS = (2, 4, 8, 16)
POOL_GROUP_W = POOL_W // len(POOL_WINDOWS)
POOL_HALO = 8
GMLP_W = 256
GMLP_CHUNK = 128
GMLP_GROUPS = 4
GMLP_GROUP_W = GMLP_W // GMLP_GROUPS
N_BRANCH = 4
FFN_HIDDEN = ((8 * D_MODEL // 3 + 255) // 256) * 256

Q_OFF = 0
K_OFF = Q_OFF + ATTN_W
V_OFF = K_OFF + KV_W
HY_OFF = V_OFF + KV_W
POOL_OFF = HY_OFF + 3 * HYENA_W
GM_OFF = POOL_OFF + POOL_W
GATE_OFF = GM_OFF + 2 * GMLP_W
IN_W = GATE_OFF + N_BRANCH * D_MODEL

P_Q = 0
P_KV = P_Q + ATTN_W
P_POOL = P_KV + 2 * KV_W
P_GMU = P_POOL + POOL_W
P_GMV = P_GMU + GMLP_W
P_HY = P_GMV + GMLP_W
P_W = P_HY + 3 * HYENA_W
P_QKV = P_POOL

LANES = 128
VMEM_LIMIT = 52 * 1024 * 1024


def _cparams(sem):
    return pltpu.CompilerParams(dimension_semantics=sem, vmem_limit_bytes=VMEM_LIMIT)


def _tile(n, cap, mult=8):
    best = None
    for t in range(mult, min(n, cap) + 1, mult):
        if n % t == 0:
            best = t
    assert best is not None, (n, cap, mult)
    return best


def _split(x):
    hi = x.astype(BF16)
    lo = (x - hi.astype(F32)).astype(BF16)
    return hi, lo


def _dot(a, b):
    return jnp.dot(a, b, preferred_element_type=F32)


def _dot3(a_hi, a_lo, x):
    x_hi, x_lo = _split(x)
    return _dot(a_hi, x_hi) + (_dot(a_lo, x_hi) + _dot(a_hi, x_lo))


def _dot3x(x, b_hi, b_lo):
    x_hi, x_lo = _split(x)
    return _dot(x_hi, b_hi) + (_dot(x_lo, b_hi) + _dot(x_hi, b_lo))


def _sigmoid(x):
    return 1.0 / (1.0 + jnp.exp(-x))


def _silu(x):
    return x * _sigmoid(x)


def _modulated_norm(x, nw, scale, shift):
    ms = jnp.mean(x * x, axis=-1, keepdims=True)
    y = x * lax.rsqrt(ms + RMS_EPS) * nw
    return y * (1.0 + scale) + shift


def _mod_kernel(a_ref, w_ref, b_ref, o_ref):
    a = _silu(a_ref[...]).astype(BF16)
    o_ref[...] = _dot(a, w_ref[...].astype(BF16)) + b_ref[...]


def _mod_call(act, w_mod, b_mod):
    L, D, W6 = w_mod.shape
    R = act.shape[0]
    tn = 1024
    return pl.pallas_call(
        _mod_kernel,
        grid=(L, W6 // tn),
        in_specs=[
            pl.BlockSpec((R, D), lambda l, j: (0, 0)),
            pl.BlockSpec((None, D, tn), lambda l, j: (l, 0, j)),
            pl.BlockSpec((None, 1, tn), lambda l, j: (l, 0, j)),
        ],
        out_specs=pl.BlockSpec((None, R, tn), lambda l, j: (l, 0, j)),
        out_shape=jax.ShapeDtypeStruct((L, R, W6), F32),
        compiler_params=_cparams(("arbitrary", "arbitrary")),
        name="mod",
    )(act, w_mod, b_mod.reshape(L, 1, W6))


ROW_BLOCK = 256
INPROJ_ROWS = 1024


def _row_blocks(tm):
    sub = ROW_BLOCK if tm % ROW_BLOCK == 0 else tm
    return [slice(r, r + sub) for r in range(0, tm, sub)]


def _inproj_kernel(x_ref, sh_ref, sc_ref, nw_ref, w_ref, o_ref):
    for rows in _row_blocks(x_ref.shape[0]):
        h = _modulated_norm(x_ref[rows, :], nw_ref[...], sc_ref[...], sh_ref[...]).astype(BF16)
        o_ref[rows, :] = _dot(h, w_ref[...])


def _inproj_call(x, shift, scale, norm_w, w):
    B, n, D = x.shape
    Nw = w.shape[1]
    tm = _tile(n, INPROJ_ROWS)
    return pl.pallas_call(
        _inproj_kernel,
        grid=(B, n // tm),
        in_specs=[
            pl.BlockSpec((None, tm, D), lambda b, i: (b, i, 0)),
            pl.BlockSpec((None, 1, D), lambda b, i: (b, 0, 0)),
            pl.BlockSpec((None, 1, D), lambda b, i: (b, 0, 0)),
            pl.BlockSpec((1, D), lambda b, i: (0, 0)),
            pl.BlockSpec((D, Nw), lambda b, i: (0, 0)),
        ],
        out_specs=pl.BlockSpec((None, tm, Nw), lambda b, i: (b, i, 0)),
        out_shape=jax.ShapeDtypeStruct((B, n, Nw), F32),
        compiler_params=_cparams(("parallel", "parallel")),
        name="inproj",
    )(x, shift, scale, norm_w.reshape(1, D), w)


def _head_norm_rope(t, w, cs, sn, bd_ref):
    sq_hi, sq_lo = _split(t * t)
    ms = _dot(sq_hi, bd_ref[...]) + _dot(sq_lo, bd_ref[...])
    y = t * lax.rsqrt(ms + RMS_EPS) * w
    lane = lax.broadcasted_iota(jnp.int32, y.shape, 1)
    swapped = jnp.where(lane % 2 == 0, pltpu.roll(y, LANES - 1, 1), pltpu.roll(y, 1, 1))
    return y * cs + swapped * sn


V_ROWS = 80
Q_PER_KV = N_Q_HEADS // N_KV_HEADS
LAG_INIT_KEYS = 128
LAG_LIMIT = 24.0
ATTN_QUERIES = 512
ATTN_KEYS = 768
ATTN_UNROLL = 2


def _value_rows(vt_g):
    tail = lax.broadcasted_iota(jnp.int32, (V_ROWS - HEAD_DIM, vt_g.shape[1]), 0)
    return jnp.concatenate([vt_g, jnp.where(tail == 0, 1.0, 0.0)], axis=0).astype(BF16)


FP8 = jnp.float8_e4m3fn
FP8_MAX_WEIGHT = 32.0


def _split8(x):
    hi = x.astype(FP8).astype(F32)
    return hi, (x - hi).astype(FP8).astype(F32)


def _qkprep_kernel(q_ref, kv_ref, cs_ref, sn_ref, qw_ref, kw_ref, bd_ref, kp_ref, qo_ref, ko_ref, vo_ref, *, fp8):
    cs = cs_ref[...]
    sn = sn_ref[...]
    tm = q_ref.shape[0]
    zeros = jnp.zeros((HEAD_DIM, tm), F32)
    heads_per_chunk = LANES // HEAD_DIM
    for c in range(ATTN_W // LANES):
        sl = slice(c * LANES, (c + 1) * LANES)
        r = _head_norm_rope(q_ref[:, sl], qw_ref[:, sl], cs, sn, bd_ref)
        if fp8:
            hi_t, lo_t = _split8(r.T)
            for half in range(heads_per_chunk):
                rows = slice(half * HEAD_DIM, (half + 1) * HEAD_DIM)
                qo_ref[c * heads_per_chunk + half] = jnp.concatenate(
                    [hi_t[rows], lo_t[rows], hi_t[rows], lo_t[rows]], axis=0).astype(FP8)
        else:
            rt = r.T
            for half in range(heads_per_chunk):
                h = c * heads_per_chunk + half
                blk = rt[half * HEAD_DIM:(half + 1) * HEAD_DIM]
                parts = [blk if g == h // Q_PER_KV else zeros for g in range(N_KV_HEADS)]
                qo_ref[h] = jnp.concatenate(parts, axis=0).astype(BF16)
    k = _head_norm_rope(kv_ref[:, :KV_W], kw_ref[...], cs, sn, bd_ref)
    if fp8:
        hi, lo = _split8(k)
        hi, lo = hi.astype(BF16), lo.astype(BF16)
        for g in range(N_KV_HEADS):
            ko_ref[g] = (_dot(hi, kp_ref[g, 0]) + _dot(lo, kp_ref[g, 1])).astype(FP8)
    else:
        ko_ref[0] = k.astype(BF16)
    vt = kv_ref[:, KV_W:].T
    for g in range(N_KV_HEADS):
        vo_ref[g] = _value_rows(vt[g * HEAD_DIM:(g + 1) * HEAD_DIM])


def _k_placement():
    pm = np.zeros((N_KV_HEADS, 2, KV_W, 4 * HEAD_DIM), np.float32)
    d = np.arange(HEAD_DIM)
    for g in range(N_KV_HEADS):
        for part in range(2):
            for rep in range(2):
                pm[g, part, g * HEAD_DIM + d, (2 * part + rep) * HEAD_DIM + d] = 1.0
    return jnp.asarray(pm, BF16)


def _qkprep_call(p, cs, sn, qw, kw, fp8):
    B, n, _ = p.shape
    tm = _tile(n, 512, LANES)
    bd = np.kron(np.eye(LANES // HEAD_DIM), np.full((HEAD_DIM, HEAD_DIM), 1.0 / HEAD_DIM))
    bd = jnp.asarray(bd, BF16)
    kd, kg, dt = (4 * HEAD_DIM, N_KV_HEADS, FP8) if fp8 else (KV_W, 1, BF16)
    return pl.pallas_call(
        functools.partial(_qkprep_kernel, fp8=fp8),
        grid=(n // tm, B),
        in_specs=[
            pl.BlockSpec((None, tm, ATTN_W), lambda i, b: (b, i, P_Q // ATTN_W)),
            pl.BlockSpec((None, tm, 2 * KV_W), lambda i, b: (b, i, P_KV // (2 * KV_W))),
            pl.BlockSpec((tm, LANES), lambda i, b: (i, 0)),
            pl.BlockSpec((tm, LANES), lambda i, b: (i, 0)),
            pl.BlockSpec((1, ATTN_W), lambda i, b: (0, 0)),
            pl.BlockSpec((1, KV_W), lambda i, b: (0, 0)),
            pl.BlockSpec((LANES, LANES), lambda i, b: (0, 0)),
            pl.BlockSpec((N_KV_HEADS, 2, KV_W, 4 * HEAD_DIM), lambda i, b: (0, 0, 0, 0)),
        ],
        out_specs=[
            pl.BlockSpec((None, N_Q_HEADS, kd, tm), lambda i, b: (b, 0, 0, i)),
            pl.BlockSpec((None, kg, tm, kd), lambda i, b: (b, 0, i, 0)),
            pl.BlockSpec((None, N_KV_HEADS, V_ROWS, tm), lambda i, b: (b, 0, 0, i)),
        ],
        out_shape=[
            jax.ShapeDtypeStruct((B, N_Q_HEADS, kd, n), dt),
            jax.ShapeDtypeStruct((B, kg, n, kd), dt),
            jax.ShapeDtypeStruct((B, N_KV_HEADS, V_ROWS, n), BF16),
        ],
        compiler_params=_cparams(("parallel", "parallel")),
        name="qkprep",
    )(p, p, cs, sn, qw, kw, bd, _k_placement())


def _attn_kernel(qt_ref, k_ref, vt_ref, o_ref, m_ref, acc_ref, *, tk):
    nk = vt_ref.shape[0]
    tq = m_ref.shape[-1]
    per_kv_head = k_ref.shape[0] > 1

    def keys(j, h):
        g = h // Q_PER_KV if per_kv_head else 0
        return k_ref[g, pl.ds(pl.multiple_of(j * tk, tk), tk), :]

    def lagged_chunk(j, jump):
        s_next = _dot(keys(j, 0), qt_ref[0])
        for h in range(N_Q_HEADS):
            s = s_next
            if h + 1 < N_Q_HEADS:
                s_next = _dot(keys(j, h + 1), qt_ref[h + 1])
            m_old = m_ref[h]
            p = jnp.exp2((s - m_old).astype(BF16))
            cmax = jnp.max(s, axis=0, keepdims=True)
            m_new = jnp.maximum(m_old, cmax)
            jump = jnp.maximum(jump, cmax - m_old)
            acc_ref[h] = jnp.exp2(m_old - m_new) * (acc_ref[h] + _dot(vt_ref[j, h // Q_PER_KV], p))
            m_ref[h] = m_new
        return jump

    def exact_chunk(j, carry):
        for h in range(N_Q_HEADS):
            s = _dot(keys(j, h), qt_ref[h])
            m_old = m_ref[h]
            m_new = jnp.maximum(m_old, jnp.max(s, axis=0, keepdims=True))
            p = jnp.exp2(s - m_new).astype(BF16)
            acc_ref[h] = jnp.exp2(m_old - m_new) * acc_ref[h] + _dot(vt_ref[j, h // Q_PER_KV], p)
            m_ref[h] = m_new
        return carry

    for h in range(N_Q_HEADS):
        k0 = k_ref[h // Q_PER_KV if per_kv_head else 0, 0:LAG_INIT_KEYS, :]
        m_ref[h] = jnp.max(_dot(k0, qt_ref[h]), axis=0, keepdims=True)
    acc_ref[...] = jnp.zeros(acc_ref.shape, F32)
    jump = lax.fori_loop(0, nk, lagged_chunk, jnp.zeros((1, tq), F32), unroll=ATTN_UNROLL)

    @pl.when(jnp.logical_not(jnp.max(jump) <= LAG_LIMIT))
    def _():
        m_ref[...] = jnp.full(m_ref.shape, -jnp.inf, F32)
        acc_ref[...] = jnp.zeros(acc_ref.shape, F32)
        lax.fori_loop(0, nk, exact_chunk, 0)

    for c in range(ATTN_W // LANES):
        heads = []
        for h in range(c * (LANES // HEAD_DIM), (c + 1) * (LANES // HEAD_DIM)):
            a = acc_ref[h]
            heads.append(a[:HEAD_DIM] / a[HEAD_DIM:HEAD_DIM + 1])
        o_ref[:, c * LANES:(c + 1) * LANES] = jnp.concatenate(heads, axis=0).T.astype(BF16)


def _attn_call(qt, k, vt):
    B, _, kd, n = qt.shape
    _, kg, Lk, _ = k.shape
    tq = _tile(n, ATTN_QUERIES, LANES)
    tk = _tile(Lk, ATTN_KEYS, LANES)
    nk = Lk // tk
    vt = vt.reshape(B, N_KV_HEADS, V_ROWS, nk, tk).transpose(0, 3, 1, 2, 4)
    return pl.pallas_call(
        functools.partial(_attn_kernel, tk=tk),
        grid=(B, n // tq),
        in_specs=[
            pl.BlockSpec((None, N_Q_HEADS, kd, tq), lambda b, i: (b, 0, 0, i)),
            pl.BlockSpec((None, kg, Lk, kd), lambda b, i: (b, 0, 0, 0)),
            pl.BlockSpec((None, nk, N_KV_HEADS, V_ROWS, tk), lambda b, i: (b, 0, 0, 0, 0)),
        ],
        out_specs=pl.BlockSpec((None, tq, ATTN_W), lambda b, i: (b, i, 0)),
        out_shape=jax.ShapeDtypeStruct((B, n, ATTN_W), BF16),
        scratch_shapes=[
            pltpu.VMEM((N_Q_HEADS, 1, tq), F32),
            pltpu.VMEM((N_Q_HEADS, V_ROWS, tq), F32),
        ],
        compiler_params=_cparams(("parallel", "parallel")),
        name="attn",
    )(qt, k, vt)


HALVES = HYENA_W // LANES


def _split_shape(B, n):
    assert B % 2 == 0
    return jax.ShapeDtypeStruct((B // 2, HALVES, 2, n, LANES), F32)


def _split_spec(tm):
    return pl.BlockSpec((None, HALVES, None, tm, LANES), lambda b, i: (b // 2, 0, b % 2, i, 0))


def _shift_rows(cur, prev_row, next_row, rows):
    tm = cur.shape[0]
    up = jnp.where(rows == 0, prev_row, pltpu.roll(cur, 1, 0))
    dn = jnp.where(rows == tm - 1, next_row, pltpu.roll(cur, tm - 1, 0))
    return up, dn


def _local_kernel(hy_ref, hyp_ref, hyn_ref, po_ref, pop_ref, pon_ref, gu_ref, gv_ref,
                  cw_ref, cb_ref, pw_ref, ps_ref, gnw_ref, gws_ref, gb_ref,
                  v_ref, x1_ref, x2_ref, pl_ref, gm_ref, *, n_seq):
    i = pl.program_id(1)
    last = pl.num_programs(1) - 1
    tm = hy_ref.shape[0]
    has_prev = (i > 0).astype(F32)
    has_next = (i < last).astype(F32)

    cur = hy_ref[...]
    rows = lax.broadcasted_iota(jnp.int32, cur.shape, 0)
    prev_row = hyp_ref[POOL_HALO - 1:POOL_HALO, :] * has_prev
    next_row = hyn_ref[0:1, :] * has_next
    up, dn = _shift_rows(cur, prev_row, next_row, rows)
    z = up * cw_ref[0:1, :] + cur * cw_ref[1:2, :] + dn * cw_ref[2:3, :] + cb_ref[...]
    for s, o_ref in enumerate((v_ref, x1_ref, x2_ref)):
        for hf in range(HYENA_W // LANES):
            lo = s * HYENA_W + hf * LANES
            o_ref[hf] = z[:, lo:lo + LANES]

    zc = po_ref[...]
    ext = jnp.concatenate([pop_ref[...] * has_prev, zc, pon_ref[...] * has_next], axis=0)
    te = tm + 2 * POOL_HALO
    lane = lax.broadcasted_iota(jnp.int32, zc.shape, 1)
    prow = lax.broadcasted_iota(jnp.int32, zc.shape, 0) + i * tm
    half = jnp.left_shift(1, lane // POOL_GROUP_W)
    def ahead(a, k):
        return pltpu.roll(a, (te - k) % te, 0)

    assert POOL_WINDOWS == (2, 4, 8, 16)
    sums = [ext + ahead(ext, -1)]
    for win in POOL_WINDOWS[:-1]:
        sums.append(ahead(sums[-1], win // 2) + ahead(sums[-1], -(win // 2)))
    group = lane // POOL_GROUP_W
    acc = sums[-1][POOL_HALO:POOL_HALO + tm]
    for g in range(len(POOL_WINDOWS) - 2, -1, -1):
        acc = jnp.where(group == g, sums[g][POOL_HALO:POOL_HALO + tm], acc)
    cnt = jnp.minimum(prow + half, n_seq) - jnp.maximum(prow - half, 0)
    dlt = acc / cnt.astype(F32) - zc
    pl_ref[...] = (_dot(dlt.astype(BF16), pw_ref[...]) * ps_ref[...]).astype(BF16)

    gv = gv_ref[...]
    mu = jnp.mean(gv, axis=-1, keepdims=True)
    ctr = gv - mu
    var = jnp.mean(ctr * ctr, axis=-1, keepdims=True)
    vn = (ctr * lax.rsqrt(var + RMS_EPS) * gnw_ref[...]).astype(BF16)
    group = lax.broadcasted_iota(jnp.int32, (GMLP_CHUNK, GMLP_W), 1) // GMLP_GROUP_W
    for c in range(tm // GMLP_CHUNK):
        rs = slice(c * GMLP_CHUNK, (c + 1) * GMLP_CHUNK)
        vc = vn[rs]
        mixed = gb_ref[...]
        for g in range(GMLP_GROUPS):
            mg = _dot(gws_ref[g], vc)
            mixed = mixed + jnp.where(group == g, mg, 0.0)
        gm_ref[rs, :] = (gu_ref[rs, :] * mixed).astype(BF16)


def _local_call(p, lw):
    B, n, _ = p.shape
    tm = _tile(n, 512, GMLP_CHUNK)
    nb8 = n // POOL_HALO
    r8 = tm // POOL_HALO
    HW = 3 * HYENA_W

    def cur(w, off):
        return pl.BlockSpec((None, tm, w), lambda b, i: (b, i, off // w))

    def prev(w, off):
        return pl.BlockSpec((None, POOL_HALO, w), lambda b, i: (b, jnp.maximum(i * r8 - 1, 0), off // w))

    def nxt(w, off):
        return pl.BlockSpec((None, POOL_HALO, w), lambda b, i: (b, jnp.minimum((i + 1) * r8, nb8 - 1), off // w))

    def full(shape):
        return pl.BlockSpec(shape, lambda b, i: (0,) * len(shape))

    def out(w):
        return pl.BlockSpec((None, tm, w), lambda b, i: (b, i, 0))

    return pl.pallas_call(
        functools.partial(_local_kernel, n_seq=n),
        grid=(B, n // tm),
        in_specs=[
            cur(HW, P_HY), prev(HW, P_HY), nxt(HW, P_HY),
            cur(POOL_W, P_POOL), prev(POOL_W, P_POOL), nxt(POOL_W, P_POOL),
            cur(GMLP_W, P_GMU), cur(GMLP_W, P_GMV),
            full((3, HW)), full((1, HW)), full((POOL_W, POOL_W)), full((1, POOL_W)),
            full((1, GMLP_W)), full((GMLP_GROUPS, GMLP_CHUNK, GMLP_CHUNK)), full((GMLP_CHUNK, GMLP_W)),
        ],
        out_specs=[_split_spec(tm), _split_spec(tm), _split_spec(tm), out(POOL_W), out(GMLP_W)],
        out_shape=[
            _split_shape(B, n), _split_shape(B, n), _split_shape(B, n),
            jax.ShapeDtypeStruct((B, n, POOL_W), BF16),
            jax.ShapeDtypeStruct((B, n, GMLP_W), BF16),
        ],
        compiler_params=_cparams(("parallel", "parallel")),
        name="local",
    )(p, p, p, p, p, p, p, p, lw["conv_w"], lw["conv_b"], lw["pool_bd"], lw["pool_scale"],
      lw["gm_norm_w"], lw["gm_ws"], lw["gm_bias"])


def _filt_kernel(z_ref, t_ref, w1h_ref, w1l_ref, b1_ref, w2h_ref, w2l_ref, b2_ref, fr_ref,
                 w3h_ref, w3l_ref, dec_ref, k_ref, s_ref, *, n_seq):
    i = pl.program_id(0)
    tb = z_ref.shape[1]
    fr = _lane_tile(fr_ref[...], tb)
    hdn = jnp.sin(fr * (_dot3(w1h_ref[...], w1l_ref[...], z_ref[...]) + _lane_tile(b1_ref[...], tb)))
    for u in range(HYENA_INNER):
        hdn = jnp.sin(fr * (_dot3(w2h_ref[u], w2l_ref[u], hdn) + _lane_tile(b2_ref[u], tb)))
    hdn = hdn.T
    t = t_ref[...]
    tt = jnp.concatenate([t] * (HYENA_ORDER * HYENA_W // LANES), axis=1)
    window = jnp.exp(-tt * jnp.abs(dec_ref[...]))
    k = _dot3x(hdn, w3h_ref[...], w3l_ref[...]) * window
    rows = lax.broadcasted_iota(jnp.int32, k.shape, 0) + i * tb
    k = jnp.where(rows == n_seq, 0.0, k)
    for q in range(k_ref.shape[0]):
        k_ref[q] = k[:, q * LANES:(q + 1) * LANES]

    @pl.when(i == 0)
    def _():
        s_ref[...] = jnp.zeros(s_ref.shape, F32)

    s_ref[...] += jnp.sum(jnp.abs(k), axis=0, keepdims=True)


def _filt_call(n, lw):
    N = 2 * n
    tb = _tile(n, 512)
    nb_half = n // tb
    pos = np.arange(N)
    pos = np.where(pos < n, pos, N - pos).astype(np.float64)
    t = pos / (n - 1)
    bands = np.linspace(1e-4, HYENA_BANDS - 1, HYENA_BANDS)
    ang = (2.0 * math.pi / n) * pos[:, None] * bands
    z = np.concatenate([t[:, None], np.cos(ang), np.sin(ang)], axis=-1)
    zp = np.zeros((LANES, N), np.float32)
    zp[:z.shape[1]] = z.T
    tl = np.broadcast_to(t[:, None], (N, LANES)).astype(np.float32)
    CW = HYENA_ORDER * HYENA_W

    def full(shape):
        return pl.BlockSpec(shape, lambda i: (0,) * len(shape))

    def side(rows_):
        return pl.BlockSpec((None, rows_, CW), lambda i: (i // nb_half, 0, 0))

    return pl.pallas_call(
        functools.partial(_filt_kernel, n_seq=n),
        grid=(N // tb,),
        in_specs=[
            pl.BlockSpec((LANES, tb), lambda i: (0, i)),
            pl.BlockSpec((tb, LANES), lambda i: (i, 0)),
            full((HYENA_HIDDEN, LANES)), full((HYENA_HIDDEN, LANES)), full((HYENA_HIDDEN, LANES)),
            full((HYENA_INNER, HYENA_HIDDEN, HYENA_HIDDEN)), full((HYENA_INNER, HYENA_HIDDEN, HYENA_HIDDEN)),
            full((HYENA_INNER, HYENA_HIDDEN, LANES)), full((HYENA_HIDDEN, LANES)),
            side(HYENA_HIDDEN), side(HYENA_HIDDEN), side(1),
        ],
        out_specs=[pl.BlockSpec((CW // LANES, tb, LANES), lambda i: (0, i, 0)), pl.BlockSpec((1, CW), lambda i: (0, 0))],
        out_shape=[jax.ShapeDtypeStruct((CW // LANES, N, LANES), F32), jax.ShapeDtypeStruct((1, CW), F32)],
        compiler_params=_cparams(("arbitrary",)),
        name="filt",
    )(jnp.asarray(zp), jnp.asarray(tl), lw["hy_w1h"], lw["hy_w1l"], lw["hy_b1"], lw["hy_w2h"], lw["hy_w2l"],
      lw["hy_b2"], lw["hy_freq"], lw["hy_w3h"], lw["hy_w3l"], lw["hy_decay"])


def _fft_dims(N):
    e = int(round(math.log2(N)))
    assert 2 ** e == N
    N1 = 2 ** ((e + 1) // 2)
    return N1, N // N1


def _hilo(a):
    a = np.asarray(a, np.float64)
    hi = jnp.asarray(a, F32).astype(BF16)
    lo = (jnp.asarray(a, F32) - hi.astype(F32)).astype(BF16)
    return hi, lo


@functools.lru_cache(maxsize=None)
def _fft_consts_np(N):
    N1, N2 = _fft_dims(N)
    h = N1 // 2
    k1 = np.arange(N1)
    F1 = np.exp(-2j * np.pi * np.outer(k1, k1) / N1)
    k2 = np.arange(N2)
    F2 = np.exp(-2j * np.pi * np.outer(k2, k2) / N2)
    tw = np.exp(-2j * np.pi * np.outer(k1, k2) / N)
    g1 = np.zeros((2 * N1, N1))
    g1[0::2, :h] = F1.real[:, :h]
    g1[0::2, h:] = -F1.imag[:, :h]
    g1[1::2, :h] = F1.imag[:, :h]
    g1[1::2, h:] = F1.real[:, :h]
    g1f = np.zeros((2 * N1, N1))
    g1f[0::2] = F1.real
    g1f[1::2] = F1.imag
    g2 = np.block([[F2.real, -F2.imag], [F2.imag, F2.real]])
    g2c = np.block([[F2.real, F2.imag], [-F2.imag, F2.real]])
    g3 = np.zeros((N1, 2 * N1))
    g3[:h, 0::2] = F1.real[:h]
    g3[:h, 1::2] = F1.imag[:h]
    g3[h:, 0::2] = -F1.imag[:h]
    g3[h:, 1::2] = F1.real[:h]
    g3 /= N
    twt = np.stack([tw.real, tw.imag], axis=1)
    return dict(g1=g1, g1f=g1f, g2=g2, g2c=g2c, g3=g3, tw=twt)


def _fft_consts(N):
    c = _fft_consts_np(N)
    out = {k: _hilo(c[k]) for k in ("g1", "g1f", "g2", "g2c", "g3")}
    N1, N2 = _fft_dims(N)
    tw = jnp.asarray(c["tw"], F32)
    out["tw"] = jnp.broadcast_to(tw[..., None], (N1, 2, N2, LANES))
    return out


T2_BLOCK = 16


def _load_8th(ref, j, rows):
    return ref.reshape(rows * T2_BLOCK, LANES)[pl.ds(j, rows, stride=T2_BLOCK), :]


def _store_8th(ref, j, rows, val):
    ref.reshape(rows * T2_BLOCK, LANES)[pl.ds(j, rows, stride=T2_BLOCK), :] = val


def _dft_in_kernel(gh_ref, gl_ref, x_ref, o_ref):
    N1 = x_ref.shape[1]
    for j in range(T2_BLOCK):
        xj = jnp.concatenate([_load_8th(x_ref.at[hf], j, N1) for hf in range(HALVES)], axis=1)
        a = _dot3(gh_ref[...], gl_ref[...], xj)
        for hf in range(HALVES):
            _store_8th(o_ref.at[hf], j, 2 * N1, a[:, hf * LANES:(hf + 1) * LANES])


def _dft_in_call(g, u5):
    gh, gl = g
    G, _, N1, N2, _ = u5.shape
    assert gh.shape == (2 * N1, N1)
    mat = pl.BlockSpec((2 * N1, N1), lambda b, i: (0, 0))
    return pl.pallas_call(
        _dft_in_kernel,
        grid=(G, N2 // T2_BLOCK),
        in_specs=[mat, mat, pl.BlockSpec((None, HALVES, N1, T2_BLOCK, LANES), lambda b, i: (b, 0, 0, i, 0))],
        out_specs=pl.BlockSpec((None, HALVES, N1, 2, T2_BLOCK, LANES), lambda b, i: (b, 0, 0, 0, i, 0)),
        out_shape=jax.ShapeDtypeStruct((G, HALVES, N1, 2, N2, LANES), F32),
        compiler_params=_cparams(("parallel", "parallel")),
        name="dft_in",
    )(gh, gl, u5)


def _dft_out_kernel(gh_ref, gl_ref, d_ref, gate_ref, u_ref, sk_ref, o_ref):
    N1 = gate_ref.shape[1]
    for j in range(T2_BLOCK):
        dj = jnp.concatenate([_load_8th(d_ref.at[hf], j, 2 * N1) for hf in range(HALVES)], axis=1)
        y = _dot3(gh_ref[...], gl_ref[...], dj)
        for hf in range(HALVES):
            conv = y[:, hf * LANES:(hf + 1) * LANES]
            u = _load_8th(u_ref.at[hf], j, N1)
            _store_8th(o_ref.at[hf], j, N1, _load_8th(gate_ref.at[hf], j, N1) * (conv + u * sk_ref[hf]))


def _dft_out_call(g, d6, gate5, u5, skip):
    gh, gl = g
    G, _, N1, N2, _ = u5.shape
    assert gh.shape == (N1, 2 * N1)
    mat = pl.BlockSpec((N1, 2 * N1), lambda b, i: (0, 0))
    blk = pl.BlockSpec((None, HALVES, N1, T2_BLOCK, LANES), lambda b, i: (b, 0, 0, i, 0))
    return pl.pallas_call(
        _dft_out_kernel,
        grid=(G, N2 // T2_BLOCK),
        in_specs=[mat, mat, pl.BlockSpec((None, HALVES, N1, 2, T2_BLOCK, LANES), lambda b, i: (b, 0, 0, 0, i, 0)),
                  blk, blk, pl.BlockSpec((HALVES, 1, LANES), lambda b, i: (0, 0, 0))],
        out_specs=blk,
        out_shape=jax.ShapeDtypeStruct(u5.shape, F32),
        compiler_params=_cparams(("parallel", "parallel")),
        name="dft_out",
    )(gh, gl, d6, gate5, u5, skip)


def _cmul(ar, ai, br, bi):
    return ar * br - ai * bi, ar * bi + ai * br


def _lane_tile(t, c):
    return jnp.concatenate([t] * (c // LANES), axis=-1)


def _both_halves(a_ref, k, part):
    return jnp.concatenate([a_ref[hf, k, part] for hf in range(HALVES)], axis=-1)


def _spec_kernel(a_ref, tw_ref, g2h_ref, g2l_ref, s_ref, o_ref):
    _, kb, _, N2, _ = a_ref.shape
    inv = 1.0 / s_ref[...]
    for k in range(kb):
        twr = _lane_tile(tw_ref[k, 0], HYENA_W)
        twi = _lane_tile(tw_ref[k, 1], HYENA_W)
        br, bi = _cmul(_both_halves(a_ref, k, 0), _both_halves(a_ref, k, 1), twr, twi)
        zz = _dot3(g2h_ref[...], g2l_ref[...], jnp.concatenate([br, bi], axis=0))
        o_ref[k, 0] = zz[:N2] * inv
        o_ref[k, 1] = zz[N2:] * inv


def _conv_kernel(a_ref, tw_ref, kf_ref, g2h_ref, g2l_ref, g2ch_ref, g2cl_ref, o_ref):
    _, kb, _, N2, _ = a_ref.shape
    for k in range(kb):
        twr = _lane_tile(tw_ref[k, 0], HYENA_W)
        twi = _lane_tile(tw_ref[k, 1], HYENA_W)
        br, bi = _cmul(_both_halves(a_ref, k, 0), _both_halves(a_ref, k, 1), twr, twi)
        zz = _dot3(g2h_ref[...], g2l_ref[...], jnp.concatenate([br, bi], axis=0))
        yr, yi = _cmul(zz[:N2], zz[N2:], kf_ref[k, 0], kf_ref[k, 1])
        cc = _dot3(g2ch_ref[...], g2cl_ref[...], jnp.concatenate([yr, yi], axis=0))
        dr, di = _cmul(cc[:N2], cc[N2:], twr, -twi)
        for hf in range(HALVES):
            o_ref[hf, k, 0] = dr[:, hf * LANES:(hf + 1) * LANES]
            o_ref[hf, k, 1] = di[:, hf * LANES:(hf + 1) * LANES]


def _spec_call(a6, fc, s):
    G, _, N1, _, N2, _ = a6.shape
    kb = _tile(N1, 8, 1)
    mat = pl.BlockSpec((2 * N2, 2 * N2), lambda i, b: (0, 0))
    return pl.pallas_call(
        _spec_kernel,
        grid=(N1 // kb, G),
        in_specs=[pl.BlockSpec((None, HALVES, kb, 2, N2, LANES), lambda i, b: (b, 0, i, 0, 0, 0)),
                  pl.BlockSpec((kb, 2, N2, LANES), lambda i, b: (i, 0, 0, 0)), mat, mat,
                  pl.BlockSpec((None, 1, HYENA_W), lambda i, b: (b, 0, 0))],
        out_specs=pl.BlockSpec((None, kb, 2, N2, HYENA_W), lambda i, b: (b, i, 0, 0, 0)),
        out_shape=jax.ShapeDtypeStruct((G, N1, 2, N2, HYENA_W), F32),
        compiler_params=_cparams(("parallel", "parallel")),
        name="filt_spec",
    )(a6, fc["tw"], fc["g2"][0], fc["g2"][1], s)


def _conv_call(a6, kf, order, fc):
    G, _, N1, _, N2, _ = a6.shape
    kb = _tile(N1, 8, 1)
    blk = pl.BlockSpec((None, HALVES, kb, 2, N2, LANES), lambda i, b: (b, 0, i, 0, 0, 0))
    mat = pl.BlockSpec((2 * N2, 2 * N2), lambda i, b: (0, 0))
    return pl.pallas_call(
        _conv_kernel,
        grid=(N1 // kb, G),
        in_specs=[blk, pl.BlockSpec((kb, 2, N2, LANES), lambda i, b: (i, 0, 0, 0)),
                  pl.BlockSpec((None, kb, 2, N2, HYENA_W), lambda i, b: (order, i, 0, 0, 0)), mat, mat, mat, mat],
        out_specs=blk,
        out_shape=jax.ShapeDtypeStruct(a6.shape, F32),
        compiler_params=_cparams(("parallel", "parallel")),
        name="spec_conv",
    )(a6, fc["tw"], kf, fc["g2"][0], fc["g2"][1], fc["g2c"][0], fc["g2c"][1])


def _filter_spectra(n, lw, fc):
    N1, N2 = _fft_dims(2 * n)
    k, s = _filt_call(n, lw)
    a6 = _dft_in_call(fc["g1f"], k.reshape(HYENA_ORDER, HALVES, N1, N2, LANES))
    return _spec_call(a6, fc, s.reshape(HYENA_ORDER, 1, HYENA_W))


def _hyena(v, x1, x2, kf, lw, fc):
    P, _, _, n, _ = v.shape
    N1, N2 = _fft_dims(2 * n)
    shape5 = (P, HALVES, N1, N2, LANES)
    y = v.reshape(shape5)
    for o, gate in enumerate((x1, x2)):
        d = _conv_call(_dft_in_call(fc["g1"], y), kf, o, fc)
        y = _dft_out_call(fc["g3"], d, gate.reshape(shape5), y, lw["hy_skip"][o].reshape(HALVES, 1, LANES))
    return y.reshape(v.shape)


def _merge_kernel(at_ref, hy_ref, po_ref, gm_ref, x_ref, sh_ref, sc_ref, nw_ref, gr_ref,
                  wgate_ref, wa_ref, wh_ref, wp_ref, wg_ref, wo_ref, o_ref):
    D = x_ref.shape[1]
    for rows in _row_blocks(x_ref.shape[0]):
        x = x_ref[rows, :]
        h = _modulated_norm(x, nw_ref[...], sc_ref[...], sh_ref[...]).astype(BF16)
        hy = jnp.concatenate([hy_ref[hf, rows, :] for hf in range(HALVES)], axis=-1).astype(BF16)
        branches = ((at_ref[rows, :], wa_ref), (hy, wh_ref), (po_ref[rows, :], wp_ref), (gm_ref[rows, :], wg_ref))
        merged = None
        for i, (br, w_ref) in enumerate(branches):
            term = _sigmoid(_dot(h, wgate_ref[:, i * D:(i + 1) * D])) * _dot(br, w_ref[...])
            merged = term if merged is None else merged + term
        o_ref[rows, :] = x + gr_ref[...] * _dot(merged.astype(BF16), wo_ref[...])


MERGE_ROWS = 1024


def _merge_call(attn, hy, po, gm, x, shift, scale, norm_w, gate_res, lw):
    B, n, D = x.shape
    tm = _tile(n, MERGE_ROWS)

    def rows(w):
        return pl.BlockSpec((None, tm, w), lambda b, i: (b, i, 0))

    def full(shape):
        return pl.BlockSpec(shape, lambda b, i: (0,) * len(shape))

    vec = pl.BlockSpec((None, 1, D), lambda b, i: (b, 0, 0))
    return pl.pallas_call(
        _merge_kernel,
        grid=(B, n // tm),
        in_specs=[
            rows(ATTN_W), _split_spec(tm), rows(POOL_W), rows(GMLP_W), rows(D),
            vec, vec, full((1, D)), vec,
            full((D, N_BRANCH * D)),
            full((ATTN_W, D)), full((HYENA_W, D)), full((POOL_W, D)), full((GMLP_W, D)), full((D, D)),
        ],
        out_specs=rows(D),
        out_shape=jax.ShapeDtypeStruct((B, n, D), F32),
        compiler_params=_cparams(("parallel", "parallel")),
        name="merge",
    )(attn, hy, po, gm, x, shift, scale, norm_w.reshape(1, D), gate_res, lw["w_gate"], lw["w_br_attn"],
      lw["w_br_hyena"], lw["w_br_pool"], lw["w_br_gmlp"], lw["w_out"])


FFN_ROWS = 1024


def _ffn_kernel(x_ref, sh_ref, sc_ref, nw_ref, gr_ref, wg_ref, wu_ref, wd_ref, fw_ref, o_ref, *, final):
    for rows in _row_blocks(x_ref.shape[0]):
        x = x_ref[rows, :]
        h = _modulated_norm(x, nw_ref[...], sc_ref[...], sh_ref[...]).astype(BF16)
        a = _silu(_dot(h, wg_ref[...])) * _dot(h, wu_ref[...])
        y = x + gr_ref[...] * _dot(a.astype(BF16), wd_ref[...])
        if final:
            y = y * lax.rsqrt(jnp.mean(y * y, axis=-1, keepdims=True) + RMS_EPS) * fw_ref[...]
        o_ref[rows, :] = y


def _ffn_call(x, shift, scale, norm_w, gate_res, wg, wu, wd, final_w=None):
    B, n, D = x.shape
    H = wg.shape[1]
    tm = _tile(n, FFN_ROWS)
    vec = pl.BlockSpec((None, 1, D), lambda b, i: (b, 0, 0))
    one = pl.BlockSpec((1, D), lambda b, i: (0, 0))
    final = final_w is not None
    fw = (final_w if final else norm_w).reshape(1, D)
    return pl.pallas_call(
        functools.partial(_ffn_kernel, final=final),
        grid=(B, n // tm),
        in_specs=[
            pl.BlockSpec((None, tm, D), lambda b, i: (b, i, 0)),
            vec, vec, one, vec,
            pl.BlockSpec((D, H), lambda b, i: (0, 0)),
            pl.BlockSpec((D, H), lambda b, i: (0, 0)),
            pl.BlockSpec((H, D), lambda b, i: (0, 0)),
            one,
        ],
        out_specs=pl.BlockSpec((None, tm, D), lambda b, i: (b, i, 0)),
        out_shape=jax.ShapeDtypeStruct((B, n, D), F32),
        compiler_params=_cparams(("parallel", "parallel")),
        name="ffn",
    )(x, shift, scale, norm_w.reshape(1, D), gate_res, wg, wu, wd, fw)


def _rope_tables(n):
    tok = np.arange(n)
    axis_dim = HEAD_DIM // 2
    inv_freq = ROPE_THETA ** (-np.arange(0, axis_dim, 2, dtype=np.float64) / axis_dim)
    inv_freq = inv_freq.astype(np.float32).astype(np.float64)
    ang = np.concatenate([(tok // GRID_W)[:, None] * inv_freq, (tok % GRID_W)[:, None] * inv_freq], axis=-1)
    ang = ang.astype(np.float32)
    cos = np.repeat(np.cos(ang), 2, axis=-1)
    sin = np.repeat(np.sin(ang), 2, axis=-1)
    sin[:, 0::2] *= -1.0
    reps = LANES // HEAD_DIM
    return jnp.asarray(np.tile(cos, (1, reps)), F32), jnp.asarray(np.tile(sin, (1, reps)), F32)


def _identity_tables(n):
    return jnp.ones((n, LANES), F32), jnp.zeros((n, LANES), F32)


def _layer_weights(l, a):
    w_in = a["w_in"][l]
    w_perm = jnp.concatenate([
        w_in[:, Q_OFF:K_OFF], w_in[:, K_OFF:HY_OFF], w_in[:, POOL_OFF:GM_OFF], w_in[:, GM_OFF:GATE_OFF],
        w_in[:, HY_OFF:POOL_OFF]], axis=1).astype(BF16)
    w3 = a["hy_w3"][l].reshape(HYENA_HIDDEN, HYENA_ORDER, 2, HYENA_W).transpose(2, 0, 1, 3)
    w3 = w3.reshape(2, HYENA_HIDDEN, HYENA_ORDER * HYENA_W)
    dec = a["hy_decay"][l].reshape(HYENA_ORDER, 2, HYENA_W).transpose(1, 0, 2).reshape(2, 1, HYENA_ORDER * HYENA_W)
    w1 = jnp.zeros((HYENA_HIDDEN, LANES), F32).at[:, :a["hy_w1"].shape[1]].set(a["hy_w1"][l].T)
    w1h, w1l = _split(w1)
    w2h, w2l = _split(jnp.swapaxes(a["hy_w2"][l], 1, 2))

    def lanes_of(v):
        return jnp.broadcast_to(v[..., None], v.shape + (LANES,))
    w3h, w3l = _split(w3)
    qw16 = jnp.tile(a["q_norm_w"][l], N_Q_HEADS).reshape(1, ATTN_W) * Q_SCALE
    kw16 = jnp.tile(a["k_norm_w"][l], N_KV_HEADS).reshape(1, KV_W)
    rms_q = jnp.sqrt(jnp.mean(qw16 * qw16))
    rms_k = jnp.sqrt(jnp.mean(kw16 * kw16))
    bal = jnp.exp2(jnp.clip(jnp.round(0.5 * jnp.log2(rms_k / rms_q)), -20.0, 20.0))
    bal = jnp.where(jnp.isfinite(bal), bal, 1.0)
    fp8_ok = (jnp.max(jnp.abs(qw16 * bal)) <= FP8_MAX_WEIGHT) & (jnp.max(jnp.abs(kw16 / bal)) <= FP8_MAX_WEIGHT)
    eye = jnp.eye(len(POOL_WINDOWS), dtype=F32)
    pool_bd = jnp.einsum("gh,gcd->gchd", eye, a["pool_w"][l]).reshape(POOL_W, POOL_W).astype(BF16)
    gm_bias = jnp.repeat(a["gm_bs"][l].T, GMLP_GROUP_W, axis=1)
    return dict(
        w_in=w_perm, w_gate=w_in[:, GATE_OFF:].astype(BF16),
        qw16=qw16, kw16=kw16, qw8=qw16 * bal, kw8=kw16 / bal, fp8_ok=fp8_ok,
        conv_w=a["hy_conv_w"][l], conv_b=a["hy_conv_b"][l].reshape(1, -1),
        hy_w1h=w1h, hy_w1l=w1l, hy_b1=lanes_of(a["hy_b1"][l]), hy_w2h=w2h, hy_w2l=w2l, hy_b2=lanes_of(a["hy_b2"][l]),
        hy_w3h=w3h, hy_w3l=w3l, hy_freq=lanes_of(a["hy_freq"][l]), hy_decay=dec, hy_skip=a["hy_skip"][l],
        pool_bd=pool_bd, pool_scale=a["pool_scale"][l].reshape(1, -1),
        gm_norm_w=a["gm_norm_w"][l].reshape(1, -1), gm_ws=a["gm_ws"][l].astype(BF16), gm_bias=gm_bias,
        w_br_attn=a["w_br_attn"][l].astype(BF16), w_br_hyena=a["w_br_hyena"][l].astype(BF16),
        w_br_pool=a["w_br_pool"][l].astype(BF16), w_br_gmlp=a["w_br_gmlp"][l].astype(BF16),
        w_out=a["w_out"][l].astype(BF16),
        ffn_w_gate=a["ffn_w_gate"][l].astype(BF16), ffn_w_up=a["ffn_w_up"][l].astype(BF16),
        ffn_w_down=a["ffn_w_down"][l].astype(BF16),
        norm1_w=a["norm1_w"][l], norm2_w=a["norm2_w"][l],
    )


Q_SCALE = (HEAD_DIM ** -0.5) * math.log2(math.e)


def _attend(p, p_ctx, rope, rope_ctx, lw):
    def run(fp8):
        qw, kw = (lw["qw8"], lw["kw8"]) if fp8 else (lw["qw16"], lw["kw16"])
        qt, k, vt = _qkprep_call(p, rope[0], rope[1], qw, kw, fp8)
        if p_ctx is not None:
            _, k_c, vt_c = _qkprep_call(p_ctx, rope_ctx[0], rope_ctx[1], qw, kw, fp8)
            k = jnp.concatenate([k_c, k], axis=2)
            vt = jnp.concatenate([vt_c, vt], axis=3)
        return _attn_call(qt, k, vt)

    return lax.cond(lw["fp8_ok"], lambda: run(True), lambda: run(False))


def _mixer_block(x, mods, lw, rope, p_ctx, rope_ctx, kf, fc, final_w=None):
    sh1, s1, g1, sh2, s2, g2 = mods
    p = _inproj_call(x, sh1, s1, lw["norm1_w"], lw["w_in"])
    attn = _attend(p, p_ctx, rope, rope_ctx, lw)
    hv, hx1, hx2, po, gm = _local_call(p, lw)
    hy = _hyena(hv, hx1, hx2, kf, lw, fc)
    x = _merge_call(attn, hy, po, gm, x, sh1, s1, lw["norm1_w"], g1, lw)
    x = _ffn_call(x, sh2, s2, lw["norm2_w"], g2, lw["ffn_w_gate"], lw["ffn_w_up"], lw["ffn_w_down"], final_w)
    return x, p


def kernel(x, c, ctx, c_ctx, w_mod, b_mod, norm1_w, norm2_w, w_in, q_norm_w, k_norm_w, hy_conv_w, hy_conv_b,
           hy_w1, hy_b1, hy_w2, hy_b2, hy_w3, hy_freq, hy_decay, hy_skip, pool_w, pool_scale, gm_norm_w, gm_ws,
           gm_bs, w_br_attn, w_br_hyena, w_br_pool, w_br_gmlp, w_out, ffn_w_gate, ffn_w_up, ffn_w_down,
           final_norm_w):
    a = dict(w_in=w_in, q_norm_w=q_norm_w, k_norm_w=k_norm_w, hy_conv_w=hy_conv_w, hy_conv_b=hy_conv_b,
             hy_w1=hy_w1, hy_b1=hy_b1, hy_w2=hy_w2, hy_b2=hy_b2, hy_w3=hy_w3, hy_freq=hy_freq, hy_decay=hy_decay,
             hy_skip=hy_skip, pool_w=pool_w, pool_scale=pool_scale, gm_norm_w=gm_norm_w, gm_ws=gm_ws, gm_bs=gm_bs,
             w_br_attn=w_br_attn, w_br_hyena=w_br_hyena, w_br_pool=w_br_pool, w_br_gmlp=w_br_gmlp, w_out=w_out,
             ffn_w_gate=ffn_w_gate, ffn_w_up=ffn_w_up, ffn_w_down=ffn_w_down, norm1_w=norm1_w, norm2_w=norm2_w)
    B, n, D = x.shape
    nc = ctx.shape[1]
    depth = w_mod.shape[0]

    R = ((B + 1 + 15) // 16) * 16
    act = jnp.zeros((R, D), F32).at[:B].set(c).at[B].set(c_ctx)
    mod = _mod_call(act, w_mod, b_mod)

    rope = _rope_tables(n)
    rope_ctx = _identity_tables(nc)
    fc = _fft_consts(2 * n)
    fc_ctx = _fft_consts(2 * nc)

    xc = ctx
    for l in range(depth):
        lw = _layer_weights(l, a)
        mods = [m.reshape(B, 1, D) for m in jnp.split(mod[l, :B], 6, axis=-1)]
        cmods = [jnp.broadcast_to(m.reshape(1, 1, D), (B, 1, D)) for m in jnp.split(mod[l, B], 6, axis=-1)]
        if l < depth - 1:
            kf_ctx = _filter_spectra(nc, lw, fc_ctx)
            xc, pc = _mixer_block(xc, cmods, lw, rope_ctx, None, None, kf_ctx, fc_ctx)
        else:
            pc = _inproj_call(xc, cmods[0], cmods[1], lw["norm1_w"], lw["w_in"][:, :P_QKV])
        kf = _filter_spectra(n, lw, fc)
        x, _ = _mixer_block(x, mods, lw, rope, pc, rope_ctx, kf, fc, final_norm_w if l == depth - 1 else None)
    return x
```

```python
import functools
import math

import numpy as np
import jax
import jax.numpy as jnp
from jax import lax
from jax.experimental import pallas as pl
from jax.experimental.pallas import tpu as pltpu

F32 = jnp.float32
BF16 = jnp.bfloat16

D_MODEL = 1024
GRID_W = 64
RMS_EPS = 1e-6
HEAD_DIM = 64
N_Q_HEADS = 8
N_KV_HEADS = 2
ATTN_W = N_Q_HEADS * HEAD_DIM
KV_W = N_KV_HEADS * HEAD_DIM
ROPE_THETA = 10000.0
HYENA_W = 256
HYENA_ORDER = 2
HYENA_BANDS = 16
HYENA_HIDDEN = 64
HYENA_INNER = 2
POOL_W = 256
POOL_WINDOWS = (2, 4, 8, 16)
POOL_GROUP_W = POOL_W // len(POOL_WINDOWS)
POOL_HALO = 8
GMLP_W = 256
GMLP_CHUNK = 128
GMLP_GROUPS = 4
GMLP_GROUP_W = GMLP_W // GMLP_GROUPS
N_BRANCH = 4

Q_OFF = 0
K_OFF = Q_OFF + ATTN_W
V_OFF = K_OFF + KV_W
HY_OFF = V_OFF + KV_W
POOL_OFF = HY_OFF + 3 * HYENA_W
GM_OFF = POOL_OFF + POOL_W
GATE_OFF = GM_OFF + 2 * GMLP_W

P_Q = 0
P_KV = P_Q + ATTN_W
P_POOL = P_KV + 2 * KV_W
P_GMU = P_POOL + POOL_W
P_GMV = P_GMU + GMLP_W
P_HY = P_GMV + GMLP_W
P_W = P_HY + 3 * HYENA_W
P_QKV = P_POOL

LANES = 128
VMEM_LIMIT = 52 * 1024 * 1024


def _cparams(sem):
    return pltpu.CompilerParams(dimension_semantics=sem, vmem_limit_bytes=VMEM_LIMIT)


def _tile(n, cap, mult=8):
    best = None
    for t in range(mult, min(n, cap) + 1, mult):
        if n % t == 0:
            best = t
    assert best is not None, (n, cap, mult)
    return best


def _split(x):
    hi = x.astype(BF16)
    lo = (x - hi.astype(F32)).astype(BF16)
    return hi, lo


def _dot(a, b):
    return jnp.dot(a, b, preferred_element_type=F32)


def _dot3(a_hi, a_lo, x):
    x_hi, x_lo = _split(x)
    return _dot(a_hi, x_hi) + (_dot(a_lo, x_hi) + _dot(a_hi, x_lo))


def _dot3x(x, b_hi, b_lo):
    x_hi, x_lo = _split(x)
    return _dot(x_hi, b_hi) + (_dot(x_lo, b_hi) + _dot(x_hi, b_lo))


def _sigmoid(x):
    return 1.0 / (1.0 + jnp.exp(-x))


def _silu(x):
    return x * _sigmoid(x)


def _modulated_norm(x, nw, scale, shift):
    ms = jnp.mean(x * x, axis=-1, keepdims=True)
    y = x * lax.rsqrt(ms + RMS_EPS) * nw
    return y * (1.0 + scale) + shift


def _mod_kernel(a_ref, w_ref, b_ref, o_ref):
    a = _silu(a_ref[...]).astype(BF16)
    o_ref[...] = _dot(a, w_ref[...].astype(BF16)) + b_ref[...]


def _mod_call(act, w_mod, b_mod):
    L, D, W6 = w_mod.shape
    R = act.shape[0]
    tn = 1024
    return pl.pallas_call(
        _mod_kernel,
        grid=(L, W6 // tn),
        in_specs=[
            pl.BlockSpec((R, D), lambda l, j: (0, 0)),
            pl.BlockSpec((None, D, tn), lambda l, j: (l, 0, j)),
            pl.BlockSpec((None, 1, tn), lambda l, j: (l, 0, j)),
        ],
        out_specs=pl.BlockSpec((None, R, tn), lambda l, j: (l, 0, j)),
        out_shape=jax.ShapeDtypeStruct((L, R, W6), F32),
        compiler_params=_cparams(("arbitrary", "arbitrary")),
        name="mod",
    )(act, w_mod, b_mod.reshape(L, 1, W6))


ROW_BLOCK = 256
INPROJ_ROWS = 1024


def _row_blocks(tm):
    sub = ROW_BLOCK if tm % ROW_BLOCK == 0 else tm
    return [slice(r, r + sub) for r in range(0, tm, sub)]


def _inproj_kernel(x_ref, sh_ref, sc_ref, nw_ref, w_ref, o_ref):
    for rows in _row_blocks(x_ref.shape[0]):
        h = _modulated_norm(x_ref[rows, :], nw_ref[...], sc_ref[...], sh_ref[...]).astype(BF16)
        o_ref[rows, :] = _dot(h, w_ref[...])


def _inproj_call(x, shift, scale, norm_w, w):
    B, n, D = x.shape
    Nw = w.shape[1]
    tm = _tile(n, INPROJ_ROWS)
    return pl.pallas_call(
        _inproj_kernel,
        grid=(B, n // tm),
        in_specs=[
            pl.BlockSpec((None, tm, D), lambda b, i: (b, i, 0)),
            pl.BlockSpec((None, 1, D), lambda b, i: (b, 0, 0)),
            pl.BlockSpec((None, 1, D), lambda b, i: (b, 0, 0)),
            pl.BlockSpec((1, D), lambda b, i: (0, 0)),
            pl.BlockSpec((D, Nw), lambda b, i: (0, 0)),
        ],
        out_specs=pl.BlockSpec((None, tm, Nw), lambda b, i: (b, i, 0)),
        out_shape=jax.ShapeDtypeStruct((B, n, Nw), F32),
        compiler_params=_cparams(("parallel", "parallel")),
        name="inproj",
    )(x, shift, scale, norm_w.reshape(1, D), w)


def _head_norm_rope(t, w, cs, sn, bd_ref):
    sq_hi, sq_lo = _split(t * t)
    ms = _dot(sq_hi, bd_ref[...]) + _dot(sq_lo, bd_ref[...])
    y = t * lax.rsqrt(ms + RMS_EPS) * w
    lane = lax.broadcasted_iota(jnp.int32, y.shape, 1)
    swapped = jnp.where(lane % 2 == 0, pltpu.roll(y, LANES - 1, 1), pltpu.roll(y, 1, 1))
    return y * cs + swapped * sn


V_ROWS = 80
Q_PER_KV = N_Q_HEADS // N_KV_HEADS
LAG_INIT_KEYS = 128
LAG_LIMIT = 12.0
ATTN_QUERIES = 512
ATTN_KEYS = 768
ATTN_UNROLL = 2


def _value_rows(vt_g):
    tail = lax.broadcasted_iota(jnp.int32, (V_ROWS - HEAD_DIM, vt_g.shape[1]), 0)
    return jnp.concatenate([vt_g, jnp.where(tail == 0, 1.0, 0.0)], axis=0).astype(BF16)


FP8 = jnp.float8_e4m3fn
FP8_MAX_WEIGHT = 32.0


def _split8(x):
    hi = x.astype(FP8).astype(F32)
    return hi, (x - hi).astype(FP8).astype(F32)


def _qkprep_kernel(q_ref, kv_ref, cs_ref, sn_ref, qw_ref, kw_ref, bd_ref, kp_ref, qo_ref, ko_ref, vo_ref, *, fp8):
    cs = cs_ref[...]
    sn = sn_ref[...]
    tm = q_ref.shape[0]
    zeros = jnp.zeros((HEAD_DIM, tm), F32)
    heads_per_chunk = LANES // HEAD_DIM
    for c in range(ATTN_W // LANES):
        sl = slice(c * LANES, (c + 1) * LANES)
        r = _head_norm_rope(q_ref[:, sl], qw_ref[:, sl], cs, sn, bd_ref)
        if fp8:
            hi_t, lo_t = _split8(r.T)
            for half in range(heads_per_chunk):
                rows = slice(half * HEAD_DIM, (half + 1) * HEAD_DIM)
                qo_ref[c * heads_per_chunk + half] = jnp.concatenate(
                    [hi_t[rows], lo_t[rows], hi_t[rows], lo_t[rows]], axis=0).astype(FP8)
        else:
            rt = r.T
            for half in range(heads_per_chunk):
                h = c * heads_per_chunk + half
                blk = rt[half * HEAD_DIM:(half + 1) * HEAD_DIM]
                parts = [blk if g == h // Q_PER_KV else zeros for g in range(N_KV_HEADS)]
                qo_ref[h] = jnp.concatenate(parts, axis=0).astype(BF16)
    k = _head_norm_rope(kv_ref[:, :KV_W], kw_ref[...], cs, sn, bd_ref)
    if fp8:
        hi, lo = _split8(k)
        hi, lo = hi.astype(BF16), lo.astype(BF16)
        for g in range(N_KV_HEADS):
            ko_ref[g] = (_dot(hi, kp_ref[g, 0]) + _dot(lo, kp_ref[g, 1])).astype(FP8)
    else:
        ko_ref[0] = k.astype(BF16)
    vt = kv_ref[:, KV_W:].T
    for g in range(N_KV_HEADS):
        vo_ref[g] = _value_rows(vt[g * HEAD_DIM:(g + 1) * HEAD_DIM])


def _k_placement():
    pm = np.zeros((N_KV_HEADS, 2, KV_W, 4 * HEAD_DIM), np.float32)
    d = np.arange(HEAD_DIM)
    for g in range(N_KV_HEADS):
        for part in range(2):
            for rep in range(2):
                pm[g, part, g * HEAD_DIM + d, (2 * part + rep) * HEAD_DIM + d] = 1.0
    return jnp.asarray(pm, BF16)


def _qkprep_call(p, cs, sn, qw, kw, fp8):
    B, n, _ = p.shape
    tm = _tile(n, 512, LANES)
    bd = np.kron(np.eye(LANES // HEAD_DIM), np.full((HEAD_DIM, HEAD_DIM), 1.0 / HEAD_DIM))
    bd = jnp.asarray(bd, BF16)
    kd, kg, dt = (4 * HEAD_DIM, N_KV_HEADS, FP8) if fp8 else (KV_W, 1, BF16)
    return pl.pallas_call(
        functools.partial(_qkprep_kernel, fp8=fp8),
        grid=(n // tm, B),
        in_specs=[
            pl.BlockSpec((None, tm, ATTN_W), lambda i, b: (b, i, P_Q // ATTN_W)),
            pl.BlockSpec((None, tm, 2 * KV_W), lambda i, b: (b, i, P_KV // (2 * KV_W))),
            pl.BlockSpec((tm, LANES), lambda i, b: (i, 0)),
            pl.BlockSpec((tm, LANES), lambda i, b: (i, 0)),
            pl.BlockSpec((1, ATTN_W), lambda i, b: (0, 0)),
            pl.BlockSpec((1, KV_W), lambda i, b: (0, 0)),
            pl.BlockSpec((LANES, LANES), lambda i, b: (0, 0)),
            pl.BlockSpec((N_KV_HEADS, 2, KV_W, 4 * HEAD_DIM), lambda i, b: (0, 0, 0, 0)),
        ],
        out_specs=[
            pl.BlockSpec((None, N_Q_HEADS, kd, tm), lambda i, b: (b, 0, 0, i)),
            pl.BlockSpec((None, kg, tm, kd), lambda i, b: (b, 0, i, 0)),
            pl.BlockSpec((None, N_KV_HEADS, V_ROWS, tm), lambda i, b: (b, 0, 0, i)),
        ],
        out_shape=[
            jax.ShapeDtypeStruct((B, N_Q_HEADS, kd, n), dt),
            jax.ShapeDtypeStruct((B, kg, n, kd), dt),
            jax.ShapeDtypeStruct((B, N_KV_HEADS, V_ROWS, n), BF16),
        ],
        compiler_params=_cparams(("parallel", "parallel")),
        name="qkprep",
    )(p, p, cs, sn, qw, kw, bd, _k_placement())


def _attn_kernel(qt_ref, k_ref, vt_ref, o_ref, m_ref, acc_ref, *, tk):
    nk = vt_ref.shape[0]
    tq = m_ref.shape[-1]
    per_kv_head = k_ref.shape[0] > 1

    def keys(j, h):
        g = h // Q_PER_KV if per_kv_head else 0
        return k_ref[g, pl.ds(pl.multiple_of(j * tk, tk), tk), :]

    def lagged_chunk(j, jump):
        s_next = _dot(keys(j, 0), qt_ref[0])
        for h in range(N_Q_HEADS):
            s = s_next
            if h + 1 < N_Q_HEADS:
                s_next = _dot(keys(j, h + 1), qt_ref[h + 1])
            m_old = m_ref[h]
            p = jnp.exp2((s - m_old).astype(BF16))
            cmax = jnp.max(s, axis=0, keepdims=True)
            m_new = jnp.maximum(m_old, cmax)
            jump = jnp.maximum(jump, cmax - m_old)
            acc_ref[h] = jnp.exp2(m_old - m_new) * (acc_ref[h] + _dot(vt_ref[j, h // Q_PER_KV], p))
            m_ref[h] = m_new
        return jump

    def exact_chunk(j, carry):
        for h in range(N_Q_HEADS):
            s = _dot(keys(j, h), qt_ref[h])
            m_old = m_ref[h]
            m_new = jnp.maximum(m_old, jnp.max(s, axis=0, keepdims=True))
            p = jnp.exp2(s - m_new).astype(BF16)
            acc_ref[h] = jnp.exp2(m_old - m_new) * acc_ref[h] + _dot(vt_ref[j, h // Q_PER_KV], p)
            m_ref[h] = m_new
        return carry

    k_lo = (k_ref.shape[1] // 2) // LAG_INIT_KEYS * LAG_INIT_KEYS
    for h in range(N_Q_HEADS):
        k0 = k_ref[h // Q_PER_KV if per_kv_head else 0, k_lo:k_lo + LAG_INIT_KEYS, :]
        m_ref[h] = jnp.max(_dot(k0, qt_ref[h]), axis=0, keepdims=True)
    acc_ref[...] = jnp.zeros(acc_ref.shape, F32)
    jump = lax.fori_loop(0, nk, lagged_chunk, jnp.zeros((1, tq), F32), unroll=ATTN_UNROLL)

    @pl.when(jnp.logical_not(jnp.max(jump) <= LAG_LIMIT))
    def _():
        m_ref[...] = jnp.full(m_ref.shape, -jnp.inf, F32)
        acc_ref[...] = jnp.zeros(acc_ref.shape, F32)
        lax.fori_loop(0, nk, exact_chunk, 0)

    for c in range(ATTN_W // LANES):
        heads = []
        for h in range(c * (LANES // HEAD_DIM), (c + 1) * (LANES // HEAD_DIM)):
            a = acc_ref[h]
            heads.append(a[:HEAD_DIM] / a[HEAD_DIM:HEAD_DIM + 1])
        o_ref[:, c * LANES:(c + 1) * LANES] = jnp.concatenate(heads, axis=0).T.astype(BF16)


def _attn_call(qt, k, vt):
    B, _, kd, n = qt.shape
    _, kg, Lk, _ = k.shape
    tq = _tile(n, ATTN_QUERIES, LANES)
    tk = _tile(Lk, ATTN_KEYS, LANES)
    nk = Lk // tk
    vt = vt.reshape(B, N_KV_HEADS, V_ROWS, nk, tk).transpose(0, 3, 1, 2, 4)
    return pl.pallas_call(
        functools.partial(_attn_kernel, tk=tk),
        grid=(B, n // tq),
        in_specs=[
            pl.BlockSpec((None, N_Q_HEADS, kd, tq), lambda b, i: (b, 0, 0, i)),
            pl.BlockSpec((None, kg, Lk, kd), lambda b, i: (b, 0, 0, 0)),
            pl.BlockSpec((None, nk, N_KV_HEADS, V_ROWS, tk), lambda b, i: (b, 0, 0, 0, 0)),
        ],
        out_specs=pl.BlockSpec((None, tq, ATTN_W), lambda b, i: (b, i, 0)),
        out_shape=jax.ShapeDtypeStruct((B, n, ATTN_W), BF16),
        scratch_shapes=[
            pltpu.VMEM((N_Q_HEADS, 1, tq), F32),
            pltpu.VMEM((N_Q_HEADS, V_ROWS, tq), F32),
        ],
        compiler_params=_cparams(("parallel", "parallel")),
        name="attn",
    )(qt, k, vt)


HALVES = HYENA_W // LANES


def _split_shape(B, n):
    assert B % 2 == 0
    return jax.ShapeDtypeStruct((B // 2, HALVES, 2, n, LANES), F32)


def _split_spec(tm):
    return pl.BlockSpec((None, HALVES, None, tm, LANES), lambda b, i: (b // 2, 0, b % 2, i, 0))


def _shift_rows(cur, prev_row, next_row, rows):
    tm = cur.shape[0]
    up = jnp.where(rows == 0, prev_row, pltpu.roll(cur, 1, 0))
    dn = jnp.where(rows == tm - 1, next_row, pltpu.roll(cur, tm - 1, 0))
    return up, dn


def _local_kernel(hy_ref, hyp_ref, hyn_ref, po_ref, pop_ref, pon_ref, gu_ref, gv_ref,
                  cw_ref, cb_ref, pw_ref, ps_ref, gnw_ref, gws_ref, gb_ref,
                  v_ref, x1_ref, x2_ref, pl_ref, gm_ref, *, n_seq):
    i = pl.program_id(1)
    last = pl.num_programs(1) - 1
    tm = hy_ref.shape[0]
    has_prev = (i > 0).astype(F32)
    has_next = (i < last).astype(F32)

    cur = hy_ref[...]
    prev_row = hyp_ref[POOL_HALO - 1:POOL_HALO, :] * has_prev
    next_row = hyn_ref[0:1, :] * has_next
    rows = lax.broadcasted_iota(jnp.int32, cur.shape, 0)
    up, dn = _shift_rows(cur, prev_row, next_row, rows)
    z = up * cw_ref[0:1, :] + cur * cw_ref[1:2, :] + dn * cw_ref[2:3, :] + cb_ref[...]
    for s, o_ref in enumerate((v_ref, x1_ref, x2_ref)):
        for hf in range(HYENA_W // LANES):
            lo = s * HYENA_W + hf * LANES
            o_ref[hf] = z[:, lo:lo + LANES]

    zc = po_ref[...]
    ext = jnp.concatenate([pop_ref[...] * has_prev, zc, pon_ref[...] * has_next], axis=0)
    te = tm + 2 * POOL_HALO
    lane = lax.broadcasted_iota(jnp.int32, zc.shape, 1)
    prow = lax.broadcasted_iota(jnp.int32, zc.shape, 0) + i * tm
    half = jnp.left_shift(1, lane // POOL_GROUP_W)
    def ahead(a, k):
        return pltpu.roll(a, (te - k) % te, 0)

    assert POOL_WINDOWS == (2, 4, 8, 16)
    sums = [ext + ahead(ext, -1)]
    for win in POOL_WINDOWS[:-1]:
        sums.append(ahead(sums[-1], win // 2) + ahead(sums[-1], -(win // 2)))
    group = lane // POOL_GROUP_W
    acc = sums[-1][POOL_HALO:POOL_HALO + tm]
    for g in range(len(POOL_WINDOWS) - 2, -1, -1):
        acc = jnp.where(group == g, sums[g][POOL_HALO:POOL_HALO + tm], acc)
    cnt = jnp.minimum(prow + half, n_seq) - jnp.maximum(prow - half, 0)
    dlt = acc / cnt.astype(F32) - zc
    pl_ref[...] = (_dot(dlt.astype(BF16), pw_ref[...]) * ps_ref[...]).astype(BF16)

    gv = gv_ref[...]
    mu = jnp.mean(gv, axis=-1, keepdims=True)
    ctr = gv - mu
    var = jnp.mean(ctr * ctr, axis=-1, keepdims=True)
    vn = (ctr * lax.rsqrt(var + RMS_EPS) * gnw_ref[...]).astype(BF16)
    group = lax.broadcasted_iota(jnp.int32, (GMLP_CHUNK, GMLP_W), 1) // GMLP_GROUP_W
    for c in range(tm // GMLP_CHUNK):
        rs = slice(c * GMLP_CHUNK, (c + 1) * GMLP_CHUNK)
        vc = vn[rs]
        mixed = gb_ref[...]
        for g in range(GMLP_GROUPS):
            mg = _dot(gws_ref[g], vc)
            mixed = mixed + jnp.where(group == g, mg, 0.0)
        gm_ref[rs, :] = (gu_ref[rs, :] * mixed).astype(BF16)


def _local_call(p, lw):
    B, n, _ = p.shape
    tm = _tile(n, 512, GMLP_CHUNK)
    nb8 = n // POOL_HALO
    r8 = tm // POOL_HALO
    HW = 3 * HYENA_W

    def cur(w, off):
        return pl.BlockSpec((None, tm, w), lambda b, i: (b, i, off // w))

    def prev(w, off):
        return pl.BlockSpec((None, POOL_HALO, w), lambda b, i: (b, jnp.maximum(i * r8 - 1, 0), off // w))

    def nxt(w, off):
        return pl.BlockSpec((None, POOL_HALO, w), lambda b, i: (b, jnp.minimum((i + 1) * r8, nb8 - 1), off // w))

    def full(shape):
        return pl.BlockSpec(shape, lambda b, i: (0,) * len(shape))

    def out(w):
        return pl.BlockSpec((None, tm, w), lambda b, i: (b, i, 0))

    return pl.pallas_call(
        functools.partial(_local_kernel, n_seq=n),
        grid=(B, n // tm),
        in_specs=[
            cur(HW, P_HY), prev(HW, P_HY), nxt(HW, P_HY),
            cur(POOL_W, P_POOL), prev(POOL_W, P_POOL), nxt(POOL_W, P_POOL),
            cur(GMLP_W, P_GMU), cur(GMLP_W, P_GMV),
            full((3, HW)), full((1, HW)), full((POOL_W, POOL_W)), full((1, POOL_W)),
            full((1, GMLP_W)), full((GMLP_GROUPS, GMLP_CHUNK, GMLP_CHUNK)), full((GMLP_CHUNK, GMLP_W)),
        ],
        out_specs=[_split_spec(tm), _split_spec(tm), _split_spec(tm), out(POOL_W), out(GMLP_W)],
        out_shape=[
            _split_shape(B, n), _split_shape(B, n), _split_shape(B, n),
            jax.ShapeDtypeStruct((B, n, POOL_W), BF16),
            jax.ShapeDtypeStruct((B, n, GMLP_W), BF16),
        ],
        compiler_params=_cparams(("parallel", "parallel")),
        name="local",
    )(p, p, p, p, p, p, p, p, lw["conv_w"], lw["conv_b"], lw["pool_bd"], lw["pool_scale"],
      lw["gm_norm_w"], lw["gm_ws"], lw["gm_bias"])


def _filt_kernel(z_ref, t_ref, w1h_ref, w1l_ref, b1_ref, w2h_ref, w2l_ref, b2_ref, fr_ref,
                 w3h_ref, w3l_ref, dec_ref, k_ref, s_ref, *, n_seq):
    i = pl.program_id(0)
    tb = z_ref.shape[1]
    fr = _lane_tile(fr_ref[...], tb)
    hdn = jnp.sin(fr * (_dot3(w1h_ref[...], w1l_ref[...], z_ref[...]) + _lane_tile(b1_ref[...], tb)))
    for u in range(HYENA_INNER):
        hdn = jnp.sin(fr * (_dot3(w2h_ref[u], w2l_ref[u], hdn) + _lane_tile(b2_ref[u], tb)))
    hdn = hdn.T
    t = t_ref[...]
    tt = jnp.concatenate([t] * (HYENA_ORDER * HYENA_W // LANES), axis=1)
    window = jnp.exp(-tt * jnp.abs(dec_ref[...]))
    k = _dot3x(hdn, w3h_ref[...], w3l_ref[...]) * window
    rows = lax.broadcasted_iota(jnp.int32, k.shape, 0) + i * tb
    k = jnp.where(rows == n_seq, 0.0, k)
    for q in range(k_ref.shape[0]):
        k_ref[q] = k[:, q * LANES:(q + 1) * LANES]

    @pl.when(i == 0)
    def _():
        s_ref[...] = jnp.zeros(s_ref.shape, F32)

    s_ref[...] += jnp.sum(jnp.abs(k), axis=0, keepdims=True)


def _filt_call(n, lw):
    N = 2 * n
    tb = _tile(n, 512)
    nb_half = n // tb
    pos = np.arange(N)
    pos = np.where(pos < n, pos, N - pos).astype(np.float64)
    t = pos / (n - 1)
    bands = np.linspace(1e-4, HYENA_BANDS - 1, HYENA_BANDS)
    ang = (2.0 * math.pi / n) * pos[:, None] * bands
    z = np.concatenate([t[:, None], np.cos(ang), np.sin(ang)], axis=-1)
    zp = np.zeros((LANES, N), np.float32)
    zp[:z.shape[1]] = z.T
    tl = np.broadcast_to(t[:, None], (N, LANES)).astype(np.float32)
    CW = HYENA_ORDER * HYENA_W

    def full(shape):
        return pl.BlockSpec(shape, lambda i: (0,) * len(shape))

    def side(rows_):
        return pl.BlockSpec((None, rows_, CW), lambda i: (i // nb_half, 0, 0))

    return pl.pallas_call(
        functools.partial(_filt_kernel, n_seq=n),
        grid=(N // tb,),
        in_specs=[
            pl.BlockSpec((LANES, tb), lambda i: (0, i)),
            pl.BlockSpec((tb, LANES), lambda i: (i, 0)),
            full((HYENA_HIDDEN, LANES)), full((HYENA_HIDDEN, LANES)), full((HYENA_HIDDEN, LANES)),
            full((HYENA_INNER, HYENA_HIDDEN, HYENA_HIDDEN)), full((HYENA_INNER, HYENA_HIDDEN, HYENA_HIDDEN)),
            full((HYENA_INNER, HYENA_HIDDEN, LANES)), full((HYENA_HIDDEN, LANES)),
            side(HYENA_HIDDEN), side(HYENA_HIDDEN), side(1),
        ],
        out_specs=[pl.BlockSpec((CW // LANES, tb, LANES), lambda i: (0, i, 0)), pl.BlockSpec((1, CW), lambda i: (0, 0))],
        out_shape=[jax.ShapeDtypeStruct((CW // LANES, N, LANES), F32), jax.ShapeDtypeStruct((1, CW), F32)],
        compiler_params=_cparams(("arbitrary",)),
        name="filt",
    )(jnp.asarray(zp), jnp.asarray(tl), lw["hy_w1h"], lw["hy_w1l"], lw["hy_b1"], lw["hy_w2h"], lw["hy_w2l"],
      lw["hy_b2"], lw["hy_freq"], lw["hy_w3h"], lw["hy_w3l"], lw["hy_decay"])


def _fft_dims(N):
    e = int(round(math.log2(N)))
    assert 2 ** e == N
    N1 = 2 ** ((e + 1) // 2)
    return N1, N // N1


def _hilo(a):
    a = np.asarray(a, np.float64)
    hi = jnp.asarray(a, F32).astype(BF16)
    lo = (jnp.asarray(a, F32) - hi.astype(F32)).astype(BF16)
    return hi, lo


@functools.lru_cache(maxsize=None)
def _fft_consts_np(N):
    N1, N2 = _fft_dims(N)
    h = N1 // 2
    k1 = np.arange(N1)
    F1 = np.exp(-2j * np.pi * np.outer(k1, k1) / N1)
    k2 = np.arange(N2)
    F2 = np.exp(-2j * np.pi * np.outer(k2, k2) / N2)
    tw = np.exp(-2j * np.pi * np.outer(k1, k2) / N)
    g1 = np.zeros((2 * N1, N1))
    g1[0::2, :h] = F1.real[:, :h]
    g1[0::2, h:] = -F1.imag[:, :h]
    g1[1::2, :h] = F1.imag[:, :h]
    g1[1::2, h:] = F1.real[:, :h]
    g1f = np.zeros((2 * N1, N1))
    g1f[0::2] = F1.real
    g1f[1::2] = F1.imag
    g2 = np.block([[F2.real, -F2.imag], [F2.imag, F2.real]])
    g2c = np.block([[F2.real, F2.imag], [-F2.imag, F2.real]])
    g3 = np.zeros((N1, 2 * N1))
    g3[:h, 0::2] = F1.real[:h]
    g3[:h, 1::2] = F1.imag[:h]
    g3[h:, 0::2] = -F1.imag[:h]
    g3[h:, 1::2] = F1.real[:h]
    g3 /= N
    twt = np.stack([tw.real, tw.imag], axis=1)
    return dict(g1=g1, g1f=g1f, g2=g2, g2c=g2c, g3=g3, tw=twt)


def _fft_consts(N):
    c = _fft_consts_np(N)
    out = {k: _hilo(c[k]) for k in ("g1", "g1f", "g2", "g2c", "g3")}
    N1, N2 = _fft_dims(N)
    tw = jnp.asarray(c["tw"], F32)
    out["tw"] = jnp.broadcast_to(tw[..., None], (N1, 2, N2, LANES))
    return out


T2_BLOCK = 8


def _load_8th(ref, j, rows):
    return ref.reshape(rows * T2_BLOCK, LANES)[pl.ds(j, rows, stride=T2_BLOCK), :]


def _store_8th(ref, j, rows, val):
    ref.reshape(rows * T2_BLOCK, LANES)[pl.ds(j, rows, stride=T2_BLOCK), :] = val


def _dft_in_kernel(gh_ref, gl_ref, x_ref, o_ref):
    N1 = x_ref.shape[1]
    for j in range(T2_BLOCK):
        xj = jnp.concatenate([_load_8th(x_ref.at[hf], j, N1) for hf in range(HALVES)], axis=1)
        a = _dot3(gh_ref[...], gl_ref[...], xj)
        for hf in range(HALVES):
            _store_8th(o_ref.at[hf], j, 2 * N1, a[:, hf * LANES:(hf + 1) * LANES])


def _dft_in_call(g, u5):
    gh, gl = g
    G, _, N1, N2, _ = u5.shape
    assert gh.shape == (2 * N1, N1)
    mat = pl.BlockSpec((2 * N1, N1), lambda b, i: (0, 0))
    return pl.pallas_call(
        _dft_in_kernel,
        grid=(G, N2 // T2_BLOCK),
        in_specs=[mat, mat, pl.BlockSpec((None, HALVES, N1, T2_BLOCK, LANES), lambda b, i: (b, 0, 0, i, 0))],
        out_specs=pl.BlockSpec((None, HALVES, N1, 2, T2_BLOCK, LANES), lambda b, i: (b, 0, 0, 0, i, 0)),
        out_shape=jax.ShapeDtypeStruct((G, HALVES, N1, 2, N2, LANES), F32),
        compiler_params=_cparams(("parallel", "parallel")),
        name="dft_in",
    )(gh, gl, u5)


def _dft_out_kernel(gh_ref, gl_ref, d_ref, gate_ref, u_ref, sk_ref, o_ref):
    N1 = gate_ref.shape[1]
    for j in range(T2_BLOCK):
        dj = jnp.concatenate([_load_8th(d_ref.at[hf], j, 2 * N1) for hf in range(HALVES)], axis=1)
        y = _dot3(gh_ref[...], gl_ref[...], dj)
        for hf in range(HALVES):
            conv = y[:, hf * LANES:(hf + 1) * LANES]
            u = _load_8th(u_ref.at[hf], j, N1)
            _store_8th(o_ref.at[hf], j, N1, _load_8th(gate_ref.at[hf], j, N1) * (conv + u * sk_ref[hf]))


def _dft_out_call(g, d6, gate5, u5, skip):
    gh, gl = g
    G, _, N1, N2, _ = u5.shape
    assert gh.shape == (N1, 2 * N1)
    mat = pl.BlockSpec((N1, 2 * N1), lambda b, i: (0, 0))
    blk = pl.BlockSpec((None, HALVES, N1, T2_BLOCK, LANES), lambda b, i: (b, 0, 0, i, 0))
    return pl.pallas_call(
        _dft_out_kernel,
        grid=(G, N2 // T2_BLOCK),
        in_specs=[mat, mat, pl.BlockSpec((None, HALVES, N1, 2, T2_BLOCK, LANES), lambda b, i: (b, 0, 0, 0, i, 0)),
                  blk, blk, pl.BlockSpec((HALVES, 1, LANES), lambda b, i: (0, 0, 0))],
        out_specs=blk,
        out_shape=jax.ShapeDtypeStruct(u5.shape, F32),
        compiler_params=_cparams(("parallel", "parallel")),
        name="dft_out",
    )(gh, gl, d6, gate5, u5, skip)


def _cmul(ar, ai, br, bi):
    return ar * br - ai * bi, ar * bi + ai * br


def _lane_tile(t, c):
    return jnp.concatenate([t] * (c // LANES), axis=-1)


def _both_halves(a_ref, k, part):
    return jnp.concatenate([a_ref[hf, k, part] for hf in range(HALVES)], axis=-1)


def _spec_kernel(a_ref, tw_ref, g2h_ref, g2l_ref, s_ref, o_ref):
    _, kb, _, N2, _ = a_ref.shape
    inv = 1.0 / s_ref[...]
    for k in range(kb):
        twr = _lane_tile(tw_ref[k, 0], HYENA_W)
        twi = _lane_tile(tw_ref[k, 1], HYENA_W)
        br, bi = _cmul(_both_halves(a_ref, k, 0), _both_halves(a_ref, k, 1), twr, twi)
        zz = _dot3(g2h_ref[...], g2l_ref[...], jnp.concatenate([br, bi], axis=0))
        o_ref[k, 0] = zz[:N2] * inv
        o_ref[k, 1] = zz[N2:] * inv


def _conv_kernel(a_ref, tw_ref, kf_ref, g2h_ref, g2l_ref, g2ch_ref, g2cl_ref, o_ref):
    _, kb, _, N2, _ = a_ref.shape
    for k in range(kb):
        twr = _lane_tile(tw_ref[k, 0], HYENA_W)
        twi = _lane_tile(tw_ref[k, 1], HYENA_W)
        br, bi = _cmul(_both_halves(a_ref, k, 0), _both_halves(a_ref, k, 1), twr, twi)
        zz = _dot3(g2h_ref[...], g2l_ref[...], jnp.concatenate([br, bi], axis=0))
        yr, yi = _cmul(zz[:N2], zz[N2:], kf_ref[k, 0], kf_ref[k, 1])
        cc = _dot3(g2ch_ref[...], g2cl_ref[...], jnp.concatenate([yr, yi], axis=0))
        dr, di = _cmul(cc[:N2], cc[N2:], twr, -twi)
        for hf in range(HALVES):
            o_ref[hf, k, 0] = dr[:, hf * LANES:(hf + 1) * LANES]
            o_ref[hf, k, 1] = di[:, hf * LANES:(hf + 1) * LANES]


def _spec_call(a6, fc, s):
    G, _, N1, _, N2, _ = a6.shape
    kb = _tile(N1, 8, 1)
    mat = pl.BlockSpec((2 * N2, 2 * N2), lambda i, b: (0, 0))
    return pl.pallas_call(
        _spec_kernel,
        grid=(N1 // kb, G),
        in_specs=[pl.BlockSpec((None, HALVES, kb, 2, N2, LANES), lambda i, b: (b, 0, i, 0, 0, 0)),
                  pl.BlockSpec((kb, 2, N2, LANES), lambda i, b: (i, 0, 0, 0)), mat, mat,
                  pl.BlockSpec((None, 1, HYENA_W), lambda i, b: (b, 0, 0))],
        out_specs=pl.BlockSpec((None, kb, 2, N2, HYENA_W), lambda i, b: (b, i, 0, 0, 0)),
        out_shape=jax.ShapeDtypeStruct((G, N1, 2, N2, HYENA_W), F32),
        compiler_params=_cparams(("parallel", "parallel")),
        name="filt_spec",
    )(a6, fc["tw"], fc["g2"][0], fc["g2"][1], s)


def _conv_call(a6, kf, order, fc):
    G, _, N1, _, N2, _ = a6.shape
    kb = _tile(N1, 8, 1)
    blk = pl.BlockSpec((None, HALVES, kb, 2, N2, LANES), lambda i, b: (b, 0, i, 0, 0, 0))
    mat = pl.BlockSpec((2 * N2, 2 * N2), lambda i, b: (0, 0))
    return pl.pallas_call(
        _conv_kernel,
        grid=(N1 // kb, G),
        in_specs=[blk, pl.BlockSpec((kb, 2, N2, LANES), lambda i, b: (i, 0, 0, 0)),
                  pl.BlockSpec((None, kb, 2, N2, HYENA_W), lambda i, b: (order, i, 0, 0, 0)), mat, mat, mat, mat],
        out_specs=blk,
        out_shape=jax.ShapeDtypeStruct(a6.shape, F32),
        compiler_params=_cparams(("parallel", "parallel")),
        name="spec_conv",
    )(a6, fc["tw"], kf, fc["g2"][0], fc["g2"][1], fc["g2c"][0], fc["g2c"][1])


def _filter_spectra(n, lw, fc):
    N1, N2 = _fft_dims(2 * n)
    k, s = _filt_call(n, lw)
    a6 = _dft_in_call(fc["g1f"], k.reshape(HYENA_ORDER, HALVES, N1, N2, LANES))
    return _spec_call(a6, fc, s.reshape(HYENA_ORDER, 1, HYENA_W))


def _hyena(v, x1, x2, kf, lw, fc):
    P, _, _, n, _ = v.shape
    N1, N2 = _fft_dims(2 * n)
    shape5 = (P, HALVES, N1, N2, LANES)
    y = v.reshape(shape5)
    for o, gate in enumerate((x1, x2)):
        d = _conv_call(_dft_in_call(fc["g1"], y), kf, o, fc)
        y = _dft_out_call(fc["g3"], d, gate.reshape(shape5), y, lw["hy_skip"][o].reshape(HALVES, 1, LANES))
    return y.reshape(v.shape)


def _merge_kernel(at_ref, hy_ref, po_ref, gm_ref, x_ref, sh_ref, sc_ref, nw_ref, gr_ref,
                  wgate_ref, wa_ref, wh_ref, wp_ref, wg_ref, wo_ref, o_ref):
    D = x_ref.shape[1]
    for rows in _row_blocks(x_ref.shape[0]):
        x = x_ref[rows, :]
        h = _modulated_norm(x, nw_ref[...], sc_ref[...], sh_ref[...]).astype(BF16)
        hy = jnp.concatenate([hy_ref[hf, rows, :] for hf in range(HALVES)], axis=-1).astype(BF16)
        branches = ((at_ref[rows, :], wa_ref), (hy, wh_ref), (po_ref[rows, :], wp_ref), (gm_ref[rows, :], wg_ref))
        merged = None
        for i, (br, w_ref) in enumerate(branches):
            term = _sigmoid(_dot(h, wgate_ref[:, i * D:(i + 1) * D])) * _dot(br, w_ref[...])
            merged = term if merged is None else merged + term
        o_ref[rows, :] = x + gr_ref[...] * _dot(merged.astype(BF16), wo_ref[...])


MERGE_ROWS = 1024


def _merge_call(attn, hy, po, gm, x, shift, scale, norm_w, gate_res, lw):
    B, n, D = x.shape
    tm = _tile(n, MERGE_ROWS)

    def rows(w):
        return pl.BlockSpec((None, tm, w), lambda b, i: (b, i, 0))

    def full(shape):
        return pl.BlockSpec(shape, lambda b, i: (0,) * len(shape))

    vec = pl.BlockSpec((None, 1, D), lambda b, i: (b, 0, 0))
    return pl.pallas_call(
        _merge_kernel,
        grid=(B, n // tm),
        in_specs=[
            rows(ATTN_W), _split_spec(tm), rows(POOL_W), rows(GMLP_W), rows(D),
            vec, vec, full((1, D)), vec,
            full((D, N_BRANCH * D)),
            full((ATTN_W, D)), full((HYENA_W, D)), full((POOL_W, D)), full((GMLP_W, D)), full((D, D)),
        ],
        out_specs=rows(D),
        out_shape=jax.ShapeDtypeStruct((B, n, D), F32),
        compiler_params=_cparams(("parallel", "parallel")),
        name="merge",
    )(attn, hy, po, gm, x, shift, scale, norm_w.reshape(1, D), gate_res, lw["w_gate"], lw["w_br_attn"],
      lw["w_br_hyena"], lw["w_br_pool"], lw["w_br_gmlp"], lw["w_out"])


FFN_ROWS = 1024


def _ffn_kernel(x_ref, sh_ref, sc_ref, nw_ref, gr_ref, wg_ref, wu_ref, wd_ref, fw_ref, o_ref, *, final):
    for rows in _row_blocks(x_ref.shape[0]):
        x = x_ref[rows, :]
        h = _modulated_norm(x, nw_ref[...], sc_ref[...], sh_ref[...]).astype(BF16)
        a = _silu(_dot(h, wg_ref[...])) * _dot(h, wu_ref[...])
        y = x + gr_ref[...] * _dot(a.astype(BF16), wd_ref[...])
        if final:
            y = y * lax.rsqrt(jnp.mean(y * y, axis=-1, keepdims=True) + RMS_EPS) * fw_ref[...]
        o_ref[rows, :] = y


def _ffn_call(x, shift, scale, norm_w, gate_res, wg, wu, wd, final_w=None):
    B, n, D = x.shape
    H = wg.shape[1]
    tm = _tile(n, FFN_ROWS)
    vec = pl.BlockSpec((None, 1, D), lambda b, i: (b, 0, 0))
    one = pl.BlockSpec((1, D), lambda b, i: (0, 0))
    final = final_w is not None
    fw = (final_w if final else norm_w).reshape(1, D)
    return pl.pallas_call(
        functools.partial(_ffn_kernel, final=final),
        grid=(B, n // tm),
        in_specs=[
            pl.BlockSpec((None, tm, D), lambda b, i: (b, i, 0)),
            vec, vec, one, vec,
            pl.BlockSpec((D, H), lambda b, i: (0, 0)),
            pl.BlockSpec((D, H), lambda b, i: (0, 0)),
            pl.BlockSpec((H, D), lambda b, i: (0, 0)),
            one,
        ],
        out_specs=pl.BlockSpec((None, tm, D), lambda b, i: (b, i, 0)),
        out_shape=jax.ShapeDtypeStruct((B, n, D), F32),
        compiler_params=_cparams(("parallel", "parallel")),
        name="ffn",
    )(x, shift, scale, norm_w.reshape(1, D), gate_res, wg, wu, wd, fw)


def _rope_tables(n):
    tok = np.arange(n)
    axis_dim = HEAD_DIM // 2
    inv_freq = ROPE_THETA ** (-np.arange(0, axis_dim, 2, dtype=np.float64) / axis_dim)
    inv_freq = inv_freq.astype(np.float32).astype(np.float64)
    ang = np.concatenate([(tok // GRID_W)[:, None] * inv_freq, (tok % GRID_W)[:, None] * inv_freq], axis=-1)
    ang = ang.astype(np.float32)
    cos = np.repeat(np.cos(ang), 2, axis=-1)
    sin = np.repeat(np.sin(ang), 2, axis=-1)
    sin[:, 0::2] *= -1.0
    reps = LANES // HEAD_DIM
    return jnp.asarray(np.tile(cos, (1, reps)), F32), jnp.asarray(np.tile(sin, (1, reps)), F32)


def _identity_tables(n):
    return jnp.ones((n, LANES), F32), jnp.zeros((n, LANES), F32)


def _layer_weights(l, a):
    w_in = a["w_in"][l]
    w_perm = jnp.concatenate([
        w_in[:, Q_OFF:K_OFF], w_in[:, K_OFF:HY_OFF], w_in[:, POOL_OFF:GM_OFF], w_in[:, GM_OFF:GATE_OFF],
        w_in[:, HY_OFF:POOL_OFF]], axis=1).astype(BF16)
    w3 = a["hy_w3"][l].reshape(HYENA_HIDDEN, HYENA_ORDER, 2, HYENA_W).transpose(2, 0, 1, 3)
    w3 = w3.reshape(2, HYENA_HIDDEN, HYENA_ORDER * HYENA_W)
    dec = a["hy_decay"][l].reshape(HYENA_ORDER, 2, HYENA_W).transpose(1, 0, 2).reshape(2, 1, HYENA_ORDER * HYENA_W)
    w1 = jnp.zeros((HYENA_HIDDEN, LANES), F32).at[:, :a["hy_w1"].shape[1]].set(a["hy_w1"][l].T)
    w1h, w1l = _split(w1)
    w2h, w2l = _split(jnp.swapaxes(a["hy_w2"][l], 1, 2))

    def lanes_of(v):
        return jnp.broadcast_to(v[..., None], v.shape + (LANES,))
    w3h, w3l = _split(w3)
    qw16 = jnp.tile(a["q_norm_w"][l], N_Q_HEADS).reshape(1, ATTN_W) * Q_SCALE
    kw16 = jnp.tile(a["k_norm_w"][l], N_KV_HEADS).reshape(1, KV_W)
    rms_q = jnp.sqrt(jnp.mean(qw16 * qw16))
    rms_k = jnp.sqrt(jnp.mean(kw16 * kw16))
    bal = jnp.exp2(jnp.clip(jnp.round(0.5 * jnp.log2(rms_k / rms_q)), -20.0, 20.0))
    bal = jnp.where(jnp.isfinite(bal), bal, 1.0)
    fp8_ok = (jnp.max(jnp.abs(qw16 * bal)) <= FP8_MAX_WEIGHT) & (jnp.max(jnp.abs(kw16 / bal)) <= FP8_MAX_WEIGHT)
    eye = jnp.eye(len(POOL_WINDOWS), dtype=F32)
    pool_bd = jnp.einsum("gh,gcd->gchd", eye, a["pool_w"][l]).reshape(POOL_W, POOL_W).astype(BF16)
    gm_bias = jnp.repeat(a["gm_bs"][l].T, GMLP_GROUP_W, axis=1)
    return dict(
        w_in=w_perm, w_gate=w_in[:, GATE_OFF:].astype(BF16),
        qw16=qw16, kw16=kw16, qw8=qw16 * bal, kw8=kw16 / bal, fp8_ok=fp8_ok,
        conv_w=a["hy_conv_w"][l], conv_b=a["hy_conv_b"][l].reshape(1, -1),
        hy_w1h=w1h, hy_w1l=w1l, hy_b1=lanes_of(a["hy_b1"][l]), hy_w2h=w2h, hy_w2l=w2l, hy_b2=lanes_of(a["hy_b2"][l]),
        hy_w3h=w3h, hy_w3l=w3l, hy_freq=lanes_of(a["hy_freq"][l]), hy_decay=dec, hy_skip=a["hy_skip"][l],
        pool_bd=pool_bd, pool_scale=a["pool_scale"][l].reshape(1, -1),
        gm_norm_w=a["gm_norm_w"][l].reshape(1, -1), gm_ws=a["gm_ws"][l].astype(BF16), gm_bias=gm_bias,
        w_br_attn=a["w_br_attn"][l].astype(BF16), w_br_hyena=a["w_br_hyena"][l].astype(BF16),
        w_br_pool=a["w_br_pool"][l].astype(BF16), w_br_gmlp=a["w_br_gmlp"][l].astype(BF16),
        w_out=a["w_out"][l].astype(BF16),
        ffn_w_gate=a["ffn_w_gate"][l].astype(BF16), ffn_w_up=a["ffn_w_up"][l].astype(BF16),
        ffn_w_down=a["ffn_w_down"][l].astype(BF16),
        norm1_w=a["norm1_w"][l], norm2_w=a["norm2_w"][l],
    )


Q_SCALE = (HEAD_DIM ** -0.5) * math.log2(math.e)


def _attend(p, p_ctx, rope, rope_ctx, lw):
    def run(fp8):
        qw, kw = (lw["qw8"], lw["kw8"]) if fp8 else (lw["qw16"], lw["kw16"])
        qt, k, vt = _qkprep_call(p, rope[0], rope[1], qw, kw, fp8)
        if p_ctx is not None:
            _, k_c, vt_c = _qkprep_call(p_ctx, rope_ctx[0], rope_ctx[1], qw, kw, fp8)
            k = jnp.concatenate([k_c, k], axis=2)
            vt = jnp.concatenate([vt_c, vt], axis=3)
        return _attn_call(qt, k, vt)

    return lax.cond(lw["fp8_ok"], lambda: run(True), lambda: run(False))


def _mixer_block(x, mods, lw, rope, p_ctx, rope_ctx, kf, fc, final_w=None):
    sh1, s1, g1, sh2, s2, g2 = mods
    p = _inproj_call(x, sh1, s1, lw["norm1_w"], lw["w_in"])
    attn = _attend(p, p_ctx, rope, rope_ctx, lw)
    hv, hx1, hx2, po, gm = _local_call(p, lw)
    hy = _hyena(hv, hx1, hx2, kf, lw, fc)
    x = _merge_call(attn, hy, po, gm, x, sh1, s1, lw["norm1_w"], g1, lw)
    x = _ffn_call(x, sh2, s2, lw["norm2_w"], g2, lw["ffn_w_gate"], lw["ffn_w_up"], lw["ffn_w_down"], final_w)
    return x, p


def kernel(x, c, ctx, c_ctx, w_mod, b_mod, norm1_w, norm2_w, w_in, q_norm_w, k_norm_w, hy_conv_w, hy_conv_b,
           hy_w1, hy_b1, hy_w2, hy_b2, hy_w3, hy_freq, hy_decay, hy_skip, pool_w, pool_scale, gm_norm_w, gm_ws,
           gm_bs, w_br_attn, w_br_hyena, w_br_pool, w_br_gmlp, w_out, ffn_w_gate, ffn_w_up, ffn_w_down,
           final_norm_w):
    a = dict(w_in=w_in, q_norm_w=q_norm_w, k_norm_w=k_norm_w, hy_conv_w=hy_conv_w, hy_conv_b=hy_conv_b,
             hy_w1=hy_w1, hy_b1=hy_b1, hy_w2=hy_w2, hy_b2=hy_b2, hy_w3=hy_w3, hy_freq=hy_freq, hy_decay=hy_decay,
             hy_skip=hy_skip, pool_w=pool_w, pool_scale=pool_scale, gm_norm_w=gm_norm_w, gm_ws=gm_ws, gm_bs=gm_bs,
             w_br_attn=w_br_attn, w_br_hyena=w_br_hyena, w_br_pool=w_br_pool, w_br_gmlp=w_br_gmlp, w_out=w_out,
             ffn_w_gate=ffn_w_gate, ffn_w_up=ffn_w_up, ffn_w_down=ffn_w_down, norm1_w=norm1_w, norm2_w=norm2_w)
    B, n, D = x.shape
    nc = ctx.shape[1]
    depth = w_mod.shape[0]

    R = ((B + 1 + 15) // 16) * 16
    act = jnp.zeros((R, D), F32).at[:B].set(c).at[B].set(c_ctx)
    mod = _mod_call(act, w_mod, b_mod)

    rope = _rope_tables(n)
    rope_ctx = _identity_tables(nc)
    fc = _fft_consts(2 * n)
    fc_ctx = _fft_consts(2 * nc)

    xc = ctx
    for l in range(depth):
        lw = _layer_weights(l, a)
        mods = [m.reshape(B, 1, D) for m in jnp.split(mod[l, :B], 6, axis=-1)]
        cmods = [jnp.broadcast_to(m.reshape(1, 1, D), (B, 1, D)) for m in jnp.split(mod[l, B], 6, axis=-1)]
        if l < depth - 1:
            kf_ctx = _filter_spectra(nc, lw, fc_ctx)
            xc, pc = _mixer_block(xc, cmods, lw, rope_ctx, None, None, kf_ctx, fc_ctx)
        else:
            pc = _inproj_call(xc, cmods[0], cmods[1], lw["norm1_w"], lw["w_in"][:, :P_QKV])
        kf = _filter_spectra(n, lw, fc)
        x, _ = _mixer_block(x, mods, lw, rope, pc, rope_ctx, kf, fc, final_norm_w if l == depth - 1 else None)
    return x
```

```python
import functools
import math

import numpy as np
import jax
import jax.numpy as jnp
from jax import lax
from jax.experimental import pallas as pl
from jax.experimental.pallas import tpu as pltpu

F32 = jnp.float32
BF16 = jnp.bfloat16

D_MODEL = 1024
GRID_W = 64
RMS_EPS = 1e-6
HEAD_DIM = 64
N_Q_HEADS = 8
N_KV_HEADS = 2
ATTN_W = N_Q_HEADS * HEAD_DIM
KV_W = N_KV_HEADS * HEAD_DIM
ROPE_THETA = 10000.0
HYENA_W = 256
HYENA_ORDER = 2
HYENA_BANDS = 16
HYENA_HIDDEN = 64
HYENA_INNER = 2
POOL_W = 256
POOL_WINDOWS = (2, 4, 8, 16)
POOL_GROUP_W = POOL_W // len(POOL_WINDOWS)
POOL_HALO = 8
GMLP_W = 256
GMLP_CHUNK = 128
GMLP_GROUPS = 4
GMLP_GROUP_W = GMLP_W // GMLP_GROUPS
N_BRANCH = 4

Q_OFF = 0
K_OFF = Q_OFF + ATTN_W
V_OFF = K_OFF + KV_W
HY_OFF = V_OFF + KV_W
POOL_OFF = HY_OFF + 3 * HYENA_W
GM_OFF = POOL_OFF + POOL_W
GATE_OFF = GM_OFF + 2 * GMLP_W

P_Q = 0
P_KV = P_Q + ATTN_W
P_POOL = P_KV + 2 * KV_W
P_GMU = P_POOL + POOL_W
P_GMV = P_GMU + GMLP_W
P_HY = P_GMV + GMLP_W
P_W = P_HY + 3 * HYENA_W
P_QKV = P_POOL

LANES = 128
VMEM_LIMIT = 52 * 1024 * 1024


def _cparams(sem):
    return pltpu.CompilerParams(dimension_semantics=sem, vmem_limit_bytes=VMEM_LIMIT)


def _tile(n, cap, mult=8):
    best = None
    for t in range(mult, min(n, cap) + 1, mult):
        if n % t == 0:
            best = t
    assert best is not None, (n, cap, mult)
    return best


def _split(x):
    hi = x.astype(BF16)
    lo = (x - hi.astype(F32)).astype(BF16)
    return hi, lo


def _dot(a, b):
    return jnp.dot(a, b, preferred_element_type=F32)


def _dot3(a_hi, a_lo, x):
    x_hi, x_lo = _split(x)
    return _dot(a_hi, x_hi) + (_dot(a_lo, x_hi) + _dot(a_hi, x_lo))


def _dot3x(x, b_hi, b_lo):
    x_hi, x_lo = _split(x)
    return _dot(x_hi, b_hi) + (_dot(x_lo, b_hi) + _dot(x_hi, b_lo))


def _sigmoid(x):
    return 1.0 / (1.0 + jnp.exp(-x))


def _silu(x):
    return x * _sigmoid(x)


def _modulated_norm(x, nw, scale, shift):
    ms = jnp.mean(x * x, axis=-1, keepdims=True)
    y = x * lax.rsqrt(ms + RMS_EPS) * nw
    return y * (1.0 + scale) + shift


def _mod_kernel(a_ref, w_ref, b_ref, o_ref):
    a = _silu(a_ref[...]).astype(BF16)
    o_ref[...] = _dot(a, w_ref[...].astype(BF16)) + b_ref[...]


def _mod_call(act, w_mod, b_mod):
    L, D, W6 = w_mod.shape
    R = act.shape[0]
    tn = 1024
    return pl.pallas_call(
        _mod_kernel,
        grid=(L, W6 // tn),
        in_specs=[
            pl.BlockSpec((R, D), lambda l, j: (0, 0)),
            pl.BlockSpec((None, D, tn), lambda l, j: (l, 0, j)),
            pl.BlockSpec((None, 1, tn), lambda l, j: (l, 0, j)),
        ],
        out_specs=pl.BlockSpec((None, R, tn), lambda l, j: (l, 0, j)),
        out_shape=jax.ShapeDtypeStruct((L, R, W6), F32),
        compiler_params=_cparams(("arbitrary", "arbitrary")),
        name="mod",
    )(act, w_mod, b_mod.reshape(L, 1, W6))


ROW_BLOCK = 256
INPROJ_ROWS = 1024


def _row_blocks(tm):
    sub = ROW_BLOCK if tm % ROW_BLOCK == 0 else tm
    return [slice(r, r + sub) for r in range(0, tm, sub)]


def _inproj_kernel(x_ref, sh_ref, sc_ref, nw_ref, w_ref, o_ref):
    for rows in _row_blocks(x_ref.shape[0]):
        h = _modulated_norm(x_ref[rows, :], nw_ref[...], sc_ref[...], sh_ref[...]).astype(BF16)
        o_ref[rows, :] = _dot(h, w_ref[...])


def _inproj_call(x, shift, scale, norm_w, w):
    B, n, D = x.shape
    Nw = w.shape[1]
    tm = _tile(n, INPROJ_ROWS)
    return pl.pallas_call(
        _inproj_kernel,
        grid=(B, n // tm),
        in_specs=[
            pl.BlockSpec((None, tm, D), lambda b, i: (b, i, 0)),
            pl.BlockSpec((None, 1, D), lambda b, i: (b, 0, 0)),
            pl.BlockSpec((None, 1, D), lambda b, i: (b, 0, 0)),
            pl.BlockSpec((1, D), lambda b, i: (0, 0)),
            pl.BlockSpec((D, Nw), lambda b, i: (0, 0)),
        ],
        out_specs=pl.BlockSpec((None, tm, Nw), lambda b, i: (b, i, 0)),
        out_shape=jax.ShapeDtypeStruct((B, n, Nw), F32),
        compiler_params=_cparams(("parallel", "parallel")),
        name="inproj",
    )(x, shift, scale, norm_w.reshape(1, D), w)


def _head_norm_rope(t, w, cs, sn, bd_ref):
    sq_hi, sq_lo = _split(t * t)
    ms = _dot(sq_hi, bd_ref[...]) + _dot(sq_lo, bd_ref[...])
    y = t * lax.rsqrt(ms + RMS_EPS) * w
    lane = lax.broadcasted_iota(jnp.int32, y.shape, 1)
    swapped = jnp.where(lane % 2 == 0, pltpu.roll(y, LANES - 1, 1), pltpu.roll(y, 1, 1))
    return y * cs + swapped * sn


V_ROWS = 80
Q_PER_KV = N_Q_HEADS // N_KV_HEADS
LAG_INIT_KEYS = 128
LAG_LIMIT = 12.0
ATTN_QUERIES = 512
ATTN_KEYS = 768
ATTN_UNROLL = 2


def _value_rows(vt_g):
    tail = lax.broadcasted_iota(jnp.int32, (V_ROWS - HEAD_DIM, vt_g.shape[1]), 0)
    return jnp.concatenate([vt_g, jnp.where(tail == 0, 1.0, 0.0)], axis=0).astype(BF16)


FP8 = jnp.float8_e4m3fn
FP8_MAX_WEIGHT = 32.0


def _split8(x):
    hi = x.astype(FP8).astype(F32)
    return hi, (x - hi).astype(FP8).astype(F32)


def _qkprep_kernel(q_ref, kv_ref, cs_ref, sn_ref, qw_ref, kw_ref, bd_ref, kp_ref, qo_ref, ko_ref, vo_ref, *, fp8):
    cs = cs_ref[...]
    sn = sn_ref[...]
    tm = q_ref.shape[0]
    zeros = jnp.zeros((HEAD_DIM, tm), F32)
    heads_per_chunk = LANES // HEAD_DIM
    for c in range(ATTN_W // LANES):
        sl = slice(c * LANES, (c + 1) * LANES)
        r = _head_norm_rope(q_ref[:, sl], qw_ref[:, sl], cs, sn, bd_ref)
        if fp8:
            hi_t, lo_t = _split8(r.T)
            for half in range(heads_per_chunk):
                rows = slice(half * HEAD_DIM, (half + 1) * HEAD_DIM)
                qo_ref[c * heads_per_chunk + half] = jnp.concatenate(
                    [hi_t[rows], lo_t[rows], hi_t[rows], lo_t[rows]], axis=0).astype(FP8)
        else:
            rt = r.T
            for half in range(heads_per_chunk):
                h = c * heads_per_chunk + half
                blk = rt[half * HEAD_DIM:(half + 1) * HEAD_DIM]
                parts = [blk if g == h // Q_PER_KV else zeros for g in range(N_KV_HEADS)]
                qo_ref[h] = jnp.concatenate(parts, axis=0).astype(BF16)
    k = _head_norm_rope(kv_ref[:, :KV_W], kw_ref[...], cs, sn, bd_ref)
    if fp8:
        hi, lo = _split8(k)
        hi, lo = hi.astype(BF16), lo.astype(BF16)
        for g in range(N_KV_HEADS):
            ko_ref[g] = (_dot(hi, kp_ref[g, 0]) + _dot(lo, kp_ref[g, 1])).astype(FP8)
    else:
        ko_ref[0] = k.astype(BF16)
    vt = kv_ref[:, KV_W:].T
    for g in range(N_KV_HEADS):
        vo_ref[g] = _value_rows(vt[g * HEAD_DIM:(g + 1) * HEAD_DIM])


def _k_placement():
    pm = np.zeros((N_KV_HEADS, 2, KV_W, 4 * HEAD_DIM), np.float32)
    d = np.arange(HEAD_DIM)
    for g in range(N_KV_HEADS):
        for part in range(2):
            for rep in range(2):
                pm[g, part, g * HEAD_DIM + d, (2 * part + rep) * HEAD_DIM + d] = 1.0
    return jnp.asarray(pm, BF16)


def _qkprep_call(p, cs, sn, qw, kw, fp8):
    B, n, _ = p.shape
    tm = _tile(n, 512, LANES)
    bd = np.kron(np.eye(LANES // HEAD_DIM), np.full((HEAD_DIM, HEAD_DIM), 1.0 / HEAD_DIM))
    bd = jnp.asarray(bd, BF16)
    kd, kg, dt = (4 * HEAD_DIM, N_KV_HEADS, FP8) if fp8 else (KV_W, 1, BF16)
    return pl.pallas_call(
        functools.partial(_qkprep_kernel, fp8=fp8),
        grid=(n // tm, B),
        in_specs=[
            pl.BlockSpec((None, tm, ATTN_W), lambda i, b: (b, i, P_Q // ATTN_W)),
            pl.BlockSpec((None, tm, 2 * KV_W), lambda i, b: (b, i, P_KV // (2 * KV_W))),
            pl.BlockSpec((tm, LANES), lambda i, b: (i, 0)),
            pl.BlockSpec((tm, LANES), lambda i, b: (i, 0)),
            pl.BlockSpec((1, ATTN_W), lambda i, b: (0, 0)),
            pl.BlockSpec((1, KV_W), lambda i, b: (0, 0)),
            pl.BlockSpec((LANES, LANES), lambda i, b: (0, 0)),
            pl.BlockSpec((N_KV_HEADS, 2, KV_W, 4 * HEAD_DIM), lambda i, b: (0, 0, 0, 0)),
        ],
        out_specs=[
            pl.BlockSpec((None, N_Q_HEADS, kd, tm), lambda i, b: (b, 0, 0, i)),
            pl.BlockSpec((None, kg, tm, kd), lambda i, b: (b, 0, i, 0)),
            pl.BlockSpec((None, N_KV_HEADS, V_ROWS, tm), lambda i, b: (b, 0, 0, i)),
        ],
        out_shape=[
            jax.ShapeDtypeStruct((B, N_Q_HEADS, kd, n), dt),
            jax.ShapeDtypeStruct((B, kg, n, kd), dt),
            jax.ShapeDtypeStruct((B, N_KV_HEADS, V_ROWS, n), BF16),
        ],
        compiler_params=_cparams(("parallel", "parallel")),
        name="qkprep",
    )(p, p, cs, sn, qw, kw, bd, _k_placement())


def _attn_kernel(qt_ref, k_ref, vt_ref, o_ref, m_ref, acc_ref, *, tk):
    nk = vt_ref.shape[0]
    tq = m_ref.shape[-1]
    per_kv_head = k_ref.shape[0] > 1

    def keys(j, h):
        g = h // Q_PER_KV if per_kv_head else 0
        return k_ref[g, pl.ds(pl.multiple_of(j * tk, tk), tk), :]

    def lagged_chunk(j, jump):
        s_next = _dot(keys(j, 0), qt_ref[0])
        for h in range(N_Q_HEADS):
            s = s_next
            if h + 1 < N_Q_HEADS:
                s_next = _dot(keys(j, h + 1), qt_ref[h + 1])
            m_old = m_ref[h]
            p = jnp.exp2((s - m_old).astype(BF16))
            cmax = jnp.max(s, axis=0, keepdims=True)
            m_new = jnp.maximum(m_old, cmax)
            jump = jnp.maximum(jump, cmax - m_old)
            acc_ref[h] = jnp.exp2(m_old - m_new) * (acc_ref[h] + _dot(vt_ref[j, h // Q_PER_KV], p))
            m_ref[h] = m_new
        return jump

    def exact_chunk(j, carry):
        for h in range(N_Q_HEADS):
            s = _dot(keys(j, h), qt_ref[h])
            m_old = m_ref[h]
            m_new = jnp.maximum(m_old, jnp.max(s, axis=0, keepdims=True))
            p = jnp.exp2(s - m_new).astype(BF16)
            acc_ref[h] = jnp.exp2(m_old - m_new) * acc_ref[h] + _dot(vt_ref[j, h // Q_PER_KV], p)
            m_ref[h] = m_new
        return carry

    k_lo = (k_ref.shape[1] // 2) // LAG_INIT_KEYS * LAG_INIT_KEYS
    for h in range(N_Q_HEADS):
        k0 = k_ref[h // Q_PER_KV if per_kv_head else 0, k_lo:k_lo + LAG_INIT_KEYS, :]
        m_ref[h] = jnp.max(_dot(k0, qt_ref[h]), axis=0, keepdims=True)
    acc_ref[...] = jnp.zeros(acc_ref.shape, F32)
    jump = lax.fori_loop(0, nk, lagged_chunk, jnp.zeros((1, tq), F32), unroll=ATTN_UNROLL)

    @pl.when(jnp.logical_not(jnp.max(jump) <= LAG_LIMIT))
    def _():
        m_ref[...] = jnp.full(m_ref.shape, -jnp.inf, F32)
        acc_ref[...] = jnp.zeros(acc_ref.shape, F32)
        lax.fori_loop(0, nk, exact_chunk, 0)

    for c in range(ATTN_W // LANES):
        heads = []
        for h in range(c * (LANES // HEAD_DIM), (c + 1) * (LANES // HEAD_DIM)):
            a = acc_ref[h]
            heads.append(a[:HEAD_DIM] / a[HEAD_DIM:HEAD_DIM + 1])
        o_ref[:, c * LANES:(c + 1) * LANES] = jnp.concatenate(heads, axis=0).T.astype(BF16)


def _attn_call(qt, k, vt):
    B, _, kd, n = qt.shape
    _, kg, Lk, _ = k.shape
    tq = _tile(n, ATTN_QUERIES, LANES)
    tk = _tile(Lk, ATTN_KEYS, LANES)
    nk = Lk // tk
    vt = vt.reshape(B, N_KV_HEADS, V_ROWS, nk, tk).transpose(0, 3, 1, 2, 4)
    return pl.pallas_call(
        functools.partial(_attn_kernel, tk=tk),
        grid=(B, n // tq),
        in_specs=[
            pl.BlockSpec((None, N_Q_HEADS, kd, tq), lambda b, i: (b, 0, 0, i)),
            pl.BlockSpec((None, kg, Lk, kd), lambda b, i: (b, 0, 0, 0)),
            pl.BlockSpec((None, nk, N_KV_HEADS, V_ROWS, tk), lambda b, i: (b, 0, 0, 0, 0)),
        ],
        out_specs=pl.BlockSpec((None, tq, ATTN_W), lambda b, i: (b, i, 0)),
        out_shape=jax.ShapeDtypeStruct((B, n, ATTN_W), BF16),
        scratch_shapes=[
            pltpu.VMEM((N_Q_HEADS, 1, tq), F32),
            pltpu.VMEM((N_Q_HEADS, V_ROWS, tq), F32),
        ],
        compiler_params=_cparams(("parallel", "parallel")),
        name="attn",
    )(qt, k, vt)


HALVES = HYENA_W // LANES


def _split_shape(B, n):
    assert B % 2 == 0
    return jax.ShapeDtypeStruct((B // 2, HALVES, 2, n, LANES), F32)


def _split_spec(tm):
    return pl.BlockSpec((None, HALVES, None, tm, LANES), lambda b, i: (b // 2, 0, b % 2, i, 0))


def _shift_rows(cur, prev_row, next_row, rows):
    tm = cur.shape[0]
    up = jnp.where(rows == 0, prev_row, pltpu.roll(cur, 1, 0))
    dn = jnp.where(rows == tm - 1, next_row, pltpu.roll(cur, tm - 1, 0))
    return up, dn


def _local_kernel(hy_ref, hyp_ref, hyn_ref, po_ref, pop_ref, pon_ref, gu_ref, gv_ref,
                  cw_ref, cb_ref, pw_ref, ps_ref, gnw_ref, gws_ref, gb_ref,
                  v_ref, x1_ref, x2_ref, pl_ref, gm_ref, *, n_seq):
    i = pl.program_id(1)
    last = pl.num_programs(1) - 1
    tm = hy_ref.shape[0]
    has_prev = (i > 0).astype(F32)
    has_next = (i < last).astype(F32)

    cur = hy_ref[...]
    prev_row = hyp_ref[POOL_HALO - 1:POOL_HALO, :] * has_prev
    next_row = hyn_ref[0:1, :] * has_next
    rows = lax.broadcasted_iota(jnp.int32, cur.shape, 0)
    up, dn = _shift_rows(cur, prev_row, next_row, rows)
    z = up * cw_ref[0:1, :] + cur * cw_ref[1:2, :] + dn * cw_ref[2:3, :] + cb_ref[...]
    for s, o_ref in enumerate((v_ref, x1_ref, x2_ref)):
        for hf in range(HYENA_W // LANES):
            lo = s * HYENA_W + hf * LANES
            o_ref[hf] = z[:, lo:lo + LANES]

    zc = po_ref[...]
    ext = jnp.concatenate([pop_ref[...] * has_prev, zc, pon_ref[...] * has_next], axis=0)
    te = tm + 2 * POOL_HALO
    lane = lax.broadcasted_iota(jnp.int32, zc.shape, 1)
    prow = lax.broadcasted_iota(jnp.int32, zc.shape, 0) + i * tm
    half = jnp.left_shift(1, lane // POOL_GROUP_W)
    def ahead(a, k):
        return pltpu.roll(a, (te - k) % te, 0)

    assert POOL_WINDOWS == (2, 4, 8, 16)
    sums = [ext + ahead(ext, -1)]
    for win in POOL_WINDOWS[:-1]:
        sums.append(ahead(sums[-1], win // 2) + ahead(sums[-1], -(win // 2)))
    group = lane // POOL_GROUP_W
    acc = sums[-1][POOL_HALO:POOL_HALO + tm]
    for g in range(len(POOL_WINDOWS) - 2, -1, -1):
        acc = jnp.where(group == g, sums[g][POOL_HALO:POOL_HALO + tm], acc)
    cnt = jnp.minimum(prow + half, n_seq) - jnp.maximum(prow - half, 0)
    dlt = acc / cnt.astype(F32) - zc
    pl_ref[...] = (_dot(dlt.astype(BF16), pw_ref[...]) * ps_ref[...]).astype(BF16)

    gv = gv_ref[...]
    mu = jnp.mean(gv, axis=-1, keepdims=True)
    ctr = gv - mu
    var = jnp.mean(ctr * ctr, axis=-1, keepdims=True)
    vn = (ctr * lax.rsqrt(var + RMS_EPS) * gnw_ref[...]).astype(BF16)
    group = lax.broadcasted_iota(jnp.int32, (GMLP_CHUNK, GMLP_W), 1) // GMLP_GROUP_W
    for c in range(tm // GMLP_CHUNK):
        rs = slice(c * GMLP_CHUNK, (c + 1) * GMLP_CHUNK)
        vc = vn[rs]
        mixed = gb_ref[...]
        for g in range(GMLP_GROUPS):
            mg = _dot(gws_ref[g], vc)
            mixed = mixed + jnp.where(group == g, mg, 0.0)
        gm_ref[rs, :] = (gu_ref[rs, :] * mixed).astype(BF16)


def _local_call(p, lw):
    B, n, _ = p.shape
    tm = _tile(n, 512, GMLP_CHUNK)
    nb8 = n // POOL_HALO
    r8 = tm // POOL_HALO
    HW = 3 * HYENA_W

    def cur(w, off):
        return pl.BlockSpec((None, tm, w), lambda b, i: (b, i, off // w))

    def prev(w, off):
        return pl.BlockSpec((None, POOL_HALO, w), lambda b, i: (b, jnp.maximum(i * r8 - 1, 0), off // w))

    def nxt(w, off):
        return pl.BlockSpec((None, POOL_HALO, w), lambda b, i: (b, jnp.minimum((i + 1) * r8, nb8 - 1), off // w))

    def full(shape):
        return pl.BlockSpec(shape, lambda b, i: (0,) * len(shape))

    def out(w):
        return pl.BlockSpec((None, tm, w), lambda b, i: (b, i, 0))

    return pl.pallas_call(
        functools.partial(_local_kernel, n_seq=n),
        grid=(B, n // tm),
        in_specs=[
            cur(HW, P_HY), prev(HW, P_HY), nxt(HW, P_HY),
            cur(POOL_W, P_POOL), prev(POOL_W, P_POOL), nxt(POOL_W, P_POOL),
            cur(GMLP_W, P_GMU), cur(GMLP_W, P_GMV),
            full((3, HW)), full((1, HW)), full((POOL_W, POOL_W)), full((1, POOL_W)),
            full((1, GMLP_W)), full((GMLP_GROUPS, GMLP_CHUNK, GMLP_CHUNK)), full((GMLP_CHUNK, GMLP_W)),
        ],
        out_specs=[_split_spec(tm), _split_spec(tm), _split_spec(tm), out(POOL_W), out(GMLP_W)],
        out_shape=[
            _split_shape(B, n), _split_shape(B, n), _split_shape(B, n),
            jax.ShapeDtypeStruct((B, n, POOL_W), BF16),
            jax.ShapeDtypeStruct((B, n, GMLP_W), BF16),
        ],
        compiler_params=_cparams(("parallel", "parallel")),
        name="local",
    )(p, p, p, p, p, p, p, p, lw["conv_w"], lw["conv_b"], lw["pool_bd"], lw["pool_scale"],
      lw["gm_norm_w"], lw["gm_ws"], lw["gm_bias"])


def _filt_kernel(z_ref, t_ref, w1h_ref, w1l_ref, b1_ref, w2h_ref, w2l_ref, b2_ref, fr_ref,
                 w3h_ref, w3l_ref, dec_ref, k_ref, s_ref, *, n_seq):
    i = pl.program_id(0)
    tb = z_ref.shape[1]
    fr = _lane_tile(fr_ref[...], tb)
    hdn = jnp.sin(fr * (_dot3(w1h_ref[...], w1l_ref[...], z_ref[...]) + _lane_tile(b1_ref[...], tb)))
    for u in range(HYENA_INNER):
        hdn = jnp.sin(fr * (_dot3(w2h_ref[u], w2l_ref[u], hdn) + _lane_tile(b2_ref[u], tb)))
    hdn = hdn.T
    t = t_ref[...]
    tt = jnp.concatenate([t] * (HYENA_ORDER * HYENA_W // LANES), axis=1)
    window = jnp.exp(-tt * jnp.abs(dec_ref[...]))
    k = _dot3x(hdn, w3h_ref[...], w3l_ref[...]) * window
    rows = lax.broadcasted_iota(jnp.int32, k.shape, 0) + i * tb
    k = jnp.where(rows == n_seq, 0.0, k)
    for q in range(k_ref.shape[0]):
        k_ref[q] = k[:, q * LANES:(q + 1) * LANES]

    @pl.when(i == 0)
    def _():
        s_ref[...] = jnp.zeros(s_ref.shape, F32)

    s_ref[...] += jnp.sum(jnp.abs(k), axis=0, keepdims=True)


def _filt_call(n, lw):
    N = 2 * n
    tb = _tile(n, 512)
    nb_half = n // tb
    pos = np.arange(N)
    pos = np.where(pos < n, pos, N - pos).astype(np.float64)
    t = pos / (n - 1)
    bands = np.linspace(1e-4, HYENA_BANDS - 1, HYENA_BANDS)
    ang = (2.0 * math.pi / n) * pos[:, None] * bands
    z = np.concatenate([t[:, None], np.cos(ang), np.sin(ang)], axis=-1)
    zp = np.zeros((LANES, N), np.float32)
    zp[:z.shape[1]] = z.T
    tl = np.broadcast_to(t[:, None], (N, LANES)).astype(np.float32)
    CW = HYENA_ORDER * HYENA_W

    def full(shape):
        return pl.BlockSpec(shape, lambda i: (0,) * len(shape))

    def side(rows_):
        return pl.BlockSpec((None, rows_, CW), lambda i: (i // nb_half, 0, 0))

    return pl.pallas_call(
        functools.partial(_filt_kernel, n_seq=n),
        grid=(N // tb,),
        in_specs=[
            pl.BlockSpec((LANES, tb), lambda i: (0, i)),
            pl.BlockSpec((tb, LANES), lambda i: (i, 0)),
            full((HYENA_HIDDEN, LANES)), full((HYENA_HIDDEN, LANES)), full((HYENA_HIDDEN, LANES)),
            full((HYENA_INNER, HYENA_HIDDEN, HYENA_HIDDEN)), full((HYENA_INNER, HYENA_HIDDEN, HYENA_HIDDEN)),
            full((HYENA_INNER, HYENA_HIDDEN, LANES)), full((HYENA_HIDDEN, LANES)),
            side(HYENA_HIDDEN), side(HYENA_HIDDEN), side(1),
        ],
        out_specs=[pl.BlockSpec((CW // LANES, tb, LANES), lambda i: (0, i, 0)), pl.BlockSpec((1, CW), lambda i: (0, 0))],
        out_shape=[jax.ShapeDtypeStruct((CW // LANES, N, LANES), F32), jax.ShapeDtypeStruct((1, CW), F32)],
        compiler_params=_cparams(("arbitrary",)),
        name="filt",
    )(jnp.asarray(zp), jnp.asarray(tl), lw["hy_w1h"], lw["hy_w1l"], lw["hy_b1"], lw["hy_w2h"], lw["hy_w2l"],
      lw["hy_b2"], lw["hy_freq"], lw["hy_w3h"], lw["hy_w3l"], lw["hy_decay"])


def _fft_dims(N):
    e = int(round(math.log2(N)))
    assert 2 ** e == N
    N1 = 2 ** ((e + 1) // 2)
    return N1, N // N1


def _hilo(a):
    a = np.asarray(a, np.float64)
    hi = jnp.asarray(a, F32).astype(BF16)
    lo = (jnp.asarray(a, F32) - hi.astype(F32)).astype(BF16)
    return hi, lo


@functools.lru_cache(maxsize=None)
def _fft_consts_np(N):
    N1, N2 = _fft_dims(N)
    h = N1 // 2
    k1 = np.arange(N1)
    F1 = np.exp(-2j * np.pi * np.outer(k1, k1) / N1)
    k2 = np.arange(N2)
    F2 = np.exp(-2j * np.pi * np.outer(k2, k2) / N2)
    tw = np.exp(-2j * np.pi * np.outer(k1, k2) / N)
    g1 = np.zeros((2 * N1, N1))
    g1[0::2, :h] = F1.real[:, :h]
    g1[0::2, h:] = -F1.imag[:, :h]
    g1[1::2, :h] = F1.imag[:, :h]
    g1[1::2, h:] = F1.real[:, :h]
    g1f = np.zeros((2 * N1, N1))
    g1f[0::2] = F1.real
    g1f[1::2] = F1.imag
    g2 = np.block([[F2.real, -F2.imag], [F2.imag, F2.real]])
    g2c = np.block([[F2.real, F2.imag], [-F2.imag, F2.real]])
    g3 = np.zeros((N1, 2 * N1))
    g3[:h, 0::2] = F1.real[:h]
    g3[:h, 1::2] = F1.imag[:h]
    g3[h:, 0::2] = -F1.imag[:h]
    g3[h:, 1::2] = F1.real[:h]
    g3 /= N
    twt = np.stack([tw.real, tw.imag], axis=1)
    return dict(g1=g1, g1f=g1f, g2=g2, g2c=g2c, g3=g3, tw=twt)


def _fft_consts(N):
    c = _fft_consts_np(N)
    out = {k: _hilo(c[k]) for k in ("g1", "g1f", "g2", "g2c", "g3")}
    N1, N2 = _fft_dims(N)
    tw = jnp.asarray(c["tw"], F32)
    out["tw"] = jnp.broadcast_to(tw[..., None], (N1, 2, N2, LANES))
    return out


T2_BLOCK = 8


def _load_8th(ref, j, rows):
    return ref.reshape(rows * T2_BLOCK, LANES)[pl.ds(j, rows, stride=T2_BLOCK), :]


def _store_8th(ref, j, rows, val):
    ref.reshape(rows * T2_BLOCK, LANES)[pl.ds(j, rows, stride=T2_BLOCK), :] = val


def _dft_in_kernel(gh_ref, gl_ref, x_ref, o_ref):
    N1 = x_ref.shape[1]
    for j in range(T2_BLOCK):
        xj = jnp.concatenate([_load_8th(x_ref.at[hf], j, N1) for hf in range(HALVES)], axis=1)
        a = _dot3(gh_ref[...], gl_ref[...], xj)
        for hf in range(HALVES):
            _store_8th(o_ref.at[hf], j, 2 * N1, a[:, hf * LANES:(hf + 1) * LANES])


def _dft_in_call(g, u5):
    gh, gl = g
    G, _, N1, N2, _ = u5.shape
    assert gh.shape == (2 * N1, N1)
    mat = pl.BlockSpec((2 * N1, N1), lambda b, i: (0, 0))
    return pl.pallas_call(
        _dft_in_kernel,
        grid=(G, N2 // T2_BLOCK),
        in_specs=[mat, mat, pl.BlockSpec((None, HALVES, N1, T2_BLOCK, LANES), lambda b, i: (b, 0, 0, i, 0))],
        out_specs=pl.BlockSpec((None, HALVES, N1, 2, T2_BLOCK, LANES), lambda b, i: (b, 0, 0, 0, i, 0)),
        out_shape=jax.ShapeDtypeStruct((G, HALVES, N1, 2, N2, LANES), F32),
        compiler_params=_cparams(("parallel", "parallel")),
        name="dft_in",
    )(gh, gl, u5)


def _dft_out_kernel(gh_ref, gl_ref, d_ref, gate_ref, o_ref):
    N1 = gate_ref.shape[1]
    for j in range(T2_BLOCK):
        dj = jnp.concatenate([_load_8th(d_ref.at[hf], j, 2 * N1) for hf in range(HALVES)], axis=1)
        y = _dot3(gh_ref[...], gl_ref[...], dj)
        for hf in range(HALVES):
            conv = y[:, hf * LANES:(hf + 1) * LANES]
            _store_8th(o_ref.at[hf], j, N1, _load_8th(gate_ref.at[hf], j, N1) * conv)


def _dft_out_call(g, d6, gate5):
    gh, gl = g
    G, _, N1, N2, _ = gate5.shape
    assert gh.shape == (N1, 2 * N1)
    mat = pl.BlockSpec((N1, 2 * N1), lambda b, i: (0, 0))
    blk = pl.BlockSpec((None, HALVES, N1, T2_BLOCK, LANES), lambda b, i: (b, 0, 0, i, 0))
    return pl.pallas_call(
        _dft_out_kernel,
        grid=(G, N2 // T2_BLOCK),
        in_specs=[mat, mat, pl.BlockSpec((None, HALVES, N1, 2, T2_BLOCK, LANES), lambda b, i: (b, 0, 0, 0, i, 0)),
                  blk],
        out_specs=blk,
        out_shape=jax.ShapeDtypeStruct(gate5.shape, F32),
        compiler_params=_cparams(("parallel", "parallel")),
        name="dft_out",
    )(gh, gl, d6, gate5)


def _cmul(ar, ai, br, bi):
    return ar * br - ai * bi, ar * bi + ai * br


def _lane_tile(t, c):
    return jnp.concatenate([t] * (c // LANES), axis=-1)


def _both_halves(a_ref, k, part):
    return jnp.concatenate([a_ref[hf, k, part] for hf in range(HALVES)], axis=-1)


def _spec_kernel(a_ref, tw_ref, g2h_ref, g2l_ref, s_ref, sk_ref, o_ref):
    _, kb, _, N2, _ = a_ref.shape
    inv = 1.0 / s_ref[...]
    for k in range(kb):
        twr = _lane_tile(tw_ref[k, 0], HYENA_W)
        twi = _lane_tile(tw_ref[k, 1], HYENA_W)
        br, bi = _cmul(_both_halves(a_ref, k, 0), _both_halves(a_ref, k, 1), twr, twi)
        zz = _dot3(g2h_ref[...], g2l_ref[...], jnp.concatenate([br, bi], axis=0))
        o_ref[k, 0] = zz[:N2] * inv + sk_ref[...]
        o_ref[k, 1] = zz[N2:] * inv


def _conv_kernel(a_ref, tw_ref, kf_ref, g2h_ref, g2l_ref, g2ch_ref, g2cl_ref, o_ref):
    _, kb, _, N2, _ = a_ref.shape
    for k in range(kb):
        twr = _lane_tile(tw_ref[k, 0], HYENA_W)
        twi = _lane_tile(tw_ref[k, 1], HYENA_W)
        br, bi = _cmul(_both_halves(a_ref, k, 0), _both_halves(a_ref, k, 1), twr, twi)
        zz = _dot3(g2h_ref[...], g2l_ref[...], jnp.concatenate([br, bi], axis=0))
        yr, yi = _cmul(zz[:N2], zz[N2:], kf_ref[k, 0], kf_ref[k, 1])
        cc = _dot3(g2ch_ref[...], g2cl_ref[...], jnp.concatenate([yr, yi], axis=0))
        dr, di = _cmul(cc[:N2], cc[N2:], twr, -twi)
        for hf in range(HALVES):
            o_ref[hf, k, 0] = dr[:, hf * LANES:(hf + 1) * LANES]
            o_ref[hf, k, 1] = di[:, hf * LANES:(hf + 1) * LANES]


def _spec_call(a6, fc, s, skip):
    G, _, N1, _, N2, _ = a6.shape
    kb = _tile(N1, 8, 1)
    mat = pl.BlockSpec((2 * N2, 2 * N2), lambda i, b: (0, 0))
    return pl.pallas_call(
        _spec_kernel,
        grid=(N1 // kb, G),
        in_specs=[pl.BlockSpec((None, HALVES, kb, 2, N2, LANES), lambda i, b: (b, 0, i, 0, 0, 0)),
                  pl.BlockSpec((kb, 2, N2, LANES), lambda i, b: (i, 0, 0, 0)), mat, mat,
                  pl.BlockSpec((None, 1, HYENA_W), lambda i, b: (b, 0, 0)),
                  pl.BlockSpec((None, 1, HYENA_W), lambda i, b: (b, 0, 0))],
        out_specs=pl.BlockSpec((None, kb, 2, N2, HYENA_W), lambda i, b: (b, i, 0, 0, 0)),
        out_shape=jax.ShapeDtypeStruct((G, N1, 2, N2, HYENA_W), F32),
        compiler_params=_cparams(("parallel", "parallel")),
        name="filt_spec",
    )(a6, fc["tw"], fc["g2"][0], fc["g2"][1], s, skip)


def _conv_call(a6, kf, order, fc):
    G, _, N1, _, N2, _ = a6.shape
    kb = _tile(N1, 8, 1)
    blk = pl.BlockSpec((None, HALVES, kb, 2, N2, LANES), lambda i, b: (b, 0, i, 0, 0, 0))
    mat = pl.BlockSpec((2 * N2, 2 * N2), lambda i, b: (0, 0))
    return pl.pallas_call(
        _conv_kernel,
        grid=(N1 // kb, G),
        in_specs=[blk, pl.BlockSpec((kb, 2, N2, LANES), lambda i, b: (i, 0, 0, 0)),
                  pl.BlockSpec((None, kb, 2, N2, HYENA_W), lambda i, b: (order, i, 0, 0, 0)), mat, mat, mat, mat],
        out_specs=blk,
        out_shape=jax.ShapeDtypeStruct(a6.shape, F32),
        compiler_params=_cparams(("parallel", "parallel")),
        name="spec_conv",
    )(a6, fc["tw"], kf, fc["g2"][0], fc["g2"][1], fc["g2c"][0], fc["g2c"][1])


def _filter_spectra(n, lw, fc):
    N1, N2 = _fft_dims(2 * n)
    k, s = _filt_call(n, lw)
    a6 = _dft_in_call(fc["g1f"], k.reshape(HYENA_ORDER, HALVES, N1, N2, LANES))
    return _spec_call(a6, fc, s.reshape(HYENA_ORDER, 1, HYENA_W),
                      lw["hy_skip"].reshape(HYENA_ORDER, 1, HYENA_W))


def _hyena(v, x1, x2, kf, lw, fc):
    P, _, _, n, _ = v.shape
    N1, N2 = _fft_dims(2 * n)
    shape5 = (P, HALVES, N1, N2, LANES)
    y = v.reshape(shape5)
    for o, gate in enumerate((x1, x2)):
        d = _conv_call(_dft_in_call(fc["g1"], y), kf, o, fc)
        y = _dft_out_call(fc["g3"], d, gate.reshape(shape5))
    return y.reshape(v.shape)


def _merge_kernel(at_ref, hy_ref, po_ref, gm_ref, x_ref, sh_ref, sc_ref, nw_ref, gr_ref,
                  wgate_ref, wa_ref, wh_ref, wp_ref, wg_ref, wo_ref, o_ref):
    D = x_ref.shape[1]
    for rows in _row_blocks(x_ref.shape[0]):
        x = x_ref[rows, :]
        h = _modulated_norm(x, nw_ref[...], sc_ref[...], sh_ref[...]).astype(BF16)
        hy = jnp.concatenate([hy_ref[hf, rows, :] for hf in range(HALVES)], axis=-1).astype(BF16)
        branches = ((at_ref[rows, :], wa_ref), (hy, wh_ref), (po_ref[rows, :], wp_ref), (gm_ref[rows, :], wg_ref))
        merged = None
        for i, (br, w_ref) in enumerate(branches):
            term = _sigmoid(_dot(h, wgate_ref[:, i * D:(i + 1) * D])) * _dot(br, w_ref[...])
            merged = term if merged is None else merged + term
        o_ref[rows, :] = x + gr_ref[...] * _dot(merged.astype(BF16), wo_ref[...])


MERGE_ROWS = 1024


def _merge_call(attn, hy, po, gm, x, shift, scale, norm_w, gate_res, lw):
    B, n, D = x.shape
    tm = _tile(n, MERGE_ROWS)

    def rows(w):
        return pl.BlockSpec((None, tm, w), lambda b, i: (b, i, 0))

    def full(shape):
        return pl.BlockSpec(shape, lambda b, i: (0,) * len(shape))

    vec = pl.BlockSpec((None, 1, D), lambda b, i: (b, 0, 0))
    return pl.pallas_call(
        _merge_kernel,
        grid=(B, n // tm),
        in_specs=[
            rows(ATTN_W), _split_spec(tm), rows(POOL_W), rows(GMLP_W), rows(D),
            vec, vec, full((1, D)), vec,
            full((D, N_BRANCH * D)),
            full((ATTN_W, D)), full((HYENA_W, D)), full((POOL_W, D)), full((GMLP_W, D)), full((D, D)),
        ],
        out_specs=rows(D),
        out_shape=jax.ShapeDtypeStruct((B, n, D), F32),
        compiler_params=_cparams(("parallel", "parallel")),
        name="merge",
    )(attn, hy, po, gm, x, shift, scale, norm_w.reshape(1, D), gate_res, lw["w_gate"], lw["w_br_attn"],
      lw["w_br_hyena"], lw["w_br_pool"], lw["w_br_gmlp"], lw["w_out"])


FFN_ROWS = 1024


def _ffn_kernel(x_ref, sh_ref, sc_ref, nw_ref, gr_ref, wg_ref, wu_ref, wd_ref, fw_ref, o_ref, *, final):
    for rows in _row_blocks(x_ref.shape[0]):
        x = x_ref[rows, :]
        h = _modulated_norm(x, nw_ref[...], sc_ref[...], sh_ref[...]).astype(BF16)
        a = _silu(_dot(h, wg_ref[...])) * _dot(h, wu_ref[...])
        y = x + gr_ref[...] * _dot(a.astype(BF16), wd_ref[...])
        if final:
            y = y * lax.rsqrt(jnp.mean(y * y, axis=-1, keepdims=True) + RMS_EPS) * fw_ref[...]
        o_ref[rows, :] = y


def _ffn_call(x, shift, scale, norm_w, gate_res, wg, wu, wd, final_w=None):
    B, n, D = x.shape
    H = wg.shape[1]
    tm = _tile(n, FFN_ROWS)
    vec = pl.BlockSpec((None, 1, D), lambda b, i: (b, 0, 0))
    one = pl.BlockSpec((1, D), lambda b, i: (0, 0))
    final = final_w is not None
    fw = (final_w if final else norm_w).reshape(1, D)
    return pl.pallas_call(
        functools.partial(_ffn_kernel, final=final),
        grid=(B, n // tm),
        in_specs=[
            pl.BlockSpec((None, tm, D), lambda b, i: (b, i, 0)),
            vec, vec, one, vec,
            pl.BlockSpec((D, H), lambda b, i: (0, 0)),
            pl.BlockSpec((D, H), lambda b, i: (0, 0)),
            pl.BlockSpec((H, D), lambda b, i: (0, 0)),
            one,
        ],
        out_specs=pl.BlockSpec((None, tm, D), lambda b, i: (b, i, 0)),
        out_shape=jax.ShapeDtypeStruct((B, n, D), F32),
        compiler_params=_cparams(("parallel", "parallel")),
        name="ffn",
    )(x, shift, scale, norm_w.reshape(1, D), gate_res, wg, wu, wd, fw)


def _rope_tables(n):
    tok = np.arange(n)
    axis_dim = HEAD_DIM // 2
    inv_freq = ROPE_THETA ** (-np.arange(0, axis_dim, 2, dtype=np.float64) / axis_dim)
    inv_freq = inv_freq.astype(np.float32).astype(np.float64)
    ang = np.concatenate([(tok // GRID_W)[:, None] * inv_freq, (tok % GRID_W)[:, None] * inv_freq], axis=-1)
    ang = ang.astype(np.float32)
    cos = np.repeat(np.cos(ang), 2, axis=-1)
    sin = np.repeat(np.sin(ang), 2, axis=-1)
    sin[:, 0::2] *= -1.0
    reps = LANES // HEAD_DIM
    return jnp.asarray(np.tile(cos, (1, reps)), F32), jnp.asarray(np.tile(sin, (1, reps)), F32)


def _identity_tables(n):
    return jnp.ones((n, LANES), F32), jnp.zeros((n, LANES), F32)


def _layer_weights(l, a):
    w_in = a["w_in"][l]
    w_perm = jnp.concatenate([
        w_in[:, Q_OFF:K_OFF], w_in[:, K_OFF:HY_OFF], w_in[:, POOL_OFF:GM_OFF], w_in[:, GM_OFF:GATE_OFF],
        w_in[:, HY_OFF:POOL_OFF]], axis=1).astype(BF16)
    w3 = a["hy_w3"][l].reshape(HYENA_HIDDEN, HYENA_ORDER, 2, HYENA_W).transpose(2, 0, 1, 3)
    w3 = w3.reshape(2, HYENA_HIDDEN, HYENA_ORDER * HYENA_W)
    dec = a["hy_decay"][l].reshape(HYENA_ORDER, 2, HYENA_W).transpose(1, 0, 2).reshape(2, 1, HYENA_ORDER * HYENA_W)
    w1 = jnp.zeros((HYENA_HIDDEN, LANES), F32).at[:, :a["hy_w1"].shape[1]].set(a["hy_w1"][l].T)
    w1h, w1l = _split(w1)
    w2h, w2l = _split(jnp.swapaxes(a["hy_w2"][l], 1, 2))

    def lanes_of(v):
        return jnp.broadcast_to(v[..., None], v.shape + (LANES,))
    w3h, w3l = _split(w3)
    qw16 = jnp.tile(a["q_norm_w"][l], N_Q_HEADS).reshape(1, ATTN_W) * Q_SCALE
    kw16 = jnp.tile(a["k_norm_w"][l], N_KV_HEADS).reshape(1, KV_W)
    rms_q = jnp.sqrt(jnp.mean(qw16 * qw16))
    rms_k = jnp.sqrt(jnp.mean(kw16 * kw16))
    bal = jnp.exp2(jnp.clip(jnp.round(0.5 * jnp.log2(rms_k / rms_q)), -20.0, 20.0))
    bal = jnp.where(jnp.isfinite(bal), bal, 1.0)
    fp8_ok = (jnp.max(jnp.abs(qw16 * bal)) <= FP8_MAX_WEIGHT) & (jnp.max(jnp.abs(kw16 / bal)) <= FP8_MAX_WEIGHT)
    eye = jnp.eye(len(POOL_WINDOWS), dtype=F32)
    pool_bd = jnp.einsum("gh,gcd->gchd", eye, a["pool_w"][l]).reshape(POOL_W, POOL_W).astype(BF16)
    gm_bias = jnp.repeat(a["gm_bs"][l].T, GMLP_GROUP_W, axis=1)
    return dict(
        w_in=w_perm, w_gate=w_in[:, GATE_OFF:].astype(BF16),
        qw16=qw16, kw16=kw16, qw8=qw16 * bal, kw8=kw16 / bal, fp8_ok=fp8_ok,
        conv_w=a["hy_conv_w"][l], conv_b=a["hy_conv_b"][l].reshape(1, -1),
        hy_w1h=w1h, hy_w1l=w1l, hy_b1=lanes_of(a["hy_b1"][l]), hy_w2h=w2h, hy_w2l=w2l, hy_b2=lanes_of(a["hy_b2"][l]),
        hy_w3h=w3h, hy_w3l=w3l, hy_freq=lanes_of(a["hy_freq"][l]), hy_decay=dec, hy_skip=a["hy_skip"][l],
        pool_bd=pool_bd, pool_scale=a["pool_scale"][l].reshape(1, -1),
        gm_norm_w=a["gm_norm_w"][l].reshape(1, -1), gm_ws=a["gm_ws"][l].astype(BF16), gm_bias=gm_bias,
        w_br_attn=a["w_br_attn"][l].astype(BF16), w_br_hyena=a["w_br_hyena"][l].astype(BF16),
        w_br_pool=a["w_br_pool"][l].astype(BF16), w_br_gmlp=a["w_br_gmlp"][l].astype(BF16),
        w_out=a["w_out"][l].astype(BF16),
        ffn_w_gate=a["ffn_w_gate"][l].astype(BF16), ffn_w_up=a["ffn_w_up"][l].astype(BF16),
        ffn_w_down=a["ffn_w_down"][l].astype(BF16),
        norm1_w=a["norm1_w"][l], norm2_w=a["norm2_w"][l],
    )


Q_SCALE = (HEAD_DIM ** -0.5) * math.log2(math.e)


def _attend(p, p_ctx, rope, rope_ctx, lw):
    def run(fp8):
        qw, kw = (lw["qw8"], lw["kw8"]) if fp8 else (lw["qw16"], lw["kw16"])
        qt, k, vt = _qkprep_call(p, rope[0], rope[1], qw, kw, fp8)
        if p_ctx is not None:
            _, k_c, vt_c = _qkprep_call(p_ctx, rope_ctx[0], rope_ctx[1], qw, kw, fp8)
            k = jnp.concatenate([k_c, k], axis=2)
            vt = jnp.concatenate([vt_c, vt], axis=3)
        return _attn_call(qt, k, vt)

    return lax.cond(lw["fp8_ok"], lambda: run(True), lambda: run(False))


def _mixer_block(x, mods, lw, rope, p_ctx, rope_ctx, kf, fc, final_w=None):
    sh1, s1, g1, sh2, s2, g2 = mods
    p = _inproj_call(x, sh1, s1, lw["norm1_w"], lw["w_in"])
    attn = _attend(p, p_ctx, rope, rope_ctx, lw)
    hv, hx1, hx2, po, gm = _local_call(p, lw)
    hy = _hyena(hv, hx1, hx2, kf, lw, fc)
    x = _merge_call(attn, hy, po, gm, x, sh1, s1, lw["norm1_w"], g1, lw)
    x = _ffn_call(x, sh2, s2, lw["norm2_w"], g2, lw["ffn_w_gate"], lw["ffn_w_up"], lw["ffn_w_down"], final_w)
    return x, p


def kernel(x, c, ctx, c_ctx, w_mod, b_mod, norm1_w, norm2_w, w_in, q_norm_w, k_norm_w, hy_conv_w, hy_conv_b,
           hy_w1, hy_b1, hy_w2, hy_b2, hy_w3, hy_freq, hy_decay, hy_skip, pool_w, pool_scale, gm_norm_w, gm_ws,
           gm_bs, w_br_attn, w_br_hyena, w_br_pool, w_br_gmlp, w_out, ffn_w_gate, ffn_w_up, ffn_w_down,
           final_norm_w):
    a = dict(w_in=w_in, q_norm_w=q_norm_w, k_norm_w=k_norm_w, hy_conv_w=hy_conv_w, hy_conv_b=hy_conv_b,
             hy_w1=hy_w1, hy_b1=hy_b1, hy_w2=hy_w2, hy_b2=hy_b2, hy_w3=hy_w3, hy_freq=hy_freq, hy_decay=hy_decay,
             hy_skip=hy_skip, pool_w=pool_w, pool_scale=pool_scale, gm_norm_w=gm_norm_w, gm_ws=gm_ws, gm_bs=gm_bs,
             w_br_attn=w_br_attn, w_br_hyena=w_br_hyena, w_br_pool=w_br_pool, w_br_gmlp=w_br_gmlp, w_out=w_out,
             ffn_w_gate=ffn_w_gate, ffn_w_up=ffn_w_up, ffn_w_down=ffn_w_down, norm1_w=norm1_w, norm2_w=norm2_w)
    B, n, D = x.shape
    nc = ctx.shape[1]
    depth = w_mod.shape[0]

    R = ((B + 1 + 15) // 16) * 16
    act = jnp.zeros((R, D), F32).at[:B].set(c).at[B].set(c_ctx)
    mod = _mod_call(act, w_mod, b_mod)

    rope = _rope_tables(n)
    rope_ctx = _identity_tables(nc)
    fc = _fft_consts(2 * n)
    fc_ctx = _fft_consts(2 * nc)

    xc = ctx
    for l in range(depth):
        lw = _layer_weights(l, a)
        mods = [m.reshape(B, 1, D) for m in jnp.split(mod[l, :B], 6, axis=-1)]
        cmods = [jnp.broadcast_to(m.reshape(1, 1, D), (B, 1, D)) for m in jnp.split(mod[l, B], 6, axis=-1)]
        if l < depth - 1:
            kf_ctx = _filter_spectra(nc, lw, fc_ctx)
            xc, pc = _mixer_block(xc, cmods, lw, rope_ctx, None, None, kf_ctx, fc_ctx)
        else:
            pc = _inproj_call(xc, cmods[0], cmods[1], lw["norm1_w"], lw["w_in"][:, :P_QKV])
        kf = _filter_spectra(n, lw, fc)
        x, _ = _mixer_block(x, mods, lw, rope, pc, rope_ctx, kf, fc, final_norm_w if l == depth - 1 else None)
    return x
```

```python
import functools
import math

import numpy as np
import jax
import jax.numpy as jnp
from jax import lax
from jax.experimental import pallas as pl
from jax.experimental.pallas import tpu as pltpu

F32 = jnp.float32
BF16 = jnp.bfloat16

D_MODEL = 1024
GRID_W = 64
RMS_EPS = 1e-6
HEAD_DIM = 64
N_Q_HEADS = 8
N_KV_HEADS = 2
ATTN_W = N_Q_HEADS * HEAD_DIM
KV_W = N_KV_HEADS * HEAD_DIM
ROPE_THETA = 10000.0
HYENA_W = 256
HYENA_ORDER = 2
HYENA_BANDS = 16
HYENA_HIDDEN = 64
HYENA_INNER = 2
POOL_W = 256
POOL_WINDOWS = (2, 4, 8, 16)
POOL_GROUP_W = POOL_W // len(POOL_WINDOWS)
POOL_HALO = 8
GMLP_W = 256
GMLP_CHUNK = 128
GMLP_GROUPS = 4
GMLP_GROUP_W = GMLP_W // GMLP_GROUPS
N_BRANCH = 4

Q_OFF = 0
K_OFF = Q_OFF + ATTN_W
V_OFF = K_OFF + KV_W
HY_OFF = V_OFF + KV_W
POOL_OFF = HY_OFF + 3 * HYENA_W
GM_OFF = POOL_OFF + POOL_W
GATE_OFF = GM_OFF + 2 * GMLP_W

P_Q = 0
P_KV = P_Q + ATTN_W
P_POOL = P_KV + 2 * KV_W
P_GMU = P_POOL + POOL_W
P_GMV = P_GMU + GMLP_W
P_HY = P_GMV + GMLP_W
P_W = P_HY + 3 * HYENA_W
P_QKV = P_POOL

LANES = 128
VMEM_LIMIT = 52 * 1024 * 1024


def _cparams(sem):
    return pltpu.CompilerParams(dimension_semantics=sem, vmem_limit_bytes=VMEM_LIMIT)


def _tile(n, cap, mult=8):
    best = None
    for t in range(mult, min(n, cap) + 1, mult):
        if n % t == 0:
            best = t
    assert best is not None, (n, cap, mult)
    return best


def _split(x):
    hi = x.astype(BF16)
    lo = (x - hi.astype(F32)).astype(BF16)
    return hi, lo


def _dot(a, b):
    return jnp.dot(a, b, preferred_element_type=F32)


def _dot3(a_hi, a_lo, x):
    x_hi, x_lo = _split(x)
    return _dot(a_hi, x_hi) + (_dot(a_lo, x_hi) + _dot(a_hi, x_lo))


def _dot3x(x, b_hi, b_lo):
    x_hi, x_lo = _split(x)
    return _dot(x_hi, b_hi) + (_dot(x_lo, b_hi) + _dot(x_hi, b_lo))


def _sigmoid(x):
    return 1.0 / (1.0 + jnp.exp(-x))


def _silu(x):
    return x * _sigmoid(x)


def _modulated_norm(x, nw, scale, shift):
    ms = jnp.mean(x * x, axis=-1, keepdims=True)
    y = x * lax.rsqrt(ms + RMS_EPS) * nw
    return y * (1.0 + scale) + shift


def _mod_kernel(a_ref, w_ref, b_ref, o_ref):
    a = _silu(a_ref[...]).astype(BF16)
    o_ref[...] = _dot(a, w_ref[...].astype(BF16)) + b_ref[...]


def _mod_call(act, w_mod, b_mod):
    L, D, W6 = w_mod.shape
    R = act.shape[0]
    tn = 1024
    return pl.pallas_call(
        _mod_kernel,
        grid=(L, W6 // tn),
        in_specs=[
            pl.BlockSpec((R, D), lambda l, j: (0, 0)),
            pl.BlockSpec((None, D, tn), lambda l, j: (l, 0, j)),
            pl.BlockSpec((None, 1, tn), lambda l, j: (l, 0, j)),
        ],
        out_specs=pl.BlockSpec((None, R, tn), lambda l, j: (l, 0, j)),
        out_shape=jax.ShapeDtypeStruct((L, R, W6), F32),
        compiler_params=_cparams(("arbitrary", "arbitrary")),
        name="mod",
    )(act, w_mod, b_mod.reshape(L, 1, W6))


ROW_BLOCK = 256
INPROJ_ROWS = 1024


def _row_blocks(tm):
    sub = ROW_BLOCK if tm % ROW_BLOCK == 0 else tm
    return [slice(r, r + sub) for r in range(0, tm, sub)]


def _inproj_kernel(x_ref, sh_ref, sc_ref, nw_ref, w_ref, o_ref):
    for rows in _row_blocks(x_ref.shape[0]):
        h = _modulated_norm(x_ref[rows, :], nw_ref[...], sc_ref[...], sh_ref[...]).astype(BF16)
        o_ref[rows, :] = _dot(h, w_ref[...])


def _inproj_call(x, shift, scale, norm_w, w):
    B, n, D = x.shape
    Nw = w.shape[1]
    tm = _tile(n, INPROJ_ROWS)
    return pl.pallas_call(
        _inproj_kernel,
        grid=(B, n // tm),
        in_specs=[
            pl.BlockSpec((None, tm, D), lambda b, i: (b, i, 0)),
            pl.BlockSpec((None, 1, D), lambda b, i: (b, 0, 0)),
            pl.BlockSpec((None, 1, D), lambda b, i: (b, 0, 0)),
            pl.BlockSpec((1, D), lambda b, i: (0, 0)),
            pl.BlockSpec((D, Nw), lambda b, i: (0, 0)),
        ],
        out_specs=pl.BlockSpec((None, tm, Nw), lambda b, i: (b, i, 0)),
        out_shape=jax.ShapeDtypeStruct((B, n, Nw), F32),
        compiler_params=_cparams(("parallel", "parallel")),
        name="inproj",
    )(x, shift, scale, norm_w.reshape(1, D), w)


def _head_norm_rope(t, w, cs, sn, bd_ref):
    sq_hi, sq_lo = _split(t * t)
    ms = _dot(sq_hi, bd_ref[...]) + _dot(sq_lo, bd_ref[...])
    y = t * lax.rsqrt(ms + RMS_EPS) * w
    lane = lax.broadcasted_iota(jnp.int32, y.shape, 1)
    swapped = jnp.where(lane % 2 == 0, pltpu.roll(y, LANES - 1, 1), pltpu.roll(y, 1, 1))
    return y * cs + swapped * sn


V_ROWS = 80
Q_PER_KV = N_Q_HEADS // N_KV_HEADS
LAG_INIT_KEYS = 128
LAG_LIMIT = 12.0
ATTN_QUERIES = 512
ATTN_KEYS = 768
ATTN_UNROLL = 2


def _value_rows(vt_g):
    tail = lax.broadcasted_iota(jnp.int32, (V_ROWS - HEAD_DIM, vt_g.shape[1]), 0)
    return jnp.concatenate([vt_g, jnp.where(tail == 0, 1.0, 0.0)], axis=0).astype(BF16)


FP8 = jnp.float8_e4m3fn
FP8_MAX_WEIGHT = 32.0


def _split8(x):
    hi = x.astype(FP8).astype(F32)
    return hi, (x - hi).astype(FP8).astype(F32)


def _qkprep_kernel(q_ref, kv_ref, cs_ref, sn_ref, qw_ref, kw_ref, bd_ref, kp_ref, qo_ref, ko_ref, vo_ref, *, fp8):
    cs = cs_ref[...]
    sn = sn_ref[...]
    tm = q_ref.shape[0]
    zeros = jnp.zeros((HEAD_DIM, tm), F32)
    heads_per_chunk = LANES // HEAD_DIM
    for c in range(ATTN_W // LANES):
        sl = slice(c * LANES, (c + 1) * LANES)
        r = _head_norm_rope(q_ref[:, sl], qw_ref[:, sl], cs, sn, bd_ref)
        if fp8:
            hi_t, lo_t = _split8(r.T)
            for half in range(heads_per_chunk):
                rows = slice(half * HEAD_DIM, (half + 1) * HEAD_DIM)
                qo_ref[c * heads_per_chunk + half] = jnp.concatenate(
                    [hi_t[rows], lo_t[rows], hi_t[rows], lo_t[rows]], axis=0).astype(FP8)
        else:
            rt = r.T
            for half in range(heads_per_chunk):
                h = c * heads_per_chunk + half
                blk = rt[half * HEAD_DIM:(half + 1) * HEAD_DIM]
                parts = [blk if g == h // Q_PER_KV else zeros for g in range(N_KV_HEADS)]
                qo_ref[h] = jnp.concatenate(parts, axis=0).astype(BF16)
    k = _head_norm_rope(kv_ref[:, :KV_W], kw_ref[...], cs, sn, bd_ref)
    if fp8:
        hi, lo = _split8(k)
        hi, lo = hi.astype(BF16), lo.astype(BF16)
        for g in range(N_KV_HEADS):
            ko_ref[g] = (_dot(hi, kp_ref[g, 0]) + _dot(lo, kp_ref[g, 1])).astype(FP8)
    else:
        ko_ref[0] = k.astype(BF16)
    vt = kv_ref[:, KV_W:].T
    for g in range(N_KV_HEADS):
        vo_ref[g] = _value_rows(vt[g * HEAD_DIM:(g + 1) * HEAD_DIM])


def _k_placement():
    pm = np.zeros((N_KV_HEADS, 2, KV_W, 4 * HEAD_DIM), np.float32)
    d = np.arange(HEAD_DIM)
    for g in range(N_KV_HEADS):
        for part in range(2):
            for rep in range(2):
                pm[g, part, g * HEAD_DIM + d, (2 * part + rep) * HEAD_DIM + d] = 1.0
    return jnp.asarray(pm, BF16)


def _qkprep_call(p, cs, sn, qw, kw, fp8):
    B, n, _ = p.shape
    tm = _tile(n, 512, LANES)
    bd = np.kron(np.eye(LANES // HEAD_DIM), np.full((HEAD_DIM, HEAD_DIM), 1.0 / HEAD_DIM))
    bd = jnp.asarray(bd, BF16)
    kd, kg, dt = (4 * HEAD_DIM, N_KV_HEADS, FP8) if fp8 else (KV_W, 1, BF16)
    return pl.pallas_call(
        functools.partial(_qkprep_kernel, fp8=fp8),
        grid=(n // tm, B),
        in_specs=[
            pl.BlockSpec((None, tm, ATTN_W), lambda i, b: (b, i, P_Q // ATTN_W)),
            pl.BlockSpec((None, tm, 2 * KV_W), lambda i, b: (b, i, P_KV // (2 * KV_W))),
            pl.BlockSpec((tm, LANES), lambda i, b: (i, 0)),
            pl.BlockSpec((tm, LANES), lambda i, b: (i, 0)),
            pl.BlockSpec((1, ATTN_W), lambda i, b: (0, 0)),
            pl.BlockSpec((1, KV_W), lambda i, b: (0, 0)),
            pl.BlockSpec((LANES, LANES), lambda i, b: (0, 0)),
            pl.BlockSpec((N_KV_HEADS, 2, KV_W, 4 * HEAD_DIM), lambda i, b: (0, 0, 0, 0)),
        ],
        out_specs=[
            pl.BlockSpec((None, N_Q_HEADS, kd, tm), lambda i, b: (b, 0, 0, i)),
            pl.BlockSpec((None, kg, tm, kd), lambda i, b: (b, 0, i, 0)),
            pl.BlockSpec((None, N_KV_HEADS, V_ROWS, tm), lambda i, b: (b, 0, 0, i)),
        ],
        out_shape=[
            jax.ShapeDtypeStruct((B, N_Q_HEADS, kd, n), dt),
            jax.ShapeDtypeStruct((B, kg, n, kd), dt),
            jax.ShapeDtypeStruct((B, N_KV_HEADS, V_ROWS, n), BF16),
        ],
        compiler_params=_cparams(("parallel", "parallel")),
        name="qkprep",
    )(p, p, cs, sn, qw, kw, bd, _k_placement())


def _attn_kernel(qt_ref, k_ref, vt_ref, o_ref, m_ref, acc_ref, *, tk):
    nk = vt_ref.shape[0]
    tq = m_ref.shape[-1]
    per_kv_head = k_ref.shape[0] > 1

    def keys(j, h):
        g = h // Q_PER_KV if per_kv_head else 0
        return k_ref[g, pl.ds(pl.multiple_of(j * tk, tk), tk), :]

    def lagged_chunk(j, jump):
        s_next = _dot(keys(j, 0), qt_ref[0])
        for h in range(N_Q_HEADS):
            s = s_next
            if h + 1 < N_Q_HEADS:
                s_next = _dot(keys(j, h + 1), qt_ref[h + 1])
            m_old = m_ref[h]
            p = jnp.exp2((s - m_old).astype(BF16))
            cmax = jnp.max(s, axis=0, keepdims=True)
            m_new = jnp.maximum(m_old, cmax)
            jump = jnp.maximum(jump, cmax - m_old)
            acc_ref[h] = jnp.exp2(m_old - m_new) * (acc_ref[h] + _dot(vt_ref[j, h // Q_PER_KV], p))
            m_ref[h] = m_new
        return jump

    def exact_chunk(j, carry):
        for h in range(N_Q_HEADS):
            s = _dot(keys(j, h), qt_ref[h])
            m_old = m_ref[h]
            m_new = jnp.maximum(m_old, jnp.max(s, axis=0, keepdims=True))
            p = jnp.exp2(s - m_new).astype(BF16)
            acc_ref[h] = jnp.exp2(m_old - m_new) * acc_ref[h] + _dot(vt_ref[j, h // Q_PER_KV], p)
            m_ref[h] = m_new
        return carry

    k_lo = (k_ref.shape[1] // 2) // LAG_INIT_KEYS * LAG_INIT_KEYS
    for h in range(N_Q_HEADS):
        k0 = k_ref[h // Q_PER_KV if per_kv_head else 0, k_lo:k_lo + LAG_INIT_KEYS, :]
        m_ref[h] = jnp.max(_dot(k0, qt_ref[h]), axis=0, keepdims=True)
    acc_ref[...] = jnp.zeros(acc_ref.shape, F32)
    jump = lax.fori_loop(0, nk, lagged_chunk, jnp.zeros((1, tq), F32), unroll=ATTN_UNROLL)

    @pl.when(jnp.logical_not(jnp.max(jump) <= LAG_LIMIT))
    def _():
        m_ref[...] = jnp.full(m_ref.shape, -jnp.inf, F32)
        acc_ref[...] = jnp.zeros(acc_ref.shape, F32)
        lax.fori_loop(0, nk, exact_chunk, 0)

    for c in range(ATTN_W // LANES):
        heads = []
        for h in range(c * (LANES // HEAD_DIM), (c + 1) * (LANES // HEAD_DIM)):
            a = acc_ref[h]
            heads.append(a[:HEAD_DIM] / a[HEAD_DIM:HEAD_DIM + 1])
        o_ref[:, c * LANES:(c + 1) * LANES] = jnp.concatenate(heads, axis=0).T.astype(BF16)


def _attn_call(qt, k, vt):
    B, _, kd, n = qt.shape
    _, kg, Lk, _ = k.shape
    tq = _tile(n, ATTN_QUERIES, LANES)
    tk = _tile(Lk, ATTN_KEYS, LANES)
    nk = Lk // tk
    vt = vt.reshape(B, N_KV_HEADS, V_ROWS, nk, tk).transpose(0, 3, 1, 2, 4)
    return pl.pallas_call(
        functools.partial(_attn_kernel, tk=tk),
        grid=(B, n // tq),
        in_specs=[
            pl.BlockSpec((None, N_Q_HEADS, kd, tq), lambda b, i: (b, 0, 0, i)),
            pl.BlockSpec((None, kg, Lk, kd), lambda b, i: (b, 0, 0, 0)),
            pl.BlockSpec((None, nk, N_KV_HEADS, V_ROWS, tk), lambda b, i: (b, 0, 0, 0, 0)),
        ],
        out_specs=pl.BlockSpec((None, tq, ATTN_W), lambda b, i: (b, i, 0)),
        out_shape=jax.ShapeDtypeStruct((B, n, ATTN_W), BF16),
        scratch_shapes=[
            pltpu.VMEM((N_Q_HEADS, 1, tq), F32),
            pltpu.VMEM((N_Q_HEADS, V_ROWS, tq), F32),
        ],
        compiler_params=_cparams(("parallel", "parallel")),
        name="attn",
    )(qt, k, vt)


HALVES = HYENA_W // LANES


def _split_shape(B, n):
    assert B % 2 == 0
    return jax.ShapeDtypeStruct((B // 2, HALVES, 2, n, LANES), F32)


def _split_spec(tm):
    return pl.BlockSpec((None, HALVES, None, tm, LANES), lambda b, i: (b // 2, 0, b % 2, i, 0))


def _shift_rows(cur, prev_row, next_row, rows):
    tm = cur.shape[0]
    up = jnp.where(rows == 0, prev_row, pltpu.roll(cur, 1, 0))
    dn = jnp.where(rows == tm - 1, next_row, pltpu.roll(cur, tm - 1, 0))
    return up, dn


def _local_kernel(hy_ref, hyp_ref, hyn_ref, po_ref, pop_ref, pon_ref, gu_ref, gv_ref,
                  cw_ref, cb_ref, pw_ref, ps_ref, gnw_ref, gws_ref, gb_ref,
                  v_ref, x1_ref, x2_ref, pl_ref, gm_ref, *, n_seq):
    i = pl.program_id(1)
    last = pl.num_programs(1) - 1
    tm = hy_ref.shape[0]
    has_prev = (i > 0).astype(F32)
    has_next = (i < last).astype(F32)

    cur = hy_ref[...]
    prev_row = hyp_ref[POOL_HALO - 1:POOL_HALO, :] * has_prev
    next_row = hyn_ref[0:1, :] * has_next
    rows = lax.broadcasted_iota(jnp.int32, cur.shape, 0)
    up, dn = _shift_rows(cur, prev_row, next_row, rows)
    z = up * cw_ref[0:1, :] + cur * cw_ref[1:2, :] + dn * cw_ref[2:3, :] + cb_ref[...]
    for s, o_ref in enumerate((v_ref, x1_ref, x2_ref)):
        for hf in range(HYENA_W // LANES):
            lo = s * HYENA_W + hf * LANES
            o_ref[hf] = z[:, lo:lo + LANES]

    zc = po_ref[...]
    ext = jnp.concatenate([pop_ref[...] * has_prev, zc, pon_ref[...] * has_next], axis=0)
    te = tm + 2 * POOL_HALO
    lane = lax.broadcasted_iota(jnp.int32, zc.shape, 1)
    prow = lax.broadcasted_iota(jnp.int32, zc.shape, 0) + i * tm
    half = jnp.left_shift(1, lane // POOL_GROUP_W)
    def ahead(a, k):
        return pltpu.roll(a, (te - k) % te, 0)

    assert POOL_WINDOWS == (2, 4, 8, 16)
    sums = [ext + ahead(ext, -1)]
    for win in POOL_WINDOWS[:-1]:
        sums.append(ahead(sums[-1], win // 2) + ahead(sums[-1], -(win // 2)))
    group = lane // POOL_GROUP_W
    acc = sums[-1][POOL_HALO:POOL_HALO + tm]
    for g in range(len(POOL_WINDOWS) - 2, -1, -1):
        acc = jnp.where(group == g, sums[g][POOL_HALO:POOL_HALO + tm], acc)
    cnt = jnp.minimum(prow + half, n_seq) - jnp.maximum(prow - half, 0)
    dlt = acc / cnt.astype(F32) - zc
    pl_ref[...] = (_dot(dlt.astype(BF16), pw_ref[...]) * ps_ref[...]).astype(BF16)

    gv = gv_ref[...]
    mu = jnp.mean(gv, axis=-1, keepdims=True)
    ctr = gv - mu
    var = jnp.mean(ctr * ctr, axis=-1, keepdims=True)
    vn = (ctr * lax.rsqrt(var + RMS_EPS) * gnw_ref[...]).astype(BF16)
    group = lax.broadcasted_iota(jnp.int32, (GMLP_CHUNK, GMLP_W), 1) // GMLP_GROUP_W
    for c in range(tm // GMLP_CHUNK):
        rs = slice(c * GMLP_CHUNK, (c + 1) * GMLP_CHUNK)
        vc = vn[rs]
        mixed = gb_ref[...]
        for g in range(GMLP_GROUPS):
            mg = _dot(gws_ref[g], vc)
            mixed = mixed + jnp.where(group == g, mg, 0.0)
        gm_ref[rs, :] = (gu_ref[rs, :] * mixed).astype(BF16)


def _local_call(p, lw):
    B, n, _ = p.shape
    tm = _tile(n, 512, GMLP_CHUNK)
    nb8 = n // POOL_HALO
    r8 = tm // POOL_HALO
    HW = 3 * HYENA_W

    def cur(w, off):
        return pl.BlockSpec((None, tm, w), lambda b, i: (b, i, off // w))

    def prev(w, off):
        return pl.BlockSpec((None, POOL_HALO, w), lambda b, i: (b, jnp.maximum(i * r8 - 1, 0), off // w))

    def nxt(w, off):
        return pl.BlockSpec((None, POOL_HALO, w), lambda b, i: (b, jnp.minimum((i + 1) * r8, nb8 - 1), off // w))

    def full(shape):
        return pl.BlockSpec(shape, lambda b, i: (0,) * len(shape))

    def out(w):
        return pl.BlockSpec((None, tm, w), lambda b, i: (b, i, 0))

    return pl.pallas_call(
        functools.partial(_local_kernel, n_seq=n),
        grid=(B, n // tm),
        in_specs=[
            cur(HW, P_HY), prev(HW, P_HY), nxt(HW, P_HY),
            cur(POOL_W, P_POOL), prev(POOL_W, P_POOL), nxt(POOL_W, P_POOL),
            cur(GMLP_W, P_GMU), cur(GMLP_W, P_GMV),
            full((3, HW)), full((1, HW)), full((POOL_W, POOL_W)), full((1, POOL_W)),
            full((1, GMLP_W)), full((GMLP_GROUPS, GMLP_CHUNK, GMLP_CHUNK)), full((GMLP_CHUNK, GMLP_W)),
        ],
        out_specs=[_split_spec(tm), _split_spec(tm), _split_spec(tm), out(POOL_W), out(GMLP_W)],
        out_shape=[
            _split_shape(B, n), _split_shape(B, n), _split_shape(B, n),
            jax.ShapeDtypeStruct((B, n, POOL_W), BF16),
            jax.ShapeDtypeStruct((B, n, GMLP_W), BF16),
        ],
        compiler_params=_cparams(("parallel", "parallel")),
        name="local",
    )(p, p, p, p, p, p, p, p, lw["conv_w"], lw["conv_b"], lw["pool_bd"], lw["pool_scale"],
      lw["gm_norm_w"], lw["gm_ws"], lw["gm_bias"])


def _filt_kernel(z_ref, t_ref, w1h_ref, w1l_ref, b1_ref, w2h_ref, w2l_ref, b2_ref, fr_ref,
                 w3h_ref, w3l_ref, dec_ref, k_ref, s_ref, *, n_seq):
    i = pl.program_id(0)
    tb = z_ref.shape[1]
    fr = _lane_tile(fr_ref[...], tb)
    hdn = jnp.sin(fr * (_dot3(w1h_ref[...], w1l_ref[...], z_ref[...]) + _lane_tile(b1_ref[...], tb)))
    for u in range(HYENA_INNER):
        hdn = jnp.sin(fr * (_dot3(w2h_ref[u], w2l_ref[u], hdn) + _lane_tile(b2_ref[u], tb)))
    hdn = hdn.T
    t = t_ref[...]
    tt = jnp.concatenate([t] * (HYENA_ORDER * HYENA_W // LANES), axis=1)
    window = jnp.exp(-tt * jnp.abs(dec_ref[...]))
    k = _dot3x(hdn, w3h_ref[...], w3l_ref[...]) * window
    rows = lax.broadcasted_iota(jnp.int32, k.shape, 0) + i * tb
    k = jnp.where(rows == n_seq, 0.0, k)
    for q in range(k_ref.shape[0]):
        k_ref[q] = k[:, q * LANES:(q + 1) * LANES]

    @pl.when(i == 0)
    def _():
        s_ref[...] = jnp.zeros(s_ref.shape, F32)

    s_ref[...] += jnp.sum(jnp.abs(k), axis=0, keepdims=True)


def _filt_call(n, lw):
    N = 2 * n
    tb = _tile(n, 512)
    nb_half = n // tb
    pos = np.arange(N)
    pos = np.where(pos < n, pos, N - pos).astype(np.float64)
    t = pos / (n - 1)
    bands = np.linspace(1e-4, HYENA_BANDS - 1, HYENA_BANDS)
    ang = (2.0 * math.pi / n) * pos[:, None] * bands
    z = np.concatenate([t[:, None], np.cos(ang), np.sin(ang)], axis=-1)
    zp = np.zeros((LANES, N), np.float32)
    zp[:z.shape[1]] = z.T
    tl = np.broadcast_to(t[:, None], (N, LANES)).astype(np.float32)
    CW = HYENA_ORDER * HYENA_W

    def full(shape):
        return pl.BlockSpec(shape, lambda i: (0,) * len(shape))

    def side(rows_):
        return pl.BlockSpec((None, rows_, CW), lambda i: (i // nb_half, 0, 0))

    return pl.pallas_call(
        functools.partial(_filt_kernel, n_seq=n),
        grid=(N // tb,),
        in_specs=[
            pl.BlockSpec((LANES, tb), lambda i: (0, i)),
            pl.BlockSpec((tb, LANES), lambda i: (i, 0)),
            full((HYENA_HIDDEN, LANES)), full((HYENA_HIDDEN, LANES)), full((HYENA_HIDDEN, LANES)),
            full((HYENA_INNER, HYENA_HIDDEN, HYENA_HIDDEN)), full((HYENA_INNER, HYENA_HIDDEN, HYENA_HIDDEN)),
            full((HYENA_INNER, HYENA_HIDDEN, LANES)), full((HYENA_HIDDEN, LANES)),
            side(HYENA_HIDDEN), side(HYENA_HIDDEN), side(1),
        ],
        out_specs=[pl.BlockSpec((CW // LANES, tb, LANES), lambda i: (0, i, 0)), pl.BlockSpec((1, CW), lambda i: (0, 0))],
        out_shape=[jax.ShapeDtypeStruct((CW // LANES, N, LANES), F32), jax.ShapeDtypeStruct((1, CW), F32)],
        compiler_params=_cparams(("arbitrary",)),
        name="filt",
    )(jnp.asarray(zp), jnp.asarray(tl), lw["hy_w1h"], lw["hy_w1l"], lw["hy_b1"], lw["hy_w2h"], lw["hy_w2l"],
      lw["hy_b2"], lw["hy_freq"], lw["hy_w3h"], lw["hy_w3l"], lw["hy_decay"])


def _fft_dims(N):
    e = int(round(math.log2(N)))
    assert 2 ** e == N
    N1 = 2 ** ((e + 1) // 2)
    return N1, N // N1


def _hilo(a):
    a = np.asarray(a, np.float64)
    hi = jnp.asarray(a, F32).astype(BF16)
    lo = (jnp.asarray(a, F32) - hi.astype(F32)).astype(BF16)
    return hi, lo


@functools.lru_cache(maxsize=None)
def _fft_consts_np(N):
    N1, N2 = _fft_dims(N)
    h = N1 // 2
    k1 = np.arange(N1)
    F1 = np.exp(-2j * np.pi * np.outer(k1, k1) / N1)
    k2 = np.arange(N2)
    F2 = np.exp(-2j * np.pi * np.outer(k2, k2) / N2)
    tw = np.exp(-2j * np.pi * np.outer(k1, k2) / N)
    g1 = np.zeros((2 * N1, N1))
    g1[0::2, :h] = F1.real[:, :h]
    g1[0::2, h:] = -F1.imag[:, :h]
    g1[1::2, :h] = F1.imag[:, :h]
    g1[1::2, h:] = F1.real[:, :h]
    g1f = np.zeros((2 * N1, N1))
    g1f[0::2] = F1.real
    g1f[1::2] = F1.imag
    g2 = np.block([[F2.real, -F2.imag], [F2.imag, F2.real]])
    g2c = np.block([[F2.real, F2.imag], [-F2.imag, F2.real]])
    g3 = np.zeros((N1, 2 * N1))
    g3[:h, 0::2] = F1.real[:h]
    g3[:h, 1::2] = F1.imag[:h]
    g3[h:, 0::2] = -F1.imag[:h]
    g3[h:, 1::2] = F1.real[:h]
    g3 /= N
    twt = np.stack([tw.real, tw.imag], axis=1)
    return dict(g1=g1, g1f=g1f, g2=g2, g2c=g2c, g3=g3, tw=twt)


def _fft_consts(N):
    c = _fft_consts_np(N)
    out = {k: _hilo(c[k]) for k in ("g1", "g1f", "g2", "g2c", "g3")}
    N1, N2 = _fft_dims(N)
    tw = jnp.asarray(c["tw"], F32)
    out["tw"] = jnp.broadcast_to(tw[..., None], (N1, 2, N2, LANES))
    return out


T2_BLOCK = 8


def _load_8th(ref, j, rows):
    return ref.reshape(rows * T2_BLOCK, LANES)[pl.ds(j, rows, stride=T2_BLOCK), :]


def _store_8th(ref, j, rows, val):
    ref.reshape(rows * T2_BLOCK, LANES)[pl.ds(j, rows, stride=T2_BLOCK), :] = val


def _dft_in_kernel(gh_ref, gl_ref, x_ref, o_ref):
    N1 = x_ref.shape[1]
    for j in range(T2_BLOCK):
        xj = jnp.concatenate([_load_8th(x_ref.at[hf], j, N1) for hf in range(HALVES)], axis=1)
        a = _dot3(gh_ref[...], gl_ref[...], xj)
        for hf in range(HALVES):
            _store_8th(o_ref.at[hf], j, 2 * N1, a[:, hf * LANES:(hf + 1) * LANES])


def _dft_in_call(g, u5):
    gh, gl = g
    G, _, N1, N2, _ = u5.shape
    assert gh.shape == (2 * N1, N1)
    mat = pl.BlockSpec((2 * N1, N1), lambda b, i: (0, 0))
    return pl.pallas_call(
        _dft_in_kernel,
        grid=(G, N2 // T2_BLOCK),
        in_specs=[mat, mat, pl.BlockSpec((None, HALVES, N1, T2_BLOCK, LANES), lambda b, i: (b, 0, 0, i, 0))],
        out_specs=pl.BlockSpec((None, HALVES, N1, 2, T2_BLOCK, LANES), lambda b, i: (b, 0, 0, 0, i, 0)),
        out_shape=jax.ShapeDtypeStruct((G, HALVES, N1, 2, N2, LANES), F32),
        compiler_params=_cparams(("parallel", "parallel")),
        name="dft_in",
    )(gh, gl, u5)


def _dft_out_kernel(gh_ref, gl_ref, d_ref, gate_ref, o_ref):
    N1 = gate_ref.shape[1]
    for j in range(T2_BLOCK):
        dj = jnp.concatenate([_load_8th(d_ref.at[hf], j, 2 * N1) for hf in range(HALVES)], axis=1)
        y = _dot3(gh_ref[...], gl_ref[...], dj)
        for hf in range(HALVES):
            conv = y[:, hf * LANES:(hf + 1) * LANES]
            _store_8th(o_ref.at[hf], j, N1, _load_8th(gate_ref.at[hf], j, N1) * conv)


def _dft_out_call(g, d6, gate5):
    gh, gl = g
    G, _, N1, N2, _ = gate5.shape
    assert gh.shape == (N1, 2 * N1)
    mat = pl.BlockSpec((N1, 2 * N1), lambda b, i: (0, 0))
    blk = pl.BlockSpec((None, HALVES, N1, T2_BLOCK, LANES), lambda b, i: (b, 0, 0, i, 0))
    return pl.pallas_call(
        _dft_out_kernel,
        grid=(G, N2 // T2_BLOCK),
        in_specs=[mat, mat, pl.BlockSpec((None, HALVES, N1, 2, T2_BLOCK, LANES), lambda b, i: (b, 0, 0, 0, i, 0)),
                  blk],
        out_specs=blk,
        out_shape=jax.ShapeDtypeStruct(gate5.shape, F32),
        compiler_params=_cparams(("parallel", "parallel")),
        name="dft_out",
    )(gh, gl, d6, gate5)


def _dft_mid_kernel(g3h_ref, g3l_ref, g1h_ref, g1l_ref, d_ref, gate_ref, o_ref):
    N1 = gate_ref.shape[1]
    for j in range(T2_BLOCK):
        dj = jnp.concatenate([_load_8th(d_ref.at[hf], j, 2 * N1) for hf in range(HALVES)], axis=1)
        y = _dot3(g3h_ref[...], g3l_ref[...], dj)
        gate = jnp.concatenate([_load_8th(gate_ref.at[hf], j, N1) for hf in range(HALVES)], axis=1)
        a = _dot3(g1h_ref[...], g1l_ref[...], gate * y)
        for hf in range(HALVES):
            _store_8th(o_ref.at[hf], j, 2 * N1, a[:, hf * LANES:(hf + 1) * LANES])


def _dft_mid_call(g3, g1, d6, gate5):
    G, _, N1, N2, _ = gate5.shape
    assert g3[0].shape == (N1, 2 * N1) and g1[0].shape == (2 * N1, N1)
    mat3 = pl.BlockSpec((N1, 2 * N1), lambda b, i: (0, 0))
    mat1 = pl.BlockSpec((2 * N1, N1), lambda b, i: (0, 0))
    wide = pl.BlockSpec((None, HALVES, N1, 2, T2_BLOCK, LANES), lambda b, i: (b, 0, 0, 0, i, 0))
    return pl.pallas_call(
        _dft_mid_kernel,
        grid=(G, N2 // T2_BLOCK),
        in_specs=[mat3, mat3, mat1, mat1, wide,
                  pl.BlockSpec((None, HALVES, N1, T2_BLOCK, LANES), lambda b, i: (b, 0, 0, i, 0))],
        out_specs=wide,
        out_shape=jax.ShapeDtypeStruct(d6.shape, F32),
        compiler_params=_cparams(("parallel", "parallel")),
        name="dft_mid",
    )(g3[0], g3[1], g1[0], g1[1], d6, gate5)


def _cmul(ar, ai, br, bi):
    return ar * br - ai * bi, ar * bi + ai * br


def _lane_tile(t, c):
    return jnp.concatenate([t] * (c // LANES), axis=-1)


def _both_halves(a_ref, k, part):
    return jnp.concatenate([a_ref[hf, k, part] for hf in range(HALVES)], axis=-1)


def _spec_kernel(a_ref, tw_ref, g2h_ref, g2l_ref, s_ref, sk_ref, o_ref):
    _, kb, _, N2, _ = a_ref.shape
    inv = 1.0 / s_ref[...]
    for k in range(kb):
        twr = _lane_tile(tw_ref[k, 0], HYENA_W)
        twi = _lane_tile(tw_ref[k, 1], HYENA_W)
        br, bi = _cmul(_both_halves(a_ref, k, 0), _both_halves(a_ref, k, 1), twr, twi)
        zz = _dot3(g2h_ref[...], g2l_ref[...], jnp.concatenate([br, bi], axis=0))
        o_ref[k, 0] = zz[:N2] * inv + sk_ref[...]
        o_ref[k, 1] = zz[N2:] * inv


def _conv_kernel(a_ref, tw_ref, kf_ref, g2h_ref, g2l_ref, g2ch_ref, g2cl_ref, o_ref):
    _, kb, _, N2, _ = a_ref.shape
    for k in range(kb):
        twr = _lane_tile(tw_ref[k, 0], HYENA_W)
        twi = _lane_tile(tw_ref[k, 1], HYENA_W)
        br, bi = _cmul(_both_halves(a_ref, k, 0), _both_halves(a_ref, k, 1), twr, twi)
        zz = _dot3(g2h_ref[...], g2l_ref[...], jnp.concatenate([br, bi], axis=0))
        yr, yi = _cmul(zz[:N2], zz[N2:], kf_ref[k, 0], kf_ref[k, 1])
        cc = _dot3(g2ch_ref[...], g2cl_ref[...], jnp.concatenate([yr, yi], axis=0))
        dr, di = _cmul(cc[:N2], cc[N2:], twr, -twi)
        for hf in range(HALVES):
            o_ref[hf, k, 0] = dr[:, hf * LANES:(hf + 1) * LANES]
            o_ref[hf, k, 1] = di[:, hf * LANES:(hf + 1) * LANES]


def _spec_call(a6, fc, s, skip):
    G, _, N1, _, N2, _ = a6.shape
    kb = _tile(N1, 8, 1)
    mat = pl.BlockSpec((2 * N2, 2 * N2), lambda i, b: (0, 0))
    return pl.pallas_call(
        _spec_kernel,
        grid=(N1 // kb, G),
        in_specs=[pl.BlockSpec((None, HALVES, kb, 2, N2, LANES), lambda i, b: (b, 0, i, 0, 0, 0)),
                  pl.BlockSpec((kb, 2, N2, LANES), lambda i, b: (i, 0, 0, 0)), mat, mat,
                  pl.BlockSpec((None, 1, HYENA_W), lambda i, b: (b, 0, 0)),
                  pl.BlockSpec((None, 1, HYENA_W), lambda i, b: (b, 0, 0))],
        out_specs=pl.BlockSpec((None, kb, 2, N2, HYENA_W), lambda i, b: (b, i, 0, 0, 0)),
        out_shape=jax.ShapeDtypeStruct((G, N1, 2, N2, HYENA_W), F32),
        compiler_params=_cparams(("parallel", "parallel")),
        name="filt_spec",
    )(a6, fc["tw"], fc["g2"][0], fc["g2"][1], s, skip)


def _conv_call(a6, kf, order, fc):
    G, _, N1, _, N2, _ = a6.shape
    kb = _tile(N1, 8, 1)
    blk = pl.BlockSpec((None, HALVES, kb, 2, N2, LANES), lambda i, b: (b, 0, i, 0, 0, 0))
    mat = pl.BlockSpec((2 * N2, 2 * N2), lambda i, b: (0, 0))
    return pl.pallas_call(
        _conv_kernel,
        grid=(N1 // kb, G),
        in_specs=[blk, pl.BlockSpec((kb, 2, N2, LANES), lambda i, b: (i, 0, 0, 0)),
                  pl.BlockSpec((None, kb, 2, N2, HYENA_W), lambda i, b: (order, i, 0, 0, 0)), mat, mat, mat, mat],
        out_specs=blk,
        out_shape=jax.ShapeDtypeStruct(a6.shape, F32),
        compiler_params=_cparams(("parallel", "parallel")),
        name="spec_conv",
    )(a6, fc["tw"], kf, fc["g2"][0], fc["g2"][1], fc["g2c"][0], fc["g2c"][1])


def _filter_spectra(n, lw, fc):
    N1, N2 = _fft_dims(2 * n)
    k, s = _filt_call(n, lw)
    a6 = _dft_in_call(fc["g1f"], k.reshape(HYENA_ORDER, HALVES, N1, N2, LANES))
    return _spec_call(a6, fc, s.reshape(HYENA_ORDER, 1, HYENA_W),
                      lw["hy_skip"].reshape(HYENA_ORDER, 1, HYENA_W))


def _hyena(v, x1, x2, kf, lw, fc):
    P, _, _, n, _ = v.shape
    N1, N2 = _fft_dims(2 * n)
    shape5 = (P, HALVES, N1, N2, LANES)
    gates = (x1, x2)
    a = _dft_in_call(fc["g1"], v.reshape(shape5))
    for o in range(HYENA_ORDER - 1):
        a = _dft_mid_call(fc["g3"], fc["g1"], _conv_call(a, kf, o, fc), gates[o].reshape(shape5))
    y = _dft_out_call(fc["g3"], _conv_call(a, kf, HYENA_ORDER - 1, fc), gates[-1].reshape(shape5))
    return y.reshape(v.shape)


def _merge_kernel(at_ref, hy_ref, po_ref, gm_ref, x_ref, sh_ref, sc_ref, nw_ref, gr_ref,
                  wgate_ref, wa_ref, wh_ref, wp_ref, wg_ref, wo_ref, o_ref):
    D = x_ref.shape[1]
    for rows in _row_blocks(x_ref.shape[0]):
        x = x_ref[rows, :]
        h = _modulated_norm(x, nw_ref[...], sc_ref[...], sh_ref[...]).astype(BF16)
        hy = jnp.concatenate([hy_ref[hf, rows, :] for hf in range(HALVES)], axis=-1).astype(BF16)
        branches = ((at_ref[rows, :], wa_ref), (hy, wh_ref), (po_ref[rows, :], wp_ref), (gm_ref[rows, :], wg_ref))
        merged = None
        for i, (br, w_ref) in enumerate(branches):
            term = _sigmoid(_dot(h, wgate_ref[:, i * D:(i + 1) * D])) * _dot(br, w_ref[...])
            merged = term if merged is None else merged + term
        o_ref[rows, :] = x + gr_ref[...] * _dot(merged.astype(BF16), wo_ref[...])


MERGE_ROWS = 1024


def _merge_call(attn, hy, po, gm, x, shift, scale, norm_w, gate_res, lw):
    B, n, D = x.shape
    tm = _tile(n, MERGE_ROWS)

    def rows(w):
        return pl.BlockSpec((None, tm, w), lambda b, i: (b, i, 0))

    def full(shape):
        return pl.BlockSpec(shape, lambda b, i: (0,) * len(shape))

    vec = pl.BlockSpec((None, 1, D), lambda b, i: (b, 0, 0))
    return pl.pallas_call(
        _merge_kernel,
        grid=(B, n // tm),
        in_specs=[
            rows(ATTN_W), _split_spec(tm), rows(POOL_W), rows(GMLP_W), rows(D),
            vec, vec, full((1, D)), vec,
            full((D, N_BRANCH * D)),
            full((ATTN_W, D)), full((HYENA_W, D)), full((POOL_W, D)), full((GMLP_W, D)), full((D, D)),
        ],
        out_specs=rows(D),
        out_shape=jax.ShapeDtypeStruct((B, n, D), F32),
        compiler_params=_cparams(("parallel", "parallel")),
        name="merge",
    )(attn, hy, po, gm, x, shift, scale, norm_w.reshape(1, D), gate_res, lw["w_gate"], lw["w_br_attn"],
      lw["w_br_hyena"], lw["w_br_pool"], lw["w_br_gmlp"], lw["w_out"])


FFN_ROWS = 1024


def _ffn_kernel(x_ref, sh_ref, sc_ref, nw_ref, gr_ref, wg_ref, wu_ref, wd_ref, fw_ref, o_ref, *, final):
    for rows in _row_blocks(x_ref.shape[0]):
        x = x_ref[rows, :]
        h = _modulated_norm(x, nw_ref[...], sc_ref[...], sh_ref[...]).astype(BF16)
        a = _silu(_dot(h, wg_ref[...])) * _dot(h, wu_ref[...])
        y = x + gr_ref[...] * _dot(a.astype(BF16), wd_ref[...])
        if final:
            y = y * lax.rsqrt(jnp.mean(y * y, axis=-1, keepdims=True) + RMS_EPS) * fw_ref[...]
        o_ref[rows, :] = y


def _ffn_call(x, shift, scale, norm_w, gate_res, wg, wu, wd, final_w=None):
    B, n, D = x.shape
    H = wg.shape[1]
    tm = _tile(n, FFN_ROWS)
    vec = pl.BlockSpec((None, 1, D), lambda b, i: (b, 0, 0))
    one = pl.BlockSpec((1, D), lambda b, i: (0, 0))
    final = final_w is not None
    fw = (final_w if final else norm_w).reshape(1, D)
    return pl.pallas_call(
        functools.partial(_ffn_kernel, final=final),
        grid=(B, n // tm),
        in_specs=[
            pl.BlockSpec((None, tm, D), lambda b, i: (b, i, 0)),
            vec, vec, one, vec,
            pl.BlockSpec((D, H), lambda b, i: (0, 0)),
            pl.BlockSpec((D, H), lambda b, i: (0, 0)),
            pl.BlockSpec((H, D), lambda b, i: (0, 0)),
            one,
        ],
        out_specs=pl.BlockSpec((None, tm, D), lambda b, i: (b, i, 0)),
        out_shape=jax.ShapeDtypeStruct((B, n, D), F32),
        compiler_params=_cparams(("parallel", "parallel")),
        name="ffn",
    )(x, shift, scale, norm_w.reshape(1, D), gate_res, wg, wu, wd, fw)


def _rope_tables(n):
    tok = np.arange(n)
    axis_dim = HEAD_DIM // 2
    inv_freq = ROPE_THETA ** (-np.arange(0, axis_dim, 2, dtype=np.float64) / axis_dim)
    inv_freq = inv_freq.astype(np.float32).astype(np.float64)
    ang = np.concatenate([(tok // GRID_W)[:, None] * inv_freq, (tok % GRID_W)[:, None] * inv_freq], axis=-1)
    ang = ang.astype(np.float32)
    cos = np.repeat(np.cos(ang), 2, axis=-1)
    sin = np.repeat(np.sin(ang), 2, axis=-1)
    sin[:, 0::2] *= -1.0
    reps = LANES // HEAD_DIM
    return jnp.asarray(np.tile(cos, (1, reps)), F32), jnp.asarray(np.tile(sin, (1, reps)), F32)


def _identity_tables(n):
    return jnp.ones((n, LANES), F32), jnp.zeros((n, LANES), F32)


def _layer_weights(l, a):
    w_in = a["w_in"][l]
    w_perm = jnp.concatenate([
        w_in[:, Q_OFF:K_OFF], w_in[:, K_OFF:HY_OFF], w_in[:, POOL_OFF:GM_OFF], w_in[:, GM_OFF:GATE_OFF],
        w_in[:, HY_OFF:POOL_OFF]], axis=1).astype(BF16)
    w3 = a["hy_w3"][l].reshape(HYENA_HIDDEN, HYENA_ORDER, 2, HYENA_W).transpose(2, 0, 1, 3)
    w3 = w3.reshape(2, HYENA_HIDDEN, HYENA_ORDER * HYENA_W)
    dec = a["hy_decay"][l].reshape(HYENA_ORDER, 2, HYENA_W).transpose(1, 0, 2).reshape(2, 1, HYENA_ORDER * HYENA_W)
    w1 = jnp.zeros((HYENA_HIDDEN, LANES), F32).at[:, :a["hy_w1"].shape[1]].set(a["hy_w1"][l].T)
    w1h, w1l = _split(w1)
    w2h, w2l = _split(jnp.swapaxes(a["hy_w2"][l], 1, 2))

    def lanes_of(v):
        return jnp.broadcast_to(v[..., None], v.shape + (LANES,))
    w3h, w3l = _split(w3)
    qw16 = jnp.tile(a["q_norm_w"][l], N_Q_HEADS).reshape(1, ATTN_W) * Q_SCALE
    kw16 = jnp.tile(a["k_norm_w"][l], N_KV_HEADS).reshape(1, KV_W)
    rms_q = jnp.sqrt(jnp.mean(qw16 * qw16))
    rms_k = jnp.sqrt(jnp.mean(kw16 * kw16))
    bal = jnp.exp2(jnp.clip(jnp.round(0.5 * jnp.log2(rms_k / rms_q)), -20.0, 20.0))
    bal = jnp.where(jnp.isfinite(bal), bal, 1.0)
    fp8_ok = (jnp.max(jnp.abs(qw16 * bal)) <= FP8_MAX_WEIGHT) & (jnp.max(jnp.abs(kw16 / bal)) <= FP8_MAX_WEIGHT)
    eye = jnp.eye(len(POOL_WINDOWS), dtype=F32)
    pool_bd = jnp.einsum("gh,gcd->gchd", eye, a["pool_w"][l]).reshape(POOL_W, POOL_W).astype(BF16)
    gm_bias = jnp.repeat(a["gm_bs"][l].T, GMLP_GROUP_W, axis=1)
    return dict(
        w_in=w_perm, w_gate=w_in[:, GATE_OFF:].astype(BF16),
        qw16=qw16, kw16=kw16, qw8=qw16 * bal, kw8=kw16 / bal, fp8_ok=fp8_ok,
        conv_w=a["hy_conv_w"][l], conv_b=a["hy_conv_b"][l].reshape(1, -1),
        hy_w1h=w1h, hy_w1l=w1l, hy_b1=lanes_of(a["hy_b1"][l]), hy_w2h=w2h, hy_w2l=w2l, hy_b2=lanes_of(a["hy_b2"][l]),
        hy_w3h=w3h, hy_w3l=w3l, hy_freq=lanes_of(a["hy_freq"][l]), hy_decay=dec, hy_skip=a["hy_skip"][l],
        pool_bd=pool_bd, pool_scale=a["pool_scale"][l].reshape(1, -1),
        gm_norm_w=a["gm_norm_w"][l].reshape(1, -1), gm_ws=a["gm_ws"][l].astype(BF16), gm_bias=gm_bias,
        w_br_attn=a["w_br_attn"][l].astype(BF16), w_br_hyena=a["w_br_hyena"][l].astype(BF16),
        w_br_pool=a["w_br_pool"][l].astype(BF16), w_br_gmlp=a["w_br_gmlp"][l].astype(BF16),
        w_out=a["w_out"][l].astype(BF16),
        ffn_w_gate=a["ffn_w_gate"][l].astype(BF16), ffn_w_up=a["ffn_w_up"][l].astype(BF16),
        ffn_w_down=a["ffn_w_down"][l].astype(BF16),
        norm1_w=a["norm1_w"][l], norm2_w=a["norm2_w"][l],
    )


Q_SCALE = (HEAD_DIM ** -0.5) * math.log2(math.e)


def _attend(p, p_ctx, rope, rope_ctx, lw):
    def run(fp8):
        qw, kw = (lw["qw8"], lw["kw8"]) if fp8 else (lw["qw16"], lw["kw16"])
        qt, k, vt = _qkprep_call(p, rope[0], rope[1], qw, kw, fp8)
        if p_ctx is not None:
            _, k_c, vt_c = _qkprep_call(p_ctx, rope_ctx[0], rope_ctx[1], qw, kw, fp8)
            k = jnp.concatenate([k_c, k], axis=2)
            vt = jnp.concatenate([vt_c, vt], axis=3)
        return _attn_call(qt, k, vt)

    return lax.cond(lw["fp8_ok"], lambda: run(True), lambda: run(False))


def _mixer_block(x, mods, lw, rope, p_ctx, rope_ctx, kf, fc, final_w=None):
    sh1, s1, g1, sh2, s2, g2 = mods
    p = _inproj_call(x, sh1, s1, lw["norm1_w"], lw["w_in"])
    attn = _attend(p, p_ctx, rope, rope_ctx, lw)
    hv, hx1, hx2, po, gm = _local_call(p, lw)
    hy = _hyena(hv, hx1, hx2, kf, lw, fc)
    x = _merge_call(attn, hy, po, gm, x, sh1, s1, lw["norm1_w"], g1, lw)
    x = _ffn_call(x, sh2, s2, lw["norm2_w"], g2, lw["ffn_w_gate"], lw["ffn_w_up"], lw["ffn_w_down"], final_w)
    return x, p


def kernel(x, c, ctx, c_ctx, w_mod, b_mod, norm1_w, norm2_w, w_in, q_norm_w, k_norm_w, hy_conv_w, hy_conv_b,
           hy_w1, hy_b1, hy_w2, hy_b2, hy_w3, hy_freq, hy_decay, hy_skip, pool_w, pool_scale, gm_norm_w, gm_ws,
           gm_bs, w_br_attn, w_br_hyena, w_br_pool, w_br_gmlp, w_out, ffn_w_gate, ffn_w_up, ffn_w_down,
           final_norm_w):
    a = dict(w_in=w_in, q_norm_w=q_norm_w, k_norm_w=k_norm_w, hy_conv_w=hy_conv_w, hy_conv_b=hy_conv_b,
             hy_w1=hy_w1, hy_b1=hy_b1, hy_w2=hy_w2, hy_b2=hy_b2, hy_w3=hy_w3, hy_freq=hy_freq, hy_decay=hy_decay,
             hy_skip=hy_skip, pool_w=pool_w, pool_scale=pool_scale, gm_norm_w=gm_norm_w, gm_ws=gm_ws, gm_bs=gm_bs,
             w_br_attn=w_br_attn, w_br_hyena=w_br_hyena, w_br_pool=w_br_pool, w_br_gmlp=w_br_gmlp, w_out=w_out,
             ffn_w_gate=ffn_w_gate, ffn_w_up=ffn_w_up, ffn_w_down=ffn_w_down, norm1_w=norm1_w, norm2_w=norm2_w)
    B, n, D = x.shape
    nc = ctx.shape[1]
    depth = w_mod.shape[0]

    R = ((B + 1 + 15) // 16) * 16
    act = jnp.zeros((R, D), F32).at[:B].set(c).at[B].set(c_ctx)
    mod = _mod_call(act, w_mod, b_mod)

    rope = _rope_tables(n)
    rope_ctx = _identity_tables(nc)
    fc = _fft_consts(2 * n)
    fc_ctx = _fft_consts(2 * nc)

    xc = ctx
    for l in range(depth):
        lw = _layer_weights(l, a)
        mods = [m.reshape(B, 1, D) for m in jnp.split(mod[l, :B], 6, axis=-1)]
        cmods = [jnp.broadcast_to(m.reshape(1, 1, D), (B, 1, D)) for m in jnp.split(mod[l, B], 6, axis=-1)]
        if l < depth - 1:
            kf_ctx = _filter_spectra(nc, lw, fc_ctx)
            xc, pc = _mixer_block(xc, cmods, lw, rope_ctx, None, None, kf_ctx, fc_ctx)
        else:
            pc = _inproj_call(xc, cmods[0], cmods[1], lw["norm1_w"], lw["w_in"][:, :P_QKV])
        kf = _filter_spectra(n, lw, fc)
        x, _ = _mixer_block(x, mods, lw, rope, pc, rope_ctx, kf, fc, final_norm_w if l == depth - 1 else None)
    return x
```

```python
import functools
import math

import numpy as np
import jax
import jax.numpy as jnp
from jax import lax
from jax.experimental import pallas as pl
from jax.experimental.pallas import tpu as pltpu

F32 = jnp.float32
BF16 = jnp.bfloat16

D_MODEL = 1024
GRID_W = 64
RMS_EPS = 1e-6
HEAD_DIM = 64
N_Q_HEADS = 8
N_KV_HEADS = 2
ATTN_W = N_Q_HEADS * HEAD_DIM
KV_W = N_KV_HEADS * HEAD_DIM
ROPE_THETA = 10000.0
HYENA_W = 256
HYENA_ORDER = 2
HYENA_BANDS = 16
HYENA_HIDDEN = 64
HYENA_INNER = 2
POOL_W = 256
POOL_WINDOWS = (2, 4, 8, 16)
POOL_GROUP_W = POOL_W // len(POOL_WINDOWS)
POOL_HALO = 8
GMLP_W = 256
GMLP_CHUNK = 128
GMLP_GROUPS = 4
GMLP_GROUP_W = GMLP_W // GMLP_GROUPS
N_BRANCH = 4

Q_OFF = 0
K_OFF = Q_OFF + ATTN_W
V_OFF = K_OFF + KV_W
HY_OFF = V_OFF + KV_W
POOL_OFF = HY_OFF + 3 * HYENA_W
GM_OFF = POOL_OFF + POOL_W
GATE_OFF = GM_OFF + 2 * GMLP_W

P_Q = 0
P_KV = P_Q + ATTN_W
P_POOL = P_KV + 2 * KV_W
P_GMU = P_POOL + POOL_W
P_GMV = P_GMU + GMLP_W
P_HY = P_GMV + GMLP_W
P_W = P_HY + 3 * HYENA_W
P_QKV = P_POOL

LANES = 128
VMEM_LIMIT = 52 * 1024 * 1024


def _cparams(sem):
    return pltpu.CompilerParams(dimension_semantics=sem, vmem_limit_bytes=VMEM_LIMIT)


def _tile(n, cap, mult=8):
    best = None
    for t in range(mult, min(n, cap) + 1, mult):
        if n % t == 0:
            best = t
    assert best is not None, (n, cap, mult)
    return best


def _split(x):
    hi = x.astype(BF16)
    lo = (x - hi.astype(F32)).astype(BF16)
    return hi, lo


def _dot(a, b):
    return jnp.dot(a, b, preferred_element_type=F32)


def _dot3(a_hi, a_lo, x):
    x_hi, x_lo = _split(x)
    return _dot(a_hi, x_hi) + (_dot(a_lo, x_hi) + _dot(a_hi, x_lo))


def _dot3x(x, b_hi, b_lo):
    x_hi, x_lo = _split(x)
    return _dot(x_hi, b_hi) + (_dot(x_lo, b_hi) + _dot(x_hi, b_lo))


def _sigmoid(x):
    return 1.0 / (1.0 + jnp.exp(-x))


def _silu(x):
    return x * _sigmoid(x)


def _modulated_norm(x, nw, scale, shift):
    ms = jnp.mean(x * x, axis=-1, keepdims=True)
    y = x * lax.rsqrt(ms + RMS_EPS) * nw
    return y * (1.0 + scale) + shift


def _mod_kernel(a_ref, w_ref, b_ref, o_ref):
    a = _silu(a_ref[...]).astype(BF16)
    o_ref[...] = _dot(a, w_ref[...].astype(BF16)) + b_ref[...]


def _mod_call(act, w_mod, b_mod):
    L, D, W6 = w_mod.shape
    R = act.shape[0]
    tn = 1024
    return pl.pallas_call(
        _mod_kernel,
        grid=(L, W6 // tn),
        in_specs=[
            pl.BlockSpec((R, D), lambda l, j: (0, 0)),
            pl.BlockSpec((None, D, tn), lambda l, j: (l, 0, j)),
            pl.BlockSpec((None, 1, tn), lambda l, j: (l, 0, j)),
        ],
        out_specs=pl.BlockSpec((None, R, tn), lambda l, j: (l, 0, j)),
        out_shape=jax.ShapeDtypeStruct((L, R, W6), F32),
        compiler_params=_cparams(("arbitrary", "arbitrary")),
        name="mod",
    )(act, w_mod, b_mod.reshape(L, 1, W6))


ROW_BLOCK = 256
INPROJ_ROWS = 1024


def _row_blocks(tm):
    sub = ROW_BLOCK if tm % ROW_BLOCK == 0 else tm
    return [slice(r, r + sub) for r in range(0, tm, sub)]


def _inproj_kernel(x_ref, sh_ref, sc_ref, nw_ref, w_ref, o_ref):
    for rows in _row_blocks(x_ref.shape[0]):
        h = _modulated_norm(x_ref[rows, :], nw_ref[...], sc_ref[...], sh_ref[...]).astype(BF16)
        o_ref[rows, :] = _dot(h, w_ref[...])


def _inproj_call(x, shift, scale, norm_w, w):
    B, n, D = x.shape
    Nw = w.shape[1]
    tm = _tile(n, INPROJ_ROWS)
    return pl.pallas_call(
        _inproj_kernel,
        grid=(B, n // tm),
        in_specs=[
            pl.BlockSpec((None, tm, D), lambda b, i: (b, i, 0)),
            pl.BlockSpec((None, 1, D), lambda b, i: (b, 0, 0)),
            pl.BlockSpec((None, 1, D), lambda b, i: (b, 0, 0)),
            pl.BlockSpec((1, D), lambda b, i: (0, 0)),
            pl.BlockSpec((D, Nw), lambda b, i: (0, 0)),
        ],
        out_specs=pl.BlockSpec((None, tm, Nw), lambda b, i: (b, i, 0)),
        out_shape=jax.ShapeDtypeStruct((B, n, Nw), F32),
        compiler_params=_cparams(("parallel", "parallel")),
        name="inproj",
    )(x, shift, scale, norm_w.reshape(1, D), w)


def _head_norm_rope(t, w, cs, sn, bd_ref):
    sq_hi, sq_lo = _split(t * t)
    ms = _dot(sq_hi, bd_ref[...]) + _dot(sq_lo, bd_ref[...])
    y = t * lax.rsqrt(ms + RMS_EPS) * w
    lane = lax.broadcasted_iota(jnp.int32, y.shape, 1)
    swapped = jnp.where(lane % 2 == 0, pltpu.roll(y, LANES - 1, 1), pltpu.roll(y, 1, 1))
    return y * cs + swapped * sn


V_ROWS = 80
Q_PER_KV = N_Q_HEADS // N_KV_HEADS
LAG_INIT_KEYS = 128
LAG_LIMIT = 12.0
ATTN_QUERIES = 512
ATTN_KEYS = 768
ATTN_UNROLL = 2


def _value_rows(vt_g):
    tail = lax.broadcasted_iota(jnp.int32, (V_ROWS - HEAD_DIM, vt_g.shape[1]), 0)
    return jnp.concatenate([vt_g, jnp.where(tail == 0, 1.0, 0.0)], axis=0).astype(BF16)


FP8 = jnp.float8_e4m3fn
FP8_MAX_WEIGHT = 32.0


def _split8(x):
    hi = x.astype(FP8).astype(F32)
    return hi, (x - hi).astype(FP8).astype(F32)


def _qkprep_kernel(q_ref, kv_ref, cs_ref, sn_ref, qw_ref, kw_ref, bd_ref, kp_ref, qo_ref, ko_ref, vo_ref, *, fp8):
    cs = cs_ref[...]
    sn = sn_ref[...]
    tm = q_ref.shape[0]
    zeros = jnp.zeros((HEAD_DIM, tm), F32)
    heads_per_chunk = LANES // HEAD_DIM
    for c in range(ATTN_W // LANES):
        sl = slice(c * LANES, (c + 1) * LANES)
        r = _head_norm_rope(q_ref[:, sl], qw_ref[:, sl], cs, sn, bd_ref)
        if fp8:
            hi_t, lo_t = _split8(r.T)
            for half in range(heads_per_chunk):
                rows = slice(half * HEAD_DIM, (half + 1) * HEAD_DIM)
                qo_ref[c * heads_per_chunk + half] = jnp.concatenate(
                    [hi_t[rows], lo_t[rows], hi_t[rows], lo_t[rows]], axis=0).astype(FP8)
        else:
            rt = r.T
            for half in range(heads_per_chunk):
                h = c * heads_per_chunk + half
                blk = rt[half * HEAD_DIM:(half + 1) * HEAD_DIM]
                parts = [blk if g == h // Q_PER_KV else zeros for g in range(N_KV_HEADS)]
                qo_ref[h] = jnp.concatenate(parts, axis=0).astype(BF16)
    k = _head_norm_rope(kv_ref[:, :KV_W], kw_ref[...], cs, sn, bd_ref)
    if fp8:
        hi, lo = _split8(k)
        hi, lo = hi.astype(BF16), lo.astype(BF16)
        for g in range(N_KV_HEADS):
            ko_ref[g] = (_dot(hi, kp_ref[g, 0]) + _dot(lo, kp_ref[g, 1])).astype(FP8)
    else:
        ko_ref[0] = k.astype(BF16)
    vt = kv_ref[:, KV_W:].T
    for g in range(N_KV_HEADS):
        vo_ref[g] = _value_rows(vt[g * HEAD_DIM:(g + 1) * HEAD_DIM])


def _k_placement():
    pm = np.zeros((N_KV_HEADS, 2, KV_W, 4 * HEAD_DIM), np.float32)
    d = np.arange(HEAD_DIM)
    for g in range(N_KV_HEADS):
        for part in range(2):
            for rep in range(2):
                pm[g, part, g * HEAD_DIM + d, (2 * part + rep) * HEAD_DIM + d] = 1.0
    return jnp.asarray(pm, BF16)


def _qkprep_call(p, cs, sn, qw, kw, fp8):
    B, n, _ = p.shape
    tm = _tile(n, 512, LANES)
    bd = np.kron(np.eye(LANES // HEAD_DIM), np.full((HEAD_DIM, HEAD_DIM), 1.0 / HEAD_DIM))
    bd = jnp.asarray(bd, BF16)
    kd, kg, dt = (4 * HEAD_DIM, N_KV_HEADS, FP8) if fp8 else (KV_W, 1, BF16)
    return pl.pallas_call(
        functools.partial(_qkprep_kernel, fp8=fp8),
        grid=(n // tm, B),
        in_specs=[
            pl.BlockSpec((None, tm, ATTN_W), lambda i, b: (b, i, P_Q // ATTN_W)),
            pl.BlockSpec((None, tm, 2 * KV_W), lambda i, b: (b, i, P_KV // (2 * KV_W))),
            pl.BlockSpec((tm, LANES), lambda i, b: (i, 0)),
            pl.BlockSpec((tm, LANES), lambda i, b: (i, 0)),
            pl.BlockSpec((1, ATTN_W), lambda i, b: (0, 0)),
            pl.BlockSpec((1, KV_W), lambda i, b: (0, 0)),
            pl.BlockSpec((LANES, LANES), lambda i, b: (0, 0)),
            pl.BlockSpec((N_KV_HEADS, 2, KV_W, 4 * HEAD_DIM), lambda i, b: (0, 0, 0, 0)),
        ],
        out_specs=[
            pl.BlockSpec((None, N_Q_HEADS, kd, tm), lambda i, b: (b, 0, 0, i)),
            pl.BlockSpec((None, kg, tm, kd), lambda i, b: (b, 0, i, 0)),
            pl.BlockSpec((None, N_KV_HEADS, V_ROWS, tm), lambda i, b: (b, 0, 0, i)),
        ],
        out_shape=[
            jax.ShapeDtypeStruct((B, N_Q_HEADS, kd, n), dt),
            jax.ShapeDtypeStruct((B, kg, n, kd), dt),
            jax.ShapeDtypeStruct((B, N_KV_HEADS, V_ROWS, n), BF16),
        ],
        compiler_params=_cparams(("parallel", "parallel")),
        name="qkprep",
    )(p, p, cs, sn, qw, kw, bd, _k_placement())


def _attn_kernel(qt_ref, k_ref, vt_ref, o_ref, m_ref, acc_ref, *, tk):
    nk = vt_ref.shape[0]
    tq = m_ref.shape[-1]
    per_kv_head = k_ref.shape[0] > 1

    def keys(j, h):
        g = h // Q_PER_KV if per_kv_head else 0
        return k_ref[g, pl.ds(pl.multiple_of(j * tk, tk), tk), :]

    def lagged_chunk(j, jump):
        s_next = _dot(keys(j, 0), qt_ref[0])
        for h in range(N_Q_HEADS):
            s = s_next
            if h + 1 < N_Q_HEADS:
                s_next = _dot(keys(j, h + 1), qt_ref[h + 1])
            m_old = m_ref[h]
            p = jnp.exp2((s - m_old).astype(BF16))
            cmax = jnp.max(s, axis=0, keepdims=True)
            m_new = jnp.maximum(m_old, cmax)
            jump = jnp.maximum(jump, cmax - m_old)
            acc_ref[h] = jnp.exp2(m_old - m_new) * (acc_ref[h] + _dot(vt_ref[j, h // Q_PER_KV], p))
            m_ref[h] = m_new
        return jump

    def exact_chunk(j, carry):
        for h in range(N_Q_HEADS):
            s = _dot(keys(j, h), qt_ref[h])
            m_old = m_ref[h]
            m_new = jnp.maximum(m_old, jnp.max(s, axis=0, keepdims=True))
            p = jnp.exp2(s - m_new).astype(BF16)
            acc_ref[h] = jnp.exp2(m_old - m_new) * acc_ref[h] + _dot(vt_ref[j, h // Q_PER_KV], p)
            m_ref[h] = m_new
        return carry

    k_lo = (k_ref.shape[1] // 2) // LAG_INIT_KEYS * LAG_INIT_KEYS
    for h in range(N_Q_HEADS):
        k0 = k_ref[h // Q_PER_KV if per_kv_head else 0, k_lo:k_lo + LAG_INIT_KEYS, :]
        m_ref[h] = jnp.max(_dot(k0, qt_ref[h]), axis=0, keepdims=True)
    acc_ref[...] = jnp.zeros(acc_ref.shape, F32)
    jump = lax.fori_loop(0, nk, lagged_chunk, jnp.zeros((1, tq), F32), unroll=ATTN_UNROLL)

    @pl.when(jnp.logical_not(jnp.max(jump) <= LAG_LIMIT))
    def _():
        m_ref[...] = jnp.full(m_ref.shape, -jnp.inf, F32)
        acc_ref[...] = jnp.zeros(acc_ref.shape, F32)
        lax.fori_loop(0, nk, exact_chunk, 0)

    for c in range(ATTN_W // LANES):
        heads = []
        for h in range(c * (LANES // HEAD_DIM), (c + 1) * (LANES // HEAD_DIM)):
            a = acc_ref[h]
            heads.append(a[:HEAD_DIM] / a[HEAD_DIM:HEAD_DIM + 1])
        o_ref[:, c * LANES:(c + 1) * LANES] = jnp.concatenate(heads, axis=0).T.astype(BF16)


def _attn_call(qt, k, vt):
    B, _, kd, n = qt.shape
    _, kg, Lk, _ = k.shape
    tq = _tile(n, ATTN_QUERIES, LANES)
    tk = _tile(Lk, ATTN_KEYS, LANES)
    nk = Lk // tk
    vt = vt.reshape(B, N_KV_HEADS, V_ROWS, nk, tk).transpose(0, 3, 1, 2, 4)
    return pl.pallas_call(
        functools.partial(_attn_kernel, tk=tk),
        grid=(B, n // tq),
        in_specs=[
            pl.BlockSpec((None, N_Q_HEADS, kd, tq), lambda b, i: (b, 0, 0, i)),
            pl.BlockSpec((None, kg, Lk, kd), lambda b, i: (b, 0, 0, 0)),
            pl.BlockSpec((None, nk, N_KV_HEADS, V_ROWS, tk), lambda b, i: (b, 0, 0, 0, 0)),
        ],
        out_specs=pl.BlockSpec((None, tq, ATTN_W), lambda b, i: (b, i, 0)),
        out_shape=jax.ShapeDtypeStruct((B, n, ATTN_W), BF16),
        scratch_shapes=[
            pltpu.VMEM((N_Q_HEADS, 1, tq), F32),
            pltpu.VMEM((N_Q_HEADS, V_ROWS, tq), F32),
        ],
        compiler_params=_cparams(("parallel", "parallel")),
        name="attn",
    )(qt, k, vt)


HALVES = HYENA_W // LANES


def _split_shape(B, n):
    assert B % 2 == 0
    return jax.ShapeDtypeStruct((B // 2, HALVES, 2, n, LANES), F32)


def _split_spec(tm):
    return pl.BlockSpec((None, HALVES, None, tm, LANES), lambda b, i: (b // 2, 0, b % 2, i, 0))


def _shift_rows(cur, prev_row, next_row, rows):
    tm = cur.shape[0]
    up = jnp.where(rows == 0, prev_row, pltpu.roll(cur, 1, 0))
    dn = jnp.where(rows == tm - 1, next_row, pltpu.roll(cur, tm - 1, 0))
    return up, dn


def _local_kernel(hy_ref, hyp_ref, hyn_ref, po_ref, pop_ref, pon_ref, gu_ref, gv_ref,
                  cw_ref, cb_ref, pw_ref, ps_ref, gnw_ref, gws_ref, gb_ref,
                  v_ref, x1_ref, x2_ref, pl_ref, gm_ref, *, n_seq):
    i = pl.program_id(1)
    last = pl.num_programs(1) - 1
    tm = hy_ref.shape[0]
    has_prev = (i > 0).astype(F32)
    has_next = (i < last).astype(F32)

    cur = hy_ref[...]
    prev_row = hyp_ref[POOL_HALO - 1:POOL_HALO, :] * has_prev
    next_row = hyn_ref[0:1, :] * has_next
    rows = lax.broadcasted_iota(jnp.int32, cur.shape, 0)
    up, dn = _shift_rows(cur, prev_row, next_row, rows)
    z = up * cw_ref[0:1, :] + cur * cw_ref[1:2, :] + dn * cw_ref[2:3, :] + cb_ref[...]
    for s, o_ref in enumerate((v_ref, x1_ref, x2_ref)):
        for hf in range(HYENA_W // LANES):
            lo = s * HYENA_W + hf * LANES
            o_ref[hf] = z[:, lo:lo + LANES]

    zc = po_ref[...]
    ext = jnp.concatenate([pop_ref[...] * has_prev, zc, pon_ref[...] * has_next], axis=0)
    te = tm + 2 * POOL_HALO
    lane = lax.broadcasted_iota(jnp.int32, zc.shape, 1)
    prow = lax.broadcasted_iota(jnp.int32, zc.shape, 0) + i * tm
    half = jnp.left_shift(1, lane // POOL_GROUP_W)
    def ahead(a, k):
        return pltpu.roll(a, (te - k) % te, 0)

    assert POOL_WINDOWS == (2, 4, 8, 16)
    sums = [ext + ahead(ext, -1)]
    for win in POOL_WINDOWS[:-1]:
        sums.append(ahead(sums[-1], win // 2) + ahead(sums[-1], -(win // 2)))
    group = lane // POOL_GROUP_W
    acc = sums[-1][POOL_HALO:POOL_HALO + tm]
    for g in range(len(POOL_WINDOWS) - 2, -1, -1):
        acc = jnp.where(group == g, sums[g][POOL_HALO:POOL_HALO + tm], acc)
    cnt = jnp.minimum(prow + half, n_seq) - jnp.maximum(prow - half, 0)
    dlt = acc / cnt.astype(F32) - zc
    pl_ref[...] = (_dot(dlt.astype(BF16), pw_ref[...]) * ps_ref[...]).astype(BF16)

    gv = gv_ref[...]
    mu = jnp.mean(gv, axis=-1, keepdims=True)
    ctr = gv - mu
    var = jnp.mean(ctr * ctr, axis=-1, keepdims=True)
    vn = (ctr * lax.rsqrt(var + RMS_EPS) * gnw_ref[...]).astype(BF16)
    group = lax.broadcasted_iota(jnp.int32, (GMLP_CHUNK, GMLP_W), 1) // GMLP_GROUP_W
    for c in range(tm // GMLP_CHUNK):
        rs = slice(c * GMLP_CHUNK, (c + 1) * GMLP_CHUNK)
        vc = vn[rs]
        mixed = gb_ref[...]
        for g in range(GMLP_GROUPS):
            mg = _dot(gws_ref[g], vc)
            mixed = mixed + jnp.where(group == g, mg, 0.0)
        gm_ref[rs, :] = (gu_ref[rs, :] * mixed).astype(BF16)


LOCAL_ROWS = 1024


def _local_call(p, lw):
    B, n, _ = p.shape
    tm = _tile(n, LOCAL_ROWS, GMLP_CHUNK)
    nb8 = n // POOL_HALO
    r8 = tm // POOL_HALO
    HW = 3 * HYENA_W

    def cur(w, off):
        return pl.BlockSpec((None, tm, w), lambda b, i: (b, i, off // w))

    def prev(w, off):
        return pl.BlockSpec((None, POOL_HALO, w), lambda b, i: (b, jnp.maximum(i * r8 - 1, 0), off // w))

    def nxt(w, off):
        return pl.BlockSpec((None, POOL_HALO, w), lambda b, i: (b, jnp.minimum((i + 1) * r8, nb8 - 1), off // w))

    def full(shape):
        return pl.BlockSpec(shape, lambda b, i: (0,) * len(shape))

    def out(w):
        return pl.BlockSpec((None, tm, w), lambda b, i: (b, i, 0))

    return pl.pallas_call(
        functools.partial(_local_kernel, n_seq=n),
        grid=(B, n // tm),
        in_specs=[
            cur(HW, P_HY), prev(HW, P_HY), nxt(HW, P_HY),
            cur(POOL_W, P_POOL), prev(POOL_W, P_POOL), nxt(POOL_W, P_POOL),
            cur(GMLP_W, P_GMU), cur(GMLP_W, P_GMV),
            full((3, HW)), full((1, HW)), full((POOL_W, POOL_W)), full((1, POOL_W)),
            full((1, GMLP_W)), full((GMLP_GROUPS, GMLP_CHUNK, GMLP_CHUNK)), full((GMLP_CHUNK, GMLP_W)),
        ],
        out_specs=[_split_spec(tm), _split_spec(tm), _split_spec(tm), out(POOL_W), out(GMLP_W)],
        out_shape=[
            _split_shape(B, n), _split_shape(B, n), _split_shape(B, n),
            jax.ShapeDtypeStruct((B, n, POOL_W), BF16),
            jax.ShapeDtypeStruct((B, n, GMLP_W), BF16),
        ],
        compiler_params=_cparams(("parallel", "parallel")),
        name="local",
    )(p, p, p, p, p, p, p, p, lw["conv_w"], lw["conv_b"], lw["pool_bd"], lw["pool_scale"],
      lw["gm_norm_w"], lw["gm_ws"], lw["gm_bias"])


def _filt_kernel(z_ref, t_ref, w1h_ref, w1l_ref, b1_ref, w2h_ref, w2l_ref, b2_ref, fr_ref,
                 w3h_ref, w3l_ref, dec_ref, k_ref, s_ref, *, n_seq):
    i = pl.program_id(0)
    tb = z_ref.shape[1]
    fr = _lane_tile(fr_ref[...], tb)
    hdn = jnp.sin(fr * (_dot3(w1h_ref[...], w1l_ref[...], z_ref[...]) + _lane_tile(b1_ref[...], tb)))
    for u in range(HYENA_INNER):
        hdn = jnp.sin(fr * (_dot3(w2h_ref[u], w2l_ref[u], hdn) + _lane_tile(b2_ref[u], tb)))
    hdn = hdn.T
    t = t_ref[...]
    tt = jnp.concatenate([t] * (HYENA_ORDER * HYENA_W // LANES), axis=1)
    window = jnp.exp(-tt * jnp.abs(dec_ref[...]))
    k = _dot3x(hdn, w3h_ref[...], w3l_ref[...]) * window
    rows = lax.broadcasted_iota(jnp.int32, k.shape, 0) + i * tb
    k = jnp.where(rows == n_seq, 0.0, k)
    for q in range(k_ref.shape[0]):
        k_ref[q] = k[:, q * LANES:(q + 1) * LANES]

    @pl.when(i == 0)
    def _():
        s_ref[...] = jnp.zeros(s_ref.shape, F32)

    s_ref[...] += jnp.sum(jnp.abs(k), axis=0, keepdims=True)


def _filt_call(n, lw):
    N = 2 * n
    tb = _tile(n, 512)
    nb_half = n // tb
    pos = np.arange(N)
    pos = np.where(pos < n, pos, N - pos).astype(np.float64)
    t = pos / (n - 1)
    bands = np.linspace(1e-4, HYENA_BANDS - 1, HYENA_BANDS)
    ang = (2.0 * math.pi / n) * pos[:, None] * bands
    z = np.concatenate([t[:, None], np.cos(ang), np.sin(ang)], axis=-1)
    zp = np.zeros((LANES, N), np.float32)
    zp[:z.shape[1]] = z.T
    tl = np.broadcast_to(t[:, None], (N, LANES)).astype(np.float32)
    CW = HYENA_ORDER * HYENA_W

    def full(shape):
        return pl.BlockSpec(shape, lambda i: (0,) * len(shape))

    def side(rows_):
        return pl.BlockSpec((None, rows_, CW), lambda i: (i // nb_half, 0, 0))

    return pl.pallas_call(
        functools.partial(_filt_kernel, n_seq=n),
        grid=(N // tb,),
        in_specs=[
            pl.BlockSpec((LANES, tb), lambda i: (0, i)),
            pl.BlockSpec((tb, LANES), lambda i: (i, 0)),
            full((HYENA_HIDDEN, LANES)), full((HYENA_HIDDEN, LANES)), full((HYENA_HIDDEN, LANES)),
            full((HYENA_INNER, HYENA_HIDDEN, HYENA_HIDDEN)), full((HYENA_INNER, HYENA_HIDDEN, HYENA_HIDDEN)),
            full((HYENA_INNER, HYENA_HIDDEN, LANES)), full((HYENA_HIDDEN, LANES)),
            side(HYENA_HIDDEN), side(HYENA_HIDDEN), side(1),
        ],
        out_specs=[pl.BlockSpec((CW // LANES, tb, LANES), lambda i: (0, i, 0)), pl.BlockSpec((1, CW), lambda i: (0, 0))],
        out_shape=[jax.ShapeDtypeStruct((CW // LANES, N, LANES), F32), jax.ShapeDtypeStruct((1, CW), F32)],
        compiler_params=_cparams(("arbitrary",)),
        name="filt",
    )(jnp.asarray(zp), jnp.asarray(tl), lw["hy_w1h"], lw["hy_w1l"], lw["hy_b1"], lw["hy_w2h"], lw["hy_w2l"],
      lw["hy_b2"], lw["hy_freq"], lw["hy_w3h"], lw["hy_w3l"], lw["hy_decay"])


def _fft_dims(N):
    e = int(round(math.log2(N)))
    assert 2 ** e == N
    N1 = 2 ** ((e + 1) // 2)
    return N1, N // N1


def _hilo(a):
    a = np.asarray(a, np.float64)
    hi = jnp.asarray(a, F32).astype(BF16)
    lo = (jnp.asarray(a, F32) - hi.astype(F32)).astype(BF16)
    return hi, lo


@functools.lru_cache(maxsize=None)
def _fft_consts_np(N):
    N1, N2 = _fft_dims(N)
    h = N1 // 2
    k1 = np.arange(N1)
    F1 = np.exp(-2j * np.pi * np.outer(k1, k1) / N1)
    k2 = np.arange(N2)
    F2 = np.exp(-2j * np.pi * np.outer(k2, k2) / N2)
    tw = np.exp(-2j * np.pi * np.outer(k1, k2) / N)
    g1 = np.zeros((2 * N1, N1))
    g1[0::2, :h] = F1.real[:, :h]
    g1[0::2, h:] = -F1.imag[:, :h]
    g1[1::2, :h] = F1.imag[:, :h]
    g1[1::2, h:] = F1.real[:, :h]
    g1f = np.zeros((2 * N1, N1))
    g1f[0::2] = F1.real
    g1f[1::2] = F1.imag
    g2 = np.block([[F2.real, -F2.imag], [F2.imag, F2.real]])
    g2c = np.block([[F2.real, F2.imag], [-F2.imag, F2.real]])
    g3 = np.zeros((N1, 2 * N1))
    g3[:h, 0::2] = F1.real[:h]
    g3[:h, 1::2] = F1.imag[:h]
    g3[h:, 0::2] = -F1.imag[:h]
    g3[h:, 1::2] = F1.real[:h]
    g3 /= N
    twt = np.stack([tw.real, tw.imag], axis=1)
    return dict(g1=g1, g1f=g1f, g2=g2, g2c=g2c, g3=g3, tw=twt)


def _fft_consts(N):
    c = _fft_consts_np(N)
    out = {k: _hilo(c[k]) for k in ("g1", "g1f", "g2", "g2c", "g3")}
    N1, N2 = _fft_dims(N)
    tw = jnp.asarray(c["tw"], F32)
    out["tw"] = jnp.broadcast_to(tw[..., None], (N1, 2, N2, LANES))
    return out


T2_BLOCK = 8


def _load_8th(ref, j, rows):
    return ref.reshape(rows * T2_BLOCK, LANES)[pl.ds(j, rows, stride=T2_BLOCK), :]


def _store_8th(ref, j, rows, val):
    ref.reshape(rows * T2_BLOCK, LANES)[pl.ds(j, rows, stride=T2_BLOCK), :] = val


def _dft_in_kernel(gh_ref, gl_ref, x_ref, o_ref):
    N1 = x_ref.shape[1]
    for j in range(T2_BLOCK):
        xj = jnp.concatenate([_load_8th(x_ref.at[hf], j, N1) for hf in range(HALVES)], axis=1)
        a = _dot3(gh_ref[...], gl_ref[...], xj)
        for hf in range(HALVES):
            _store_8th(o_ref.at[hf], j, 2 * N1, a[:, hf * LANES:(hf + 1) * LANES])


def _dft_in_call(g, u5):
    gh, gl = g
    G, _, N1, N2, _ = u5.shape
    assert gh.shape == (2 * N1, N1)
    mat = pl.BlockSpec((2 * N1, N1), lambda b, i: (0, 0))
    return pl.pallas_call(
        _dft_in_kernel,
        grid=(G, N2 // T2_BLOCK),
        in_specs=[mat, mat, pl.BlockSpec((None, HALVES, N1, T2_BLOCK, LANES), lambda b, i: (b, 0, 0, i, 0))],
        out_specs=pl.BlockSpec((None, HALVES, N1, 2, T2_BLOCK, LANES), lambda b, i: (b, 0, 0, 0, i, 0)),
        out_shape=jax.ShapeDtypeStruct((G, HALVES, N1, 2, N2, LANES), F32),
        compiler_params=_cparams(("parallel", "parallel")),
        name="dft_in",
    )(gh, gl, u5)


def _dft_out_kernel(gh_ref, gl_ref, d_ref, gate_ref, o_ref):
    N1 = gate_ref.shape[1]
    for j in range(T2_BLOCK):
        dj = jnp.concatenate([_load_8th(d_ref.at[hf], j, 2 * N1) for hf in range(HALVES)], axis=1)
        y = _dot3(gh_ref[...], gl_ref[...], dj)
        for hf in range(HALVES):
            conv = y[:, hf * LANES:(hf + 1) * LANES]
            _store_8th(o_ref.at[hf], j, N1, _load_8th(gate_ref.at[hf], j, N1) * conv)


def _dft_out_call(g, d6, gate5):
    gh, gl = g
    G, _, N1, N2, _ = gate5.shape
    assert gh.shape == (N1, 2 * N1)
    mat = pl.BlockSpec((N1, 2 * N1), lambda b, i: (0, 0))
    blk = pl.BlockSpec((None, HALVES, N1, T2_BLOCK, LANES), lambda b, i: (b, 0, 0, i, 0))
    return pl.pallas_call(
        _dft_out_kernel,
        grid=(G, N2 // T2_BLOCK),
        in_specs=[mat, mat, pl.BlockSpec((None, HALVES, N1, 2, T2_BLOCK, LANES), lambda b, i: (b, 0, 0, 0, i, 0)),
                  blk],
        out_specs=blk,
        out_shape=jax.ShapeDtypeStruct(gate5.shape, F32),
        compiler_params=_cparams(("parallel", "parallel")),
        name="dft_out",
    )(gh, gl, d6, gate5)


def _dft_mid_kernel(g3h_ref, g3l_ref, g1h_ref, g1l_ref, d_ref, gate_ref, o_ref):
    N1 = gate_ref.shape[1]
    for j in range(T2_BLOCK):
        dj = jnp.concatenate([_load_8th(d_ref.at[hf], j, 2 * N1) for hf in range(HALVES)], axis=1)
        y = _dot3(g3h_ref[...], g3l_ref[...], dj)
        gate = jnp.concatenate([_load_8th(gate_ref.at[hf], j, N1) for hf in range(HALVES)], axis=1)
        a = _dot3(g1h_ref[...], g1l_ref[...], gate * y)
        for hf in range(HALVES):
            _store_8th(o_ref.at[hf], j, 2 * N1, a[:, hf * LANES:(hf + 1) * LANES])


def _dft_mid_call(g3, g1, d6, gate5):
    G, _, N1, N2, _ = gate5.shape
    assert g3[0].shape == (N1, 2 * N1) and g1[0].shape == (2 * N1, N1)
    mat3 = pl.BlockSpec((N1, 2 * N1), lambda b, i: (0, 0))
    mat1 = pl.BlockSpec((2 * N1, N1), lambda b, i: (0, 0))
    wide = pl.BlockSpec((None, HALVES, N1, 2, T2_BLOCK, LANES), lambda b, i: (b, 0, 0, 0, i, 0))
    return pl.pallas_call(
        _dft_mid_kernel,
        grid=(G, N2 // T2_BLOCK),
        in_specs=[mat3, mat3, mat1, mat1, wide,
                  pl.BlockSpec((None, HALVES, N1, T2_BLOCK, LANES), lambda b, i: (b, 0, 0, i, 0))],
        out_specs=wide,
        out_shape=jax.ShapeDtypeStruct(d6.shape, F32),
        compiler_params=_cparams(("parallel", "parallel")),
        name="dft_mid",
    )(g3[0], g3[1], g1[0], g1[1], d6, gate5)


def _cmul(ar, ai, br, bi):
    return ar * br - ai * bi, ar * bi + ai * br


def _lane_tile(t, c):
    return jnp.concatenate([t] * (c // LANES), axis=-1)


def _both_halves(a_ref, k, part):
    return jnp.concatenate([a_ref[hf, k, part] for hf in range(HALVES)], axis=-1)


def _spec_kernel(a_ref, tw_ref, g2h_ref, g2l_ref, s_ref, sk_ref, o_ref):
    _, kb, _, N2, _ = a_ref.shape
    inv = 1.0 / s_ref[...]
    for k in range(kb):
        twr = _lane_tile(tw_ref[k, 0], HYENA_W)
        twi = _lane_tile(tw_ref[k, 1], HYENA_W)
        br, bi = _cmul(_both_halves(a_ref, k, 0), _both_halves(a_ref, k, 1), twr, twi)
        zz = _dot3(g2h_ref[...], g2l_ref[...], jnp.concatenate([br, bi], axis=0))
        o_ref[k, 0] = zz[:N2] * inv + sk_ref[...]
        o_ref[k, 1] = zz[N2:] * inv


def _conv_kernel(a_ref, tw_ref, kf_ref, g2h_ref, g2l_ref, g2ch_ref, g2cl_ref, o_ref):
    _, kb, _, N2, _ = a_ref.shape
    for k in range(kb):
        twr = _lane_tile(tw_ref[k, 0], HYENA_W)
        twi = _lane_tile(tw_ref[k, 1], HYENA_W)
        br, bi = _cmul(_both_halves(a_ref, k, 0), _both_halves(a_ref, k, 1), twr, twi)
        zz = _dot3(g2h_ref[...], g2l_ref[...], jnp.concatenate([br, bi], axis=0))
        yr, yi = _cmul(zz[:N2], zz[N2:], kf_ref[k, 0], kf_ref[k, 1])
        cc = _dot3(g2ch_ref[...], g2cl_ref[...], jnp.concatenate([yr, yi], axis=0))
        dr, di = _cmul(cc[:N2], cc[N2:], twr, -twi)
        for hf in range(HALVES):
            o_ref[hf, k, 0] = dr[:, hf * LANES:(hf + 1) * LANES]
            o_ref[hf, k, 1] = di[:, hf * LANES:(hf + 1) * LANES]


SPEC_K1 = 16


def _spec_call(a6, fc, s, skip):
    G, _, N1, _, N2, _ = a6.shape
    kb = _tile(N1, SPEC_K1, 1)
    mat = pl.BlockSpec((2 * N2, 2 * N2), lambda i, b: (0, 0))
    return pl.pallas_call(
        _spec_kernel,
        grid=(N1 // kb, G),
        in_specs=[pl.BlockSpec((None, HALVES, kb, 2, N2, LANES), lambda i, b: (b, 0, i, 0, 0, 0)),
                  pl.BlockSpec((kb, 2, N2, LANES), lambda i, b: (i, 0, 0, 0)), mat, mat,
                  pl.BlockSpec((None, 1, HYENA_W), lambda i, b: (b, 0, 0)),
                  pl.BlockSpec((None, 1, HYENA_W), lambda i, b: (b, 0, 0))],
        out_specs=pl.BlockSpec((None, kb, 2, N2, HYENA_W), lambda i, b: (b, i, 0, 0, 0)),
        out_shape=jax.ShapeDtypeStruct((G, N1, 2, N2, HYENA_W), F32),
        compiler_params=_cparams(("parallel", "parallel")),
        name="filt_spec",
    )(a6, fc["tw"], fc["g2"][0], fc["g2"][1], s, skip)


def _conv_call(a6, kf, order, fc):
    G, _, N1, _, N2, _ = a6.shape
    kb = _tile(N1, SPEC_K1, 1)
    blk = pl.BlockSpec((None, HALVES, kb, 2, N2, LANES), lambda i, b: (b, 0, i, 0, 0, 0))
    mat = pl.BlockSpec((2 * N2, 2 * N2), lambda i, b: (0, 0))
    return pl.pallas_call(
        _conv_kernel,
        grid=(N1 // kb, G),
        in_specs=[blk, pl.BlockSpec((kb, 2, N2, LANES), lambda i, b: (i, 0, 0, 0)),
                  pl.BlockSpec((None, kb, 2, N2, HYENA_W), lambda i, b: (order, i, 0, 0, 0)), mat, mat, mat, mat],
        out_specs=blk,
        out_shape=jax.ShapeDtypeStruct(a6.shape, F32),
        compiler_params=_cparams(("parallel", "parallel")),
        name="spec_conv",
    )(a6, fc["tw"], kf, fc["g2"][0], fc["g2"][1], fc["g2c"][0], fc["g2c"][1])


def _filter_spectra(n, lw, fc):
    N1, N2 = _fft_dims(2 * n)
    k, s = _filt_call(n, lw)
    a6 = _dft_in_call(fc["g1f"], k.reshape(HYENA_ORDER, HALVES, N1, N2, LANES))
    return _spec_call(a6, fc, s.reshape(HYENA_ORDER, 1, HYENA_W),
                      lw["hy_skip"].reshape(HYENA_ORDER, 1, HYENA_W))


def _hyena(v, x1, x2, kf, lw, fc):
    P, _, _, n, _ = v.shape
    N1, N2 = _fft_dims(2 * n)
    shape5 = (P, HALVES, N1, N2, LANES)
    gates = (x1, x2)
    a = _dft_in_call(fc["g1"], v.reshape(shape5))
    for o in range(HYENA_ORDER - 1):
        a = _dft_mid_call(fc["g3"], fc["g1"], _conv_call(a, kf, o, fc), gates[o].reshape(shape5))
    y = _dft_out_call(fc["g3"], _conv_call(a, kf, HYENA_ORDER - 1, fc), gates[-1].reshape(shape5))
    return y.reshape(v.shape)


def _merge_kernel(at_ref, hy_ref, po_ref, gm_ref, x_ref, sh_ref, sc_ref, nw_ref, gr_ref,
                  wgate_ref, wa_ref, wh_ref, wp_ref, wg_ref, wo_ref, o_ref):
    D = x_ref.shape[1]
    for rows in _row_blocks(x_ref.shape[0]):
        x = x_ref[rows, :]
        h = _modulated_norm(x, nw_ref[...], sc_ref[...], sh_ref[...]).astype(BF16)
        hy = jnp.concatenate([hy_ref[hf, rows, :] for hf in range(HALVES)], axis=-1).astype(BF16)
        branches = ((at_ref[rows, :], wa_ref), (hy, wh_ref), (po_ref[rows, :], wp_ref), (gm_ref[rows, :], wg_ref))
        merged = None
        for i, (br, w_ref) in enumerate(branches):
            term = _sigmoid(_dot(h, wgate_ref[:, i * D:(i + 1) * D])) * _dot(br, w_ref[...])
            merged = term if merged is None else merged + term
        o_ref[rows, :] = x + gr_ref[...] * _dot(merged.astype(BF16), wo_ref[...])


MERGE_ROWS = 1024


def _merge_call(attn, hy, po, gm, x, shift, scale, norm_w, gate_res, lw):
    B, n, D = x.shape
    tm = _tile(n, MERGE_ROWS)

    def rows(w):
        return pl.BlockSpec((None, tm, w), lambda b, i: (b, i, 0))

    def full(shape):
        return pl.BlockSpec(shape, lambda b, i: (0,) * len(shape))

    vec = pl.BlockSpec((None, 1, D), lambda b, i: (b, 0, 0))
    return pl.pallas_call(
        _merge_kernel,
        grid=(B, n // tm),
        in_specs=[
            rows(ATTN_W), _split_spec(tm), rows(POOL_W), rows(GMLP_W), rows(D),
            vec, vec, full((1, D)), vec,
            full((D, N_BRANCH * D)),
            full((ATTN_W, D)), full((HYENA_W, D)), full((POOL_W, D)), full((GMLP_W, D)), full((D, D)),
        ],
        out_specs=rows(D),
        out_shape=jax.ShapeDtypeStruct((B, n, D), F32),
        compiler_params=_cparams(("parallel", "parallel")),
        name="merge",
    )(attn, hy, po, gm, x, shift, scale, norm_w.reshape(1, D), gate_res, lw["w_gate"], lw["w_br_attn"],
      lw["w_br_hyena"], lw["w_br_pool"], lw["w_br_gmlp"], lw["w_out"])


FFN_ROWS = 1024


def _ffn_kernel(x_ref, sh_ref, sc_ref, nw_ref, gr_ref, wg_ref, wu_ref, wd_ref, fw_ref, o_ref, *, final):
    for rows in _row_blocks(x_ref.shape[0]):
        x = x_ref[rows, :]
        h = _modulated_norm(x, nw_ref[...], sc_ref[...], sh_ref[...]).astype(BF16)
        a = _silu(_dot(h, wg_ref[...])) * _dot(h, wu_ref[...])
        y = x + gr_ref[...] * _dot(a.astype(BF16), wd_ref[...])
        if final:
            y = y * lax.rsqrt(jnp.mean(y * y, axis=-1, keepdims=True) + RMS_EPS) * fw_ref[...]
        o_ref[rows, :] = y


def _ffn_call(x, shift, scale, norm_w, gate_res, wg, wu, wd, final_w=None):
    B, n, D = x.shape
    H = wg.shape[1]
    tm = _tile(n, FFN_ROWS)
    vec = pl.BlockSpec((None, 1, D), lambda b, i: (b, 0, 0))
    one = pl.BlockSpec((1, D), lambda b, i: (0, 0))
    final = final_w is not None
    fw = (final_w if final else norm_w).reshape(1, D)
    return pl.pallas_call(
        functools.partial(_ffn_kernel, final=final),
        grid=(B, n // tm),
        in_specs=[
            pl.BlockSpec((None, tm, D), lambda b, i: (b, i, 0)),
            vec, vec, one, vec,
            pl.BlockSpec((D, H), lambda b, i: (0, 0)),
            pl.BlockSpec((D, H), lambda b, i: (0, 0)),
            pl.BlockSpec((H, D), lambda b, i: (0, 0)),
            one,
        ],
        out_specs=pl.BlockSpec((None, tm, D), lambda b, i: (b, i, 0)),
        out_shape=jax.ShapeDtypeStruct((B, n, D), F32),
        compiler_params=_cparams(("parallel", "parallel")),
        name="ffn",
    )(x, shift, scale, norm_w.reshape(1, D), gate_res, wg, wu, wd, fw)


def _rope_tables(n):
    tok = np.arange(n)
    axis_dim = HEAD_DIM // 2
    inv_freq = ROPE_THETA ** (-np.arange(0, axis_dim, 2, dtype=np.float64) / axis_dim)
    inv_freq = inv_freq.astype(np.float32).astype(np.float64)
    ang = np.concatenate([(tok // GRID_W)[:, None] * inv_freq, (tok % GRID_W)[:, None] * inv_freq], axis=-1)
    ang = ang.astype(np.float32)
    cos = np.repeat(np.cos(ang), 2, axis=-1)
    sin = np.repeat(np.sin(ang), 2, axis=-1)
    sin[:, 0::2] *= -1.0
    reps = LANES // HEAD_DIM
    return jnp.asarray(np.tile(cos, (1, reps)), F32), jnp.asarray(np.tile(sin, (1, reps)), F32)


def _identity_tables(n):
    return jnp.ones((n, LANES), F32), jnp.zeros((n, LANES), F32)


def _layer_weights(l, a):
    w_in = a["w_in"][l]
    w_perm = jnp.concatenate([
        w_in[:, Q_OFF:K_OFF], w_in[:, K_OFF:HY_OFF], w_in[:, POOL_OFF:GM_OFF], w_in[:, GM_OFF:GATE_OFF],
        w_in[:, HY_OFF:POOL_OFF]], axis=1).astype(BF16)
    w3 = a["hy_w3"][l].reshape(HYENA_HIDDEN, HYENA_ORDER, 2, HYENA_W).transpose(2, 0, 1, 3)
    w3 = w3.reshape(2, HYENA_HIDDEN, HYENA_ORDER * HYENA_W)
    dec = a["hy_decay"][l].reshape(HYENA_ORDER, 2, HYENA_W).transpose(1, 0, 2).reshape(2, 1, HYENA_ORDER * HYENA_W)
    w1 = jnp.zeros((HYENA_HIDDEN, LANES), F32).at[:, :a["hy_w1"].shape[1]].set(a["hy_w1"][l].T)
    w1h, w1l = _split(w1)
    w2h, w2l = _split(jnp.swapaxes(a["hy_w2"][l], 1, 2))

    def lanes_of(v):
        return jnp.broadcast_to(v[..., None], v.shape + (LANES,))
    w3h, w3l = _split(w3)
    qw16 = jnp.tile(a["q_norm_w"][l], N_Q_HEADS).reshape(1, ATTN_W) * Q_SCALE
    kw16 = jnp.tile(a["k_norm_w"][l], N_KV_HEADS).reshape(1, KV_W)
    rms_q = jnp.sqrt(jnp.mean(qw16 * qw16))
    rms_k = jnp.sqrt(jnp.mean(kw16 * kw16))
    bal = jnp.exp2(jnp.clip(jnp.round(0.5 * jnp.log2(rms_k / rms_q)), -20.0, 20.0))
    bal = jnp.where(jnp.isfinite(bal), bal, 1.0)
    fp8_ok = (jnp.max(jnp.abs(qw16 * bal)) <= FP8_MAX_WEIGHT) & (jnp.max(jnp.abs(kw16 / bal)) <= FP8_MAX_WEIGHT)
    eye = jnp.eye(len(POOL_WINDOWS), dtype=F32)
    pool_bd = jnp.einsum("gh,gcd->gchd", eye, a["pool_w"][l]).reshape(POOL_W, POOL_W).astype(BF16)
    gm_bias = jnp.repeat(a["gm_bs"][l].T, GMLP_GROUP_W, axis=1)
    return dict(
        w_in=w_perm, w_gate=w_in[:, GATE_OFF:].astype(BF16),
        qw16=qw16, kw16=kw16, qw8=qw16 * bal, kw8=kw16 / bal, fp8_ok=fp8_ok,
        conv_w=a["hy_conv_w"][l], conv_b=a["hy_conv_b"][l].reshape(1, -1),
        hy_w1h=w1h, hy_w1l=w1l, hy_b1=lanes_of(a["hy_b1"][l]), hy_w2h=w2h, hy_w2l=w2l, hy_b2=lanes_of(a["hy_b2"][l]),
        hy_w3h=w3h, hy_w3l=w3l, hy_freq=lanes_of(a["hy_freq"][l]), hy_decay=dec, hy_skip=a["hy_skip"][l],
        pool_bd=pool_bd, pool_scale=a["pool_scale"][l].reshape(1, -1),
        gm_norm_w=a["gm_norm_w"][l].reshape(1, -1), gm_ws=a["gm_ws"][l].astype(BF16), gm_bias=gm_bias,
        w_br_attn=a["w_br_attn"][l].astype(BF16), w_br_hyena=a["w_br_hyena"][l].astype(BF16),
        w_br_pool=a["w_br_pool"][l].astype(BF16), w_br_gmlp=a["w_br_gmlp"][l].astype(BF16),
        w_out=a["w_out"][l].astype(BF16),
        ffn_w_gate=a["ffn_w_gate"][l].astype(BF16), ffn_w_up=a["ffn_w_up"][l].astype(BF16),
        ffn_w_down=a["ffn_w_down"][l].astype(BF16),
        norm1_w=a["norm1_w"][l], norm2_w=a["norm2_w"][l],
    )


Q_SCALE = (HEAD_DIM ** -0.5) * math.log2(math.e)


def _attend(p, p_ctx, rope, rope_ctx, lw):
    def run(fp8):
        qw, kw = (lw["qw8"], lw["kw8"]) if fp8 else (lw["qw16"], lw["kw16"])
        qt, k, vt = _qkprep_call(p, rope[0], rope[1], qw, kw, fp8)
        if p_ctx is not None:
            _, k_c, vt_c = _qkprep_call(p_ctx, rope_ctx[0], rope_ctx[1], qw, kw, fp8)
            k = jnp.concatenate([k_c, k], axis=2)
            vt = jnp.concatenate([vt_c, vt], axis=3)
        return _attn_call(qt, k, vt)

    return lax.cond(lw["fp8_ok"], lambda: run(True), lambda: run(False))


def _mixer_block(x, mods, lw, rope, p_ctx, rope_ctx, kf, fc, final_w=None):
    sh1, s1, g1, sh2, s2, g2 = mods
    p = _inproj_call(x, sh1, s1, lw["norm1_w"], lw["w_in"])
    attn = _attend(p, p_ctx, rope, rope_ctx, lw)
    hv, hx1, hx2, po, gm = _local_call(p, lw)
    hy = _hyena(hv, hx1, hx2, kf, lw, fc)
    x = _merge_call(attn, hy, po, gm, x, sh1, s1, lw["norm1_w"], g1, lw)
    x = _ffn_call(x, sh2, s2, lw["norm2_w"], g2, lw["ffn_w_gate"], lw["ffn_w_up"], lw["ffn_w_down"], final_w)
    return x, p


def kernel(x, c, ctx, c_ctx, w_mod, b_mod, norm1_w, norm2_w, w_in, q_norm_w, k_norm_w, hy_conv_w, hy_conv_b,
           hy_w1, hy_b1, hy_w2, hy_b2, hy_w3, hy_freq, hy_decay, hy_skip, pool_w, pool_scale, gm_norm_w, gm_ws,
           gm_bs, w_br_attn, w_br_hyena, w_br_pool, w_br_gmlp, w_out, ffn_w_gate, ffn_w_up, ffn_w_down,
           final_norm_w):
    a = dict(w_in=w_in, q_norm_w=q_norm_w, k_norm_w=k_norm_w, hy_conv_w=hy_conv_w, hy_conv_b=hy_conv_b,
             hy_w1=hy_w1, hy_b1=hy_b1, hy_w2=hy_w2, hy_b2=hy_b2, hy_w3=hy_w3, hy_freq=hy_freq, hy_decay=hy_decay,
             hy_skip=hy_skip, pool_w=pool_w, pool_scale=pool_scale, gm_norm_w=gm_norm_w, gm_ws=gm_ws, gm_bs=gm_bs,
             w_br_attn=w_br_attn, w_br_hyena=w_br_hyena, w_br_pool=w_br_pool, w_br_gmlp=w_br_gmlp, w_out=w_out,
             ffn_w_gate=ffn_w_gate, ffn_w_up=ffn_w_up, ffn_w_down=ffn_w_down, norm1_w=norm1_w, norm2_w=norm2_w)
    B, n, D = x.shape
    nc = ctx.shape[1]
    depth = w_mod.shape[0]

    R = ((B + 1 + 15) // 16) * 16
    act = jnp.zeros((R, D), F32).at[:B].set(c).at[B].set(c_ctx)
    mod = _mod_call(act, w_mod, b_mod)

    rope = _rope_tables(n)
    rope_ctx = _identity_tables(nc)
    fc = _fft_consts(2 * n)
    fc_ctx = _fft_consts(2 * nc)

    xc = ctx
    for l in range(depth):
        lw = _layer_weights(l, a)
        mods = [m.reshape(B, 1, D) for m in jnp.split(mod[l, :B], 6, axis=-1)]
        cmods = [jnp.broadcast_to(m.reshape(1, 1, D), (B, 1, D)) for m in jnp.split(mod[l, B], 6, axis=-1)]
        if l < depth - 1:
            kf_ctx = _filter_spectra(nc, lw, fc_ctx)
            xc, pc = _mixer_block(xc, cmods, lw, rope_ctx, None, None, kf_ctx, fc_ctx)
        else:
            pc = _inproj_call(xc, cmods[0], cmods[1], lw["norm1_w"], lw["w_in"][:, :P_QKV])
        kf = _filter_spectra(n, lw, fc)
        x, _ = _mixer_block(x, mods, lw, rope, pc, rope_ctx, kf, fc, final_norm_w if l == depth - 1 else None)
    return x
```

```python
import functools
import math

import numpy as np
import jax
import jax.numpy as jnp
from jax import lax
from jax.experimental import pallas as pl
from jax.experimental.pallas import tpu as pltpu

F32 = jnp.float32
BF16 = jnp.bfloat16

D_MODEL = 1024
GRID_W = 64
RMS_EPS = 1e-6
HEAD_DIM = 64
N_Q_HEADS = 8
N_KV_HEADS = 2
ATTN_W = N_Q_HEADS * HEAD_DIM
KV_W = N_KV_HEADS * HEAD_DIM
ROPE_THETA = 10000.0
HYENA_W = 256
HYENA_ORDER = 2
HYENA_BANDS = 16
HYENA_HIDDEN = 64
HYENA_INNER = 2
POOL_W = 256
POOL_WINDOWS = (2, 4, 8, 16)
POOL_GROUP_W = POOL_W // len(POOL_WINDOWS)
POOL_HALO = 8
GMLP_W = 256
GMLP_CHUNK = 128
GMLP_GROUPS = 4
GMLP_GROUP_W = GMLP_W // GMLP_GROUPS
N_BRANCH = 4

Q_OFF = 0
K_OFF = Q_OFF + ATTN_W
V_OFF = K_OFF + KV_W
HY_OFF = V_OFF + KV_W
POOL_OFF = HY_OFF + 3 * HYENA_W
GM_OFF = POOL_OFF + POOL_W
GATE_OFF = GM_OFF + 2 * GMLP_W

P_Q = 0
P_KV = P_Q + ATTN_W
P_POOL = P_KV + 2 * KV_W
P_GMU = P_POOL + POOL_W
P_GMV = P_GMU + GMLP_W
P_HY = P_GMV + GMLP_W
P_W = P_HY + 3 * HYENA_W
P_QKV = P_POOL

LANES = 128
VMEM_LIMIT = 52 * 1024 * 1024


def _cparams(sem):
    return pltpu.CompilerParams(dimension_semantics=sem, vmem_limit_bytes=VMEM_LIMIT)


def _tile(n, cap, mult=8):
    best = None
    for t in range(mult, min(n, cap) + 1, mult):
        if n % t == 0:
            best = t
    assert best is not None, (n, cap, mult)
    return best


def _split(x):
    hi = x.astype(BF16)
    lo = (x - hi.astype(F32)).astype(BF16)
    return hi, lo


def _dot(a, b):
    return jnp.dot(a, b, preferred_element_type=F32)


def _dot3(a_hi, a_lo, x):
    x_hi, x_lo = _split(x)
    return _dot(a_hi, x_hi) + (_dot(a_lo, x_hi) + _dot(a_hi, x_lo))


def _dot3x(x, b_hi, b_lo):
    x_hi, x_lo = _split(x)
    return _dot(x_hi, b_hi) + (_dot(x_lo, b_hi) + _dot(x_hi, b_lo))


def _sigmoid(x):
    return 1.0 / (1.0 + jnp.exp(-x))


def _silu(x):
    return x * _sigmoid(x)


def _modulated_norm(x, nw, scale, shift):
    ms = jnp.mean(x * x, axis=-1, keepdims=True)
    y = x * lax.rsqrt(ms + RMS_EPS) * nw
    return y * (1.0 + scale) + shift


def _mod_kernel(a_ref, w_ref, b_ref, o_ref):
    a = _silu(a_ref[...]).astype(BF16)
    o_ref[...] = _dot(a, w_ref[...].astype(BF16)) + b_ref[...]


def _mod_call(act, w_mod, b_mod):
    L, D, W6 = w_mod.shape
    R = act.shape[0]
    tn = 1024
    return pl.pallas_call(
        _mod_kernel,
        grid=(L, W6 // tn),
        in_specs=[
            pl.BlockSpec((R, D), lambda l, j: (0, 0)),
            pl.BlockSpec((None, D, tn), lambda l, j: (l, 0, j)),
            pl.BlockSpec((None, 1, tn), lambda l, j: (l, 0, j)),
        ],
        out_specs=pl.BlockSpec((None, R, tn), lambda l, j: (l, 0, j)),
        out_shape=jax.ShapeDtypeStruct((L, R, W6), F32),
        compiler_params=_cparams(("arbitrary", "arbitrary")),
        name="mod",
    )(act, w_mod, b_mod.reshape(L, 1, W6))


ROW_BLOCK = 256
INPROJ_ROWS = 1024


def _row_blocks(tm):
    sub = ROW_BLOCK if tm % ROW_BLOCK == 0 else tm
    return [slice(r, r + sub) for r in range(0, tm, sub)]


def _inproj_kernel(x_ref, sh_ref, sc_ref, nw_ref, w_ref, o_ref):
    for rows in _row_blocks(x_ref.shape[0]):
        h = _modulated_norm(x_ref[rows, :], nw_ref[...], sc_ref[...], sh_ref[...]).astype(BF16)
        o_ref[rows, :] = _dot(h, w_ref[...])


def _inproj_call(x, shift, scale, norm_w, w):
    B, n, D = x.shape
    Nw = w.shape[1]
    tm = _tile(n, INPROJ_ROWS)
    return pl.pallas_call(
        _inproj_kernel,
        grid=(B, n // tm),
        in_specs=[
            pl.BlockSpec((None, tm, D), lambda b, i: (b, i, 0)),
            pl.BlockSpec((None, 1, D), lambda b, i: (b, 0, 0)),
            pl.BlockSpec((None, 1, D), lambda b, i: (b, 0, 0)),
            pl.BlockSpec((1, D), lambda b, i: (0, 0)),
            pl.BlockSpec((D, Nw), lambda b, i: (0, 0)),
        ],
        out_specs=pl.BlockSpec((None, tm, Nw), lambda b, i: (b, i, 0)),
        out_shape=jax.ShapeDtypeStruct((B, n, Nw), F32),
        compiler_params=_cparams(("parallel", "parallel")),
        name="inproj",
    )(x, shift, scale, norm_w.reshape(1, D), w)


def _head_norm_rope(t, w, cs, sn, bd_ref):
    sq_hi, sq_lo = _split(t * t)
    ms = _dot(sq_hi, bd_ref[...]) + _dot(sq_lo, bd_ref[...])
    y = t * lax.rsqrt(ms + RMS_EPS) * w
    lane = lax.broadcasted_iota(jnp.int32, y.shape, 1)
    swapped = jnp.where(lane % 2 == 0, pltpu.roll(y, LANES - 1, 1), pltpu.roll(y, 1, 1))
    return y * cs + swapped * sn


V_ROWS = 80
Q_PER_KV = N_Q_HEADS // N_KV_HEADS
LAG_INIT_KEYS = 128
LAG_LIMIT = 12.0
ATTN_QUERIES = 512
ATTN_KEYS = 768
ATTN_UNROLL = 4
QKPREP_ROWS = 1024


def _value_rows(vt_g):
    tail = lax.broadcasted_iota(jnp.int32, (V_ROWS - HEAD_DIM, vt_g.shape[1]), 0)
    return jnp.concatenate([vt_g, jnp.where(tail == 0, 1.0, 0.0)], axis=0).astype(BF16)


FP8 = jnp.float8_e4m3fn
FP8_MAX_WEIGHT = 32.0


def _split8(x):
    hi = x.astype(FP8).astype(F32)
    return hi, (x - hi).astype(FP8).astype(F32)


def _qkprep_kernel(q_ref, kv_ref, cs_ref, sn_ref, qw_ref, kw_ref, bd_ref, kp_ref, qo_ref, ko_ref, vo_ref, *, fp8):
    cs = cs_ref[...]
    sn = sn_ref[...]
    tm = q_ref.shape[0]
    zeros = jnp.zeros((HEAD_DIM, tm), F32)
    heads_per_chunk = LANES // HEAD_DIM
    for c in range(ATTN_W // LANES):
        sl = slice(c * LANES, (c + 1) * LANES)
        r = _head_norm_rope(q_ref[:, sl], qw_ref[:, sl], cs, sn, bd_ref)
        if fp8:
            hi_t, lo_t = _split8(r.T)
            for half in range(heads_per_chunk):
                rows = slice(half * HEAD_DIM, (half + 1) * HEAD_DIM)
                qo_ref[c * heads_per_chunk + half] = jnp.concatenate(
                    [hi_t[rows], lo_t[rows], hi_t[rows], lo_t[rows]], axis=0).astype(FP8)
        else:
            rt = r.T
            for half in range(heads_per_chunk):
                h = c * heads_per_chunk + half
                blk = rt[half * HEAD_DIM:(half + 1) * HEAD_DIM]
                parts = [blk if g == h // Q_PER_KV else zeros for g in range(N_KV_HEADS)]
                qo_ref[h] = jnp.concatenate(parts, axis=0).astype(BF16)
    k = _head_norm_rope(kv_ref[:, :KV_W], kw_ref[...], cs, sn, bd_ref)
    if fp8:
        hi, lo = _split8(k)
        hi, lo = hi.astype(BF16), lo.astype(BF16)
        for g in range(N_KV_HEADS):
            ko_ref[g] = (_dot(hi, kp_ref[g, 0]) + _dot(lo, kp_ref[g, 1])).astype(FP8)
    else:
        ko_ref[0] = k.astype(BF16)
    vt = kv_ref[:, KV_W:].T
    for g in range(N_KV_HEADS):
        vo_ref[g] = _value_rows(vt[g * HEAD_DIM:(g + 1) * HEAD_DIM])


def _k_placement():
    pm = np.zeros((N_KV_HEADS, 2, KV_W, 4 * HEAD_DIM), np.float32)
    d = np.arange(HEAD_DIM)
    for g in range(N_KV_HEADS):
        for part in range(2):
            for rep in range(2):
                pm[g, part, g * HEAD_DIM + d, (2 * part + rep) * HEAD_DIM + d] = 1.0
    return jnp.asarray(pm, BF16)


def _qkprep_call(p, cs, sn, qw, kw, fp8):
    B, n, _ = p.shape
    tm = _tile(n, QKPREP_ROWS, LANES)
    bd = np.kron(np.eye(LANES // HEAD_DIM), np.full((HEAD_DIM, HEAD_DIM), 1.0 / HEAD_DIM))
    bd = jnp.asarray(bd, BF16)
    kd, kg, dt = (4 * HEAD_DIM, N_KV_HEADS, FP8) if fp8 else (KV_W, 1, BF16)
    return pl.pallas_call(
        functools.partial(_qkprep_kernel, fp8=fp8),
        grid=(n // tm, B),
        in_specs=[
            pl.BlockSpec((None, tm, ATTN_W), lambda i, b: (b, i, P_Q // ATTN_W)),
            pl.BlockSpec((None, tm, 2 * KV_W), lambda i, b: (b, i, P_KV // (2 * KV_W))),
            pl.BlockSpec((tm, LANES), lambda i, b: (i, 0)),
            pl.BlockSpec((tm, LANES), lambda i, b: (i, 0)),
            pl.BlockSpec((1, ATTN_W), lambda i, b: (0, 0)),
            pl.BlockSpec((1, KV_W), lambda i, b: (0, 0)),
            pl.BlockSpec((LANES, LANES), lambda i, b: (0, 0)),
            pl.BlockSpec((N_KV_HEADS, 2, KV_W, 4 * HEAD_DIM), lambda i, b: (0, 0, 0, 0)),
        ],
        out_specs=[
            pl.BlockSpec((None, N_Q_HEADS, kd, tm), lambda i, b: (b, 0, 0, i)),
            pl.BlockSpec((None, kg, tm, kd), lambda i, b: (b, 0, i, 0)),
            pl.BlockSpec((None, N_KV_HEADS, V_ROWS, tm), lambda i, b: (b, 0, 0, i)),
        ],
        out_shape=[
            jax.ShapeDtypeStruct((B, N_Q_HEADS, kd, n), dt),
            jax.ShapeDtypeStruct((B, kg, n, kd), dt),
            jax.ShapeDtypeStruct((B, N_KV_HEADS, V_ROWS, n), BF16),
        ],
        compiler_params=_cparams(("parallel", "parallel")),
        name="qkprep",
    )(p, p, cs, sn, qw, kw, bd, _k_placement())


def _attn_kernel(qt_ref, k_ref, vt_ref, o_ref, m_ref, acc_ref, *, tk):
    nk = vt_ref.shape[0]
    tq = m_ref.shape[-1]
    per_kv_head = k_ref.shape[0] > 1

    def keys(j, h):
        g = h // Q_PER_KV if per_kv_head else 0
        return k_ref[g, pl.ds(pl.multiple_of(j * tk, tk), tk), :]

    def lagged_chunk(j, jump):
        s_next = _dot(keys(j, 0), qt_ref[0])
        for h in range(N_Q_HEADS):
            s = s_next
            if h + 1 < N_Q_HEADS:
                s_next = _dot(keys(j, h + 1), qt_ref[h + 1])
            m_old = m_ref[h]
            p = jnp.exp2((s - m_old).astype(BF16))
            cmax = jnp.max(s, axis=0, keepdims=True)
            m_new = jnp.maximum(m_old, cmax)
            jump = jnp.maximum(jump, cmax - m_old)
            acc_ref[h] = jnp.exp2(m_old - m_new) * (acc_ref[h] + _dot(vt_ref[j, h // Q_PER_KV], p))
            m_ref[h] = m_new
        return jump

    def exact_chunk(j, carry):
        for h in range(N_Q_HEADS):
            s = _dot(keys(j, h), qt_ref[h])
            m_old = m_ref[h]
            m_new = jnp.maximum(m_old, jnp.max(s, axis=0, keepdims=True))
            p = jnp.exp2(s - m_new).astype(BF16)
            acc_ref[h] = jnp.exp2(m_old - m_new) * acc_ref[h] + _dot(vt_ref[j, h // Q_PER_KV], p)
            m_ref[h] = m_new
        return carry

    k_lo = (k_ref.shape[1] // 2) // LAG_INIT_KEYS * LAG_INIT_KEYS
    for h in range(N_Q_HEADS):
        k0 = k_ref[h // Q_PER_KV if per_kv_head else 0, k_lo:k_lo + LAG_INIT_KEYS, :]
        m_ref[h] = jnp.max(_dot(k0, qt_ref[h]), axis=0, keepdims=True)
    acc_ref[...] = jnp.zeros(acc_ref.shape, F32)
    jump = lax.fori_loop(0, nk, lagged_chunk, jnp.zeros((1, tq), F32), unroll=ATTN_UNROLL)

    @pl.when(jnp.logical_not(jnp.max(jump) <= LAG_LIMIT))
    def _():
        m_ref[...] = jnp.full(m_ref.shape, -jnp.inf, F32)
        acc_ref[...] = jnp.zeros(acc_ref.shape, F32)
        lax.fori_loop(0, nk, exact_chunk, 0)

    for c in range(ATTN_W // LANES):
        heads = []
        for h in range(c * (LANES // HEAD_DIM), (c + 1) * (LANES // HEAD_DIM)):
            a = acc_ref[h]
            heads.append(a[:HEAD_DIM] / a[HEAD_DIM:HEAD_DIM + 1])
        o_ref[:, c * LANES:(c + 1) * LANES] = jnp.concatenate(heads, axis=0).T.astype(BF16)


def _attn_call(qt, k, vt):
    B, _, kd, n = qt.shape
    _, kg, Lk, _ = k.shape
    tq = _tile(n, ATTN_QUERIES, LANES)
    tk = _tile(Lk, ATTN_KEYS, LANES)
    nk = Lk // tk
    vt = vt.reshape(B, N_KV_HEADS, V_ROWS, nk, tk).transpose(0, 3, 1, 2, 4)
    return pl.pallas_call(
        functools.partial(_attn_kernel, tk=tk),
        grid=(B, n // tq),
        in_specs=[
            pl.BlockSpec((None, N_Q_HEADS, kd, tq), lambda b, i: (b, 0, 0, i)),
            pl.BlockSpec((None, kg, Lk, kd), lambda b, i: (b, 0, 0, 0)),
            pl.BlockSpec((None, nk, N_KV_HEADS, V_ROWS, tk), lambda b, i: (b, 0, 0, 0, 0)),
        ],
        out_specs=pl.BlockSpec((None, tq, ATTN_W), lambda b, i: (b, i, 0)),
        out_shape=jax.ShapeDtypeStruct((B, n, ATTN_W), BF16),
        scratch_shapes=[
            pltpu.VMEM((N_Q_HEADS, 1, tq), F32),
            pltpu.VMEM((N_Q_HEADS, V_ROWS, tq), F32),
        ],
        compiler_params=_cparams(("parallel", "parallel")),
        name="attn",
    )(qt, k, vt)


HALVES = HYENA_W // LANES


def _split_shape(B, n):
    assert B % 2 == 0
    return jax.ShapeDtypeStruct((B // 2, HALVES, 2, n, LANES), F32)


def _split_spec(tm):
    return pl.BlockSpec((None, HALVES, None, tm, LANES), lambda b, i: (b // 2, 0, b % 2, i, 0))


def _shift_rows(cur, prev_row, next_row, rows):
    tm = cur.shape[0]
    up = jnp.where(rows == 0, prev_row, pltpu.roll(cur, 1, 0))
    dn = jnp.where(rows == tm - 1, next_row, pltpu.roll(cur, tm - 1, 0))
    return up, dn


def _local_kernel(hy_ref, hyp_ref, hyn_ref, po_ref, pop_ref, pon_ref, gu_ref, gv_ref,
                  cw_ref, cb_ref, pw_ref, ps_ref, gnw_ref, gws_ref, gb_ref,
                  v_ref, x1_ref, x2_ref, pl_ref, gm_ref, *, n_seq):
    i = pl.program_id(1)
    last = pl.num_programs(1) - 1
    tm = hy_ref.shape[0]
    has_prev = (i > 0).astype(F32)
    has_next = (i < last).astype(F32)

    cur = hy_ref[...]
    prev_row = hyp_ref[POOL_HALO - 1:POOL_HALO, :] * has_prev
    next_row = hyn_ref[0:1, :] * has_next
    rows = lax.broadcasted_iota(jnp.int32, cur.shape, 0)
    up, dn = _shift_rows(cur, prev_row, next_row, rows)
    z = up * cw_ref[0:1, :] + cur * cw_ref[1:2, :] + dn * cw_ref[2:3, :] + cb_ref[...]
    for s, o_ref in enumerate((v_ref, x1_ref, x2_ref)):
        for hf in range(HYENA_W // LANES):
            lo = s * HYENA_W + hf * LANES
            o_ref[hf] = z[:, lo:lo + LANES]

    zc = po_ref[...]
    ext = jnp.concatenate([pop_ref[...] * has_prev, zc, pon_ref[...] * has_next], axis=0)
    te = tm + 2 * POOL_HALO
    lane = lax.broadcasted_iota(jnp.int32, zc.shape, 1)
    prow = lax.broadcasted_iota(jnp.int32, zc.shape, 0) + i * tm
    half = jnp.left_shift(1, lane // POOL_GROUP_W)
    def ahead(a, k):
        return pltpu.roll(a, (te - k) % te, 0)

    assert POOL_WINDOWS == (2, 4, 8, 16)
    sums = [ext + ahead(ext, -1)]
    for win in POOL_WINDOWS[:-1]:
        sums.append(ahead(sums[-1], win // 2) + ahead(sums[-1], -(win // 2)))
    group = lane // POOL_GROUP_W
    acc = sums[-1][POOL_HALO:POOL_HALO + tm]
    for g in range(len(POOL_WINDOWS) - 2, -1, -1):
        acc = jnp.where(group == g, sums[g][POOL_HALO:POOL_HALO + tm], acc)
    cnt = jnp.minimum(prow + half, n_seq) - jnp.maximum(prow - half, 0)
    dlt = acc / cnt.astype(F32) - zc
    pl_ref[...] = (_dot(dlt.astype(BF16), pw_ref[...]) * ps_ref[...]).astype(BF16)

    gv = gv_ref[...]
    mu = jnp.mean(gv, axis=-1, keepdims=True)
    ctr = gv - mu
    var = jnp.mean(ctr * ctr, axis=-1, keepdims=True)
    vn = (ctr * lax.rsqrt(var + RMS_EPS) * gnw_ref[...]).astype(BF16)
    group = lax.broadcasted_iota(jnp.int32, (GMLP_CHUNK, GMLP_W), 1) // GMLP_GROUP_W
    for c in range(tm // GMLP_CHUNK):
        rs = slice(c * GMLP_CHUNK, (c + 1) * GMLP_CHUNK)
        vc = vn[rs]
        mixed = gb_ref[...]
        for g in range(GMLP_GROUPS):
            mg = _dot(gws_ref[g], vc)
            mixed = mixed + jnp.where(group == g, mg, 0.0)
        gm_ref[rs, :] = (gu_ref[rs, :] * mixed).astype(BF16)


LOCAL_ROWS = 1024


def _local_call(p, lw):
    B, n, _ = p.shape
    tm = _tile(n, LOCAL_ROWS, GMLP_CHUNK)
    nb8 = n // POOL_HALO
    r8 = tm // POOL_HALO
    HW = 3 * HYENA_W

    def cur(w, off):
        return pl.BlockSpec((None, tm, w), lambda b, i: (b, i, off // w))

    def prev(w, off):
        return pl.BlockSpec((None, POOL_HALO, w), lambda b, i: (b, jnp.maximum(i * r8 - 1, 0), off // w))

    def nxt(w, off):
        return pl.BlockSpec((None, POOL_HALO, w), lambda b, i: (b, jnp.minimum((i + 1) * r8, nb8 - 1), off // w))

    def full(shape):
        return pl.BlockSpec(shape, lambda b, i: (0,) * len(shape))

    def out(w):
        return pl.BlockSpec((None, tm, w), lambda b, i: (b, i, 0))

    return pl.pallas_call(
        functools.partial(_local_kernel, n_seq=n),
        grid=(B, n // tm),
        in_specs=[
            cur(HW, P_HY), prev(HW, P_HY), nxt(HW, P_HY),
            cur(POOL_W, P_POOL), prev(POOL_W, P_POOL), nxt(POOL_W, P_POOL),
            cur(GMLP_W, P_GMU), cur(GMLP_W, P_GMV),
            full((3, HW)), full((1, HW)), full((POOL_W, POOL_W)), full((1, POOL_W)),
            full((1, GMLP_W)), full((GMLP_GROUPS, GMLP_CHUNK, GMLP_CHUNK)), full((GMLP_CHUNK, GMLP_W)),
        ],
        out_specs=[_split_spec(tm), _split_spec(tm), _split_spec(tm), out(POOL_W), out(GMLP_W)],
        out_shape=[
            _split_shape(B, n), _split_shape(B, n), _split_shape(B, n),
            jax.ShapeDtypeStruct((B, n, POOL_W), BF16),
            jax.ShapeDtypeStruct((B, n, GMLP_W), BF16),
        ],
        compiler_params=_cparams(("parallel", "parallel")),
        name="local",
    )(p, p, p, p, p, p, p, p, lw["conv_w"], lw["conv_b"], lw["pool_bd"], lw["pool_scale"],
      lw["gm_norm_w"], lw["gm_ws"], lw["gm_bias"])


def _filt_kernel(z_ref, t_ref, w1h_ref, w1l_ref, b1_ref, w2h_ref, w2l_ref, b2_ref, fr_ref,
                 w3h_ref, w3l_ref, dec_ref, k_ref, s_ref, *, n_seq):
    i = pl.program_id(0)
    tb = z_ref.shape[1]
    fr = _lane_tile(fr_ref[...], tb)
    hdn = jnp.sin(fr * (_dot3(w1h_ref[...], w1l_ref[...], z_ref[...]) + _lane_tile(b1_ref[...], tb)))
    for u in range(HYENA_INNER):
        hdn = jnp.sin(fr * (_dot3(w2h_ref[u], w2l_ref[u], hdn) + _lane_tile(b2_ref[u], tb)))
    hdn = hdn.T
    t = t_ref[...]
    tt = jnp.concatenate([t] * (HYENA_ORDER * HYENA_W // LANES), axis=1)
    window = jnp.exp(-tt * jnp.abs(dec_ref[...]))
    k = _dot3x(hdn, w3h_ref[...], w3l_ref[...]) * window
    rows = lax.broadcasted_iota(jnp.int32, k.shape, 0) + i * tb
    k = jnp.where(rows == n_seq, 0.0, k)
    for q in range(k_ref.shape[0]):
        k_ref[q] = k[:, q * LANES:(q + 1) * LANES]

    @pl.when(i == 0)
    def _():
        s_ref[...] = jnp.zeros(s_ref.shape, F32)

    s_ref[...] += jnp.sum(jnp.abs(k), axis=0, keepdims=True)


def _filt_call(n, lw):
    N = 2 * n
    tb = _tile(n, 512)
    nb_half = n // tb
    pos = np.arange(N)
    pos = np.where(pos < n, pos, N - pos).astype(np.float64)
    t = pos / (n - 1)
    bands = np.linspace(1e-4, HYENA_BANDS - 1, HYENA_BANDS)
    ang = (2.0 * math.pi / n) * pos[:, None] * bands
    z = np.concatenate([t[:, None], np.cos(ang), np.sin(ang)], axis=-1)
    zp = np.zeros((LANES, N), np.float32)
    zp[:z.shape[1]] = z.T
    tl = np.broadcast_to(t[:, None], (N, LANES)).astype(np.float32)
    CW = HYENA_ORDER * HYENA_W

    def full(shape):
        return pl.BlockSpec(shape, lambda i: (0,) * len(shape))

    def side(rows_):
        return pl.BlockSpec((None, rows_, CW), lambda i: (i // nb_half, 0, 0))

    return pl.pallas_call(
        functools.partial(_filt_kernel, n_seq=n),
        grid=(N // tb,),
        in_specs=[
            pl.BlockSpec((LANES, tb), lambda i: (0, i)),
            pl.BlockSpec((tb, LANES), lambda i: (i, 0)),
            full((HYENA_HIDDEN, LANES)), full((HYENA_HIDDEN, LANES)), full((HYENA_HIDDEN, LANES)),
            full((HYENA_INNER, HYENA_HIDDEN, HYENA_HIDDEN)), full((HYENA_INNER, HYENA_HIDDEN, HYENA_HIDDEN)),
            full((HYENA_INNER, HYENA_HIDDEN, LANES)), full((HYENA_HIDDEN, LANES)),
            side(HYENA_HIDDEN), side(HYENA_HIDDEN), side(1),
        ],
        out_specs=[pl.BlockSpec((CW // LANES, tb, LANES), lambda i: (0, i, 0)), pl.BlockSpec((1, CW), lambda i: (0, 0))],
        out_shape=[jax.ShapeDtypeStruct((CW // LANES, N, LANES), F32), jax.ShapeDtypeStruct((1, CW), F32)],
        compiler_params=_cparams(("arbitrary",)),
        name="filt",
    )(jnp.asarray(zp), jnp.asarray(tl), lw["hy_w1h"], lw["hy_w1l"], lw["hy_b1"], lw["hy_w2h"], lw["hy_w2l"],
      lw["hy_b2"], lw["hy_freq"], lw["hy_w3h"], lw["hy_w3l"], lw["hy_decay"])


def _fft_dims(N):
    e = int(round(math.log2(N)))
    assert 2 ** e == N
    N1 = 2 ** ((e + 1) // 2)
    return N1, N // N1


def _hilo(a):
    a = np.asarray(a, np.float64)
    hi = jnp.asarray(a, F32).astype(BF16)
    lo = (jnp.asarray(a, F32) - hi.astype(F32)).astype(BF16)
    return hi, lo


@functools.lru_cache(maxsize=None)
def _fft_consts_np(N):
    N1, N2 = _fft_dims(N)
    h = N1 // 2
    k1 = np.arange(N1)
    F1 = np.exp(-2j * np.pi * np.outer(k1, k1) / N1)
    k2 = np.arange(N2)
    F2 = np.exp(-2j * np.pi * np.outer(k2, k2) / N2)
    tw = np.exp(-2j * np.pi * np.outer(k1, k2) / N)
    g1 = np.zeros((2 * N1, N1))
    g1[0::2, :h] = F1.real[:, :h]
    g1[0::2, h:] = -F1.imag[:, :h]
    g1[1::2, :h] = F1.imag[:, :h]
    g1[1::2, h:] = F1.real[:, :h]
    g1f = np.zeros((2 * N1, N1))
    g1f[0::2] = F1.real
    g1f[1::2] = F1.imag
    g2 = np.block([[F2.real, -F2.imag], [F2.imag, F2.real]])
    g2c = np.block([[F2.real, F2.imag], [-F2.imag, F2.real]])
    g3 = np.zeros((N1, 2 * N1))
    g3[:h, 0::2] = F1.real[:h]
    g3[:h, 1::2] = F1.imag[:h]
    g3[h:, 0::2] = -F1.imag[:h]
    g3[h:, 1::2] = F1.real[:h]
    g3 /= N
    twt = np.stack([tw.real, tw.imag], axis=1)
    return dict(g1=g1, g1f=g1f, g2=g2, g2c=g2c, g3=g3, tw=twt)


def _fft_consts(N):
    c = _fft_consts_np(N)
    out = {k: _hilo(c[k]) for k in ("g1", "g1f", "g2", "g2c", "g3")}
    N1, N2 = _fft_dims(N)
    tw = jnp.asarray(c["tw"], F32)
    out["tw"] = jnp.broadcast_to(tw[..., None], (N1, 2, N2, LANES))
    return out


T2_BLOCK = 8


def _load_8th(ref, j, rows):
    return ref.reshape(rows * T2_BLOCK, LANES)[pl.ds(j, rows, stride=T2_BLOCK), :]


def _store_8th(ref, j, rows, val):
    ref.reshape(rows * T2_BLOCK, LANES)[pl.ds(j, rows, stride=T2_BLOCK), :] = val


def _dft_in_kernel(gh_ref, gl_ref, x_ref, o_ref):
    N1 = x_ref.shape[1]
    for j in range(T2_BLOCK):
        xj = jnp.concatenate([_load_8th(x_ref.at[hf], j, N1) for hf in range(HALVES)], axis=1)
        a = _dot3(gh_ref[...], gl_ref[...], xj)
        for hf in range(HALVES):
            _store_8th(o_ref.at[hf], j, 2 * N1, a[:, hf * LANES:(hf + 1) * LANES])


def _dft_in_call(g, u5):
    gh, gl = g
    G, _, N1, N2, _ = u5.shape
    assert gh.shape == (2 * N1, N1)
    mat = pl.BlockSpec((2 * N1, N1), lambda b, i: (0, 0))
    return pl.pallas_call(
        _dft_in_kernel,
        grid=(G, N2 // T2_BLOCK),
        in_specs=[mat, mat, pl.BlockSpec((None, HALVES, N1, T2_BLOCK, LANES), lambda b, i: (b, 0, 0, i, 0))],
        out_specs=pl.BlockSpec((None, HALVES, N1, 2, T2_BLOCK, LANES), lambda b, i: (b, 0, 0, 0, i, 0)),
        out_shape=jax.ShapeDtypeStruct((G, HALVES, N1, 2, N2, LANES), F32),
        compiler_params=_cparams(("parallel", "parallel")),
        name="dft_in",
    )(gh, gl, u5)


def _dft_out_kernel(gh_ref, gl_ref, d_ref, gate_ref, o_ref):
    N1 = gate_ref.shape[1]
    for j in range(T2_BLOCK):
        dj = jnp.concatenate([_load_8th(d_ref.at[hf], j, 2 * N1) for hf in range(HALVES)], axis=1)
        y = _dot3(gh_ref[...], gl_ref[...], dj)
        for hf in range(HALVES):
            conv = y[:, hf * LANES:(hf + 1) * LANES]
            _store_8th(o_ref.at[hf], j, N1, _load_8th(gate_ref.at[hf], j, N1) * conv)


def _dft_out_call(g, d6, gate5):
    gh, gl = g
    G, _, N1, N2, _ = gate5.shape
    assert gh.shape == (N1, 2 * N1)
    mat = pl.BlockSpec((N1, 2 * N1), lambda b, i: (0, 0))
    blk = pl.BlockSpec((None, HALVES, N1, T2_BLOCK, LANES), lambda b, i: (b, 0, 0, i, 0))
    return pl.pallas_call(
        _dft_out_kernel,
        grid=(G, N2 // T2_BLOCK),
        in_specs=[mat, mat, pl.BlockSpec((None, HALVES, N1, 2, T2_BLOCK, LANES), lambda b, i: (b, 0, 0, 0, i, 0)),
                  blk],
        out_specs=blk,
        out_shape=jax.ShapeDtypeStruct(gate5.shape, F32),
        compiler_params=_cparams(("parallel", "parallel")),
        name="dft_out",
    )(gh, gl, d6, gate5)


def _dft_mid_kernel(g3h_ref, g3l_ref, g1h_ref, g1l_ref, d_ref, gate_ref, o_ref):
    N1 = gate_ref.shape[1]
    for j in range(T2_BLOCK):
        dj = jnp.concatenate([_load_8th(d_ref.at[hf], j, 2 * N1) for hf in range(HALVES)], axis=1)
        y = _dot3(g3h_ref[...], g3l_ref[...], dj)
        gate = jnp.concatenate([_load_8th(gate_ref.at[hf], j, N1) for hf in range(HALVES)], axis=1)
        a = _dot3(g1h_ref[...], g1l_ref[...], gate * y)
        for hf in range(HALVES):
            _store_8th(o_ref.at[hf], j, 2 * N1, a[:, hf * LANES:(hf + 1) * LANES])


def _dft_mid_call(g3, g1, d6, gate5):
    G, _, N1, N2, _ = gate5.shape
    assert g3[0].shape == (N1, 2 * N1) and g1[0].shape == (2 * N1, N1)
    mat3 = pl.BlockSpec((N1, 2 * N1), lambda b, i: (0, 0))
    mat1 = pl.BlockSpec((2 * N1, N1), lambda b, i: (0, 0))
    wide = pl.BlockSpec((None, HALVES, N1, 2, T2_BLOCK, LANES), lambda b, i: (b, 0, 0, 0, i, 0))
    return pl.pallas_call(
        _dft_mid_kernel,
        grid=(G, N2 // T2_BLOCK),
        in_specs=[mat3, mat3, mat1, mat1, wide,
                  pl.BlockSpec((None, HALVES, N1, T2_BLOCK, LANES), lambda b, i: (b, 0, 0, i, 0))],
        out_specs=wide,
        out_shape=jax.ShapeDtypeStruct(d6.shape, F32),
        compiler_params=_cparams(("parallel", "parallel")),
        name="dft_mid",
    )(g3[0], g3[1], g1[0], g1[1], d6, gate5)


def _cmul(ar, ai, br, bi):
    return ar * br - ai * bi, ar * bi + ai * br


def _lane_tile(t, c):
    return jnp.concatenate([t] * (c // LANES), axis=-1)


def _both_halves(a_ref, k, part):
    return jnp.concatenate([a_ref[hf, k, part] for hf in range(HALVES)], axis=-1)


def _spec_kernel(a_ref, tw_ref, g2h_ref, g2l_ref, s_ref, sk_ref, o_ref):
    _, kb, _, N2, _ = a_ref.shape
    inv = 1.0 / s_ref[...]
    for k in range(kb):
        twr = _lane_tile(tw_ref[k, 0], HYENA_W)
        twi = _lane_tile(tw_ref[k, 1], HYENA_W)
        br, bi = _cmul(_both_halves(a_ref, k, 0), _both_halves(a_ref, k, 1), twr, twi)
        zz = _dot3(g2h_ref[...], g2l_ref[...], jnp.concatenate([br, bi], axis=0))
        o_ref[k, 0] = zz[:N2] * inv + sk_ref[...]
        o_ref[k, 1] = zz[N2:] * inv


def _conv_kernel(a_ref, tw_ref, kf_ref, g2h_ref, g2l_ref, g2ch_ref, g2cl_ref, o_ref):
    _, kb, _, N2, _ = a_ref.shape
    for k in range(kb):
        twr = _lane_tile(tw_ref[k, 0], HYENA_W)
        twi = _lane_tile(tw_ref[k, 1], HYENA_W)
        br, bi = _cmul(_both_halves(a_ref, k, 0), _both_halves(a_ref, k, 1), twr, twi)
        zz = _dot3(g2h_ref[...], g2l_ref[...], jnp.concatenate([br, bi], axis=0))
        yr, yi = _cmul(zz[:N2], zz[N2:], kf_ref[k, 0], kf_ref[k, 1])
        cc = _dot3(g2ch_ref[...], g2cl_ref[...], jnp.concatenate([yr, yi], axis=0))
        dr, di = _cmul(cc[:N2], cc[N2:], twr, -twi)
        for hf in range(HALVES):
            o_ref[hf, k, 0] = dr[:, hf * LANES:(hf + 1) * LANES]
            o_ref[hf, k, 1] = di[:, hf * LANES:(hf + 1) * LANES]


SPEC_K1 = 16


def _spec_call(a6, fc, s, skip):
    G, _, N1, _, N2, _ = a6.shape
    kb = _tile(N1, SPEC_K1, 1)
    mat = pl.BlockSpec((2 * N2, 2 * N2), lambda i, b: (0, 0))
    return pl.pallas_call(
        _spec_kernel,
        grid=(N1 // kb, G),
        in_specs=[pl.BlockSpec((None, HALVES, kb, 2, N2, LANES), lambda i, b: (b, 0, i, 0, 0, 0)),
                  pl.BlockSpec((kb, 2, N2, LANES), lambda i, b: (i, 0, 0, 0)), mat, mat,
                  pl.BlockSpec((None, 1, HYENA_W), lambda i, b: (b, 0, 0)),
                  pl.BlockSpec((None, 1, HYENA_W), lambda i, b: (b, 0, 0))],
        out_specs=pl.BlockSpec((None, kb, 2, N2, HYENA_W), lambda i, b: (b, i, 0, 0, 0)),
        out_shape=jax.ShapeDtypeStruct((G, N1, 2, N2, HYENA_W), F32),
        compiler_params=_cparams(("parallel", "parallel")),
        name="filt_spec",
    )(a6, fc["tw"], fc["g2"][0], fc["g2"][1], s, skip)


def _conv_call(a6, kf, order, fc):
    G, _, N1, _, N2, _ = a6.shape
    kb = _tile(N1, SPEC_K1, 1)
    blk = pl.BlockSpec((None, HALVES, kb, 2, N2, LANES), lambda i, b: (b, 0, i, 0, 0, 0))
    mat = pl.BlockSpec((2 * N2, 2 * N2), lambda i, b: (0, 0))
    return pl.pallas_call(
        _conv_kernel,
        grid=(N1 // kb, G),
        in_specs=[blk, pl.BlockSpec((kb, 2, N2, LANES), lambda i, b: (i, 0, 0, 0)),
                  pl.BlockSpec((None, kb, 2, N2, HYENA_W), lambda i, b: (order, i, 0, 0, 0)), mat, mat, mat, mat],
        out_specs=blk,
        out_shape=jax.ShapeDtypeStruct(a6.shape, F32),
        compiler_params=_cparams(("parallel", "parallel")),
        name="spec_conv",
    )(a6, fc["tw"], kf, fc["g2"][0], fc["g2"][1], fc["g2c"][0], fc["g2c"][1])


def _filter_spectra(n, lw, fc):
    N1, N2 = _fft_dims(2 * n)
    k, s = _filt_call(n, lw)
    a6 = _dft_in_call(fc["g1f"], k.reshape(HYENA_ORDER, HALVES, N1, N2, LANES))
    return _spec_call(a6, fc, s.reshape(HYENA_ORDER, 1, HYENA_W),
                      lw["hy_skip"].reshape(HYENA_ORDER, 1, HYENA_W))


def _hyena(v, x1, x2, kf, lw, fc):
    P, _, _, n, _ = v.shape
    N1, N2 = _fft_dims(2 * n)
    shape5 = (P, HALVES, N1, N2, LANES)
    gates = (x1, x2)
    a = _dft_in_call(fc["g1"], v.reshape(shape5))
    for o in range(HYENA_ORDER - 1):
        a = _dft_mid_call(fc["g3"], fc["g1"], _conv_call(a, kf, o, fc), gates[o].reshape(shape5))
    y = _dft_out_call(fc["g3"], _conv_call(a, kf, HYENA_ORDER - 1, fc), gates[-1].reshape(shape5))
    return y.reshape(v.shape)


def _merge_kernel(at_ref, hy_ref, po_ref, gm_ref, x_ref, sh_ref, sc_ref, nw_ref, gr_ref,
                  wgate_ref, wa_ref, wh_ref, wp_ref, wg_ref, wo_ref, o_ref):
    D = x_ref.shape[1]
    for rows in _row_blocks(x_ref.shape[0]):
        x = x_ref[rows, :]
        h = _modulated_norm(x, nw_ref[...], sc_ref[...], sh_ref[...]).astype(BF16)
        hy = jnp.concatenate([hy_ref[hf, rows, :] for hf in range(HALVES)], axis=-1).astype(BF16)
        branches = ((at_ref[rows, :], wa_ref), (hy, wh_ref), (po_ref[rows, :], wp_ref), (gm_ref[rows, :], wg_ref))
        merged = None
        for i, (br, w_ref) in enumerate(branches):
            term = _sigmoid(_dot(h, wgate_ref[:, i * D:(i + 1) * D])) * _dot(br, w_ref[...])
            merged = term if merged is None else merged + term
        o_ref[rows, :] = x + gr_ref[...] * _dot(merged.astype(BF16), wo_ref[...])


MERGE_ROWS = 1024


def _merge_call(attn, hy, po, gm, x, shift, scale, norm_w, gate_res, lw):
    B, n, D = x.shape
    tm = _tile(n, MERGE_ROWS)

    def rows(w):
        return pl.BlockSpec((None, tm, w), lambda b, i: (b, i, 0))

    def full(shape):
        return pl.BlockSpec(shape, lambda b, i: (0,) * len(shape))

    vec = pl.BlockSpec((None, 1, D), lambda b, i: (b, 0, 0))
    return pl.pallas_call(
        _merge_kernel,
        grid=(B, n // tm),
        in_specs=[
            rows(ATTN_W), _split_spec(tm), rows(POOL_W), rows(GMLP_W), rows(D),
            vec, vec, full((1, D)), vec,
            full((D, N_BRANCH * D)),
            full((ATTN_W, D)), full((HYENA_W, D)), full((POOL_W, D)), full((GMLP_W, D)), full((D, D)),
        ],
        out_specs=rows(D),
        out_shape=jax.ShapeDtypeStruct((B, n, D), F32),
        compiler_params=_cparams(("parallel", "parallel")),
        name="merge",
    )(attn, hy, po, gm, x, shift, scale, norm_w.reshape(1, D), gate_res, lw["w_gate"], lw["w_br_attn"],
      lw["w_br_hyena"], lw["w_br_pool"], lw["w_br_gmlp"], lw["w_out"])


FFN_ROWS = 1024


def _ffn_kernel(x_ref, sh_ref, sc_ref, nw_ref, gr_ref, wg_ref, wu_ref, wd_ref, fw_ref, o_ref, *, final):
    for rows in _row_blocks(x_ref.shape[0]):
        x = x_ref[rows, :]
        h = _modulated_norm(x, nw_ref[...], sc_ref[...], sh_ref[...]).astype(BF16)
        a = _silu(_dot(h, wg_ref[...])) * _dot(h, wu_ref[...])
        y = x + gr_ref[...] * _dot(a.astype(BF16), wd_ref[...])
        if final:
            y = y * lax.rsqrt(jnp.mean(y * y, axis=-1, keepdims=True) + RMS_EPS) * fw_ref[...]
        o_ref[rows, :] = y


def _ffn_call(x, shift, scale, norm_w, gate_res, wg, wu, wd, final_w=None):
    B, n, D = x.shape
    H = wg.shape[1]
    tm = _tile(n, FFN_ROWS)
    vec = pl.BlockSpec((None, 1, D), lambda b, i: (b, 0, 0))
    one = pl.BlockSpec((1, D), lambda b, i: (0, 0))
    final = final_w is not None
    fw = (final_w if final else norm_w).reshape(1, D)
    return pl.pallas_call(
        functools.partial(_ffn_kernel, final=final),
        grid=(B, n // tm),
        in_specs=[
            pl.BlockSpec((None, tm, D), lambda b, i: (b, i, 0)),
            vec, vec, one, vec,
            pl.BlockSpec((D, H), lambda b, i: (0, 0)),
            pl.BlockSpec((D, H), lambda b, i: (0, 0)),
            pl.BlockSpec((H, D), lambda b, i: (0, 0)),
            one,
        ],
        out_specs=pl.BlockSpec((None, tm, D), lambda b, i: (b, i, 0)),
        out_shape=jax.ShapeDtypeStruct((B, n, D), F32),
        compiler_params=_cparams(("parallel", "parallel")),
        name="ffn",
    )(x, shift, scale, norm_w.reshape(1, D), gate_res, wg, wu, wd, fw)


def _rope_tables(n):
    tok = np.arange(n)
    axis_dim = HEAD_DIM // 2
    inv_freq = ROPE_THETA ** (-np.arange(0, axis_dim, 2, dtype=np.float64) / axis_dim)
    inv_freq = inv_freq.astype(np.float32).astype(np.float64)
    ang = np.concatenate([(tok // GRID_W)[:, None] * inv_freq, (tok % GRID_W)[:, None] * inv_freq], axis=-1)
    ang = ang.astype(np.float32)
    cos = np.repeat(np.cos(ang), 2, axis=-1)
    sin = np.repeat(np.sin(ang), 2, axis=-1)
    sin[:, 0::2] *= -1.0
    reps = LANES // HEAD_DIM
    return jnp.asarray(np.tile(cos, (1, reps)), F32), jnp.asarray(np.tile(sin, (1, reps)), F32)


def _identity_tables(n):
    return jnp.ones((n, LANES), F32), jnp.zeros((n, LANES), F32)


def _layer_weights(l, a):
    w_in = a["w_in"][l]
    w_perm = jnp.concatenate([
        w_in[:, Q_OFF:K_OFF], w_in[:, K_OFF:HY_OFF], w_in[:, POOL_OFF:GM_OFF], w_in[:, GM_OFF:GATE_OFF],
        w_in[:, HY_OFF:POOL_OFF]], axis=1).astype(BF16)
    w3 = a["hy_w3"][l].reshape(HYENA_HIDDEN, HYENA_ORDER, 2, HYENA_W).transpose(2, 0, 1, 3)
    w3 = w3.reshape(2, HYENA_HIDDEN, HYENA_ORDER * HYENA_W)
    dec = a["hy_decay"][l].reshape(HYENA_ORDER, 2, HYENA_W).transpose(1, 0, 2).reshape(2, 1, HYENA_ORDER * HYENA_W)
    w1 = jnp.zeros((HYENA_HIDDEN, LANES), F32).at[:, :a["hy_w1"].shape[1]].set(a["hy_w1"][l].T)
    w1h, w1l = _split(w1)
    w2h, w2l = _split(jnp.swapaxes(a["hy_w2"][l], 1, 2))

    def lanes_of(v):
        return jnp.broadcast_to(v[..., None], v.shape + (LANES,))
    w3h, w3l = _split(w3)
    qw16 = jnp.tile(a["q_norm_w"][l], N_Q_HEADS).reshape(1, ATTN_W) * Q_SCALE
    kw16 = jnp.tile(a["k_norm_w"][l], N_KV_HEADS).reshape(1, KV_W)
    rms_q = jnp.sqrt(jnp.mean(qw16 * qw16))
    rms_k = jnp.sqrt(jnp.mean(kw16 * kw16))
    bal = jnp.exp2(jnp.clip(jnp.round(0.5 * jnp.log2(rms_k / rms_q)), -20.0, 20.0))
    bal = jnp.where(jnp.isfinite(bal), bal, 1.0)
    fp8_ok = (jnp.max(jnp.abs(qw16 * bal)) <= FP8_MAX_WEIGHT) & (jnp.max(jnp.abs(kw16 / bal)) <= FP8_MAX_WEIGHT)
    eye = jnp.eye(len(POOL_WINDOWS), dtype=F32)
    pool_bd = jnp.einsum("gh,gcd->gchd", eye, a["pool_w"][l]).reshape(POOL_W, POOL_W).astype(BF16)
    gm_bias = jnp.repeat(a["gm_bs"][l].T, GMLP_GROUP_W, axis=1)
    return dict(
        w_in=w_perm, w_gate=w_in[:, GATE_OFF:].astype(BF16),
        qw16=qw16, kw16=kw16, qw8=qw16 * bal, kw8=kw16 / bal, fp8_ok=fp8_ok,
        conv_w=a["hy_conv_w"][l], conv_b=a["hy_conv_b"][l].reshape(1, -1),
        hy_w1h=w1h, hy_w1l=w1l, hy_b1=lanes_of(a["hy_b1"][l]), hy_w2h=w2h, hy_w2l=w2l, hy_b2=lanes_of(a["hy_b2"][l]),
        hy_w3h=w3h, hy_w3l=w3l, hy_freq=lanes_of(a["hy_freq"][l]), hy_decay=dec, hy_skip=a["hy_skip"][l],
        pool_bd=pool_bd, pool_scale=a["pool_scale"][l].reshape(1, -1),
        gm_norm_w=a["gm_norm_w"][l].reshape(1, -1), gm_ws=a["gm_ws"][l].astype(BF16), gm_bias=gm_bias,
        w_br_attn=a["w_br_attn"][l].astype(BF16), w_br_hyena=a["w_br_hyena"][l].astype(BF16),
        w_br_pool=a["w_br_pool"][l].astype(BF16), w_br_gmlp=a["w_br_gmlp"][l].astype(BF16),
        w_out=a["w_out"][l].astype(BF16),
        ffn_w_gate=a["ffn_w_gate"][l].astype(BF16), ffn_w_up=a["ffn_w_up"][l].astype(BF16),
        ffn_w_down=a["ffn_w_down"][l].astype(BF16),
        norm1_w=a["norm1_w"][l], norm2_w=a["norm2_w"][l],
    )


Q_SCALE = (HEAD_DIM ** -0.5) * math.log2(math.e)


def _attend(p, p_ctx, rope, rope_ctx, lw):
    def run(fp8):
        qw, kw = (lw["qw8"], lw["kw8"]) if fp8 else (lw["qw16"], lw["kw16"])
        qt, k, vt = _qkprep_call(p, rope[0], rope[1], qw, kw, fp8)
        if p_ctx is not None:
            _, k_c, vt_c = _qkprep_call(p_ctx, rope_ctx[0], rope_ctx[1], qw, kw, fp8)
            k = jnp.concatenate([k_c, k], axis=2)
            vt = jnp.concatenate([vt_c, vt], axis=3)
        return _attn_call(qt, k, vt)

    return lax.cond(lw["fp8_ok"], lambda: run(True), lambda: run(False))


def _mixer_block(x, mods, lw, rope, p_ctx, rope_ctx, kf, fc, final_w=None):
    sh1, s1, g1, sh2, s2, g2 = mods
    p = _inproj_call(x, sh1, s1, lw["norm1_w"], lw["w_in"])
    attn = _attend(p, p_ctx, rope, rope_ctx, lw)
    hv, hx1, hx2, po, gm = _local_call(p, lw)
    hy = _hyena(hv, hx1, hx2, kf, lw, fc)
    x = _merge_call(attn, hy, po, gm, x, sh1, s1, lw["norm1_w"], g1, lw)
    x = _ffn_call(x, sh2, s2, lw["norm2_w"], g2, lw["ffn_w_gate"], lw["ffn_w_up"], lw["ffn_w_down"], final_w)
    return x, p


def kernel(x, c, ctx, c_ctx, w_mod, b_mod, norm1_w, norm2_w, w_in, q_norm_w, k_norm_w, hy_conv_w, hy_conv_b,
           hy_w1, hy_b1, hy_w2, hy_b2, hy_w3, hy_freq, hy_decay, hy_skip, pool_w, pool_scale, gm_norm_w, gm_ws,
           gm_bs, w_br_attn, w_br_hyena, w_br_pool, w_br_gmlp, w_out, ffn_w_gate, ffn_w_up, ffn_w_down,
           final_norm_w):
    a = dict(w_in=w_in, q_norm_w=q_norm_w, k_norm_w=k_norm_w, hy_conv_w=hy_conv_w, hy_conv_b=hy_conv_b,
             hy_w1=hy_w1, hy_b1=hy_b1, hy_w2=hy_w2, hy_b2=hy_b2, hy_w3=hy_w3, hy_freq=hy_freq, hy_decay=hy_decay,
             hy_skip=hy_skip, pool_w=pool_w, pool_scale=pool_scale, gm_norm_w=gm_norm_w, gm_ws=gm_ws, gm_bs=gm_bs,
             w_br_attn=w_br_attn, w_br_hyena=w_br_hyena, w_br_pool=w_br_pool, w_br_gmlp=w_br_gmlp, w_out=w_out,
             ffn_w_gate=ffn_w_gate, ffn_w_up=ffn_w_up, ffn_w_down=ffn_w_down, norm1_w=norm1_w, norm2_w=norm2_w)
    B, n, D = x.shape
    nc = ctx.shape[1]
    depth = w_mod.shape[0]

    R = ((B + 1 + 15) // 16) * 16
    act = jnp.zeros((R, D), F32).at[:B].set(c).at[B].set(c_ctx)
    mod = _mod_call(act, w_mod, b_mod)

    rope = _rope_tables(n)
    rope_ctx = _identity_tables(nc)
    fc = _fft_consts(2 * n)
    fc_ctx = _fft_consts(2 * nc)

    xc = ctx
    for l in range(depth):
        lw = _layer_weights(l, a)
        mods = [m.reshape(B, 1, D) for m in jnp.split(mod[l, :B], 6, axis=-1)]
        cmods = [jnp.broadcast_to(m.reshape(1, 1, D), (B, 1, D)) for m in jnp.split(mod[l, B], 6, axis=-1)]
        if l < depth - 1:
            kf_ctx = _filter_spectra(nc, lw, fc_ctx)
            xc, pc = _mixer_block(xc, cmods, lw, rope_ctx, None, None, kf_ctx, fc_ctx)
        else:
            pc = _inproj_call(xc, cmods[0], cmods[1], lw["norm1_w"], lw["w_in"][:, :P_QKV])
        kf = _filter_spectra(n, lw, fc)
        x, _ = _mixer_block(x, mods, lw, rope, pc, rope_ctx, kf, fc, final_norm_w if l == depth - 1 else None)
    return x
```

```python
import functools
import math

import numpy as np
import jax
import jax.numpy as jnp
from jax import lax
from jax.experimental import pallas as pl
from jax.experimental.pallas import tpu as pltpu

F32 = jnp.float32
BF16 = jnp.bfloat16

D_MODEL = 1024
GRID_W = 64
RMS_EPS = 1e-6
HEAD_DIM = 64
N_Q_HEADS = 8
N_KV_HEADS = 2
ATTN_W = N_Q_HEADS * HEAD_DIM
KV_W = N_KV_HEADS * HEAD_DIM
ROPE_THETA = 10000.0
HYENA_W = 256
HYENA_ORDER = 2
HYENA_BANDS = 16
HYENA_HIDDEN = 64
HYENA_INNER = 2
POOL_W = 256
POOL_WINDOWS = (2, 4, 8, 16)
POOL_GROUP_W = POOL_W // len(POOL_WINDOWS)
POOL_HALO = 8
GMLP_W = 256
GMLP_CHUNK = 128
GMLP_GROUPS = 4
GMLP_GROUP_W = GMLP_W // GMLP_GROUPS
N_BRANCH = 4

Q_OFF = 0
K_OFF = Q_OFF + ATTN_W
V_OFF = K_OFF + KV_W
HY_OFF = V_OFF + KV_W
POOL_OFF = HY_OFF + 3 * HYENA_W
GM_OFF = POOL_OFF + POOL_W
GATE_OFF = GM_OFF + 2 * GMLP_W

P_Q = 0
P_KV = P_Q + ATTN_W
P_POOL = P_KV + 2 * KV_W
P_GMU = P_POOL + POOL_W
P_GMV = P_GMU + GMLP_W
P_HY = P_GMV + GMLP_W
P_W = P_HY + 3 * HYENA_W
P_QKV = P_POOL

LANES = 128
VMEM_LIMIT = 52 * 1024 * 1024


def _cparams(sem):
    return pltpu.CompilerParams(dimension_semantics=sem, vmem_limit_bytes=VMEM_LIMIT)


def _tile(n, cap, mult=8):
    best = None
    for t in range(mult, min(n, cap) + 1, mult):
        if n % t == 0:
            best = t
    assert best is not None, (n, cap, mult)
    return best


def _split(x):
    hi = x.astype(BF16)
    lo = (x - hi.astype(F32)).astype(BF16)
    return hi, lo


def _dot(a, b):
    return jnp.dot(a, b, preferred_element_type=F32)


def _dot3(a_hi, a_lo, x):
    x_hi, x_lo = _split(x)
    return _dot(a_hi, x_hi) + (_dot(a_lo, x_hi) + _dot(a_hi, x_lo))


def _dot3x(x, b_hi, b_lo):
    x_hi, x_lo = _split(x)
    return _dot(x_hi, b_hi) + (_dot(x_lo, b_hi) + _dot(x_hi, b_lo))


def _sigmoid(x):
    return 1.0 / (1.0 + jnp.exp(-x))


def _silu(x):
    return x * _sigmoid(x)


def _modulated_norm(x, nw, scale, shift):
    ms = jnp.mean(x * x, axis=-1, keepdims=True)
    y = x * lax.rsqrt(ms + RMS_EPS) * nw
    return y * (1.0 + scale) + shift


def _mod_kernel(a_ref, w_ref, b_ref, o_ref):
    a = _silu(a_ref[...]).astype(BF16)
    o_ref[...] = _dot(a, w_ref[...].astype(BF16)) + b_ref[...]


def _mod_call(act, w_mod, b_mod):
    L, D, W6 = w_mod.shape
    R = act.shape[0]
    tn = 1024
    return pl.pallas_call(
        _mod_kernel,
        grid=(L, W6 // tn),
        in_specs=[
            pl.BlockSpec((R, D), lambda l, j: (0, 0)),
            pl.BlockSpec((None, D, tn), lambda l, j: (l, 0, j)),
            pl.BlockSpec((None, 1, tn), lambda l, j: (l, 0, j)),
        ],
        out_specs=pl.BlockSpec((None, R, tn), lambda l, j: (l, 0, j)),
        out_shape=jax.ShapeDtypeStruct((L, R, W6), F32),
        compiler_params=_cparams(("arbitrary", "arbitrary")),
        name="mod",
    )(act, w_mod, b_mod.reshape(L, 1, W6))


ROW_BLOCK = 256
INPROJ_ROWS = 1024


def _row_blocks(tm):
    sub = ROW_BLOCK if tm % ROW_BLOCK == 0 else tm
    return [slice(r, r + sub) for r in range(0, tm, sub)]


def _inproj_kernel(x_ref, sh_ref, sc_ref, nw_ref, w_ref, o_ref):
    for rows in _row_blocks(x_ref.shape[0]):
        h = _modulated_norm(x_ref[rows, :], nw_ref[...], sc_ref[...], sh_ref[...]).astype(BF16)
        o_ref[rows, :] = _dot(h, w_ref[...])


def _inproj_call(x, shift, scale, norm_w, w):
    B, n, D = x.shape
    Nw = w.shape[1]
    tm = _tile(n, INPROJ_ROWS)
    return pl.pallas_call(
        _inproj_kernel,
        grid=(B, n // tm),
        in_specs=[
            pl.BlockSpec((None, tm, D), lambda b, i: (b, i, 0)),
            pl.BlockSpec((None, 1, D), lambda b, i: (b, 0, 0)),
            pl.BlockSpec((None, 1, D), lambda b, i: (b, 0, 0)),
            pl.BlockSpec((1, D), lambda b, i: (0, 0)),
            pl.BlockSpec((D, Nw), lambda b, i: (0, 0)),
        ],
        out_specs=pl.BlockSpec((None, tm, Nw), lambda b, i: (b, i, 0)),
        out_shape=jax.ShapeDtypeStruct((B, n, Nw), F32),
        compiler_params=_cparams(("parallel", "parallel")),
        name="inproj",
    )(x, shift, scale, norm_w.reshape(1, D), w)


def _head_norm_rope(t, w, cs, sn, bd_ref):
    sq_hi, sq_lo = _split(t * t)
    ms = _dot(sq_hi, bd_ref[...]) + _dot(sq_lo, bd_ref[...])
    y = t * lax.rsqrt(ms + RMS_EPS) * w
    lane = lax.broadcasted_iota(jnp.int32, y.shape, 1)
    swapped = jnp.where(lane % 2 == 0, pltpu.roll(y, LANES - 1, 1), pltpu.roll(y, 1, 1))
    return y * cs + swapped * sn


V_ROWS = 80
Q_PER_KV = N_Q_HEADS // N_KV_HEADS
LAG_INIT_KEYS = 128
LAG_LIMIT = 12.0
ATTN_QUERIES = 512
ATTN_KEYS = 768
ATTN_UNROLL = 4
QKPREP_ROWS = 1024


def _value_rows(vt_g):
    tail = lax.broadcasted_iota(jnp.int32, (V_ROWS - HEAD_DIM, vt_g.shape[1]), 0)
    return jnp.concatenate([vt_g, jnp.where(tail == 0, 1.0, 0.0)], axis=0).astype(BF16)


FP8 = jnp.float8_e4m3fn
FP8_MAX_WEIGHT = 32.0


def _split8(x):
    hi = x.astype(FP8).astype(F32)
    return hi, (x - hi).astype(FP8).astype(F32)


def _qkprep_kernel(q_ref, kv_ref, cs_ref, sn_ref, qw_ref, kw_ref, bd_ref, kp_ref, qo_ref, ko_ref, vo_ref, *, fp8):
    cs = cs_ref[...]
    sn = sn_ref[...]
    tm = q_ref.shape[0]
    zeros = jnp.zeros((HEAD_DIM, tm), F32)
    heads_per_chunk = LANES // HEAD_DIM
    for c in range(ATTN_W // LANES):
        sl = slice(c * LANES, (c + 1) * LANES)
        r = _head_norm_rope(q_ref[:, sl], qw_ref[:, sl], cs, sn, bd_ref)
        if fp8:
            hi_t, lo_t = _split8(r.T)
            for half in range(heads_per_chunk):
                rows = slice(half * HEAD_DIM, (half + 1) * HEAD_DIM)
                qo_ref[c * heads_per_chunk + half] = jnp.concatenate(
                    [hi_t[rows], lo_t[rows], hi_t[rows], lo_t[rows]], axis=0).astype(FP8)
        else:
            rt = r.T
            for half in range(heads_per_chunk):
                h = c * heads_per_chunk + half
                blk = rt[half * HEAD_DIM:(half + 1) * HEAD_DIM]
                parts = [blk if g == h // Q_PER_KV else zeros for g in range(N_KV_HEADS)]
                qo_ref[h] = jnp.concatenate(parts, axis=0).astype(BF16)
    k = _head_norm_rope(kv_ref[:, :KV_W], kw_ref[...], cs, sn, bd_ref)
    if fp8:
        hi, lo = _split8(k)
        hi, lo = hi.astype(BF16), lo.astype(BF16)
        for g in range(N_KV_HEADS):
            ko_ref[g] = (_dot(hi, kp_ref[g, 0]) + _dot(lo, kp_ref[g, 1])).astype(FP8)
    else:
        ko_ref[0] = k.astype(BF16)
    vt = kv_ref[:, KV_W:].T
    for g in range(N_KV_HEADS):
        vo_ref[g] = _value_rows(vt[g * HEAD_DIM:(g + 1) * HEAD_DIM])


def _k_placement():
    pm = np.zeros((N_KV_HEADS, 2, KV_W, 4 * HEAD_DIM), np.float32)
    d = np.arange(HEAD_DIM)
    for g in range(N_KV_HEADS):
        for part in range(2):
            for rep in range(2):
                pm[g, part, g * HEAD_DIM + d, (2 * part + rep) * HEAD_DIM + d] = 1.0
    return jnp.asarray(pm, BF16)


def _qkprep_call(p, cs, sn, qw, kw, fp8):
    B, n, _ = p.shape
    tm = _tile(n, QKPREP_ROWS, LANES)
    bd = np.kron(np.eye(LANES // HEAD_DIM), np.full((HEAD_DIM, HEAD_DIM), 1.0 / HEAD_DIM))
    bd = jnp.asarray(bd, BF16)
    kd, kg, dt = (4 * HEAD_DIM, N_KV_HEADS, FP8) if fp8 else (KV_W, 1, BF16)
    return pl.pallas_call(
        functools.partial(_qkprep_kernel, fp8=fp8),
        grid=(n // tm, B),
        in_specs=[
            pl.BlockSpec((None, tm, ATTN_W), lambda i, b: (b, i, P_Q // ATTN_W)),
            pl.BlockSpec((None, tm, 2 * KV_W), lambda i, b: (b, i, P_KV // (2 * KV_W))),
            pl.BlockSpec((tm, LANES), lambda i, b: (i, 0)),
            pl.BlockSpec((tm, LANES), lambda i, b: (i, 0)),
            pl.BlockSpec((1, ATTN_W), lambda i, b: (0, 0)),
            pl.BlockSpec((1, KV_W), lambda i, b: (0, 0)),
            pl.BlockSpec((LANES, LANES), lambda i, b: (0, 0)),
            pl.BlockSpec((N_KV_HEADS, 2, KV_W, 4 * HEAD_DIM), lambda i, b: (0, 0, 0, 0)),
        ],
        out_specs=[
            pl.BlockSpec((None, N_Q_HEADS, kd, tm), lambda i, b: (b, 0, 0, i)),
            pl.BlockSpec((None, kg, tm, kd), lambda i, b: (b, 0, i, 0)),
            pl.BlockSpec((None, N_KV_HEADS, V_ROWS, tm), lambda i, b: (b, 0, 0, i)),
        ],
        out_shape=[
            jax.ShapeDtypeStruct((B, N_Q_HEADS, kd, n), dt),
            jax.ShapeDtypeStruct((B, kg, n, kd), dt),
            jax.ShapeDtypeStruct((B, N_KV_HEADS, V_ROWS, n), BF16),
        ],
        compiler_params=_cparams(("parallel", "parallel")),
        name="qkprep",
    )(p, p, cs, sn, qw, kw, bd, _k_placement())


def _attn_kernel(qt_ref, k_ref, vt_ref, o_ref, m_ref, acc_ref, *, tk):
    nk = vt_ref.shape[0]
    tq = m_ref.shape[-1]
    per_kv_head = k_ref.shape[0] > 1

    def keys(j, h):
        g = h // Q_PER_KV if per_kv_head else 0
        return k_ref[g, pl.ds(pl.multiple_of(j * tk, tk), tk), :]

    def lagged_chunk(j, jump):
        s_next = _dot(keys(j, 0), qt_ref[0])
        for h in range(N_Q_HEADS):
            s = s_next
            if h + 1 < N_Q_HEADS:
                s_next = _dot(keys(j, h + 1), qt_ref[h + 1])
            m_old = m_ref[h]
            p = jnp.exp2((s - m_old).astype(BF16))
            cmax = jnp.max(s, axis=0, keepdims=True)
            m_new = jnp.maximum(m_old, cmax)
            jump = jnp.maximum(jump, cmax - m_old)
            acc_ref[h] = jnp.exp2(m_old - m_new) * (acc_ref[h] + _dot(vt_ref[j, h // Q_PER_KV], p))
            m_ref[h] = m_new
        return jump

    def exact_chunk(j, carry):
        for h in range(N_Q_HEADS):
            s = _dot(keys(j, h), qt_ref[h])
            m_old = m_ref[h]
            m_new = jnp.maximum(m_old, jnp.max(s, axis=0, keepdims=True))
            p = jnp.exp2(s - m_new).astype(BF16)
            acc_ref[h] = jnp.exp2(m_old - m_new) * acc_ref[h] + _dot(vt_ref[j, h // Q_PER_KV], p)
            m_ref[h] = m_new
        return carry

    k_lo = (k_ref.shape[1] // 2) // LAG_INIT_KEYS * LAG_INIT_KEYS
    for h in range(N_Q_HEADS):
        k0 = k_ref[h // Q_PER_KV if per_kv_head else 0, k_lo:k_lo + LAG_INIT_KEYS, :]
        m_ref[h] = jnp.max(_dot(k0, qt_ref[h]), axis=0, keepdims=True)
    acc_ref[...] = jnp.zeros(acc_ref.shape, F32)
    jump = lax.fori_loop(0, nk, lagged_chunk, jnp.zeros((1, tq), F32), unroll=ATTN_UNROLL)

    @pl.when(jnp.logical_not(jnp.max(jump) <= LAG_LIMIT))
    def _():
        m_ref[...] = jnp.full(m_ref.shape, -jnp.inf, F32)
        acc_ref[...] = jnp.zeros(acc_ref.shape, F32)
        lax.fori_loop(0, nk, exact_chunk, 0)

    for c in range(ATTN_W // LANES):
        heads = []
        for h in range(c * (LANES // HEAD_DIM), (c + 1) * (LANES // HEAD_DIM)):
            a = acc_ref[h]
            heads.append(a[:HEAD_DIM] / a[HEAD_DIM:HEAD_DIM + 1])
        o_ref[:, c * LANES:(c + 1) * LANES] = jnp.concatenate(heads, axis=0).T.astype(BF16)


def _attn_call(qt, k, vt):
    B, _, kd, n = qt.shape
    _, kg, Lk, _ = k.shape
    tq = _tile(n, ATTN_QUERIES, LANES)
    tk = _tile(Lk, ATTN_KEYS, LANES)
    nk = Lk // tk
    vt = vt.reshape(B, N_KV_HEADS, V_ROWS, nk, tk).transpose(0, 3, 1, 2, 4)
    return pl.pallas_call(
        functools.partial(_attn_kernel, tk=tk),
        grid=(B, n // tq),
        in_specs=[
            pl.BlockSpec((None, N_Q_HEADS, kd, tq), lambda b, i: (b, 0, 0, i)),
            pl.BlockSpec((None, kg, Lk, kd), lambda b, i: (b, 0, 0, 0)),
            pl.BlockSpec((None, nk, N_KV_HEADS, V_ROWS, tk), lambda b, i: (b, 0, 0, 0, 0)),
        ],
        out_specs=pl.BlockSpec((None, tq, ATTN_W), lambda b, i: (b, i, 0)),
        out_shape=jax.ShapeDtypeStruct((B, n, ATTN_W), BF16),
        scratch_shapes=[
            pltpu.VMEM((N_Q_HEADS, 1, tq), F32),
            pltpu.VMEM((N_Q_HEADS, V_ROWS, tq), F32),
        ],
        compiler_params=_cparams(("parallel", "parallel")),
        name="attn",
    )(qt, k, vt)


HALVES = HYENA_W // LANES


def _split_shape(B, n):
    assert B % 2 == 0
    return jax.ShapeDtypeStruct((B // 2, HALVES, 2, n, LANES), F32)


def _split_spec(tm):
    return pl.BlockSpec((None, HALVES, None, tm, LANES), lambda b, i: (b // 2, 0, b % 2, i, 0))


def _shift_rows(cur, prev_row, next_row, rows):
    tm = cur.shape[0]
    up = jnp.where(rows == 0, prev_row, pltpu.roll(cur, 1, 0))
    dn = jnp.where(rows == tm - 1, next_row, pltpu.roll(cur, tm - 1, 0))
    return up, dn


def _local_kernel(hy_ref, hyp_ref, hyn_ref, po_ref, pop_ref, pon_ref, gu_ref, gv_ref,
                  cw_ref, cb_ref, pw_ref, ps_ref, gnw_ref, gws_ref, gb_ref,
                  v_ref, x1_ref, x2_ref, pl_ref, gm_ref, *, n_seq):
    i = pl.program_id(1)
    last = pl.num_programs(1) - 1
    tm = hy_ref.shape[0]
    has_prev = (i > 0).astype(F32)
    has_next = (i < last).astype(F32)

    cur = hy_ref[...]
    prev_row = hyp_ref[POOL_HALO - 1:POOL_HALO, :] * has_prev
    next_row = hyn_ref[0:1, :] * has_next
    rows = lax.broadcasted_iota(jnp.int32, cur.shape, 0)
    up, dn = _shift_rows(cur, prev_row, next_row, rows)
    z = up * cw_ref[0:1, :] + cur * cw_ref[1:2, :] + dn * cw_ref[2:3, :] + cb_ref[...]
    for s, o_ref in enumerate((v_ref, x1_ref, x2_ref)):
        for hf in range(HYENA_W // LANES):
            lo = s * HYENA_W + hf * LANES
            o_ref[hf] = z[:, lo:lo + LANES]

    zc = po_ref[...]
    ext = jnp.concatenate([pop_ref[...] * has_prev, zc, pon_ref[...] * has_next], axis=0)
    te = tm + 2 * POOL_HALO
    lane = lax.broadcasted_iota(jnp.int32, zc.shape, 1)
    prow = lax.broadcasted_iota(jnp.int32, zc.shape, 0) + i * tm
    half = jnp.left_shift(1, lane // POOL_GROUP_W)
    def ahead(a, k):
        return pltpu.roll(a, (te - k) % te, 0)

    assert POOL_WINDOWS == (2, 4, 8, 16)
    sums = [ext + ahead(ext, -1)]
    for win in POOL_WINDOWS[:-1]:
        sums.append(ahead(sums[-1], win // 2) + ahead(sums[-1], -(win // 2)))
    group = lane // POOL_GROUP_W
    acc = sums[-1][POOL_HALO:POOL_HALO + tm]
    for g in range(len(POOL_WINDOWS) - 2, -1, -1):
        acc = jnp.where(group == g, sums[g][POOL_HALO:POOL_HALO + tm], acc)
    cnt = jnp.minimum(prow + half, n_seq) - jnp.maximum(prow - half, 0)
    dlt = acc / cnt.astype(F32) - zc
    pl_ref[...] = (_dot(dlt.astype(BF16), pw_ref[...]) * ps_ref[...]).astype(BF16)

    gv = gv_ref[...]
    mu = jnp.mean(gv, axis=-1, keepdims=True)
    ctr = gv - mu
    var = jnp.mean(ctr * ctr, axis=-1, keepdims=True)
    vn = (ctr * lax.rsqrt(var + RMS_EPS) * gnw_ref[...]).astype(BF16)
    group = lax.broadcasted_iota(jnp.int32, (GMLP_CHUNK, GMLP_W), 1) // GMLP_GROUP_W
    for c in range(tm // GMLP_CHUNK):
        rs = slice(c * GMLP_CHUNK, (c + 1) * GMLP_CHUNK)
        vc = vn[rs]
        mixed = gb_ref[...]
        for g in range(GMLP_GROUPS):
            mg = _dot(gws_ref[g], vc)
            mixed = mixed + jnp.where(group == g, mg, 0.0)
        gm_ref[rs, :] = (gu_ref[rs, :] * mixed).astype(BF16)


LOCAL_ROWS = 1024


def _local_call(p, lw):
    B, n, _ = p.shape
    tm = _tile(n, LOCAL_ROWS, GMLP_CHUNK)
    nb8 = n // POOL_HALO
    r8 = tm // POOL_HALO
    HW = 3 * HYENA_W

    def cur(w, off):
        return pl.BlockSpec((None, tm, w), lambda b, i: (b, i, off // w))

    def prev(w, off):
        return pl.BlockSpec((None, POOL_HALO, w), lambda b, i: (b, jnp.maximum(i * r8 - 1, 0), off // w))

    def nxt(w, off):
        return pl.BlockSpec((None, POOL_HALO, w), lambda b, i: (b, jnp.minimum((i + 1) * r8, nb8 - 1), off // w))

    def full(shape):
        return pl.BlockSpec(shape, lambda b, i: (0,) * len(shape))

    def out(w):
        return pl.BlockSpec((None, tm, w), lambda b, i: (b, i, 0))

    return pl.pallas_call(
        functools.partial(_local_kernel, n_seq=n),
        grid=(B, n // tm),
        in_specs=[
            cur(HW, P_HY), prev(HW, P_HY), nxt(HW, P_HY),
            cur(POOL_W, P_POOL), prev(POOL_W, P_POOL), nxt(POOL_W, P_POOL),
            cur(GMLP_W, P_GMU), cur(GMLP_W, P_GMV),
            full((3, HW)), full((1, HW)), full((POOL_W, POOL_W)), full((1, POOL_W)),
            full((1, GMLP_W)), full((GMLP_GROUPS, GMLP_CHUNK, GMLP_CHUNK)), full((GMLP_CHUNK, GMLP_W)),
        ],
        out_specs=[_split_spec(tm), _split_spec(tm), _split_spec(tm), out(POOL_W), out(GMLP_W)],
        out_shape=[
            _split_shape(B, n), _split_shape(B, n), _split_shape(B, n),
            jax.ShapeDtypeStruct((B, n, POOL_W), BF16),
            jax.ShapeDtypeStruct((B, n, GMLP_W), BF16),
        ],
        compiler_params=_cparams(("parallel", "parallel")),
        name="local",
    )(p, p, p, p, p, p, p, p, lw["conv_w"], lw["conv_b"], lw["pool_bd"], lw["pool_scale"],
      lw["gm_norm_w"], lw["gm_ws"], lw["gm_bias"])


def _filt_kernel(z_ref, t_ref, w1h_ref, w1l_ref, b1_ref, w2h_ref, w2l_ref, b2_ref, fr_ref,
                 w3h_ref, w3l_ref, dec_ref, k_ref, s_ref, *, n_seq):
    i = pl.program_id(0)
    tb = z_ref.shape[1]
    fr = _lane_tile(fr_ref[...], tb)
    hdn = jnp.sin(fr * (_dot3(w1h_ref[...], w1l_ref[...], z_ref[...]) + _lane_tile(b1_ref[...], tb)))
    for u in range(HYENA_INNER):
        hdn = jnp.sin(fr * (_dot3(w2h_ref[u], w2l_ref[u], hdn) + _lane_tile(b2_ref[u], tb)))
    hdn = hdn.T
    t = t_ref[...]
    tt = jnp.concatenate([t] * (HYENA_ORDER * HYENA_W // LANES), axis=1)
    window = jnp.exp(-tt * jnp.abs(dec_ref[...]))
    k = _dot3x(hdn, w3h_ref[...], w3l_ref[...]) * window
    rows = lax.broadcasted_iota(jnp.int32, k.shape, 0) + i * tb
    k = jnp.where(rows == n_seq, 0.0, k)
    for q in range(k_ref.shape[0]):
        k_ref[q] = k[:, q * LANES:(q + 1) * LANES]

    @pl.when(i == 0)
    def _():
        s_ref[...] = jnp.zeros(s_ref.shape, F32)

    s_ref[...] += jnp.sum(jnp.abs(k), axis=0, keepdims=True)


def _filt_call(n, lw):
    N = 2 * n
    tb = _tile(n, 512)
    nb_half = n // tb
    pos = np.arange(N)
    pos = np.where(pos < n, pos, N - pos).astype(np.float64)
    t = pos / (n - 1)
    bands = np.linspace(1e-4, HYENA_BANDS - 1, HYENA_BANDS)
    ang = (2.0 * math.pi / n) * pos[:, None] * bands
    z = np.concatenate([t[:, None], np.cos(ang), np.sin(ang)], axis=-1)
    zp = np.zeros((LANES, N), np.float32)
    zp[:z.shape[1]] = z.T
    tl = np.broadcast_to(t[:, None], (N, LANES)).astype(np.float32)
    CW = HYENA_ORDER * HYENA_W

    def full(shape):
        return pl.BlockSpec(shape, lambda i: (0,) * len(shape))

    def side(rows_):
        return pl.BlockSpec((None, rows_, CW), lambda i: (i // nb_half, 0, 0))

    return pl.pallas_call(
        functools.partial(_filt_kernel, n_seq=n),
        grid=(N // tb,),
        in_specs=[
            pl.BlockSpec((LANES, tb), lambda i: (0, i)),
            pl.BlockSpec((tb, LANES), lambda i: (i, 0)),
            full((HYENA_HIDDEN, LANES)), full((HYENA_HIDDEN, LANES)), full((HYENA_HIDDEN, LANES)),
            full((HYENA_INNER, HYENA_HIDDEN, HYENA_HIDDEN)), full((HYENA_INNER, HYENA_HIDDEN, HYENA_HIDDEN)),
            full((HYENA_INNER, HYENA_HIDDEN, LANES)), full((HYENA_HIDDEN, LANES)),
            side(HYENA_HIDDEN), side(HYENA_HIDDEN), side(1),
        ],
        out_specs=[pl.BlockSpec((CW // LANES, tb, LANES), lambda i: (0, i, 0)), pl.BlockSpec((1, CW), lambda i: (0, 0))],
        out_shape=[jax.ShapeDtypeStruct((CW // LANES, N, LANES), F32), jax.ShapeDtypeStruct((1, CW), F32)],
        compiler_params=_cparams(("arbitrary",)),
        name="filt",
    )(jnp.asarray(zp), jnp.asarray(tl), lw["hy_w1h"], lw["hy_w1l"], lw["hy_b1"], lw["hy_w2h"], lw["hy_w2l"],
      lw["hy_b2"], lw["hy_freq"], lw["hy_w3h"], lw["hy_w3l"], lw["hy_decay"])


def _fft_dims(N):
    e = int(round(math.log2(N)))
    assert 2 ** e == N
    N1 = 2 ** ((e + 1) // 2)
    return N1, N // N1


def _hilo(a):
    a = np.asarray(a, np.float64)
    hi = jnp.asarray(a, F32).astype(BF16)
    lo = (jnp.asarray(a, F32) - hi.astype(F32)).astype(BF16)
    return hi, lo


@functools.lru_cache(maxsize=None)
def _fft_consts_np(N):
    N1, N2 = _fft_dims(N)
    h = N1 // 2
    k1 = np.arange(N1)
    F1 = np.exp(-2j * np.pi * np.outer(k1, k1) / N1)
    k2 = np.arange(N2)
    F2 = np.exp(-2j * np.pi * np.outer(k2, k2) / N2)
    tw = np.exp(-2j * np.pi * np.outer(k1, k2) / N)
    g1 = np.zeros((2 * N1, N1))
    g1[0::2, :h] = F1.real[:, :h]
    g1[0::2, h:] = -F1.imag[:, :h]
    g1[1::2, :h] = F1.imag[:, :h]
    g1[1::2, h:] = F1.real[:, :h]
    g1f = np.zeros((2 * N1, N1))
    g1f[0::2] = F1.real
    g1f[1::2] = F1.imag
    g2 = np.block([[F2.real, -F2.imag], [F2.imag, F2.real]])
    g2c = np.block([[F2.real, F2.imag], [-F2.imag, F2.real]])
    g3 = np.zeros((N1, 2 * N1))
    g3[:h, 0::2] = F1.real[:h]
    g3[:h, 1::2] = F1.imag[:h]
    g3[h:, 0::2] = -F1.imag[:h]
    g3[h:, 1::2] = F1.real[:h]
    g3 /= N
    twt = np.stack([tw.real, tw.imag], axis=1)
    return dict(g1=g1, g1f=g1f, g2=g2, g2c=g2c, g3=g3, tw=twt)


def _fft_consts(N):
    c = _fft_consts_np(N)
    out = {k: _hilo(c[k]) for k in ("g1", "g1f", "g2", "g2c", "g3")}
    N1, N2 = _fft_dims(N)
    tw = jnp.asarray(c["tw"], F32)
    out["tw"] = jnp.broadcast_to(tw[..., None], (N1, 2, N2, LANES))
    return out


T2_BLOCK = 8


def _load_8th(ref, j, rows):
    return ref.reshape(rows * T2_BLOCK, LANES)[pl.ds(j, rows, stride=T2_BLOCK), :]


def _store_8th(ref, j, rows, val):
    ref.reshape(rows * T2_BLOCK, LANES)[pl.ds(j, rows, stride=T2_BLOCK), :] = val


def _dft_in_kernel(gh_ref, gl_ref, x_ref, o_ref):
    N1 = x_ref.shape[1]
    for j in range(T2_BLOCK):
        xj = jnp.concatenate([_load_8th(x_ref.at[hf], j, N1) for hf in range(HALVES)], axis=1)
        a = _dot3(gh_ref[...], gl_ref[...], xj)
        for hf in range(HALVES):
            _store_8th(o_ref.at[hf], j, 2 * N1, a[:, hf * LANES:(hf + 1) * LANES])


def _dft_in_call(g, u5):
    gh, gl = g
    G, _, N1, N2, _ = u5.shape
    assert gh.shape == (2 * N1, N1)
    mat = pl.BlockSpec((2 * N1, N1), lambda b, i: (0, 0))
    return pl.pallas_call(
        _dft_in_kernel,
        grid=(G, N2 // T2_BLOCK),
        in_specs=[mat, mat, pl.BlockSpec((None, HALVES, N1, T2_BLOCK, LANES), lambda b, i: (b, 0, 0, i, 0))],
        out_specs=pl.BlockSpec((None, HALVES, N1, 2, T2_BLOCK, LANES), lambda b, i: (b, 0, 0, 0, i, 0)),
        out_shape=jax.ShapeDtypeStruct((G, HALVES, N1, 2, N2, LANES), F32),
        compiler_params=_cparams(("parallel", "parallel")),
        name="dft_in",
    )(gh, gl, u5)


def _dft_out_kernel(gh_ref, gl_ref, d_ref, gate_ref, o_ref):
    N1 = gate_ref.shape[1]
    for j in range(T2_BLOCK):
        dj = jnp.concatenate([_load_8th(d_ref.at[hf], j, 2 * N1) for hf in range(HALVES)], axis=1)
        y = _dot3(gh_ref[...], gl_ref[...], dj)
        for hf in range(HALVES):
            conv = y[:, hf * LANES:(hf + 1) * LANES]
            _store_8th(o_ref.at[hf], j, N1, _load_8th(gate_ref.at[hf], j, N1) * conv)


def _dft_out_call(g, d6, gate5):
    gh, gl = g
    G, _, N1, N2, _ = gate5.shape
    assert gh.shape == (N1, 2 * N1)
    mat = pl.BlockSpec((N1, 2 * N1), lambda b, i: (0, 0))
    blk = pl.BlockSpec((None, HALVES, N1, T2_BLOCK, LANES), lambda b, i: (b, 0, 0, i, 0))
    return pl.pallas_call(
        _dft_out_kernel,
        grid=(G, N2 // T2_BLOCK),
        in_specs=[mat, mat, pl.BlockSpec((None, HALVES, N1, 2, T2_BLOCK, LANES), lambda b, i: (b, 0, 0, 0, i, 0)),
                  blk],
        out_specs=blk,
        out_shape=jax.ShapeDtypeStruct(gate5.shape, F32),
        compiler_params=_cparams(("parallel", "parallel")),
        name="dft_out",
    )(gh, gl, d6, gate5)


def _dft_mid_kernel(g3h_ref, g3l_ref, g1h_ref, g1l_ref, d_ref, gate_ref, o_ref):
    N1 = gate_ref.shape[1]
    for j in range(T2_BLOCK):
        dj = jnp.concatenate([_load_8th(d_ref.at[hf], j, 2 * N1) for hf in range(HALVES)], axis=1)
        y = _dot3(g3h_ref[...], g3l_ref[...], dj)
        gate = jnp.concatenate([_load_8th(gate_ref.at[hf], j, N1) for hf in range(HALVES)], axis=1)
        a = _dot3(g1h_ref[...], g1l_ref[...], gate * y)
        for hf in range(HALVES):
            _store_8th(o_ref.at[hf], j, 2 * N1, a[:, hf * LANES:(hf + 1) * LANES])


def _dft_mid_call(g3, g1, d6, gate5):
    G, _, N1, N2, _ = gate5.shape
    assert g3[0].shape == (N1, 2 * N1) and g1[0].shape == (2 * N1, N1)
    mat3 = pl.BlockSpec((N1, 2 * N1), lambda b, i: (0, 0))
    mat1 = pl.BlockSpec((2 * N1, N1), lambda b, i: (0, 0))
    wide = pl.BlockSpec((None, HALVES, N1, 2, T2_BLOCK, LANES), lambda b, i: (b, 0, 0, 0, i, 0))
    return pl.pallas_call(
        _dft_mid_kernel,
        grid=(G, N2 // T2_BLOCK),
        in_specs=[mat3, mat3, mat1, mat1, wide,
                  pl.BlockSpec((None, HALVES, N1, T2_BLOCK, LANES), lambda b, i: (b, 0, 0, i, 0))],
        out_specs=wide,
        out_shape=jax.ShapeDtypeStruct(d6.shape, F32),
        compiler_params=_cparams(("parallel", "parallel")),
        name="dft_mid",
    )(g3[0], g3[1], g1[0], g1[1], d6, gate5)


def _cmul(ar, ai, br, bi):
    return ar * br - ai * bi, ar * bi + ai * br


def _lane_tile(t, c):
    return jnp.concatenate([t] * (c // LANES), axis=-1)


def _both_halves(a_ref, k, part):
    return jnp.concatenate([a_ref[hf, k, part] for hf in range(HALVES)], axis=-1)


def _spec_kernel(a_ref, tw_ref, g2h_ref, g2l_ref, s_ref, sk_ref, o_ref):
    _, kb, _, N2, _ = a_ref.shape
    inv = 1.0 / s_ref[...]
    for k in range(kb):
        twr = _lane_tile(tw_ref[k, 0], HYENA_W)
        twi = _lane_tile(tw_ref[k, 1], HYENA_W)
        br, bi = _cmul(_both_halves(a_ref, k, 0), _both_halves(a_ref, k, 1), twr, twi)
        zz = _dot3(g2h_ref[...], g2l_ref[...], jnp.concatenate([br, bi], axis=0))
        o_ref[k, 0] = zz[:N2] * inv + sk_ref[...]
        o_ref[k, 1] = zz[N2:] * inv


def _conv_kernel(a_ref, tw_ref, kf_ref, g2h_ref, g2l_ref, g2ch_ref, g2cl_ref, o_ref):
    _, kb, _, N2, _ = a_ref.shape
    for k in range(kb):
        twr = _lane_tile(tw_ref[k, 0], HYENA_W)
        twi = _lane_tile(tw_ref[k, 1], HYENA_W)
        br, bi = _cmul(_both_halves(a_ref, k, 0), _both_halves(a_ref, k, 1), twr, twi)
        zz = _dot3(g2h_ref[...], g2l_ref[...], jnp.concatenate([br, bi], axis=0))
        yr, yi = _cmul(zz[:N2], zz[N2:], kf_ref[k, 0], kf_ref[k, 1])
        cc = _dot3(g2ch_ref[...], g2cl_ref[...], jnp.concatenate([yr, yi], axis=0))
        dr, di = _cmul(cc[:N2], cc[N2:], twr, -twi)
        for hf in range(HALVES):
            o_ref[hf, k, 0] = dr[:, hf * LANES:(hf + 1) * LANES]
            o_ref[hf, k, 1] = di[:, hf * LANES:(hf + 1) * LANES]


SPEC_K1 = 16


def _spec_call(a6, fc, s, skip):
    G, _, N1, _, N2, _ = a6.shape
    kb = _tile(N1, SPEC_K1, 1)
    mat = pl.BlockSpec((2 * N2, 2 * N2), lambda i, b: (0, 0))
    return pl.pallas_call(
        _spec_kernel,
        grid=(N1 // kb, G),
        in_specs=[pl.BlockSpec((None, HALVES, kb, 2, N2, LANES), lambda i, b: (b, 0, i, 0, 0, 0)),
                  pl.BlockSpec((kb, 2, N2, LANES), lambda i, b: (i, 0, 0, 0)), mat, mat,
                  pl.BlockSpec((None, 1, HYENA_W), lambda i, b: (b, 0, 0)),
                  pl.BlockSpec((None, 1, HYENA_W), lambda i, b: (b, 0, 0))],
        out_specs=pl.BlockSpec((None, kb, 2, N2, HYENA_W), lambda i, b: (b, i, 0, 0, 0)),
        out_shape=jax.ShapeDtypeStruct((G, N1, 2, N2, HYENA_W), F32),
        compiler_params=_cparams(("parallel", "parallel")),
        name="filt_spec",
    )(a6, fc["tw"], fc["g2"][0], fc["g2"][1], s, skip)


def _conv_call(a6, kf, order, fc):
    G, _, N1, _, N2, _ = a6.shape
    kb = _tile(N1, SPEC_K1, 1)
    blk = pl.BlockSpec((None, HALVES, kb, 2, N2, LANES), lambda i, b: (b, 0, i, 0, 0, 0))
    mat = pl.BlockSpec((2 * N2, 2 * N2), lambda i, b: (0, 0))
    return pl.pallas_call(
        _conv_kernel,
        grid=(N1 // kb, G),
        in_specs=[blk, pl.BlockSpec((kb, 2, N2, LANES), lambda i, b: (i, 0, 0, 0)),
                  pl.BlockSpec((None, kb, 2, N2, HYENA_W), lambda i, b: (order, i, 0, 0, 0)), mat, mat, mat, mat],
        out_specs=blk,
        out_shape=jax.ShapeDtypeStruct(a6.shape, F32),
        compiler_params=_cparams(("parallel", "parallel")),
        name="spec_conv",
    )(a6, fc["tw"], kf, fc["g2"][0], fc["g2"][1], fc["g2c"][0], fc["g2c"][1])


def _filter_spectra(n, lw, fc):
    N1, N2 = _fft_dims(2 * n)
    k, s = _filt_call(n, lw)
    a6 = _dft_in_call(fc["g1f"], k.reshape(HYENA_ORDER, HALVES, N1, N2, LANES))
    return _spec_call(a6, fc, s.reshape(HYENA_ORDER, 1, HYENA_W),
                      lw["hy_skip"].reshape(HYENA_ORDER, 1, HYENA_W))


def _hyena(v, x1, x2, kf, lw, fc):
    P, _, _, n, _ = v.shape
    N1, N2 = _fft_dims(2 * n)
    shape5 = (P, HALVES, N1, N2, LANES)
    gates = (x1, x2)
    a = _dft_in_call(fc["g1"], v.reshape(shape5))
    for o in range(HYENA_ORDER - 1):
        a = _dft_mid_call(fc["g3"], fc["g1"], _conv_call(a, kf, o, fc), gates[o].reshape(shape5))
    y = _dft_out_call(fc["g3"], _conv_call(a, kf, HYENA_ORDER - 1, fc), gates[-1].reshape(shape5))
    return y.reshape(v.shape)


def _merge_kernel(at_ref, hy_ref, po_ref, gm_ref, x_ref, sh_ref, sc_ref, nw_ref, gr_ref,
                  wgate_ref, wa_ref, wh_ref, wp_ref, wg_ref, wo_ref, o_ref):
    D = x_ref.shape[1]
    for rows in _row_blocks(x_ref.shape[0]):
        x = x_ref[rows, :]
        h = _modulated_norm(x, nw_ref[...], sc_ref[...], sh_ref[...]).astype(BF16)
        hy = jnp.concatenate([hy_ref[hf, rows, :] for hf in range(HALVES)], axis=-1).astype(BF16)
        branches = ((at_ref[rows, :], wa_ref), (hy, wh_ref), (po_ref[rows, :], wp_ref), (gm_ref[rows, :], wg_ref))
        merged = None
        for i, (br, w_ref) in enumerate(branches):
            term = _sigmoid(_dot(h, wgate_ref[:, i * D:(i + 1) * D])) * _dot(br, w_ref[...])
            merged = term if merged is None else merged + term
        o_ref[rows, :] = x + gr_ref[...] * _dot(merged.astype(BF16), wo_ref[...])


MERGE_ROWS = 1024


def _merge_call(attn, hy, po, gm, x, shift, scale, norm_w, gate_res, lw):
    B, n, D = x.shape
    tm = _tile(n, MERGE_ROWS)

    def rows(w):
        return pl.BlockSpec((None, tm, w), lambda b, i: (b, i, 0))

    def full(shape):
        return pl.BlockSpec(shape, lambda b, i: (0,) * len(shape))

    vec = pl.BlockSpec((None, 1, D), lambda b, i: (b, 0, 0))
    return pl.pallas_call(
        _merge_kernel,
        grid=(B, n // tm),
        in_specs=[
            rows(ATTN_W), _split_spec(tm), rows(POOL_W), rows(GMLP_W), rows(D),
            vec, vec, full((1, D)), vec,
            full((D, N_BRANCH * D)),
            full((ATTN_W, D)), full((HYENA_W, D)), full((POOL_W, D)), full((GMLP_W, D)), full((D, D)),
        ],
        out_specs=rows(D),
        out_shape=jax.ShapeDtypeStruct((B, n, D), F32),
        compiler_params=_cparams(("parallel", "parallel")),
        name="merge",
    )(attn, hy, po, gm, x, shift, scale, norm_w.reshape(1, D), gate_res, lw["w_gate"], lw["w_br_attn"],
      lw["w_br_hyena"], lw["w_br_pool"], lw["w_br_gmlp"], lw["w_out"])


FFN_ROWS = 1024


def _ffn_kernel(x_ref, sh_ref, sc_ref, nw_ref, gr_ref, wg_ref, wu_ref, wd_ref, fw_ref, o_ref, *, final):
    for rows in _row_blocks(x_ref.shape[0]):
        x = x_ref[rows, :]
        h = _modulated_norm(x, nw_ref[...], sc_ref[...], sh_ref[...]).astype(BF16)
        a = _silu(_dot(h, wg_ref[...])) * _dot(h, wu_ref[...])
        y = x + gr_ref[...] * _dot(a.astype(BF16), wd_ref[...])
        if final:
            y = y * lax.rsqrt(jnp.mean(y * y, axis=-1, keepdims=True) + RMS_EPS) * fw_ref[...]
        o_ref[rows, :] = y


def _ffn_call(x, shift, scale, norm_w, gate_res, wg, wu, wd, final_w=None):
    B, n, D = x.shape
    H = wg.shape[1]
    tm = _tile(n, FFN_ROWS)
    vec = pl.BlockSpec((None, 1, D), lambda b, i: (b, 0, 0))
    one = pl.BlockSpec((1, D), lambda b, i: (0, 0))
    final = final_w is not None
    fw = (final_w if final else norm_w).reshape(1, D)
    return pl.pallas_call(
        functools.partial(_ffn_kernel, final=final),
        grid=(B, n // tm),
        in_specs=[
            pl.BlockSpec((None, tm, D), lambda b, i: (b, i, 0)),
            vec, vec, one, vec,
            pl.BlockSpec((D, H), lambda b, i: (0, 0)),
            pl.BlockSpec((D, H), lambda b, i: (0, 0)),
            pl.BlockSpec((H, D), lambda b, i: (0, 0)),
            one,
        ],
        out_specs=pl.BlockSpec((None, tm, D), lambda b, i: (b, i, 0)),
        out_shape=jax.ShapeDtypeStruct((B, n, D), F32),
        compiler_params=_cparams(("parallel", "parallel")),
        name="ffn",
    )(x, shift, scale, norm_w.reshape(1, D), gate_res, wg, wu, wd, fw)


def _rope_tables(n):
    tok = np.arange(n)
    axis_dim = HEAD_DIM // 2
    inv_freq = ROPE_THETA ** (-np.arange(0, axis_dim, 2, dtype=np.float64) / axis_dim)
    inv_freq = inv_freq.astype(np.float32).astype(np.float64)
    ang = np.concatenate([(tok // GRID_W)[:, None] * inv_freq, (tok % GRID_W)[:, None] * inv_freq], axis=-1)
    ang = ang.astype(np.float32)
    cos = np.repeat(np.cos(ang), 2, axis=-1)
    sin = np.repeat(np.sin(ang), 2, axis=-1)
    sin[:, 0::2] *= -1.0
    reps = LANES // HEAD_DIM
    return jnp.asarray(np.tile(cos, (1, reps)), F32), jnp.asarray(np.tile(sin, (1, reps)), F32)


def _identity_tables(n):
    return jnp.ones((n, LANES), F32), jnp.zeros((n, LANES), F32)


def _layer_weights(l, a):
    w_in = a["w_in"][l]
    w_perm = jnp.concatenate([
        w_in[:, Q_OFF:K_OFF], w_in[:, K_OFF:HY_OFF], w_in[:, POOL_OFF:GM_OFF], w_in[:, GM_OFF:GATE_OFF],
        w_in[:, HY_OFF:POOL_OFF]], axis=1).astype(BF16)
    w3 = a["hy_w3"][l].reshape(HYENA_HIDDEN, HYENA_ORDER, 2, HYENA_W).transpose(2, 0, 1, 3)
    w3 = w3.reshape(2, HYENA_HIDDEN, HYENA_ORDER * HYENA_W)
    dec = a["hy_decay"][l].reshape(HYENA_ORDER, 2, HYENA_W).transpose(1, 0, 2).reshape(2, 1, HYENA_ORDER * HYENA_W)
    w1 = jnp.zeros((HYENA_HIDDEN, LANES), F32).at[:, :a["hy_w1"].shape[1]].set(a["hy_w1"][l].T)
    w1h, w1l = _split(w1)
    w2h, w2l = _split(jnp.swapaxes(a["hy_w2"][l], 1, 2))

    def lanes_of(v):
        return jnp.broadcast_to(v[..., None], v.shape + (LANES,))
    w3h, w3l = _split(w3)
    qw16 = jnp.tile(a["q_norm_w"][l], N_Q_HEADS).reshape(1, ATTN_W) * Q_SCALE
    kw16 = jnp.tile(a["k_norm_w"][l], N_KV_HEADS).reshape(1, KV_W)
    rms_q = jnp.sqrt(jnp.mean(qw16 * qw16))
    rms_k = jnp.sqrt(jnp.mean(kw16 * kw16))
    bal = jnp.exp2(jnp.clip(jnp.round(0.5 * jnp.log2(rms_k / rms_q)), -20.0, 20.0))
    bal = jnp.where(jnp.isfinite(bal), bal, 1.0)
    fp8_ok = (jnp.max(jnp.abs(qw16 * bal)) <= FP8_MAX_WEIGHT) & (jnp.max(jnp.abs(kw16 / bal)) <= FP8_MAX_WEIGHT)
    eye = jnp.eye(len(POOL_WINDOWS), dtype=F32)
    pool_bd = jnp.einsum("gh,gcd->gchd", eye, a["pool_w"][l]).reshape(POOL_W, POOL_W).astype(BF16)
    gm_bias = jnp.repeat(a["gm_bs"][l].T, GMLP_GROUP_W, axis=1)
    return dict(
        w_in=w_perm, w_gate=w_in[:, GATE_OFF:].astype(BF16),
        qw16=qw16, kw16=kw16, qw8=qw16 * bal, kw8=kw16 / bal, fp8_ok=fp8_ok,
        conv_w=a["hy_conv_w"][l], conv_b=a["hy_conv_b"][l].reshape(1, -1),
        hy_w1h=w1h, hy_w1l=w1l, hy_b1=lanes_of(a["hy_b1"][l]), hy_w2h=w2h, hy_w2l=w2l, hy_b2=lanes_of(a["hy_b2"][l]),
        hy_w3h=w3h, hy_w3l=w3l, hy_freq=lanes_of(a["hy_freq"][l]), hy_decay=dec, hy_skip=a["hy_skip"][l],
        pool_bd=pool_bd, pool_scale=a["pool_scale"][l].reshape(1, -1),
        gm_norm_w=a["gm_norm_w"][l].reshape(1, -1), gm_ws=a["gm_ws"][l].astype(BF16), gm_bias=gm_bias,
        w_br_attn=a["w_br_attn"][l].astype(BF16), w_br_hyena=a["w_br_hyena"][l].astype(BF16),
        w_br_pool=a["w_br_pool"][l].astype(BF16), w_br_gmlp=a["w_br_gmlp"][l].astype(BF16),
        w_out=a["w_out"][l].astype(BF16),
        ffn_w_gate=a["ffn_w_gate"][l].astype(BF16), ffn_w_up=a["ffn_w_up"][l].astype(BF16),
        ffn_w_down=a["ffn_w_down"][l].astype(BF16),
        norm1_w=a["norm1_w"][l], norm2_w=a["norm2_w"][l],
    )


Q_SCALE = (HEAD_DIM ** -0.5) * math.log2(math.e)


def _attend(p, p_ctx, rope, rope_ctx, lw):
    def run(fp8):
        qw, kw = (lw["qw8"], lw["kw8"]) if fp8 else (lw["qw16"], lw["kw16"])
        qt, k, vt = _qkprep_call(p, rope[0], rope[1], qw, kw, fp8)
        if p_ctx is not None:
            _, k_c, vt_c = _qkprep_call(p_ctx, rope_ctx[0], rope_ctx[1], qw, kw, fp8)
            k = jnp.concatenate([k_c, k], axis=2)
            vt = jnp.concatenate([vt_c, vt], axis=3)
        return _attn_call(qt, k, vt)

    return lax.cond(lw["fp8_ok"], lambda: run(True), lambda: run(False))


def _token_major(fn, x, vecs, *rest):
    B, n, D = x.shape
    y = fn(x.reshape(1, B * n, D), *[v[:1] for v in vecs], *rest)
    return y.reshape(B, n, y.shape[-1])


def _mixer_block(x, mods, lw, rope, p_ctx, rope_ctx, kf, fc, final_w=None, shared_mods=False):
    sh1, s1, g1, sh2, s2, g2 = mods
    if shared_mods:
        p = _token_major(lambda xf, a, b: _inproj_call(xf, a, b, lw["norm1_w"], lw["w_in"]), x, (sh1, s1))
    else:
        p = _inproj_call(x, sh1, s1, lw["norm1_w"], lw["w_in"])
    attn = _attend(p, p_ctx, rope, rope_ctx, lw)
    hv, hx1, hx2, po, gm = _local_call(p, lw)
    hy = _hyena(hv, hx1, hx2, kf, lw, fc)
    x = _merge_call(attn, hy, po, gm, x, sh1, s1, lw["norm1_w"], g1, lw)
    ffn_w = (lw["ffn_w_gate"], lw["ffn_w_up"], lw["ffn_w_down"], final_w)
    if shared_mods:
        x = _token_major(lambda xf, a, b, g: _ffn_call(xf, a, b, lw["norm2_w"], g, *ffn_w), x, (sh2, s2, g2))
    else:
        x = _ffn_call(x, sh2, s2, lw["norm2_w"], g2, *ffn_w)
    return x, p


def kernel(x, c, ctx, c_ctx, w_mod, b_mod, norm1_w, norm2_w, w_in, q_norm_w, k_norm_w, hy_conv_w, hy_conv_b,
           hy_w1, hy_b1, hy_w2, hy_b2, hy_w3, hy_freq, hy_decay, hy_skip, pool_w, pool_scale, gm_norm_w, gm_ws,
           gm_bs, w_br_attn, w_br_hyena, w_br_pool, w_br_gmlp, w_out, ffn_w_gate, ffn_w_up, ffn_w_down,
           final_norm_w):
    a = dict(w_in=w_in, q_norm_w=q_norm_w, k_norm_w=k_norm_w, hy_conv_w=hy_conv_w, hy_conv_b=hy_conv_b,
             hy_w1=hy_w1, hy_b1=hy_b1, hy_w2=hy_w2, hy_b2=hy_b2, hy_w3=hy_w3, hy_freq=hy_freq, hy_decay=hy_decay,
             hy_skip=hy_skip, pool_w=pool_w, pool_scale=pool_scale, gm_norm_w=gm_norm_w, gm_ws=gm_ws, gm_bs=gm_bs,
             w_br_attn=w_br_attn, w_br_hyena=w_br_hyena, w_br_pool=w_br_pool, w_br_gmlp=w_br_gmlp, w_out=w_out,
             ffn_w_gate=ffn_w_gate, ffn_w_up=ffn_w_up, ffn_w_down=ffn_w_down, norm1_w=norm1_w, norm2_w=norm2_w)
    B, n, D = x.shape
    nc = ctx.shape[1]
    depth = w_mod.shape[0]

    R = ((B + 1 + 15) // 16) * 16
    act = jnp.zeros((R, D), F32).at[:B].set(c).at[B].set(c_ctx)
    mod = _mod_call(act, w_mod, b_mod)

    rope = _rope_tables(n)
    rope_ctx = _identity_tables(nc)
    fc = _fft_consts(2 * n)
    fc_ctx = _fft_consts(2 * nc)

    xc = ctx
    for l in range(depth):
        lw = _layer_weights(l, a)
        mods = [m.reshape(B, 1, D) for m in jnp.split(mod[l, :B], 6, axis=-1)]
        cmods = [jnp.broadcast_to(m.reshape(1, 1, D), (B, 1, D)) for m in jnp.split(mod[l, B], 6, axis=-1)]
        if l < depth - 1:
            kf_ctx = _filter_spectra(nc, lw, fc_ctx)
            xc, pc = _mixer_block(xc, cmods, lw, rope_ctx, None, None, kf_ctx, fc_ctx, shared_mods=True)
        else:
            pc = _token_major(lambda xf, a, b: _inproj_call(xf, a, b, lw["norm1_w"], lw["w_in"][:, :P_QKV]),
                              xc, (cmods[0], cmods[1]))
        kf = _filter_spectra(n, lw, fc)
        x, _ = _mixer_block(x, mods, lw, rope, pc, rope_ctx, kf, fc, final_norm_w if l == depth - 1 else None)
    return x
```
